```python
import math
import jax, jax.numpy as jnp
from jax import lax
import numpy as np

D_MODEL = 2048
BATCH = 4
SEQ = 2048
DEPTH = 1

CHUNK = 64
Q_BLOCK = 128
D_MIX = D_MODEL
DIFF_HEADS = 8
DIFF_WIDTH = D_MIX // 2
DIFF_HEAD_DIM = DIFF_WIDTH // DIFF_HEADS
DIFF_QK_DIM = DIFF_HEAD_DIM // 2
GLA_HEADS = 4
GLA_WIDTH = D_MIX - DIFF_WIDTH
GLA_V_DIM = GLA_WIDTH // GLA_HEADS
GLA_K_DIM = GLA_V_DIM // 2
GLA_GATE_RANK = 16
GLA_GATE_TAU = 16.0
ROPE_THETA = 10000.0
NORM_EPS = 1e-6

IN_SPLITS = [
    DIFF_WIDTH,
    DIFF_WIDTH,
    DIFF_WIDTH,
    DIFF_WIDTH,
    GLA_HEADS * GLA_K_DIM,
    GLA_HEADS * GLA_K_DIM,
    GLA_WIDTH,
    GLA_WIDTH,
    GLA_GATE_RANK,
]
D_IN_PROJ = int(sum(IN_SPLITS))
IN_OFFSETS = [int(o) for o in np.cumsum(IN_SPLITS)[:-1]]

kernel_name = "hymba_diffattn_gla_chunk_causal"


def lambda_init_fn(layer_idx):
    return 0.8 - 0.6 * math.exp(-0.3 * layer_idx)


def rmsnorm(x, w):
    xf = x.astype(jnp.float32)
    y = xf * lax.rsqrt(jnp.mean(xf * xf, axis=-1, keepdims=True) + NORM_EPS)
    return y * w.astype(jnp.float32)


def rope(x, pos):
    dh = x.shape[-1]
    inv_freq = ROPE_THETA ** (-jnp.arange(0, dh, 2, dtype=jnp.float32) / dh)
    ang = pos[:, None] * inv_freq[None, :]
    bshape = (1, ang.shape[0]) + (1,) * (x.ndim - 3) + (dh // 2,)
    cos = jnp.cos(ang).reshape(bshape)
    sin = jnp.sin(ang).reshape(bshape)
    x1, x2 = x[..., : dh // 2], x[..., dh // 2:]
    return jnp.concatenate([x1 * cos - x2 * sin, x2 * cos + x1 * sin], axis=-1)


def diff_attention(q, k, v, lam):
    S = q.shape[1]
    scale = DIFF_QK_DIM ** -0.5
    chunk_id = jnp.arange(S) // CHUNK
    outs = []
    for blk in range(S // Q_BLOCK):
        s0, s1 = blk * Q_BLOCK, (blk + 1) * Q_BLOCK
        qb, kb, vb = q[:, s0:s1], k[:, :s1], v[:, :s1]
        s = jnp.einsum('bqhmd,bkhmd->bmhqk', qb, kb) * scale
        mask = chunk_id[None, :s1] <= chunk_id[s0:s1, None]
        s = jnp.where(mask, s, -jnp.inf)
        p = jax.nn.softmax(s, axis=-1)
        a = p[:, 0] - lam * p[:, 1]
        outs.append(jnp.einsum('bhqk,bkhd->bqhd', a, vb))
    return jnp.concatenate(outs, axis=1)


def gla_chunk_causal(q, k, v, log_a):
    B, S, H, dk = q.shape
    dv = v.shape[-1]
    nc = S // CHUNK
    def to_chunks(t):
        return t.reshape(B, nc, CHUNK, H, t.shape[-1]).transpose(1, 0, 3, 2, 4)
    qc, kc, vc, lc = to_chunks(q), to_chunks(k), to_chunks(v), to_chunks(log_a)
    b = jnp.cumsum(lc, axis=3)
    b_tot = b[:, :, :, -1:, :]
    k_dec = kc * jnp.exp(b_tot - b)
    chunk_decay = jnp.exp(b_tot[:, :, :, 0, :])

    def step(state, inp):
        qi, ki, vi, di = inp
        state = di[..., None] * state + jnp.einsum('bhck,bhcv->bhkv', ki, vi)
        return state, jnp.einsum('bhck,bhkv->bhcv', qi, state)

    state0 = jnp.zeros((B, H, dk, dv), jnp.float32)
    _, o = lax.scan(step, state0, (qc, k_dec, vc, chunk_decay))
    return o.transpose(1, 0, 3, 2, 4).reshape(B, S, H, dv)


def setup_inputs(seed: int = 0) -> dict:
    key = jax.random.key(seed)
    ks = jax.random.split(key, 17)
    f32 = jnp.float32
    nrm = lambda k, shp, s: jax.random.normal(k, shp, f32) * s
    return {
        "x": nrm(ks[0], (BATCH, SEQ, D_MODEL), 1.0),
        "c": nrm(ks[1], (BATCH, D_MODEL), 1.0),
        "norm_w": 1.0 + nrm(ks[2], (DEPTH, D_MODEL), 0.02),
        "w_ada": nrm(ks[3], (DEPTH, D_MODEL, 3 * D_MODEL), 0.5 * D_MODEL ** -0.5),
        "b_ada": nrm(ks[4], (DEPTH, 3 * D_MODEL), 0.02),
        "w_in": nrm(ks[5], (DEPTH, D_MODEL, D_IN_PROJ), D_MODEL ** -0.5),
        "lambda_q1": nrm(ks[6], (DEPTH, DIFF_QK_DIM), 0.1),
        "lambda_k1": nrm(ks[7], (DEPTH, DIFF_QK_DIM), 0.1),
        "lambda_q2": nrm(ks[8], (DEPTH, DIFF_QK_DIM), 0.1),
        "lambda_k2": nrm(ks[9], (DEPTH, DIFF_QK_DIM), 0.1),
        "diff_norm_w": 1.0 + nrm(ks[10], (DEPTH, DIFF_HEAD_DIM), 0.02),
        "gla_gate_w2": nrm(ks[11], (DEPTH, GLA_GATE_RANK, GLA_HEADS * GLA_K_DIM), GLA_GATE_RANK ** -0.5),
        "gla_gate_b": nrm(ks[12], (DEPTH, GLA_HEADS * GLA_K_DIM), 0.1),
        "gla_norm_w": 1.0 + nrm(ks[13], (DEPTH, GLA_V_DIM), 0.02),
        "w_out": nrm(ks[14], (DEPTH, D_MIX, D_MODEL), D_MIX ** -0.5),
        "final_norm_w": 1.0 + nrm(ks[15], (D_MODEL,), 0.02),
    }


def reference(x, c, norm_w, w_ada, b_ada, w_in, lambda_q1, lambda_k1, lambda_q2, lambda_k2,
              diff_norm_w, gla_gate_w2, gla_gate_b, gla_norm_w, w_out, final_norm_w):
    B, S, _ = x.shape
    f32 = jnp.float32
    pos = jnp.arange(S, dtype=f32)
    h_res = x.astype(f32)
    c_act = jax.nn.silu(c.astype(f32))
    for l in range(DEPTH):
        lam_init = lambda_init_fn(l)
        mod = c_act @ w_ada[l].astype(f32) + b_ada[l].astype(f32)
        shift, scale, gate = jnp.split(mod, 3, axis=-1)
        hn = rmsnorm(h_res, norm_w[l]) * (1.0 + scale[:, None, :]) + shift[:, None, :]

        proj = hn @ w_in[l].astype(f32)
        dq, dk, dv, dg, gq, gk, gv, gg, glr = jnp.split(proj, IN_OFFSETS, axis=-1)

        dq = rope(dq.reshape(B, S, DIFF_HEADS, 2, DIFF_QK_DIM), pos)
        dk = rope(dk.reshape(B, S, DIFF_HEADS, 2, DIFF_QK_DIM), pos)
        dv = dv.reshape(B, S, DIFF_HEADS, DIFF_HEAD_DIM)
        lam = (jnp.exp(jnp.sum(lambda_q1[l].astype(f32) * lambda_k1[l].astype(f32)))
               - jnp.exp(jnp.sum(lambda_q2[l].astype(f32) * lambda_k2[l].astype(f32)))
               + lam_init)
        a_out = diff_attention(dq, dk, dv, lam)
        a_out = rmsnorm(a_out, diff_norm_w[l]) * (1.0 - lam_init)
        a_out = a_out.reshape(B, S, DIFF_WIDTH) * jax.nn.silu(dg)

        gq = gq.reshape(B, S, GLA_HEADS, GLA_K_DIM) * (GLA_K_DIM ** -0.5)
        gk = gk.reshape(B, S, GLA_HEADS, GLA_K_DIM)
        gv = gv.reshape(B, S, GLA_HEADS, GLA_V_DIM)
        log_a = jax.nn.log_sigmoid(glr @ gla_gate_w2[l].astype(f32) + gla_gate_b[l].astype(f32)) / GLA_GATE_TAU
        log_a = log_a.reshape(B, S, GLA_HEADS, GLA_K_DIM)
        b_out = gla_chunk_causal(gq, gk, gv, log_a)
        b_out = rmsnorm(b_out, gla_norm_w[l]).reshape(B, S, GLA_WIDTH) * jax.nn.silu(gg)

        mixed = jnp.concatenate([a_out, b_out], axis=-1) @ w_out[l].astype(f32)
        h_res = h_res + gate[:, None, :] * mixed
    return rmsnorm(h_res, final_norm_w).astype(x.dtype)
```

```python
import functools
import math

import jax
import jax.numpy as jnp
from jax import lax
from jax.experimental import pallas as pl
from jax.experimental.pallas import tpu as pltpu

D_MODEL = 2048
BATCH = 4
SEQ = 2048
TOKENS = BATCH * SEQ
CHUNK = 64
LANES = 128

DIFF_HEADS = 8
DIFF_HEAD_DIM = 128
DIFF_QK_DIM = 64
GLA_HEADS = 4
GLA_K_DIM = 128
GLA_V_DIM = 256
GLA_GATE_RANK = 16
GLA_GATE_TAU = 16.0
ROPE_THETA = 10000.0
NORM_EPS = 1e-6
LAMBDA_INIT = 0.8 - 0.6 * math.exp(-0.3 * 0)

N_MAIN = 7168
N_COLBLK = N_MAIN // LANES
CB_DQ, CB_DK, CB_DV, CB_DG = 0, 8, 16, 24
CB_GQ, CB_GK, CB_GV, CB_GG = 32, 36, 40, 48

VMEM_LIMIT = 48 * 1024 * 1024
NEG_BIG = -1e30

BF16 = jnp.bfloat16
F32 = jnp.float32


def _nt_dot(a, b):
    return lax.dot_general(a, b, (((1,), (1,)), ((), ())), preferred_element_type=F32)


def _tn_dot(a, b):
    return lax.dot_general(a, b, (((0,), (0,)), ((), ())), preferred_element_type=F32)


ADA_TN = 768


def _ada_kernel(c_ref, w_ref, b_ref, o_ref):
    c = c_ref[...]
    c_act = (c * jax.nn.sigmoid(c)).astype(BF16)
    w = w_ref[...].astype(BF16)
    o_ref[...] = jnp.dot(c_act, w, preferred_element_type=F32) + b_ref[...]


def _ada_call(c_pad, w_ada, b_ada):
    n = w_ada.shape[1]
    return pl.pallas_call(
        _ada_kernel,
        grid=(n // ADA_TN,),
        in_specs=[
            pl.BlockSpec((8, D_MODEL), lambda j: (0, 0)),
            pl.BlockSpec((D_MODEL, ADA_TN), lambda j: (0, j)),
            pl.BlockSpec((1, ADA_TN), lambda j: (0, j)),
        ],
        out_specs=pl.BlockSpec((8, ADA_TN), lambda j: (0, j)),
        out_shape=jax.ShapeDtypeStruct((8, n), F32),
        compiler_params=pltpu.CompilerParams(
            dimension_semantics=("arbitrary",), vmem_limit_bytes=VMEM_LIMIT),
        name="ada_mod",
    )(c_pad, w_ada, b_ada)


PROJ_TM = 1024
PROJ_TN = 1024
PROJ_SUB = 256
PROJ_ROPE_TILES = 2


def _rot_half(x):
    lane = lax.broadcasted_iota(jnp.int32, x.shape, 1)
    first = (lane % DIFF_QK_DIM) < (DIFF_QK_DIM // 2)
    return jnp.where(first, pltpu.roll(x, LANES - 32, 1), pltpu.roll(x, 32, 1))


def _proj_kernel(x_ref, shift_ref, scale_ref, nw_ref, w_ref, wg_ref, cos_ref, sin_ref,
                 p_ref, g_ref, hn_ref):
    j = pl.program_id(1)
    n_sub = PROJ_TM // PROJ_SUB
    n_cb = PROJ_TN // LANES

    @pl.when(j == 0)
    def _():
        mult = nw_ref[...] * (1.0 + scale_ref[0])
        shift = shift_ref[0]
        for r in range(n_sub):
            rows = pl.ds(r * PROJ_SUB, PROJ_SUB)
            xs = x_ref[rows, :]
            ms = jnp.mean(xs * xs, axis=-1, keepdims=True)
            hn = xs * lax.rsqrt(ms + NORM_EPS) * mult + shift
            hn_ref[rows, :] = hn.astype(BF16)
        g_ref[...] = jnp.dot(hn_ref[...], wg_ref[...], preferred_element_type=F32)

    @pl.when(j < PROJ_ROPE_TILES)
    def _():
        sc = jnp.where(j == 0, DIFF_QK_DIM ** -0.5, 1.0).astype(F32)
        for r in range(n_sub):
            rows = pl.ds(r * PROJ_SUB, PROJ_SUB)
            acc = jnp.dot(hn_ref[rows, :], w_ref[...], preferred_element_type=F32)
            cos = cos_ref[rows, :] * sc
            sin = sin_ref[rows, :] * sc
            for cb in range(n_cb):
                t = acc[:, cb * LANES:(cb + 1) * LANES]
                p_ref[cb, rows, :] = (t * cos + _rot_half(t) * sin).astype(BF16)

    @pl.when(j >= PROJ_ROPE_TILES)
    def _():
        for r in range(n_sub):
            rows = pl.ds(r * PROJ_SUB, PROJ_SUB)
            acc = jnp.dot(hn_ref[rows, :], w_ref[...], preferred_element_type=F32)
            for cb in range(n_cb):
                p_ref[cb, rows, :] = acc[:, cb * LANES:(cb + 1) * LANES].astype(BF16)


def _proj_call(xf, shift, scale, norm_w, w_main, w_gate, cos, sin):
    m_tiles = TOKENS // PROJ_TM
    n_tiles = N_MAIN // PROJ_TN
    per_seq = SEQ // PROJ_TM
    n_cb = PROJ_TN // LANES
    return pl.pallas_call(
        _proj_kernel,
        grid=(m_tiles, n_tiles),
        in_specs=[
            pl.BlockSpec((PROJ_TM, D_MODEL), lambda i, j: (i, 0)),
            pl.BlockSpec((1, 1, D_MODEL), lambda i, j: (i // per_seq, 0, 0)),
            pl.BlockSpec((1, 1, D_MODEL), lambda i, j: (i // per_seq, 0, 0)),
            pl.BlockSpec((1, D_MODEL), lambda i, j: (0, 0)),
            pl.BlockSpec((D_MODEL, PROJ_TN), lambda i, j: (0, j)),
            pl.BlockSpec((D_MODEL, LANES), lambda i, j: (0, 0)),
            pl.BlockSpec((PROJ_TM, LANES), lambda i, j: (i % per_seq, 0)),
            pl.BlockSpec((PROJ_TM, LANES), lambda i, j: (i % per_seq, 0)),
        ],
        out_specs=[
            pl.BlockSpec((n_cb, PROJ_TM, LANES), lambda i, j: (j, i, 0)),
            pl.BlockSpec((PROJ_TM, LANES), lambda i, j: (i, 0)),
        ],
        out_shape=[
            jax.ShapeDtypeStruct((N_COLBLK, TOKENS, LANES), BF16),
            jax.ShapeDtypeStruct((TOKENS, LANES), F32),
        ],
        scratch_shapes=[pltpu.VMEM((PROJ_TM, D_MODEL), BF16)],
        compiler_params=pltpu.CompilerParams(
            dimension_semantics=("arbitrary", "arbitrary"), vmem_limit_bytes=VMEM_LIMIT),
        name="norm_in_proj",
    )(xf, shift, scale, norm_w, w_main, w_gate, cos, sin)


ATT_T = 256
ATT_NBLK = SEQ // ATT_T


def _attn_kernel(lq1_ref, lk1_ref, lq2_ref, lk2_ref, dnw_ref, q_ref, k_ref, v_ref, dg_ref,
                 o_ref, vt_ref, acc_a_ref, acc_b_ref):
    lam = (jnp.exp(jnp.sum(lq1_ref[...] * lk1_ref[...], keepdims=True))
           - jnp.exp(jnp.sum(lq2_ref[...] * lk2_ref[...], keepdims=True))
           + LAMBDA_INIT)

    for jb in range(ATT_NBLK):
        vb = v_ref[0, jb * ATT_T:(jb + 1) * ATT_T, :].astype(F32)
        vt_ref[jb] = vb.T.astype(BF16)

    lane = lax.broadcasted_iota(jnp.int32, (ATT_T, LANES), 1)
    comp_a = lane < DIFF_QK_DIM
    krow = lax.broadcasted_iota(jnp.int32, (ATT_T, ATT_T), 0) // CHUNK
    qcol = lax.broadcasted_iota(jnp.int32, (ATT_T, ATT_T), 1) // CHUNK
    diag_mask = krow <= qcol

    def kv_step(jb, carry, qa, qb, masked):
        m_a, l_a, m_b, l_b = carry
        kb = k_ref[0, pl.ds(pl.multiple_of(jb * ATT_T, ATT_T), ATT_T), :]
        vt = vt_ref[jb]
        new = []
        for q_c, m, l, acc_ref in ((qa, m_a, l_a, acc_a_ref), (qb, m_b, l_b, acc_b_ref)):
            s = _nt_dot(kb, q_c)
            if masked:
                s = jnp.where(diag_mask, s, NEG_BIG)
            m_new = jnp.maximum(m, jnp.max(s, axis=0, keepdims=True))
            alpha = jnp.exp(m - m_new)
            p = jnp.exp(s - m_new)
            l_new = l * alpha + jnp.sum(p, axis=0, keepdims=True)
            pv = jnp.dot(vt, p.astype(BF16), preferred_element_type=F32)
            acc_ref[...] = acc_ref[...] * alpha + pv
            new += [m_new, l_new]
        return tuple(new)

    for qi in range(ATT_NBLK):
        rows = pl.ds(qi * ATT_T, ATT_T)
        q = q_ref[0, rows, :]
        zero = jnp.zeros_like(q)
        qa = jnp.where(comp_a, q, zero)
        qb = jnp.where(comp_a, zero, q)
        acc_a_ref[...] = jnp.zeros_like(acc_a_ref)
        acc_b_ref[...] = jnp.zeros_like(acc_b_ref)
        m0 = jnp.full((1, ATT_T), NEG_BIG, F32)
        l0 = jnp.zeros((1, ATT_T), F32)
        carry = (m0, l0, m0, l0)
        if qi > 0:
            carry = lax.fori_loop(
                0, qi, functools.partial(kv_step, qa=qa, qb=qb, masked=False), carry)
        m_a, l_a, m_b, l_b = kv_step(qi, carry, qa, qb, True)

        out_t = acc_a_ref[...] * (1.0 / l_a) - lam * (acc_b_ref[...] * (1.0 / l_b))
        out = out_t.T
        ms = jnp.mean(out * out, axis=-1, keepdims=True)
        y = out * lax.rsqrt(ms + NORM_EPS) * dnw_ref[...] * (1.0 - LAMBDA_INIT)
        g = dg_ref[0, rows, :].astype(F32)
        o_ref[0, rows, :] = (y * (g * jax.nn.sigmoid(g))).astype(BF16)


def _attn_call(p, lq1, lk1, lq2, lk2, dnw):
    vec = lambda n: pl.BlockSpec((1, n), lambda b, h: (0, 0))
    slab = lambda cb0: pl.BlockSpec((1, SEQ, LANES), lambda b, h: (cb0 + h, b, 0))
    return pl.pallas_call(
        _attn_kernel,
        grid=(BATCH, DIFF_HEADS),
        in_specs=[vec(DIFF_QK_DIM)] * 4 + [vec(DIFF_HEAD_DIM),
                  slab(CB_DQ), slab(CB_DK), slab(CB_DV), slab(CB_DG)],
        out_specs=pl.BlockSpec((1, SEQ, LANES), lambda b, h: (h, b, 0)),
        out_shape=jax.ShapeDtypeStruct((DIFF_HEADS, TOKENS, LANES), BF16),
        scratch_shapes=[
            pltpu.VMEM((ATT_NBLK, LANES, ATT_T), BF16),
            pltpu.VMEM((DIFF_HEAD_DIM, ATT_T), F32),
            pltpu.VMEM((DIFF_HEAD_DIM, ATT_T), F32),
        ],
        compiler_params=pltpu.CompilerParams(
            dimension_semantics=("arbitrary", "arbitrary"), vmem_limit_bytes=VMEM_LIMIT),
        name="diff_attn",
    )(lq1, lk1, lq2, lk2, dnw, p, p, p, p)


GLA_BLK = 256
GLA_NCHUNK = SEQ // CHUNK


def _split_hi_lo(x):
    hi = x.astype(BF16)
    lo = (x - hi.astype(F32)).astype(BF16)
    return hi, lo


def _gla_kernel(g_ref, w2_ref, b2_ref, gw_ref, q_ref, k_ref, v_ref, gg_ref, o_ref,
                kdec_ref, dec_ref):
    z = jnp.dot(g_ref[...].astype(BF16), w2_ref[...], preferred_element_type=F32) + b2_ref[...]
    log_a = (jnp.minimum(z, 0.0) - jnp.log(1.0 + jnp.exp(-jnp.abs(z)))) * (1.0 / GLA_GATE_TAU)

    r = lax.broadcasted_iota(jnp.int32, (GLA_BLK, GLA_BLK), 0)
    c = lax.broadcasted_iota(jnp.int32, (GLA_BLK, GLA_BLK), 1)
    same = (r // CHUNK) == (c // CHUNK)
    after = jnp.where(same & (c > r), 1.0, 0.0).astype(BF16)
    whole = jnp.where(same, 1.0, 0.0).astype(BF16)
    for blk in range(SEQ // GLA_BLK):
        rows = pl.ds(blk * GLA_BLK, GLA_BLK)
        hi, lo = _split_hi_lo(log_a[blk * GLA_BLK:(blk + 1) * GLA_BLK, :])
        suffix = (jnp.dot(after, hi, preferred_element_type=F32)
                  + jnp.dot(after, lo, preferred_element_type=F32))
        total = (jnp.dot(whole, hi, preferred_element_type=F32)
                 + jnp.dot(whole, lo, preferred_element_type=F32))
        kdec_ref[rows, :] = (k_ref[0, rows, :].astype(F32) * jnp.exp(suffix)).astype(BF16)
        dec_ref[rows, :] = jnp.exp(total)

    state = jnp.zeros((GLA_V_DIM, GLA_K_DIM), F32)
    gw = gw_ref[...]
    for ci in range(GLA_NCHUNK):
        rows = pl.ds(ci * CHUNK, CHUNK)
        v_c = jnp.concatenate([v_ref[0, rows, :], v_ref[1, rows, :]], axis=1)
        kv_t = _tn_dot(v_c, kdec_ref[rows, :])
        state = state * dec_ref[pl.ds(ci * CHUNK, 1), :] + kv_t
        o = _nt_dot(q_ref[0, rows, :], state.astype(BF16)) * (GLA_K_DIM ** -0.5)
        ms = jnp.mean(o * o, axis=-1, keepdims=True)
        y = o * lax.rsqrt(ms + NORM_EPS) * gw
        gate = jnp.concatenate([gg_ref[0, rows, :], gg_ref[1, rows, :]], axis=1).astype(F32)
        y = y * (gate * jax.nn.sigmoid(gate))
        o_ref[0, rows, :] = y[:, :LANES].astype(BF16)
        o_ref[1, rows, :] = y[:, LANES:].astype(BF16)


def _gla_call(p, g, w2_pad, b2, gw):
    one = lambda cb0: pl.BlockSpec((1, SEQ, LANES), lambda b, h: (cb0 + h, b, 0))
    two = lambda cb0: pl.BlockSpec((2, SEQ, LANES), lambda b, h: (cb0 // 2 + h, b, 0))
    return pl.pallas_call(
        _gla_kernel,
        grid=(BATCH, GLA_HEADS),
        in_specs=[
            pl.BlockSpec((SEQ, LANES), lambda b, h: (b, 0)),
            pl.BlockSpec((LANES, GLA_K_DIM), lambda b, h: (0, h)),
            pl.BlockSpec((1, GLA_K_DIM), lambda b, h: (0, h)),
            pl.BlockSpec((1, GLA_V_DIM), lambda b, h: (0, 0)),
            one(CB_GQ), one(CB_GK), two(CB_GV), two(CB_GG),
        ],
        out_specs=pl.BlockSpec((2, SEQ, LANES), lambda b, h: (h, b, 0)),
        out_shape=jax.ShapeDtypeStruct((2 * GLA_HEADS, TOKENS, LANES), BF16),
        scratch_shapes=[
            pltpu.VMEM((SEQ, GLA_K_DIM), BF16),
            pltpu.VMEM((SEQ, GLA_K_DIM), F32),
        ],
        compiler_params=pltpu.CompilerParams(
            dimension_semantics=("arbitrary", "arbitrary"), vmem_limit_bytes=VMEM_LIMIT),
        name="gla_scan",
    )(g, w2_pad, b2, gw, p, p, p, p)


OUT_TM = 512
OUT_SUB = 256


def _out_kernel(a_ref, b_ref, w_ref, x_ref, gate_ref, fw_ref, o_ref):
    gate = gate_ref[0]
    fw = fw_ref[...]
    for r in range(OUT_TM // OUT_SUB):
        rows = pl.ds(r * OUT_SUB, OUT_SUB)
        mix = jnp.concatenate(
            [a_ref[h, rows, :] for h in range(a_ref.shape[0])]
            + [b_ref[h, rows, :] for h in range(b_ref.shape[0])], axis=1)
        y = jnp.dot(mix, w_ref[...], preferred_element_type=F32)
        h_res = x_ref[rows, :] + gate * y
        ms = jnp.mean(h_res * h_res, axis=-1, keepdims=True)
        o_ref[rows, :] = h_res * lax.rsqrt(ms + NORM_EPS) * fw


def _out_call(a, b, w_out, xf, gate, fw):
    per_seq = SEQ // OUT_TM
    return pl.pallas_call(
        _out_kernel,
        grid=(TOKENS // OUT_TM,),
        in_specs=[
            pl.BlockSpec((DIFF_HEADS, OUT_TM, LANES), lambda i: (0, i, 0)),
            pl.BlockSpec((2 * GLA_HEADS, OUT_TM, LANES), lambda i: (0, i, 0)),
            pl.BlockSpec((D_MODEL, D_MODEL), lambda i: (0, 0)),
            pl.BlockSpec((OUT_TM, D_MODEL), lambda i: (i, 0)),
            pl.BlockSpec((1, 1, D_MODEL), lambda i: (i // per_seq, 0, 0)),
            pl.BlockSpec((1, D_MODEL), lambda i: (0, 0)),
        ],
        out_specs=pl.BlockSpec((OUT_TM, D_MODEL), lambda i: (i, 0)),
        out_shape=jax.ShapeDtypeStruct((TOKENS, D_MODEL), F32),
        compiler_params=pltpu.CompilerParams(
            dimension_semantics=("arbitrary",), vmem_limit_bytes=VMEM_LIMIT),
        name="out_proj_norm",
    )(a, b, w_out, xf, gate, fw)


def _rope_tables():
    half = DIFF_QK_DIM // 2
    inv_freq = ROPE_THETA ** (-jnp.arange(0, DIFF_QK_DIM, 2, dtype=F32) / DIFF_QK_DIM)
    ang = jnp.arange(SEQ, dtype=F32)[:, None] * inv_freq[None, :]
    cos, sin = jnp.cos(ang), jnp.sin(ang)
    del half
    return jnp.tile(cos, (1, 4)), jnp.concatenate([-sin, sin, -sin, sin], axis=1)


def kernel(x, c, norm_w, w_ada, b_ada, w_in, lambda_q1, lambda_k1, lambda_q2, lambda_k2,
           diff_norm_w, gla_gate_w2, gla_gate_b, gla_norm_w, w_out, final_norm_w):
    assert x.shape == (BATCH, SEQ, D_MODEL) and w_in.shape[0] == 1
    xf = x.reshape(TOKENS, D_MODEL).astype(F32)

    c_pad = jnp.pad(c.astype(F32), ((0, 8 - BATCH), (0, 0)))
    mod = _ada_call(c_pad, w_ada[0], b_ada[0][None, :])[:BATCH]
    shift = mod[:, :D_MODEL].reshape(BATCH, 1, D_MODEL)
    scale = mod[:, D_MODEL:2 * D_MODEL].reshape(BATCH, 1, D_MODEL)
    gate = mod[:, 2 * D_MODEL:].reshape(BATCH, 1, D_MODEL)

    w_main = w_in[0][:, :N_MAIN].astype(BF16)
    w_gate = jnp.pad(w_in[0][:, N_MAIN:], ((0, 0), (0, LANES - GLA_GATE_RANK))).astype(BF16)
    cos, sin = _rope_tables()
    p, g = _proj_call(xf, shift, scale, norm_w[0][None, :], w_main, w_gate, cos, sin)

    row = lambda v: v[0][None, :].astype(F32)
    a = _attn_call(p, row(lambda_q1), row(lambda_k1), row(lambda_q2), row(lambda_k2),
                   row(diff_norm_w))

    w2_pad = jnp.pad(gla_gate_w2[0], ((0, LANES - GLA_GATE_RANK), (0, 0))).astype(BF16)
    b = _gla_call(p, g, w2_pad, row(gla_gate_b), row(gla_norm_w))

    out = _out_call(a, b, w_out[0].astype(BF16), xf, gate, final_norm_w[None, :].astype(F32))
    return out.reshape(BATCH, SEQ, D_MODEL).astype(x.dtype)
```

```python
import functools
import math

import jax
import jax.numpy as jnp
from jax import lax
from jax.experimental import pallas as pl
from jax.experimental.pallas import tpu as pltpu

D_MODEL = 2048
BATCH = 4
SEQ = 2048
TOKENS = BATCH * SEQ
CHUNK = 64
LANES = 128

DIFF_HEADS = 8
DIFF_HEAD_DIM = 128
DIFF_QK_DIM = 64
GLA_HEADS = 4
GLA_K_DIM = 128
GLA_V_DIM = 256
GLA_GATE_RANK = 16
GLA_GATE_TAU = 16.0
ROPE_THETA = 10000.0
NORM_EPS = 1e-6
LAMBDA_INIT = 0.8 - 0.6 * math.exp(-0.3 * 0)

N_MAIN = 7168
N_COLBLK = N_MAIN // LANES
CB_DQ, CB_DK, CB_DV, CB_DG = 0, 8, 16, 24
CB_GQ, CB_GK, CB_GV, CB_GG = 32, 36, 40, 48

VMEM_LIMIT = 48 * 1024 * 1024
NEG_BIG = -1e30
LOG2_E = math.log2(math.e)

BF16 = jnp.bfloat16
F32 = jnp.float32


def _nt_dot(a, b):
    return lax.dot_general(a, b, (((1,), (1,)), ((), ())), preferred_element_type=F32)


def _tn_dot(a, b):
    return lax.dot_general(a, b, (((0,), (0,)), ((), ())), preferred_element_type=F32)


ADA_TN = 768


def _ada_kernel(c_ref, w_ref, b_ref, o_ref):
    c = c_ref[...]
    c_act = (c * jax.nn.sigmoid(c)).astype(BF16)
    w = w_ref[...].astype(BF16)
    o_ref[...] = jnp.dot(c_act, w, preferred_element_type=F32) + b_ref[...]


def _ada_call(c_pad, w_ada, b_ada):
    n = w_ada.shape[1]
    return pl.pallas_call(
        _ada_kernel,
        grid=(n // ADA_TN,),
        in_specs=[
            pl.BlockSpec((8, D_MODEL), lambda j: (0, 0)),
            pl.BlockSpec((D_MODEL, ADA_TN), lambda j: (0, j)),
            pl.BlockSpec((1, ADA_TN), lambda j: (0, j)),
        ],
        out_specs=pl.BlockSpec((8, ADA_TN), lambda j: (0, j)),
        out_shape=jax.ShapeDtypeStruct((8, n), F32),
        compiler_params=pltpu.CompilerParams(
            dimension_semantics=("arbitrary",), vmem_limit_bytes=VMEM_LIMIT),
        name="ada_mod",
    )(c_pad, w_ada, b_ada)


PROJ_TM = 1024
PROJ_TN = 1024
PROJ_SUB = 256
PROJ_ROPE_TILES = 2


def _rot_half(x):
    lane = lax.broadcasted_iota(jnp.int32, x.shape, 1)
    first = (lane % DIFF_QK_DIM) < (DIFF_QK_DIM // 2)
    return jnp.where(first, pltpu.roll(x, LANES - 32, 1), pltpu.roll(x, 32, 1))


def _proj_kernel(x_ref, shift_ref, scale_ref, nw_ref, w_ref, wg_ref, cos_ref, sin_ref,
                 p_ref, g_ref, hn_ref):
    j = pl.program_id(1)
    n_sub = PROJ_TM // PROJ_SUB
    n_cb = PROJ_TN // LANES

    @pl.when(j == 0)
    def _():
        mult = nw_ref[...] * (1.0 + scale_ref[0])
        shift = shift_ref[0]
        for r in range(n_sub):
            rows = pl.ds(r * PROJ_SUB, PROJ_SUB)
            xs = x_ref[rows, :]
            ms = jnp.mean(xs * xs, axis=-1, keepdims=True)
            hn = xs * lax.rsqrt(ms + NORM_EPS) * mult + shift
            hn_ref[rows, :] = hn.astype(BF16)
        g_ref[...] = jnp.dot(hn_ref[...], wg_ref[...], preferred_element_type=F32)

    @pl.when(j < PROJ_ROPE_TILES)
    def _():
        sc = jnp.where(j == 0, DIFF_QK_DIM ** -0.5 * LOG2_E, 1.0).astype(F32)
        for r in range(n_sub):
            rows = pl.ds(r * PROJ_SUB, PROJ_SUB)
            acc = jnp.dot(hn_ref[rows, :], w_ref[...], preferred_element_type=F32)
            cos = cos_ref[rows, :] * sc
            sin = sin_ref[rows, :] * sc
            for cb in range(n_cb):
                t = acc[:, cb * LANES:(cb + 1) * LANES]
                p_ref[cb, rows, :] = (t * cos + _rot_half(t) * sin).astype(BF16)

    @pl.when(j >= PROJ_ROPE_TILES)
    def _():
        for r in range(n_sub):
            rows = pl.ds(r * PROJ_SUB, PROJ_SUB)
            acc = jnp.dot(hn_ref[rows, :], w_ref[...], preferred_element_type=F32)
            for cb in range(n_cb):
                p_ref[cb, rows, :] = acc[:, cb * LANES:(cb + 1) * LANES].astype(BF16)


def _proj_call(xf, shift, scale, norm_w, w_main, w_gate, cos, sin):
    m_tiles = TOKENS // PROJ_TM
    n_tiles = N_MAIN // PROJ_TN
    per_seq = SEQ // PROJ_TM
    n_cb = PROJ_TN // LANES
    return pl.pallas_call(
        _proj_kernel,
        grid=(m_tiles, n_tiles),
        in_specs=[
            pl.BlockSpec((PROJ_TM, D_MODEL), lambda i, j: (i, 0)),
            pl.BlockSpec((1, 1, D_MODEL), lambda i, j: (i // per_seq, 0, 0)),
            pl.BlockSpec((1, 1, D_MODEL), lambda i, j: (i // per_seq, 0, 0)),
            pl.BlockSpec((1, D_MODEL), lambda i, j: (0, 0)),
            pl.BlockSpec((D_MODEL, PROJ_TN), lambda i, j: (0, j)),
            pl.BlockSpec((D_MODEL, LANES), lambda i, j: (0, 0)),
            pl.BlockSpec((PROJ_TM, LANES), lambda i, j: (i % per_seq, 0)),
            pl.BlockSpec((PROJ_TM, LANES), lambda i, j: (i % per_seq, 0)),
        ],
        out_specs=[
            pl.BlockSpec((n_cb, PROJ_TM, LANES), lambda i, j: (j, i, 0)),
            pl.BlockSpec((PROJ_TM, LANES), lambda i, j: (i, 0)),
        ],
        out_shape=[
            jax.ShapeDtypeStruct((N_COLBLK, TOKENS, LANES), BF16),
            jax.ShapeDtypeStruct((TOKENS, LANES), F32),
        ],
        scratch_shapes=[pltpu.VMEM((PROJ_TM, D_MODEL), BF16)],
        compiler_params=pltpu.CompilerParams(
            dimension_semantics=("arbitrary", "arbitrary"), vmem_limit_bytes=VMEM_LIMIT),
        name="norm_in_proj",
    )(xf, shift, scale, norm_w, w_main, w_gate, cos, sin)


ATT_T = 256
ATT_NBLK = SEQ // ATT_T


def _attn_kernel(lq1_ref, lk1_ref, lq2_ref, lk2_ref, dnw_ref, q_ref, k_ref, v_ref, dg_ref,
                 o_ref, vt_ref):
    lam = (jnp.exp(jnp.sum(lq1_ref[...] * lk1_ref[...], keepdims=True))
           - jnp.exp(jnp.sum(lq2_ref[...] * lk2_ref[...], keepdims=True))
           + LAMBDA_INIT)

    for jb in range(ATT_NBLK):
        cols = pl.ds(jb * ATT_T, ATT_T)
        vb = v_ref[0, jb * ATT_T:(jb + 1) * ATT_T, :].astype(F32)
        vt_ref[:, cols] = vb.T.astype(BF16)

    lane = lax.broadcasted_iota(jnp.int32, (ATT_T, LANES), 1)
    comp_a = lane < DIFF_QK_DIM
    krow = lax.broadcasted_iota(jnp.int32, (ATT_T, ATT_T), 0) // CHUNK
    qcol = lax.broadcasted_iota(jnp.int32, (ATT_T, ATT_T), 1) // CHUNK
    diag_mask = krow <= qcol

    def softmax_pv(q2, qi):
        past = qi * ATT_T
        diag = pl.ds(past, ATT_T)
        s_d = jnp.where(diag_mask2, _nt_dot(k_ref[0, diag, :], q2), NEG_BIG)
        m = jnp.max(s_d, axis=0, keepdims=True)
        if past:
            s_p = _nt_dot(k_ref[0, 0:past, :], q2)
            m = jnp.maximum(m, jnp.max(s_p, axis=0, keepdims=True))
        p_d = jnp.exp2(s_d - m)
        l = jnp.sum(p_d, axis=0, keepdims=True)
        pv = jnp.dot(vt_ref[:, diag], p_d.astype(BF16), preferred_element_type=F32)
        if past:
            p_p = jnp.exp2(s_p - m)
            l = l + jnp.sum(p_p, axis=0, keepdims=True)
            pv = pv + jnp.dot(vt_ref[:, 0:past], p_p.astype(BF16), preferred_element_type=F32)
        return pv, l

    diag_mask2 = jnp.concatenate([diag_mask, diag_mask], axis=1)
    for qi in range(ATT_NBLK):
        rows = pl.ds(qi * ATT_T, ATT_T)
        q = q_ref[0, rows, :]
        zero = jnp.zeros_like(q)
        q2 = jnp.concatenate([jnp.where(comp_a, q, zero), jnp.where(comp_a, zero, q)], axis=0)
        pv, l = softmax_pv(q2, qi)
        pv = pv * (1.0 / l)
        out_t = pv[:, :ATT_T] - lam * pv[:, ATT_T:]
        out = out_t.T
        ms = jnp.mean(out * out, axis=-1, keepdims=True)
        y = out * lax.rsqrt(ms + NORM_EPS) * dnw_ref[...] * (1.0 - LAMBDA_INIT)
        g = dg_ref[0, rows, :].astype(F32)
        o_ref[0, rows, :] = (y * (g * jax.nn.sigmoid(g))).astype(BF16)


def _attn_call(p, lq1, lk1, lq2, lk2, dnw):
    vec = lambda n: pl.BlockSpec((1, n), lambda b, h: (0, 0))
    slab = lambda cb0: pl.BlockSpec((1, SEQ, LANES), lambda b, h: (cb0 + h, b, 0))
    return pl.pallas_call(
        _attn_kernel,
        grid=(BATCH, DIFF_HEADS),
        in_specs=[vec(DIFF_QK_DIM)] * 4 + [vec(DIFF_HEAD_DIM),
                  slab(CB_DQ), slab(CB_DK), slab(CB_DV), slab(CB_DG)],
        out_specs=pl.BlockSpec((1, SEQ, LANES), lambda b, h: (h, b, 0)),
        out_shape=jax.ShapeDtypeStruct((DIFF_HEADS, TOKENS, LANES), BF16),
        scratch_shapes=[pltpu.VMEM((DIFF_HEAD_DIM, SEQ), BF16)],
        compiler_params=pltpu.CompilerParams(
            dimension_semantics=("arbitrary", "arbitrary"), vmem_limit_bytes=VMEM_LIMIT),
        name="diff_attn",
    )(lq1, lk1, lq2, lk2, dnw, p, p, p, p)


GLA_BLK = 256
GLA_NCHUNK = SEQ // CHUNK


def _split_hi_lo(x):
    hi = x.astype(BF16)
    lo = (x - hi.astype(F32)).astype(BF16)
    return hi, lo


def _gla_kernel(g_ref, w2_ref, b2_ref, gw_ref, q_ref, k_ref, v_ref, gg_ref, o_ref,
                kdec_ref, dec_ref):
    z = jnp.dot(g_ref[...].astype(BF16), w2_ref[...], preferred_element_type=F32) + b2_ref[...]
    log_a = (jnp.minimum(z, 0.0) - jnp.log(1.0 + jnp.exp(-jnp.abs(z)))) * (1.0 / GLA_GATE_TAU)

    r = lax.broadcasted_iota(jnp.int32, (GLA_BLK, GLA_BLK), 0)
    c = lax.broadcasted_iota(jnp.int32, (GLA_BLK, GLA_BLK), 1)
    same = (r // CHUNK) == (c // CHUNK)
    after = jnp.where(same & (c > r), 1.0, 0.0).astype(BF16)
    whole = jnp.where(same, 1.0, 0.0).astype(BF16)
    for blk in range(SEQ // GLA_BLK):
        rows = pl.ds(blk * GLA_BLK, GLA_BLK)
        hi, lo = _split_hi_lo(log_a[blk * GLA_BLK:(blk + 1) * GLA_BLK, :])
        suffix = (jnp.dot(after, hi, preferred_element_type=F32)
                  + jnp.dot(after, lo, preferred_element_type=F32))
        total = (jnp.dot(whole, hi, preferred_element_type=F32)
                 + jnp.dot(whole, lo, preferred_element_type=F32))
        kdec_ref[rows, :] = (k_ref[0, rows, :].astype(F32) * jnp.exp(suffix)).astype(BF16)
        dec_ref[rows, :] = jnp.exp(total)

    state = jnp.zeros((GLA_V_DIM, GLA_K_DIM), F32)
    gw = gw_ref[...]
    for ci in range(GLA_NCHUNK):
        rows = pl.ds(ci * CHUNK, CHUNK)
        v_c = jnp.concatenate([v_ref[0, rows, :], v_ref[1, rows, :]], axis=1)
        kv_t = _tn_dot(v_c, kdec_ref[rows, :])
        state = state * dec_ref[pl.ds(ci * CHUNK, 1), :] + kv_t
        o = _nt_dot(q_ref[0, rows, :], state.astype(BF16)) * (GLA_K_DIM ** -0.5)
        ms = jnp.mean(o * o, axis=-1, keepdims=True)
        y = o * lax.rsqrt(ms + NORM_EPS) * gw
        gate = jnp.concatenate([gg_ref[0, rows, :], gg_ref[1, rows, :]], axis=1).astype(F32)
        y = y * (gate * jax.nn.sigmoid(gate))
        o_ref[0, rows, :] = y[:, :LANES].astype(BF16)
        o_ref[1, rows, :] = y[:, LANES:].astype(BF16)


def _gla_call(p, g, w2_pad, b2, gw):
    one = lambda cb0: pl.BlockSpec((1, SEQ, LANES), lambda b, h: (cb0 + h, b, 0))
    two = lambda cb0: pl.BlockSpec((2, SEQ, LANES), lambda b, h: (cb0 // 2 + h, b, 0))
    return pl.pallas_call(
        _gla_kernel,
        grid=(BATCH, GLA_HEADS),
        in_specs=[
            pl.BlockSpec((SEQ, LANES), lambda b, h: (b, 0)),
            pl.BlockSpec((LANES, GLA_K_DIM), lambda b, h: (0, h)),
            pl.BlockSpec((1, GLA_K_DIM), lambda b, h: (0, h)),
            pl.BlockSpec((1, GLA_V_DIM), lambda b, h: (0, 0)),
            one(CB_GQ), one(CB_GK), two(CB_GV), two(CB_GG),
        ],
        out_specs=pl.BlockSpec((2, SEQ, LANES), lambda b, h: (h, b, 0)),
        out_shape=jax.ShapeDtypeStruct((2 * GLA_HEADS, TOKENS, LANES), BF16),
        scratch_shapes=[
            pltpu.VMEM((SEQ, GLA_K_DIM), BF16),
            pltpu.VMEM((SEQ, GLA_K_DIM), F32),
        ],
        compiler_params=pltpu.CompilerParams(
            dimension_semantics=("arbitrary", "arbitrary"), vmem_limit_bytes=VMEM_LIMIT),
        name="gla_scan",
    )(g, w2_pad, b2, gw, p, p, p, p)


OUT_TM = 512
OUT_SUB = 256


def _out_kernel(a_ref, b_ref, w_ref, x_ref, gate_ref, fw_ref, o_ref):
    gate = gate_ref[0]
    fw = fw_ref[...]
    for r in range(OUT_TM // OUT_SUB):
        rows = pl.ds(r * OUT_SUB, OUT_SUB)
        mix = jnp.concatenate(
            [a_ref[h, rows, :] for h in range(a_ref.shape[0])]
            + [b_ref[h, rows, :] for h in range(b_ref.shape[0])], axis=1)
        y = jnp.dot(mix, w_ref[...], preferred_element_type=F32)
        h_res = x_ref[rows, :] + gate * y
        ms = jnp.mean(h_res * h_res, axis=-1, keepdims=True)
        o_ref[rows, :] = h_res * lax.rsqrt(ms + NORM_EPS) * fw


def _out_call(a, b, w_out, xf, gate, fw):
    per_seq = SEQ // OUT_TM
    return pl.pallas_call(
        _out_kernel,
        grid=(TOKENS // OUT_TM,),
        in_specs=[
            pl.BlockSpec((DIFF_HEADS, OUT_TM, LANES), lambda i: (0, i, 0)),
            pl.BlockSpec((2 * GLA_HEADS, OUT_TM, LANES), lambda i: (0, i, 0)),
            pl.BlockSpec((D_MODEL, D_MODEL), lambda i: (0, 0)),
            pl.BlockSpec((OUT_TM, D_MODEL), lambda i: (i, 0)),
            pl.BlockSpec((1, 1, D_MODEL), lambda i: (i // per_seq, 0, 0)),
            pl.BlockSpec((1, D_MODEL), lambda i: (0, 0)),
        ],
        out_specs=pl.BlockSpec((OUT_TM, D_MODEL), lambda i: (i, 0)),
        out_shape=jax.ShapeDtypeStruct((TOKENS, D_MODEL), F32),
        compiler_params=pltpu.CompilerParams(
            dimension_semantics=("arbitrary",), vmem_limit_bytes=VMEM_LIMIT),
        name="out_proj_norm",
    )(a, b, w_out, xf, gate, fw)


def _rope_tables():
    half = DIFF_QK_DIM // 2
    inv_freq = ROPE_THETA ** (-jnp.arange(0, DIFF_QK_DIM, 2, dtype=F32) / DIFF_QK_DIM)
    ang = jnp.arange(SEQ, dtype=F32)[:, None] * inv_freq[None, :]
    cos, sin = jnp.cos(ang), jnp.sin(ang)
    del half
    return jnp.tile(cos, (1, 4)), jnp.concatenate([-sin, sin, -sin, sin], axis=1)


def kernel(x, c, norm_w, w_ada, b_ada, w_in, lambda_q1, lambda_k1, lambda_q2, lambda_k2,
           diff_norm_w, gla_gate_w2, gla_gate_b, gla_norm_w, w_out, final_norm_w):
    assert x.shape == (BATCH, SEQ, D_MODEL) and w_in.shape[0] == 1
    xf = x.reshape(TOKENS, D_MODEL).astype(F32)

    c_pad = jnp.pad(c.astype(F32), ((0, 8 - BATCH), (0, 0)))
    mod = _ada_call(c_pad, w_ada[0], b_ada[0][None, :])[:BATCH]
    shift = mod[:, :D_MODEL].reshape(BATCH, 1, D_MODEL)
    scale = mod[:, D_MODEL:2 * D_MODEL].reshape(BATCH, 1, D_MODEL)
    gate = mod[:, 2 * D_MODEL:].reshape(BATCH, 1, D_MODEL)

    w_main = w_in[0][:, :N_MAIN].astype(BF16)
    w_gate = jnp.pad(w_in[0][:, N_MAIN:], ((0, 0), (0, LANES - GLA_GATE_RANK))).astype(BF16)
    cos, sin = _rope_tables()
    p, g = _proj_call(xf, shift, scale, norm_w[0][None, :], w_main, w_gate, cos, sin)

    row = lambda v: v[0][None, :].astype(F32)
    a = _attn_call(p, row(lambda_q1), row(lambda_k1), row(lambda_q2), row(lambda_k2),
                   row(diff_norm_w))

    w2_pad = jnp.pad(gla_gate_w2[0], ((0, LANES - GLA_GATE_RANK), (0, 0))).astype(BF16)
    b = _gla_call(p, g, w2_pad, row(gla_gate_b), row(gla_norm_w))

    out = _out_call(a, b, w_out[0].astype(BF16), xf, gate, final_norm_w[None, :].astype(F32))
    return out.reshape(BATCH, SEQ, D_MODEL).astype(x.dtype)
```

```python
import functools
import math

import jax
import jax.numpy as jnp
from jax import lax
from jax.experimental import pallas as pl
from jax.experimental.pallas import tpu as pltpu

D_MODEL = 2048
BATCH = 4
SEQ = 2048
TOKENS = BATCH * SEQ
CHUNK = 64
LANES = 128

DIFF_HEADS = 8
DIFF_HEAD_DIM = 128
DIFF_QK_DIM = 64
GLA_HEADS = 4
GLA_K_DIM = 128
GLA_V_DIM = 256
GLA_GATE_RANK = 16
GLA_GATE_TAU = 16.0
ROPE_THETA = 10000.0
NORM_EPS = 1e-6
LAMBDA_INIT = 0.8 - 0.6 * math.exp(-0.3 * 0)

N_MAIN = 7168
N_COLBLK = N_MAIN // LANES
CB_DQ, CB_DK, CB_DV, CB_DG = 0, 8, 16, 24
CB_GQ, CB_GK, CB_GV, CB_GG = 32, 36, 40, 48

VMEM_LIMIT = 48 * 1024 * 1024
NEG_BIG = -1e30
LOG2_E = math.log2(math.e)

BF16 = jnp.bfloat16
F32 = jnp.float32


def _nt_dot(a, b):
    return lax.dot_general(a, b, (((1,), (1,)), ((), ())), preferred_element_type=F32)


def _tn_dot(a, b):
    return lax.dot_general(a, b, (((0,), (0,)), ((), ())), preferred_element_type=F32)


ADA_TN = 768


def _ada_kernel(c_ref, w_ref, b_ref, o_ref):
    c = c_ref[...]
    c_act = (c * jax.nn.sigmoid(c)).astype(BF16)
    w = w_ref[...].astype(BF16)
    o_ref[...] = jnp.dot(c_act, w, preferred_element_type=F32) + b_ref[...]


def _ada_call(c_pad, w_ada, b_ada):
    n = w_ada.shape[1]
    return pl.pallas_call(
        _ada_kernel,
        grid=(n // ADA_TN,),
        in_specs=[
            pl.BlockSpec((8, D_MODEL), lambda j: (0, 0)),
            pl.BlockSpec((D_MODEL, ADA_TN), lambda j: (0, j)),
            pl.BlockSpec((1, ADA_TN), lambda j: (0, j)),
        ],
        out_specs=pl.BlockSpec((8, ADA_TN), lambda j: (0, j)),
        out_shape=jax.ShapeDtypeStruct((8, n), F32),
        compiler_params=pltpu.CompilerParams(
            dimension_semantics=("arbitrary",), vmem_limit_bytes=VMEM_LIMIT),
        name="ada_mod",
    )(c_pad, w_ada, b_ada)


PROJ_TM = 1024
PROJ_TN = 1024
PROJ_SUB = 256
PROJ_ROPE_TILES = 2


def _rot_half(x):
    lane = lax.broadcasted_iota(jnp.int32, x.shape, 1)
    first = (lane % DIFF_QK_DIM) < (DIFF_QK_DIM // 2)
    return jnp.where(first, pltpu.roll(x, LANES - 32, 1), pltpu.roll(x, 32, 1))


def _proj_kernel(x_ref, shift_ref, scale_ref, nw_ref, w_ref, wg_ref, cos_ref, sin_ref,
                 p_ref, g_ref, hn_ref):
    j = pl.program_id(1)
    n_sub = PROJ_TM // PROJ_SUB
    n_cb = PROJ_TN // LANES

    @pl.when(j == 0)
    def _():
        mult = nw_ref[...] * (1.0 + scale_ref[0])
        shift = shift_ref[0]
        for r in range(n_sub):
            rows = pl.ds(r * PROJ_SUB, PROJ_SUB)
            xs = x_ref[rows, :]
            ms = jnp.mean(xs * xs, axis=-1, keepdims=True)
            hn = xs * lax.rsqrt(ms + NORM_EPS) * mult + shift
            hn_ref[rows, :] = hn.astype(BF16)
        g_ref[...] = jnp.dot(hn_ref[...], wg_ref[...], preferred_element_type=F32)

    @pl.when(j < PROJ_ROPE_TILES)
    def _():
        sc = jnp.where(j == 0, DIFF_QK_DIM ** -0.5 * LOG2_E, 1.0).astype(F32)
        for r in range(n_sub):
            rows = pl.ds(r * PROJ_SUB, PROJ_SUB)
            acc = jnp.dot(hn_ref[rows, :], w_ref[...], preferred_element_type=F32)
            cos = cos_ref[rows, :] * sc
            sin = sin_ref[rows, :] * sc
            for cb in range(n_cb):
                t = acc[:, cb * LANES:(cb + 1) * LANES]
                p_ref[cb, rows, :] = (t * cos + _rot_half(t) * sin).astype(BF16)

    @pl.when(j >= PROJ_ROPE_TILES)
    def _():
        for r in range(n_sub):
            rows = pl.ds(r * PROJ_SUB, PROJ_SUB)
            acc = jnp.dot(hn_ref[rows, :], w_ref[...], preferred_element_type=F32)
            for cb in range(n_cb):
                p_ref[cb, rows, :] = acc[:, cb * LANES:(cb + 1) * LANES].astype(BF16)


def _proj_call(xf, shift, scale, norm_w, w_main, w_gate, cos, sin):
    m_tiles = TOKENS // PROJ_TM
    n_tiles = N_MAIN // PROJ_TN
    per_seq = SEQ // PROJ_TM
    n_cb = PROJ_TN // LANES
    return pl.pallas_call(
        _proj_kernel,
        grid=(m_tiles, n_tiles),
        in_specs=[
            pl.BlockSpec((PROJ_TM, D_MODEL), lambda i, j: (i, 0)),
            pl.BlockSpec((1, 1, D_MODEL), lambda i, j: (i // per_seq, 0, 0)),
            pl.BlockSpec((1, 1, D_MODEL), lambda i, j: (i // per_seq, 0, 0)),
            pl.BlockSpec((1, D_MODEL), lambda i, j: (0, 0)),
            pl.BlockSpec((D_MODEL, PROJ_TN), lambda i, j: (0, j)),
            pl.BlockSpec((D_MODEL, LANES), lambda i, j: (0, 0)),
            pl.BlockSpec((PROJ_TM, LANES), lambda i, j: (i % per_seq, 0)),
            pl.BlockSpec((PROJ_TM, LANES), lambda i, j: (i % per_seq, 0)),
        ],
        out_specs=[
            pl.BlockSpec((n_cb, PROJ_TM, LANES), lambda i, j: (j, i, 0)),
            pl.BlockSpec((PROJ_TM, LANES), lambda i, j: (i, 0)),
        ],
        out_shape=[
            jax.ShapeDtypeStruct((N_COLBLK, TOKENS, LANES), BF16),
            jax.ShapeDtypeStruct((TOKENS, LANES), F32),
        ],
        scratch_shapes=[pltpu.VMEM((PROJ_TM, D_MODEL), BF16)],
        compiler_params=pltpu.CompilerParams(
            dimension_semantics=("arbitrary", "arbitrary"), vmem_limit_bytes=VMEM_LIMIT),
        name="norm_in_proj",
    )(xf, shift, scale, norm_w, w_main, w_gate, cos, sin)


ATT_T = 256
ATT_NBLK = SEQ // ATT_T


def _attn_kernel(lq1_ref, lk1_ref, lq2_ref, lk2_ref, dnw_ref, q_ref, k_ref, v_ref, dg_ref,
                 o_ref, vt_ref, s_ref):
    lam = (jnp.exp(jnp.sum(lq1_ref[...] * lk1_ref[...], keepdims=True))
           - jnp.exp(jnp.sum(lq2_ref[...] * lk2_ref[...], keepdims=True))
           + LAMBDA_INIT)

    for jb in range(ATT_NBLK):
        cols = pl.ds(jb * ATT_T, ATT_T)
        vb = v_ref[0, jb * ATT_T:(jb + 1) * ATT_T, :].astype(F32)
        vt_ref[:, cols] = vb.T.astype(BF16)

    lane = lax.broadcasted_iota(jnp.int32, (ATT_T, LANES), 1)
    comp_a = lane < DIFF_QK_DIM
    krow = lax.broadcasted_iota(jnp.int32, (ATT_T, ATT_T), 0) // CHUNK
    qcol = lax.broadcasted_iota(jnp.int32, (ATT_T, ATT_T), 1) // CHUNK
    diag_mask = krow <= qcol

    diag_mask2 = jnp.concatenate([diag_mask, diag_mask], axis=1)

    def score_chunks(qi):
        slot = qi % 2
        q = q_ref[0, pl.ds(qi * ATT_T, ATT_T), :]
        zero = jnp.zeros_like(q)
        q2 = jnp.concatenate([jnp.where(comp_a, q, zero), jnp.where(comp_a, zero, q)], axis=0)
        state = {"m": None}

        def chunk(j):
            keys = pl.ds(j * ATT_T, ATT_T)
            s = _nt_dot(k_ref[0, keys, :], q2)
            if j == qi:
                s = jnp.where(diag_mask2, s, NEG_BIG)
            s_ref[slot, keys, :] = s
            mj = jnp.max(s, axis=0, keepdims=True)
            state["m"] = mj if state["m"] is None else jnp.maximum(state["m"], mj)

        return [functools.partial(chunk, j) for j in range(qi + 1)], state

    def value_chunks(qi, m):
        slot = qi % 2
        state = {"l": None, "pv": None}

        def chunk(j):
            keys = pl.ds(j * ATT_T, ATT_T)
            p = jnp.exp2(s_ref[slot, keys, :] - m)
            lj = jnp.sum(p, axis=0, keepdims=True)
            pvj = jnp.dot(vt_ref[:, keys], p.astype(BF16), preferred_element_type=F32)
            state["l"] = lj if state["l"] is None else state["l"] + lj
            state["pv"] = pvj if state["pv"] is None else state["pv"] + pvj

        return [functools.partial(chunk, j) for j in range(qi + 1)], state

    def finish(qi, pv, l):
        rows = pl.ds(qi * ATT_T, ATT_T)
        pv = pv * (1.0 / l)
        out_t = pv[:, :ATT_T] - lam * pv[:, ATT_T:]
        out = out_t.T
        ms = jnp.mean(out * out, axis=-1, keepdims=True)
        y = out * lax.rsqrt(ms + NORM_EPS) * dnw_ref[...] * (1.0 - LAMBDA_INIT)
        g = dg_ref[0, rows, :].astype(F32)
        o_ref[0, rows, :] = (y * (g * jax.nn.sigmoid(g))).astype(BF16)

    thunks, sc_state = score_chunks(0)
    for t in thunks:
        t()
    for qi in range(ATT_NBLK):
        v_thunks, v_state = value_chunks(qi, sc_state["m"])
        if qi + 1 < ATT_NBLK:
            s_thunks, sc_state = score_chunks(qi + 1)
        else:
            s_thunks = []
        for step in range(max(len(s_thunks), len(v_thunks))):
            if step < len(s_thunks):
                s_thunks[step]()
            if step < len(v_thunks):
                v_thunks[step]()
        finish(qi, v_state["pv"], v_state["l"])


def _attn_call(p, lq1, lk1, lq2, lk2, dnw):
    vec = lambda n: pl.BlockSpec((1, n), lambda b, h: (0, 0))
    slab = lambda cb0: pl.BlockSpec((1, SEQ, LANES), lambda b, h: (cb0 + h, b, 0))
    return pl.pallas_call(
        _attn_kernel,
        grid=(BATCH, DIFF_HEADS),
        in_specs=[vec(DIFF_QK_DIM)] * 4 + [vec(DIFF_HEAD_DIM),
                  slab(CB_DQ), slab(CB_DK), slab(CB_DV), slab(CB_DG)],
        out_specs=pl.BlockSpec((1, SEQ, LANES), lambda b, h: (h, b, 0)),
        out_shape=jax.ShapeDtypeStruct((DIFF_HEADS, TOKENS, LANES), BF16),
        scratch_shapes=[
            pltpu.VMEM((DIFF_HEAD_DIM, SEQ), BF16),
            pltpu.VMEM((2, SEQ, 2 * ATT_T), F32),
        ],
        compiler_params=pltpu.CompilerParams(
            dimension_semantics=("arbitrary", "arbitrary"), vmem_limit_bytes=VMEM_LIMIT),
        name="diff_attn",
    )(lq1, lk1, lq2, lk2, dnw, p, p, p, p)


GLA_BLK = 256
GLA_NCHUNK = SEQ // CHUNK


def _split_hi_lo(x):
    hi = x.astype(BF16)
    lo = (x - hi.astype(F32)).astype(BF16)
    return hi, lo


def _gla_kernel(g_ref, w2_ref, b2_ref, gw_ref, q_ref, k_ref, v_ref, gg_ref, o_ref,
                kdec_ref, tot_ref, kv_ref, st_ref):
    z = jnp.dot(g_ref[...].astype(BF16), w2_ref[...], preferred_element_type=F32) + b2_ref[...]
    log_a = (jnp.minimum(z, 0.0) - jnp.log(1.0 + jnp.exp(-jnp.abs(z)))) * (1.0 / GLA_GATE_TAU)

    r = lax.broadcasted_iota(jnp.int32, (GLA_BLK, GLA_BLK), 0)
    c = lax.broadcasted_iota(jnp.int32, (GLA_BLK, GLA_BLK), 1)
    after = jnp.where(((r // CHUNK) == (c // CHUNK)) & (c > r), 1.0, 0.0).astype(BF16)
    for blk in range(SEQ // GLA_BLK):
        rows = pl.ds(blk * GLA_BLK, GLA_BLK)
        la = log_a[blk * GLA_BLK:(blk + 1) * GLA_BLK, :]
        hi, lo = _split_hi_lo(la)
        both = jnp.dot(after, jnp.concatenate([hi, lo], axis=1), preferred_element_type=F32)
        suffix = both[:, :GLA_K_DIM] + both[:, GLA_K_DIM:]
        kdec_ref[rows, :] = (k_ref[0, rows, :].astype(F32) * jnp.exp(suffix)).astype(BF16)
        tot_ref[rows, :] = suffix + la

    dec_rows = jnp.exp(tot_ref[pl.ds(0, GLA_NCHUNK, stride=CHUNK), :])
    dec_cols = jnp.concatenate(
        [dec_rows, jnp.zeros((GLA_K_DIM - GLA_NCHUNK, GLA_K_DIM), F32)], axis=0).T

    for ci in range(GLA_NCHUNK):
        rows = pl.ds(ci * CHUNK, CHUNK)
        v_c = jnp.concatenate([v_ref[0, rows, :], v_ref[1, rows, :]], axis=1)
        kv_ref[ci] = _tn_dot(kdec_ref[rows, :], v_c)

    state = jnp.zeros((GLA_K_DIM, GLA_V_DIM), F32)
    for ci in range(GLA_NCHUNK):
        state = state * dec_cols[:, ci:ci + 1] + kv_ref[ci]
        st_ref[ci] = state.astype(BF16)

    gw = gw_ref[...]
    for ci in range(GLA_NCHUNK):
        rows = pl.ds(ci * CHUNK, CHUNK)
        o = jnp.dot(q_ref[0, rows, :], st_ref[ci], preferred_element_type=F32)
        o = o * (GLA_K_DIM ** -0.5)
        ms = jnp.mean(o * o, axis=-1, keepdims=True)
        y = o * lax.rsqrt(ms + NORM_EPS) * gw
        gate = jnp.concatenate([gg_ref[0, rows, :], gg_ref[1, rows, :]], axis=1).astype(F32)
        y = y * (gate * jax.nn.sigmoid(gate))
        o_ref[0, rows, :] = y[:, :LANES].astype(BF16)
        o_ref[1, rows, :] = y[:, LANES:].astype(BF16)


def _gla_call(p, g, w2_pad, b2, gw):
    one = lambda cb0: pl.BlockSpec((1, SEQ, LANES), lambda b, h: (cb0 + h, b, 0))
    two = lambda cb0: pl.BlockSpec((2, SEQ, LANES), lambda b, h: (cb0 // 2 + h, b, 0))
    return pl.pallas_call(
        _gla_kernel,
        grid=(BATCH, GLA_HEADS),
        in_specs=[
            pl.BlockSpec((SEQ, LANES), lambda b, h: (b, 0)),
            pl.BlockSpec((LANES, GLA_K_DIM), lambda b, h: (0, h)),
            pl.BlockSpec((1, GLA_K_DIM), lambda b, h: (0, h)),
            pl.BlockSpec((1, GLA_V_DIM), lambda b, h: (0, 0)),
            one(CB_GQ), one(CB_GK), two(CB_GV), two(CB_GG),
        ],
        out_specs=pl.BlockSpec((2, SEQ, LANES), lambda b, h: (h, b, 0)),
        out_shape=jax.ShapeDtypeStruct((2 * GLA_HEADS, TOKENS, LANES), BF16),
        scratch_shapes=[
            pltpu.VMEM((SEQ, GLA_K_DIM), BF16),
            pltpu.VMEM((SEQ, GLA_K_DIM), F32),
            pltpu.VMEM((GLA_NCHUNK, GLA_K_DIM, GLA_V_DIM), F32),
            pltpu.VMEM((GLA_NCHUNK, GLA_K_DIM, GLA_V_DIM), BF16),
        ],
        compiler_params=pltpu.CompilerParams(
            dimension_semantics=("arbitrary", "arbitrary"), vmem_limit_bytes=VMEM_LIMIT),
        name="gla_scan",
    )(g, w2_pad, b2, gw, p, p, p, p)


OUT_TM = 512
OUT_SUB = 256


def _out_kernel(a_ref, b_ref, w_ref, x_ref, gate_ref, fw_ref, o_ref):
    gate = gate_ref[0]
    fw = fw_ref[...]
    for r in range(OUT_TM // OUT_SUB):
        rows = pl.ds(r * OUT_SUB, OUT_SUB)
        mix = jnp.concatenate(
            [a_ref[h, rows, :] for h in range(a_ref.shape[0])]
            + [b_ref[h, rows, :] for h in range(b_ref.shape[0])], axis=1)
        y = jnp.dot(mix, w_ref[...], preferred_element_type=F32)
        h_res = x_ref[rows, :] + gate * y
        ms = jnp.mean(h_res * h_res, axis=-1, keepdims=True)
        o_ref[rows, :] = h_res * lax.rsqrt(ms + NORM_EPS) * fw


def _out_call(a, b, w_out, xf, gate, fw):
    per_seq = SEQ // OUT_TM
    return pl.pallas_call(
        _out_kernel,
        grid=(TOKENS // OUT_TM,),
        in_specs=[
            pl.BlockSpec((DIFF_HEADS, OUT_TM, LANES), lambda i: (0, i, 0)),
            pl.BlockSpec((2 * GLA_HEADS, OUT_TM, LANES), lambda i: (0, i, 0)),
            pl.BlockSpec((D_MODEL, D_MODEL), lambda i: (0, 0)),
            pl.BlockSpec((OUT_TM, D_MODEL), lambda i: (i, 0)),
            pl.BlockSpec((1, 1, D_MODEL), lambda i: (i // per_seq, 0, 0)),
            pl.BlockSpec((1, D_MODEL), lambda i: (0, 0)),
        ],
        out_specs=pl.BlockSpec((OUT_TM, D_MODEL), lambda i: (i, 0)),
        out_shape=jax.ShapeDtypeStruct((TOKENS, D_MODEL), F32),
        compiler_params=pltpu.CompilerParams(
            dimension_semantics=("arbitrary",), vmem_limit_bytes=VMEM_LIMIT),
        name="out_proj_norm",
    )(a, b, w_out, xf, gate, fw)


def _rope_tables():
    half = DIFF_QK_DIM // 2
    inv_freq = ROPE_THETA ** (-jnp.arange(0, DIFF_QK_DIM, 2, dtype=F32) / DIFF_QK_DIM)
    ang = jnp.arange(SEQ, dtype=F32)[:, None] * inv_freq[None, :]
    cos, sin = jnp.cos(ang), jnp.sin(ang)
    del half
    return jnp.tile(cos, (1, 4)), jnp.concatenate([-sin, sin, -sin, sin], axis=1)


def kernel(x, c, norm_w, w_ada, b_ada, w_in, lambda_q1, lambda_k1, lambda_q2, lambda_k2,
           diff_norm_w, gla_gate_w2, gla_gate_b, gla_norm_w, w_out, final_norm_w):
    assert x.shape == (BATCH, SEQ, D_MODEL) and w_in.shape[0] == 1
    xf = x.reshape(TOKENS, D_MODEL).astype(F32)

    c_pad = jnp.pad(c.astype(F32), ((0, 8 - BATCH), (0, 0)))
    mod = _ada_call(c_pad, w_ada[0], b_ada[0][None, :])[:BATCH]
    shift = mod[:, :D_MODEL].reshape(BATCH, 1, D_MODEL)
    scale = mod[:, D_MODEL:2 * D_MODEL].reshape(BATCH, 1, D_MODEL)
    gate = mod[:, 2 * D_MODEL:].reshape(BATCH, 1, D_MODEL)

    w_main = w_in[0][:, :N_MAIN].astype(BF16)
    w_gate = jnp.pad(w_in[0][:, N_MAIN:], ((0, 0), (0, LANES - GLA_GATE_RANK))).astype(BF16)
    cos, sin = _rope_tables()
    p, g = _proj_call(xf, shift, scale, norm_w[0][None, :], w_main, w_gate, cos, sin)

    row = lambda v: v[0][None, :].astype(F32)
    a = _attn_call(p, row(lambda_q1), row(lambda_k1), row(lambda_q2), row(lambda_k2),
                   row(diff_norm_w))

    w2_pad = jnp.pad(gla_gate_w2[0], ((0, LANES - GLA_GATE_RANK), (0, 0))).astype(BF16)
    b = _gla_call(p, g, w2_pad, row(gla_gate_b), row(gla_norm_w))

    out = _out_call(a, b, w_out[0].astype(BF16), xf, gate, final_norm_w[None, :].astype(F32))
    return out.reshape(BATCH, SEQ, D_MODEL).astype(x.dtype)
```

```python
import functools
import math

import jax
import jax.numpy as jnp
from jax import lax
from jax.experimental import pallas as pl
from jax.experimental.pallas import tpu as pltpu

D_MODEL = 2048
BATCH = 4
SEQ = 2048
TOKENS = BATCH * SEQ
CHUNK = 64
LANES = 128

DIFF_HEADS = 8
DIFF_HEAD_DIM = 128
DIFF_QK_DIM = 64
GLA_HEADS = 4
GLA_K_DIM = 128
GLA_V_DIM = 256
GLA_GATE_RANK = 16
GLA_GATE_TAU = 16.0
ROPE_THETA = 10000.0
NORM_EPS = 1e-6
LAMBDA_INIT = 0.8 - 0.6 * math.exp(-0.3 * 0)

N_MAIN = 7168
N_COLBLK = N_MAIN // LANES
CB_DQ, CB_DK, CB_DV, CB_DG = 0, 8, 16, 24
CB_GQ, CB_GK, CB_GV, CB_GG = 32, 36, 40, 48

VMEM_LIMIT = 48 * 1024 * 1024
NEG_BIG = -1e30
LOG2_E = math.log2(math.e)

BF16 = jnp.bfloat16
F32 = jnp.float32


def _nt_dot(a, b):
    return lax.dot_general(a, b, (((1,), (1,)), ((), ())), preferred_element_type=F32)


def _tn_dot(a, b):
    return lax.dot_general(a, b, (((0,), (0,)), ((), ())), preferred_element_type=F32)


ADA_TN = 768


def _ada_kernel(c_ref, w_ref, b_ref, o_ref):
    c = c_ref[...]
    c_act = (c * jax.nn.sigmoid(c)).astype(BF16)
    w = w_ref[...].astype(BF16)
    o_ref[...] = jnp.dot(c_act, w, preferred_element_type=F32) + b_ref[...]


def _ada_call(c_pad, w_ada, b_ada):
    n = w_ada.shape[1]
    return pl.pallas_call(
        _ada_kernel,
        grid=(n // ADA_TN,),
        in_specs=[
            pl.BlockSpec((8, D_MODEL), lambda j: (0, 0)),
            pl.BlockSpec((D_MODEL, ADA_TN), lambda j: (0, j)),
            pl.BlockSpec((1, ADA_TN), lambda j: (0, j)),
        ],
        out_specs=pl.BlockSpec((8, ADA_TN), lambda j: (0, j)),
        out_shape=jax.ShapeDtypeStruct((8, n), F32),
        compiler_params=pltpu.CompilerParams(
            dimension_semantics=("arbitrary",), vmem_limit_bytes=VMEM_LIMIT),
        name="ada_mod",
    )(c_pad, w_ada, b_ada)


PROJ_TM = 1024
PROJ_TN = 1024
PROJ_SUB = 256
PROJ_ROPE_TILES = 2
PROJ_VMEM_LIMIT = (2 * PROJ_TM * D_MODEL * 4 + 2 * D_MODEL * PROJ_TN * 4
                   + D_MODEL * PROJ_TN * 2 + PROJ_TM * D_MODEL * 2
                   + 2 * PROJ_TM * PROJ_TN * 2 + 12 * 1024 * 1024)


def _rot_half(x):
    lane = lax.broadcasted_iota(jnp.int32, x.shape, 1)
    first = (lane % DIFF_QK_DIM) < (DIFF_QK_DIM // 2)
    return jnp.where(first, pltpu.roll(x, LANES - 32, 1), pltpu.roll(x, 32, 1))


def _proj_kernel(x_ref, shift_ref, scale_ref, nw_ref, w_ref, wg_ref, cos_ref, sin_ref,
                 p_ref, g_ref, hn_ref, wb_ref):
    j = pl.program_id(1)
    n_sub = PROJ_TM // PROJ_SUB
    n_cb = PROJ_TN // LANES

    def norm_rows(rows):
        mult = nw_ref[...] * (1.0 + scale_ref[0])
        xs = x_ref[rows, :]
        ms = jnp.mean(xs * xs, axis=-1, keepdims=True)
        hn_ref[rows, :] = (xs * lax.rsqrt(ms + NORM_EPS) * mult + shift_ref[0]).astype(BF16)

    def store_rope(rows, acc, sc):
        cos = cos_ref[rows, :] * sc
        sin = sin_ref[rows, :] * sc
        for cb in range(n_cb):
            t = acc[:, cb * LANES:(cb + 1) * LANES]
            p_ref[cb, rows, :] = (t * cos + _rot_half(t) * sin).astype(BF16)

    @pl.when(j == 0)
    def _():
        wb_ref[...] = w_ref[...].astype(BF16)
        for r in range(n_sub):
            rows = pl.ds(r * PROJ_SUB, PROJ_SUB)
            norm_rows(rows)
            acc = jnp.dot(hn_ref[rows, :], wb_ref[...], preferred_element_type=F32)
            store_rope(rows, acc, DIFF_QK_DIM ** -0.5 * LOG2_E)
        g_ref[...] = jnp.dot(hn_ref[...], wg_ref[...], preferred_element_type=F32)

    @pl.when(j == 1)
    def _():
        wb_ref[...] = w_ref[...].astype(BF16)
        for r in range(n_sub):
            rows = pl.ds(r * PROJ_SUB, PROJ_SUB)
            acc = jnp.dot(hn_ref[rows, :], wb_ref[...], preferred_element_type=F32)
            store_rope(rows, acc, 1.0)

    @pl.when(j >= PROJ_ROPE_TILES)
    def _():
        wb_ref[...] = w_ref[...].astype(BF16)
        for r in range(n_sub):
            rows = pl.ds(r * PROJ_SUB, PROJ_SUB)
            acc = jnp.dot(hn_ref[rows, :], wb_ref[...], preferred_element_type=F32)
            for cb in range(n_cb):
                p_ref[cb, rows, :] = acc[:, cb * LANES:(cb + 1) * LANES].astype(BF16)


def _proj_call(xf, shift, scale, norm_w, w_main, w_gate, cos, sin):
    m_tiles = TOKENS // PROJ_TM
    n_tiles = N_MAIN // PROJ_TN
    per_seq = SEQ // PROJ_TM
    n_cb = PROJ_TN // LANES
    return pl.pallas_call(
        _proj_kernel,
        grid=(m_tiles, n_tiles),
        in_specs=[
            pl.BlockSpec((PROJ_TM, D_MODEL), lambda i, j: (i, 0)),
            pl.BlockSpec((1, 1, D_MODEL), lambda i, j: (i // per_seq, 0, 0)),
            pl.BlockSpec((1, 1, D_MODEL), lambda i, j: (i // per_seq, 0, 0)),
            pl.BlockSpec((1, D_MODEL), lambda i, j: (0, 0)),
            pl.BlockSpec((D_MODEL, PROJ_TN), lambda i, j: (0, j)),
            pl.BlockSpec((D_MODEL, LANES), lambda i, j: (0, 0)),
            pl.BlockSpec((PROJ_TM, LANES), lambda i, j: (i % per_seq, 0)),
            pl.BlockSpec((PROJ_TM, LANES), lambda i, j: (i % per_seq, 0)),
        ],
        out_specs=[
            pl.BlockSpec((n_cb, PROJ_TM, LANES), lambda i, j: (j, i, 0)),
            pl.BlockSpec((PROJ_TM, LANES), lambda i, j: (i, 0)),
        ],
        out_shape=[
            jax.ShapeDtypeStruct((N_COLBLK, TOKENS, LANES), BF16),
            jax.ShapeDtypeStruct((TOKENS, LANES), F32),
        ],
        scratch_shapes=[
            pltpu.VMEM((PROJ_TM, D_MODEL), BF16),
            pltpu.VMEM((D_MODEL, PROJ_TN), BF16),
        ],
        compiler_params=pltpu.CompilerParams(
            dimension_semantics=("arbitrary", "arbitrary"), vmem_limit_bytes=PROJ_VMEM_LIMIT),
        name="norm_in_proj",
    )(xf, shift, scale, norm_w, w_main, w_gate, cos, sin)


ATT_T = 256
ATT_NBLK = SEQ // ATT_T


def _attn_kernel(lq1_ref, lk1_ref, lq2_ref, lk2_ref, dnw_ref, q_ref, k_ref, v_ref, dg_ref,
                 o_ref, vt_ref, s_ref):
    lam = (jnp.exp(jnp.sum(lq1_ref[...] * lk1_ref[...], keepdims=True))
           - jnp.exp(jnp.sum(lq2_ref[...] * lk2_ref[...], keepdims=True))
           + LAMBDA_INIT)

    for jb in range(ATT_NBLK):
        cols = pl.ds(jb * ATT_T, ATT_T)
        vb = v_ref[0, jb * ATT_T:(jb + 1) * ATT_T, :].astype(F32)
        vt_ref[:, cols] = vb.T.astype(BF16)

    lane = lax.broadcasted_iota(jnp.int32, (ATT_T, LANES), 1)
    comp_a = lane < DIFF_QK_DIM
    krow = lax.broadcasted_iota(jnp.int32, (ATT_T, ATT_T), 0) // CHUNK
    qcol = lax.broadcasted_iota(jnp.int32, (ATT_T, ATT_T), 1) // CHUNK
    diag_mask = krow <= qcol

    diag_mask2 = jnp.concatenate([diag_mask, diag_mask], axis=1)

    def score_chunks(qi):
        slot = qi % 2
        q = q_ref[0, pl.ds(qi * ATT_T, ATT_T), :]
        zero = jnp.zeros_like(q)
        q2 = jnp.concatenate([jnp.where(comp_a, q, zero), jnp.where(comp_a, zero, q)], axis=0)
        state = {"m": None}

        def chunk(j):
            keys = pl.ds(j * ATT_T, ATT_T)
            s = _nt_dot(k_ref[0, keys, :], q2)
            if j == qi:
                s = jnp.where(diag_mask2, s, NEG_BIG)
            s_ref[slot, keys, :] = s
            mj = jnp.max(s, axis=0, keepdims=True)
            state["m"] = mj if state["m"] is None else jnp.maximum(state["m"], mj)

        return [functools.partial(chunk, j) for j in range(qi + 1)], state

    def value_chunks(qi, m):
        slot = qi % 2
        state = {"l": None, "pv": None}

        def chunk(j):
            keys = pl.ds(j * ATT_T, ATT_T)
            p = jnp.exp2(s_ref[slot, keys, :] - m)
            lj = jnp.sum(p, axis=0, keepdims=True)
            pvj = jnp.dot(vt_ref[:, keys], p.astype(BF16), preferred_element_type=F32)
            state["l"] = lj if state["l"] is None else state["l"] + lj
            state["pv"] = pvj if state["pv"] is None else state["pv"] + pvj

        return [functools.partial(chunk, j) for j in range(qi + 1)], state

    def finish(qi, pv, l):
        rows = pl.ds(qi * ATT_T, ATT_T)
        pv = pv * (1.0 / l)
        out_t = pv[:, :ATT_T] - lam * pv[:, ATT_T:]
        out = out_t.T
        ms = jnp.mean(out * out, axis=-1, keepdims=True)
        y = out * lax.rsqrt(ms + NORM_EPS) * dnw_ref[...] * (1.0 - LAMBDA_INIT)
        g = dg_ref[0, rows, :].astype(F32)
        o_ref[0, rows, :] = (y * (g * jax.nn.sigmoid(g))).astype(BF16)

    thunks, sc_state = score_chunks(0)
    for t in thunks:
        t()
    for qi in range(ATT_NBLK):
        v_thunks, v_state = value_chunks(qi, sc_state["m"])
        if qi + 1 < ATT_NBLK:
            s_thunks, sc_state = score_chunks(qi + 1)
        else:
            s_thunks = []
        for step in range(max(len(s_thunks), len(v_thunks))):
            if step < len(s_thunks):
                s_thunks[step]()
            if step < len(v_thunks):
                v_thunks[step]()
        finish(qi, v_state["pv"], v_state["l"])


def _attn_call(p, lq1, lk1, lq2, lk2, dnw):
    vec = lambda n: pl.BlockSpec((1, n), lambda b, h: (0, 0))
    slab = lambda cb0: pl.BlockSpec((1, SEQ, LANES), lambda b, h: (cb0 + h, b, 0))
    return pl.pallas_call(
        _attn_kernel,
        grid=(BATCH, DIFF_HEADS),
        in_specs=[vec(DIFF_QK_DIM)] * 4 + [vec(DIFF_HEAD_DIM),
                  slab(CB_DQ), slab(CB_DK), slab(CB_DV), slab(CB_DG)],
        out_specs=pl.BlockSpec((1, SEQ, LANES), lambda b, h: (h, b, 0)),
        out_shape=jax.ShapeDtypeStruct((DIFF_HEADS, TOKENS, LANES), BF16),
        scratch_shapes=[
            pltpu.VMEM((DIFF_HEAD_DIM, SEQ), BF16),
            pltpu.VMEM((2, SEQ, 2 * ATT_T), F32),
        ],
        compiler_params=pltpu.CompilerParams(
            dimension_semantics=("arbitrary", "arbitrary"), vmem_limit_bytes=VMEM_LIMIT),
        name="diff_attn",
    )(lq1, lk1, lq2, lk2, dnw, p, p, p, p)


GLA_BLK = 256
GLA_NCHUNK = SEQ // CHUNK


def _split_hi_lo(x):
    hi = x.astype(BF16)
    lo = (x - hi.astype(F32)).astype(BF16)
    return hi, lo


def _gla_kernel(g_ref, w2_ref, b2_ref, gw_ref, q_ref, k_ref, v_ref, gg_ref, o_ref,
                kdec_ref, tot_ref, kv_ref, st_ref):
    z = jnp.dot(g_ref[...].astype(BF16), w2_ref[...], preferred_element_type=F32) + b2_ref[...]
    log_a = (jnp.minimum(z, 0.0) - jnp.log(1.0 + jnp.exp(-jnp.abs(z)))) * (1.0 / GLA_GATE_TAU)

    r = lax.broadcasted_iota(jnp.int32, (GLA_BLK, GLA_BLK), 0)
    c = lax.broadcasted_iota(jnp.int32, (GLA_BLK, GLA_BLK), 1)
    after = jnp.where(((r // CHUNK) == (c // CHUNK)) & (c > r), 1.0, 0.0).astype(BF16)
    for blk in range(SEQ // GLA_BLK):
        rows = pl.ds(blk * GLA_BLK, GLA_BLK)
        la = log_a[blk * GLA_BLK:(blk + 1) * GLA_BLK, :]
        hi, lo = _split_hi_lo(la)
        both = jnp.dot(after, jnp.concatenate([hi, lo], axis=1), preferred_element_type=F32)
        suffix = both[:, :GLA_K_DIM] + both[:, GLA_K_DIM:]
        kdec_ref[rows, :] = (k_ref[0, rows, :].astype(F32) * jnp.exp(suffix)).astype(BF16)
        tot_ref[rows, :] = suffix + la

    dec_rows = jnp.exp(tot_ref[pl.ds(0, GLA_NCHUNK, stride=CHUNK), :])
    dec_cols = jnp.concatenate(
        [dec_rows, jnp.zeros((GLA_K_DIM - GLA_NCHUNK, GLA_K_DIM), F32)], axis=0).T

    for ci in range(GLA_NCHUNK):
        rows = pl.ds(ci * CHUNK, CHUNK)
        v_c = jnp.concatenate([v_ref[0, rows, :], v_ref[1, rows, :]], axis=1)
        kv_ref[ci] = _tn_dot(kdec_ref[rows, :], v_c)

    state = jnp.zeros((GLA_K_DIM, GLA_V_DIM), F32)
    for ci in range(GLA_NCHUNK):
        state = state * dec_cols[:, ci:ci + 1] + kv_ref[ci]
        st_ref[ci] = state.astype(BF16)

    gw = gw_ref[...]
    for ci in range(GLA_NCHUNK):
        rows = pl.ds(ci * CHUNK, CHUNK)
        o = jnp.dot(q_ref[0, rows, :], st_ref[ci], preferred_element_type=F32)
        o = o * (GLA_K_DIM ** -0.5)
        ms = jnp.mean(o * o, axis=-1, keepdims=True)
        y = o * lax.rsqrt(ms + NORM_EPS) * gw
        gate = jnp.concatenate([gg_ref[0, rows, :], gg_ref[1, rows, :]], axis=1).astype(F32)
        y = y * (gate * jax.nn.sigmoid(gate))
        o_ref[0, rows, :] = y[:, :LANES].astype(BF16)
        o_ref[1, rows, :] = y[:, LANES:].astype(BF16)


def _gla_call(p, g, w2_pad, b2, gw):
    one = lambda cb0: pl.BlockSpec((1, SEQ, LANES), lambda b, h: (cb0 + h, b, 0))
    two = lambda cb0: pl.BlockSpec((2, SEQ, LANES), lambda b, h: (cb0 // 2 + h, b, 0))
    return pl.pallas_call(
        _gla_kernel,
        grid=(BATCH, GLA_HEADS),
        in_specs=[
            pl.BlockSpec((SEQ, LANES), lambda b, h: (b, 0)),
            pl.BlockSpec((LANES, GLA_K_DIM), lambda b, h: (0, h)),
            pl.BlockSpec((1, GLA_K_DIM), lambda b, h: (0, h)),
            pl.BlockSpec((1, GLA_V_DIM), lambda b, h: (0, 0)),
            one(CB_GQ), one(CB_GK), two(CB_GV), two(CB_GG),
        ],
        out_specs=pl.BlockSpec((2, SEQ, LANES), lambda b, h: (h, b, 0)),
        out_shape=jax.ShapeDtypeStruct((2 * GLA_HEADS, TOKENS, LANES), BF16),
        scratch_shapes=[
            pltpu.VMEM((SEQ, GLA_K_DIM), BF16),
            pltpu.VMEM((SEQ, GLA_K_DIM), F32),
            pltpu.VMEM((GLA_NCHUNK, GLA_K_DIM, GLA_V_DIM), F32),
            pltpu.VMEM((GLA_NCHUNK, GLA_K_DIM, GLA_V_DIM), BF16),
        ],
        compiler_params=pltpu.CompilerParams(
            dimension_semantics=("arbitrary", "arbitrary"), vmem_limit_bytes=VMEM_LIMIT),
        name="gla_scan",
    )(g, w2_pad, b2, gw, p, p, p, p)


OUT_TM = 512
OUT_SUB = 256


def _out_kernel(a_ref, b_ref, w_ref, x_ref, gate_ref, fw_ref, o_ref, wb_ref):
    @pl.when(pl.program_id(0) == 0)
    def _():
        wb_ref[...] = w_ref[...].astype(BF16)

    gate = gate_ref[0]
    fw = fw_ref[...]
    for r in range(OUT_TM // OUT_SUB):
        rows = pl.ds(r * OUT_SUB, OUT_SUB)
        mix = jnp.concatenate(
            [a_ref[h, rows, :] for h in range(a_ref.shape[0])]
            + [b_ref[h, rows, :] for h in range(b_ref.shape[0])], axis=1)
        y = jnp.dot(mix, wb_ref[...], preferred_element_type=F32)
        h_res = x_ref[rows, :] + gate * y
        ms = jnp.mean(h_res * h_res, axis=-1, keepdims=True)
        o_ref[rows, :] = h_res * lax.rsqrt(ms + NORM_EPS) * fw


def _out_call(a, b, w_out, xf, gate, fw):
    per_seq = SEQ // OUT_TM
    return pl.pallas_call(
        _out_kernel,
        grid=(TOKENS // OUT_TM,),
        in_specs=[
            pl.BlockSpec((DIFF_HEADS, OUT_TM, LANES), lambda i: (0, i, 0)),
            pl.BlockSpec((2 * GLA_HEADS, OUT_TM, LANES), lambda i: (0, i, 0)),
            pl.BlockSpec((D_MODEL, D_MODEL), lambda i: (0, 0), pipeline_mode=pl.Buffered(1)),
            pl.BlockSpec((OUT_TM, D_MODEL), lambda i: (i, 0)),
            pl.BlockSpec((1, 1, D_MODEL), lambda i: (i // per_seq, 0, 0)),
            pl.BlockSpec((1, D_MODEL), lambda i: (0, 0)),
        ],
        out_specs=pl.BlockSpec((OUT_TM, D_MODEL), lambda i: (i, 0)),
        out_shape=jax.ShapeDtypeStruct((TOKENS, D_MODEL), F32),
        scratch_shapes=[pltpu.VMEM((D_MODEL, D_MODEL), BF16)],
        compiler_params=pltpu.CompilerParams(
            dimension_semantics=("arbitrary",), vmem_limit_bytes=VMEM_LIMIT),
        name="out_proj_norm",
    )(a, b, w_out, xf, gate, fw)


def _rope_tables():
    half = DIFF_QK_DIM // 2
    inv_freq = ROPE_THETA ** (-jnp.arange(0, DIFF_QK_DIM, 2, dtype=F32) / DIFF_QK_DIM)
    ang = jnp.arange(SEQ, dtype=F32)[:, None] * inv_freq[None, :]
    cos, sin = jnp.cos(ang), jnp.sin(ang)
    del half
    return jnp.tile(cos, (1, 4)), jnp.concatenate([-sin, sin, -sin, sin], axis=1)


def kernel(x, c, norm_w, w_ada, b_ada, w_in, lambda_q1, lambda_k1, lambda_q2, lambda_k2,
           diff_norm_w, gla_gate_w2, gla_gate_b, gla_norm_w, w_out, final_norm_w):
    assert x.shape == (BATCH, SEQ, D_MODEL) and w_in.shape[0] == 1
    xf = x.reshape(TOKENS, D_MODEL).astype(F32)

    c_pad = jnp.pad(c.astype(F32), ((0, 8 - BATCH), (0, 0)))
    mod = _ada_call(c_pad, w_ada[0], b_ada[0][None, :])[:BATCH]
    shift = mod[:, :D_MODEL].reshape(BATCH, 1, D_MODEL)
    scale = mod[:, D_MODEL:2 * D_MODEL].reshape(BATCH, 1, D_MODEL)
    gate = mod[:, 2 * D_MODEL:].reshape(BATCH, 1, D_MODEL)

    w_gate = jnp.pad(w_in[0][:, N_MAIN:], ((0, 0), (0, LANES - GLA_GATE_RANK))).astype(BF16)
    cos, sin = _rope_tables()
    p, g = _proj_call(xf, shift, scale, norm_w[0][None, :], w_in[0], w_gate, cos, sin)

    row = lambda v: v[0][None, :].astype(F32)
    a = _attn_call(p, row(lambda_q1), row(lambda_k1), row(lambda_q2), row(lambda_k2),
                   row(diff_norm_w))

    w2_pad = jnp.pad(gla_gate_w2[0], ((0, LANES - GLA_GATE_RANK), (0, 0))).astype(BF16)
    b = _gla_call(p, g, w2_pad, row(gla_gate_b), row(gla_norm_w))

    out = _out_call(a, b, w_out[0], xf, gate, final_norm_w[None, :].astype(F32))
    return out.reshape(BATCH, SEQ, D_MODEL).astype(x.dtype)
```

```python
import functools
import math

import jax
import jax.numpy as jnp
from jax import lax
from jax.experimental import pallas as pl
from jax.experimental.pallas import tpu as pltpu

D_MODEL = 2048
BATCH = 4
SEQ = 2048
TOKENS = BATCH * SEQ
CHUNK = 64
LANES = 128

DIFF_HEADS = 8
DIFF_HEAD_DIM = 128
DIFF_QK_DIM = 64
GLA_HEADS = 4
GLA_K_DIM = 128
GLA_V_DIM = 256
GLA_GATE_RANK = 16
GLA_GATE_TAU = 16.0
ROPE_THETA = 10000.0
NORM_EPS = 1e-6
LAMBDA_INIT = 0.8 - 0.6 * math.exp(-0.3 * 0)

N_MAIN = 7168
N_COLBLK = N_MAIN // LANES
CB_DQ, CB_DK, CB_DV, CB_DG = 0, 8, 16, 24
CB_GQ, CB_GK, CB_GV, CB_GG = 32, 36, 40, 48

VMEM_LIMIT = 48 * 1024 * 1024
NEG_BIG = -1e30
LOG2_E = math.log2(math.e)

BF16 = jnp.bfloat16
F32 = jnp.float32


def _nt_dot(a, b):
    return lax.dot_general(a, b, (((1,), (1,)), ((), ())), preferred_element_type=F32)


def _tn_dot(a, b):
    return lax.dot_general(a, b, (((0,), (0,)), ((), ())), preferred_element_type=F32)


ADA_TN = 768


def _ada_kernel(c_ref, w_ref, b_ref, o_ref):
    c = c_ref[...]
    c_act = (c * jax.nn.sigmoid(c)).astype(BF16)
    w = w_ref[...].astype(BF16)
    o_ref[...] = jnp.dot(c_act, w, preferred_element_type=F32) + b_ref[...]


def _ada_call(c_pad, w_ada, b_ada):
    n = w_ada.shape[1]
    return pl.pallas_call(
        _ada_kernel,
        grid=(n // ADA_TN,),
        in_specs=[
            pl.BlockSpec((8, D_MODEL), lambda j: (0, 0)),
            pl.BlockSpec((D_MODEL, ADA_TN), lambda j: (0, j)),
            pl.BlockSpec((1, ADA_TN), lambda j: (0, j)),
        ],
        out_specs=pl.BlockSpec((8, ADA_TN), lambda j: (0, j)),
        out_shape=jax.ShapeDtypeStruct((8, n), F32),
        compiler_params=pltpu.CompilerParams(
            dimension_semantics=("arbitrary",), vmem_limit_bytes=VMEM_LIMIT),
        name="ada_mod",
    )(c_pad, w_ada, b_ada)


PROJ_TM = 1024
PROJ_TN = 1024
PROJ_SUB = 256
PROJ_ROPE_TILES = 2
PROJ_VMEM_LIMIT = (2 * PROJ_TM * D_MODEL * 4 + 2 * D_MODEL * PROJ_TN * 4
                   + D_MODEL * PROJ_TN * 2 + PROJ_TM * D_MODEL * 2
                   + 2 * PROJ_TM * PROJ_TN * 2 + 12 * 1024 * 1024)


def _rot_half(x):
    lane = lax.broadcasted_iota(jnp.int32, x.shape, 1)
    first = (lane % DIFF_QK_DIM) < (DIFF_QK_DIM // 2)
    return jnp.where(first, pltpu.roll(x, LANES - 32, 1), pltpu.roll(x, 32, 1))


def _proj_kernel(x_ref, shift_ref, scale_ref, nw_ref, w_ref, wg_ref, cos_ref, sin_ref,
                 p_ref, g_ref, hn_ref, wb_ref):
    j = pl.program_id(1)
    n_sub = PROJ_TM // PROJ_SUB
    n_cb = PROJ_TN // LANES

    def norm_rows(rows):
        mult = nw_ref[...] * (1.0 + scale_ref[0])
        xs = x_ref[rows, :]
        ms = jnp.mean(xs * xs, axis=-1, keepdims=True)
        hn_ref[rows, :] = (xs * lax.rsqrt(ms + NORM_EPS) * mult + shift_ref[0]).astype(BF16)

    def store_rope(rows, acc, sc):
        cos = cos_ref[rows, :] * sc
        sin = sin_ref[rows, :] * sc
        for cb in range(n_cb):
            t = acc[:, cb * LANES:(cb + 1) * LANES]
            p_ref[cb, rows, :] = (t * cos + _rot_half(t) * sin).astype(BF16)

    @pl.when(j == 0)
    def _():
        wb_ref[...] = w_ref[...].astype(BF16)
        for r in range(n_sub):
            rows = pl.ds(r * PROJ_SUB, PROJ_SUB)
            norm_rows(rows)
            acc = _nt_dot(hn_ref[rows, :], wb_ref[...])
            store_rope(rows, acc, DIFF_QK_DIM ** -0.5 * LOG2_E)
        g_ref[...] = _nt_dot(hn_ref[...], wg_ref[...])

    @pl.when(j == 1)
    def _():
        wb_ref[...] = w_ref[...].astype(BF16)
        for r in range(n_sub):
            rows = pl.ds(r * PROJ_SUB, PROJ_SUB)
            acc = _nt_dot(hn_ref[rows, :], wb_ref[...])
            store_rope(rows, acc, 1.0)

    @pl.when(j >= PROJ_ROPE_TILES)
    def _():
        wb_ref[...] = w_ref[...].astype(BF16)
        for r in range(n_sub):
            rows = pl.ds(r * PROJ_SUB, PROJ_SUB)
            acc = _nt_dot(hn_ref[rows, :], wb_ref[...])
            for cb in range(n_cb):
                p_ref[cb, rows, :] = acc[:, cb * LANES:(cb + 1) * LANES].astype(BF16)


def _proj_call(xf, shift, scale, norm_w, w_main, w_gate, cos, sin):
    m_tiles = TOKENS // PROJ_TM
    n_tiles = N_MAIN // PROJ_TN
    per_seq = SEQ // PROJ_TM
    n_cb = PROJ_TN // LANES
    return pl.pallas_call(
        _proj_kernel,
        grid=(m_tiles, n_tiles),
        in_specs=[
            pl.BlockSpec((PROJ_TM, D_MODEL), lambda i, j: (i, 0)),
            pl.BlockSpec((1, 1, D_MODEL), lambda i, j: (i // per_seq, 0, 0)),
            pl.BlockSpec((1, 1, D_MODEL), lambda i, j: (i // per_seq, 0, 0)),
            pl.BlockSpec((1, D_MODEL), lambda i, j: (0, 0)),
            pl.BlockSpec((PROJ_TN, D_MODEL), lambda i, j: (j, 0)),
            pl.BlockSpec((LANES, D_MODEL), lambda i, j: (0, 0)),
            pl.BlockSpec((PROJ_TM, LANES), lambda i, j: (i % per_seq, 0)),
            pl.BlockSpec((PROJ_TM, LANES), lambda i, j: (i % per_seq, 0)),
        ],
        out_specs=[
            pl.BlockSpec((n_cb, PROJ_TM, LANES), lambda i, j: (j, i, 0)),
            pl.BlockSpec((PROJ_TM, LANES), lambda i, j: (i, 0)),
        ],
        out_shape=[
            jax.ShapeDtypeStruct((N_COLBLK, TOKENS, LANES), BF16),
            jax.ShapeDtypeStruct((TOKENS, LANES), F32),
        ],
        scratch_shapes=[
            pltpu.VMEM((PROJ_TM, D_MODEL), BF16),
            pltpu.VMEM((PROJ_TN, D_MODEL), BF16),
        ],
        compiler_params=pltpu.CompilerParams(
            dimension_semantics=("arbitrary", "arbitrary"), vmem_limit_bytes=PROJ_VMEM_LIMIT),
        name="norm_in_proj",
    )(xf, shift, scale, norm_w, w_main, w_gate, cos, sin)


ATT_T = 256
ATT_NBLK = SEQ // ATT_T


def _attn_kernel(lq1_ref, lk1_ref, lq2_ref, lk2_ref, dnw_ref, q_ref, k_ref, v_ref, dg_ref,
                 o_ref, vt_ref, s_ref):
    lam = (jnp.exp(jnp.sum(lq1_ref[...] * lk1_ref[...], keepdims=True))
           - jnp.exp(jnp.sum(lq2_ref[...] * lk2_ref[...], keepdims=True))
           + LAMBDA_INIT)

    for jb in range(ATT_NBLK):
        cols = pl.ds(jb * ATT_T, ATT_T)
        vb = v_ref[0, jb * ATT_T:(jb + 1) * ATT_T, :].astype(F32)
        vt_ref[:, cols] = vb.T.astype(BF16)

    lane = lax.broadcasted_iota(jnp.int32, (ATT_T, LANES), 1)
    comp_a = lane < DIFF_QK_DIM
    krow = lax.broadcasted_iota(jnp.int32, (ATT_T, ATT_T), 0) // CHUNK
    qcol = lax.broadcasted_iota(jnp.int32, (ATT_T, ATT_T), 1) // CHUNK
    diag_mask = krow <= qcol

    diag_mask2 = jnp.concatenate([diag_mask, diag_mask], axis=1)

    def score_chunks(qi):
        slot = qi % 2
        q = q_ref[0, pl.ds(qi * ATT_T, ATT_T), :]
        zero = jnp.zeros_like(q)
        q2 = jnp.concatenate([jnp.where(comp_a, q, zero), jnp.where(comp_a, zero, q)], axis=0)
        state = {"m": None}

        def chunk(j):
            keys = pl.ds(j * ATT_T, ATT_T)
            s = _nt_dot(k_ref[0, keys, :], q2)
            if j == qi:
                s = jnp.where(diag_mask2, s, NEG_BIG)
            s_ref[slot, keys, :] = s
            mj = jnp.max(s, axis=0, keepdims=True)
            state["m"] = mj if state["m"] is None else jnp.maximum(state["m"], mj)

        return [functools.partial(chunk, j) for j in range(qi + 1)], state

    def value_chunks(qi, m):
        slot = qi % 2
        state = {"l": None, "pv": None}

        def chunk(j):
            keys = pl.ds(j * ATT_T, ATT_T)
            p = jnp.exp2(s_ref[slot, keys, :] - m)
            lj = jnp.sum(p, axis=0, keepdims=True)
            pvj = jnp.dot(vt_ref[:, keys], p.astype(BF16), preferred_element_type=F32)
            state["l"] = lj if state["l"] is None else state["l"] + lj
            state["pv"] = pvj if state["pv"] is None else state["pv"] + pvj

        return [functools.partial(chunk, j) for j in range(qi + 1)], state

    def finish(qi, pv, l):
        rows = pl.ds(qi * ATT_T, ATT_T)
        pv = pv * (1.0 / l)
        out_t = pv[:, :ATT_T] - lam * pv[:, ATT_T:]
        out = out_t.T
        ms = jnp.mean(out * out, axis=-1, keepdims=True)
        y = out * lax.rsqrt(ms + NORM_EPS) * dnw_ref[...] * (1.0 - LAMBDA_INIT)
        g = dg_ref[0, rows, :].astype(F32)
        o_ref[0, rows, :] = (y * (g * jax.nn.sigmoid(g))).astype(BF16)

    thunks, sc_state = score_chunks(0)
    for t in thunks:
        t()
    for qi in range(ATT_NBLK):
        v_thunks, v_state = value_chunks(qi, sc_state["m"])
        if qi + 1 < ATT_NBLK:
            s_thunks, sc_state = score_chunks(qi + 1)
        else:
            s_thunks = []
        for step in range(max(len(s_thunks), len(v_thunks))):
            if step < len(s_thunks):
                s_thunks[step]()
            if step < len(v_thunks):
                v_thunks[step]()
        finish(qi, v_state["pv"], v_state["l"])


def _attn_call(p, lq1, lk1, lq2, lk2, dnw):
    vec = lambda n: pl.BlockSpec((1, n), lambda b, h: (0, 0))
    slab = lambda cb0: pl.BlockSpec((1, SEQ, LANES), lambda b, h: (cb0 + h, b, 0))
    return pl.pallas_call(
        _attn_kernel,
        grid=(BATCH, DIFF_HEADS),
        in_specs=[vec(DIFF_QK_DIM)] * 4 + [vec(DIFF_HEAD_DIM),
                  slab(CB_DQ), slab(CB_DK), slab(CB_DV), slab(CB_DG)],
        out_specs=pl.BlockSpec((1, SEQ, LANES), lambda b, h: (h, b, 0)),
        out_shape=jax.ShapeDtypeStruct((DIFF_HEADS, TOKENS, LANES), BF16),
        scratch_shapes=[
            pltpu.VMEM((DIFF_HEAD_DIM, SEQ), BF16),
            pltpu.VMEM((2, SEQ, 2 * ATT_T), F32),
        ],
        compiler_params=pltpu.CompilerParams(
            dimension_semantics=("arbitrary", "arbitrary"), vmem_limit_bytes=VMEM_LIMIT),
        name="diff_attn",
    )(lq1, lk1, lq2, lk2, dnw, p, p, p, p)


GLA_BLK = 256
GLA_NCHUNK = SEQ // CHUNK


def _split_hi_lo(x):
    hi = x.astype(BF16)
    lo = (x - hi.astype(F32)).astype(BF16)
    return hi, lo


def _gla_kernel(g_ref, w2_ref, b2_ref, gw_ref, q_ref, k_ref, v_ref, gg_ref, o_ref,
                kdec_ref, tot_ref, kv_ref, st_ref):
    z = jnp.dot(g_ref[...].astype(BF16), w2_ref[...], preferred_element_type=F32) + b2_ref[...]
    log_a = (jnp.minimum(z, 0.0) - jnp.log(1.0 + jnp.exp(-jnp.abs(z)))) * (1.0 / GLA_GATE_TAU)

    r = lax.broadcasted_iota(jnp.int32, (GLA_BLK, GLA_BLK), 0)
    c = lax.broadcasted_iota(jnp.int32, (GLA_BLK, GLA_BLK), 1)
    after = jnp.where(((r // CHUNK) == (c // CHUNK)) & (c > r), 1.0, 0.0).astype(BF16)
    for blk in range(SEQ // GLA_BLK):
        rows = pl.ds(blk * GLA_BLK, GLA_BLK)
        la = log_a[blk * GLA_BLK:(blk + 1) * GLA_BLK, :]
        hi, lo = _split_hi_lo(la)
        both = jnp.dot(after, jnp.concatenate([hi, lo], axis=1), preferred_element_type=F32)
        suffix = both[:, :GLA_K_DIM] + both[:, GLA_K_DIM:]
        kdec_ref[rows, :] = (k_ref[0, rows, :].astype(F32) * jnp.exp(suffix)).astype(BF16)
        tot_ref[rows, :] = suffix + la

    dec_rows = jnp.exp(tot_ref[pl.ds(0, GLA_NCHUNK, stride=CHUNK), :])
    dec_cols = jnp.concatenate(
        [dec_rows, jnp.zeros((GLA_K_DIM - GLA_NCHUNK, GLA_K_DIM), F32)], axis=0).T

    for ci in range(GLA_NCHUNK):
        rows = pl.ds(ci * CHUNK, CHUNK)
        v_c = jnp.concatenate([v_ref[0, rows, :], v_ref[1, rows, :]], axis=1)
        kv_ref[ci] = _tn_dot(kdec_ref[rows, :], v_c)

    state = jnp.zeros((GLA_K_DIM, GLA_V_DIM), F32)
    for ci in range(GLA_NCHUNK):
        state = state * dec_cols[:, ci:ci + 1] + kv_ref[ci]
        st_ref[ci] = state.astype(BF16)

    gw = gw_ref[...]
    for ci in range(GLA_NCHUNK):
        rows = pl.ds(ci * CHUNK, CHUNK)
        o = jnp.dot(q_ref[0, rows, :], st_ref[ci], preferred_element_type=F32)
        o = o * (GLA_K_DIM ** -0.5)
        ms = jnp.mean(o * o, axis=-1, keepdims=True)
        y = o * lax.rsqrt(ms + NORM_EPS) * gw
        gate = jnp.concatenate([gg_ref[0, rows, :], gg_ref[1, rows, :]], axis=1).astype(F32)
        y = y * (gate * jax.nn.sigmoid(gate))
        o_ref[0, rows, :] = y[:, :LANES].astype(BF16)
        o_ref[1, rows, :] = y[:, LANES:].astype(BF16)


def _gla_call(p, g, w2_pad, b2, gw):
    one = lambda cb0: pl.BlockSpec((1, SEQ, LANES), lambda b, h: (cb0 + h, b, 0))
    two = lambda cb0: pl.BlockSpec((2, SEQ, LANES), lambda b, h: (cb0 // 2 + h, b, 0))
    return pl.pallas_call(
        _gla_kernel,
        grid=(BATCH, GLA_HEADS),
        in_specs=[
            pl.BlockSpec((SEQ, LANES), lambda b, h: (b, 0)),
            pl.BlockSpec((LANES, GLA_K_DIM), lambda b, h: (0, h)),
            pl.BlockSpec((1, GLA_K_DIM), lambda b, h: (0, h)),
            pl.BlockSpec((1, GLA_V_DIM), lambda b, h: (0, 0)),
            one(CB_GQ), one(CB_GK), two(CB_GV), two(CB_GG),
        ],
        out_specs=pl.BlockSpec((2, SEQ, LANES), lambda b, h: (h, b, 0)),
        out_shape=jax.ShapeDtypeStruct((2 * GLA_HEADS, TOKENS, LANES), BF16),
        scratch_shapes=[
            pltpu.VMEM((SEQ, GLA_K_DIM), BF16),
            pltpu.VMEM((SEQ, GLA_K_DIM), F32),
            pltpu.VMEM((GLA_NCHUNK, GLA_K_DIM, GLA_V_DIM), F32),
            pltpu.VMEM((GLA_NCHUNK, GLA_K_DIM, GLA_V_DIM), BF16),
        ],
        compiler_params=pltpu.CompilerParams(
            dimension_semantics=("arbitrary", "arbitrary"), vmem_limit_bytes=VMEM_LIMIT),
        name="gla_scan",
    )(g, w2_pad, b2, gw, p, p, p, p)


OUT_TM = 512
OUT_SUB = 256


def _out_kernel(a_ref, b_ref, w_ref, x_ref, gate_ref, fw_ref, o_ref, wb_ref):
    @pl.when(pl.program_id(0) == 0)
    def _():
        wb_ref[...] = w_ref[...].astype(BF16)

    gate = gate_ref[0]
    fw = fw_ref[...]
    for r in range(OUT_TM // OUT_SUB):
        rows = pl.ds(r * OUT_SUB, OUT_SUB)
        mix = jnp.concatenate(
            [a_ref[h, rows, :] for h in range(a_ref.shape[0])]
            + [b_ref[h, rows, :] for h in range(b_ref.shape[0])], axis=1)
        y = jnp.dot(mix, wb_ref[...], preferred_element_type=F32)
        h_res = x_ref[rows, :] + gate * y
        ms = jnp.mean(h_res * h_res, axis=-1, keepdims=True)
        o_ref[rows, :] = h_res * lax.rsqrt(ms + NORM_EPS) * fw


def _out_call(a, b, w_out, xf, gate, fw):
    per_seq = SEQ // OUT_TM
    return pl.pallas_call(
        _out_kernel,
        grid=(TOKENS // OUT_TM,),
        in_specs=[
            pl.BlockSpec((DIFF_HEADS, OUT_TM, LANES), lambda i: (0, i, 0)),
            pl.BlockSpec((2 * GLA_HEADS, OUT_TM, LANES), lambda i: (0, i, 0)),
            pl.BlockSpec((D_MODEL, D_MODEL), lambda i: (0, 0), pipeline_mode=pl.Buffered(1)),
            pl.BlockSpec((OUT_TM, D_MODEL), lambda i: (i, 0)),
            pl.BlockSpec((1, 1, D_MODEL), lambda i: (i // per_seq, 0, 0)),
            pl.BlockSpec((1, D_MODEL), lambda i: (0, 0)),
        ],
        out_specs=pl.BlockSpec((OUT_TM, D_MODEL), lambda i: (i, 0)),
        out_shape=jax.ShapeDtypeStruct((TOKENS, D_MODEL), F32),
        scratch_shapes=[pltpu.VMEM((D_MODEL, D_MODEL), BF16)],
        compiler_params=pltpu.CompilerParams(
            dimension_semantics=("arbitrary",), vmem_limit_bytes=VMEM_LIMIT),
        name="out_proj_norm",
    )(a, b, w_out, xf, gate, fw)


def _rope_tables():
    half = DIFF_QK_DIM // 2
    inv_freq = ROPE_THETA ** (-jnp.arange(0, DIFF_QK_DIM, 2, dtype=F32) / DIFF_QK_DIM)
    ang = jnp.arange(SEQ, dtype=F32)[:, None] * inv_freq[None, :]
    cos, sin = jnp.cos(ang), jnp.sin(ang)
    del half
    return jnp.tile(cos, (1, 4)), jnp.concatenate([-sin, sin, -sin, sin], axis=1)


def kernel(x, c, norm_w, w_ada, b_ada, w_in, lambda_q1, lambda_k1, lambda_q2, lambda_k2,
           diff_norm_w, gla_gate_w2, gla_gate_b, gla_norm_w, w_out, final_norm_w):
    assert x.shape == (BATCH, SEQ, D_MODEL) and w_in.shape[0] == 1
    xf = x.reshape(TOKENS, D_MODEL).astype(F32)

    c_pad = jnp.pad(c.astype(F32), ((0, 8 - BATCH), (0, 0)))
    mod = _ada_call(c_pad, w_ada[0], b_ada[0][None, :])[:BATCH]
    shift = mod[:, :D_MODEL].reshape(BATCH, 1, D_MODEL)
    scale = mod[:, D_MODEL:2 * D_MODEL].reshape(BATCH, 1, D_MODEL)
    gate = mod[:, 2 * D_MODEL:].reshape(BATCH, 1, D_MODEL)

    w_t = w_in[0].T
    w_gate = jnp.pad(w_t[N_MAIN:], ((0, LANES - GLA_GATE_RANK), (0, 0))).astype(BF16)
    cos, sin = _rope_tables()
    p, g = _proj_call(xf, shift, scale, norm_w[0][None, :], w_t, w_gate, cos, sin)

    row = lambda v: v[0][None, :].astype(F32)
    a = _attn_call(p, row(lambda_q1), row(lambda_k1), row(lambda_q2), row(lambda_k2),
                   row(diff_norm_w))

    w2_pad = jnp.pad(gla_gate_w2[0], ((0, LANES - GLA_GATE_RANK), (0, 0))).astype(BF16)
    b = _gla_call(p, g, w2_pad, row(gla_gate_b), row(gla_norm_w))

    out = _out_call(a, b, w_out[0], xf, gate, final_norm_w[None, :].astype(F32))
    return out.reshape(BATCH, SEQ, D_MODEL).astype(x.dtype)
```

```python
import functools
import math

import jax
import jax.numpy as jnp
from jax import lax
from jax.experimental import pallas as pl
from jax.experimental.pallas import tpu as pltpu

D_MODEL = 2048
BATCH = 4
SEQ = 2048
TOKENS = BATCH * SEQ
CHUNK = 64
LANES = 128

DIFF_HEADS = 8
DIFF_HEAD_DIM = 128
DIFF_QK_DIM = 64
GLA_HEADS = 4
GLA_K_DIM = 128
GLA_V_DIM = 256
GLA_GATE_RANK = 16
GLA_GATE_TAU = 16.0
ROPE_THETA = 10000.0
NORM_EPS = 1e-6
LAMBDA_INIT = 0.8 - 0.6 * math.exp(-0.3 * 0)

N_MAIN = 7168
N_COLBLK = N_MAIN // LANES
CB_DQ, CB_DK, CB_DV, CB_DG = 0, 8, 16, 24
CB_GQ, CB_GK, CB_GV, CB_GG = 32, 36, 40, 48

VMEM_LIMIT = 48 * 1024 * 1024
NEG_BIG = -1e30
LOG2_E = math.log2(math.e)

BF16 = jnp.bfloat16
F32 = jnp.float32


def _nt_dot(a, b):
    return lax.dot_general(a, b, (((1,), (1,)), ((), ())), preferred_element_type=F32)


def _tn_dot(a, b):
    return lax.dot_general(a, b, (((0,), (0,)), ((), ())), preferred_element_type=F32)


ADA_TN = 768


def _ada_kernel(c_ref, w_ref, b_ref, o_ref):
    c = c_ref[...]
    c_act = (c * jax.nn.sigmoid(c)).astype(BF16)
    w = w_ref[...].astype(BF16)
    o_ref[...] = jnp.dot(c_act, w, preferred_element_type=F32) + b_ref[...]


def _ada_call(c_pad, w_ada, b_ada):
    n = w_ada.shape[1]
    return pl.pallas_call(
        _ada_kernel,
        grid=(n // ADA_TN,),
        in_specs=[
            pl.BlockSpec((8, D_MODEL), lambda j: (0, 0)),
            pl.BlockSpec((D_MODEL, ADA_TN), lambda j: (0, j)),
            pl.BlockSpec((1, ADA_TN), lambda j: (0, j)),
        ],
        out_specs=pl.BlockSpec((8, ADA_TN), lambda j: (0, j)),
        out_shape=jax.ShapeDtypeStruct((8, n), F32),
        compiler_params=pltpu.CompilerParams(
            dimension_semantics=("arbitrary",), vmem_limit_bytes=VMEM_LIMIT),
        name="ada_mod",
    )(c_pad, w_ada, b_ada)


PROJ_TM = 1024
PROJ_TN = 1024
PROJ_SUB = 256
PROJ_ROPE_TILES = 2
PROJ_VMEM_LIMIT = (2 * PROJ_TM * D_MODEL * 4 + 2 * D_MODEL * PROJ_TN * 4
                   + D_MODEL * PROJ_TN * 2 + PROJ_TM * D_MODEL * 2
                   + 2 * PROJ_TM * PROJ_TN * 2 + 12 * 1024 * 1024)


def _rot_half(x):
    lane = lax.broadcasted_iota(jnp.int32, x.shape, 1)
    first = (lane % DIFF_QK_DIM) < (DIFF_QK_DIM // 2)
    return jnp.where(first, pltpu.roll(x, LANES - 32, 1), pltpu.roll(x, 32, 1))


def _proj_kernel(x_ref, shift_ref, scale_ref, nw_ref, w_ref, wg_ref, cos_ref, sin_ref,
                 p_ref, g_ref, hn_ref, wb_ref):
    j = pl.program_id(1)
    n_sub = PROJ_TM // PROJ_SUB
    n_cb = PROJ_TN // LANES

    def norm_rows(rows):
        mult = nw_ref[...] * (1.0 + scale_ref[0])
        xs = x_ref[rows, :]
        ms = jnp.mean(xs * xs, axis=-1, keepdims=True)
        hn_ref[rows, :] = (xs * lax.rsqrt(ms + NORM_EPS) * mult + shift_ref[0]).astype(BF16)

    def store_rope(rows, acc, sc):
        cos = cos_ref[rows, :] * sc
        sin = sin_ref[rows, :] * sc
        for cb in range(n_cb):
            t = acc[:, cb * LANES:(cb + 1) * LANES]
            p_ref[cb, rows, :] = (t * cos + _rot_half(t) * sin).astype(BF16)

    @pl.when(j == 0)
    def _():
        wb_ref[...] = w_ref[...].astype(BF16)
        for r in range(n_sub):
            rows = pl.ds(r * PROJ_SUB, PROJ_SUB)
            norm_rows(rows)
            acc = _nt_dot(hn_ref[rows, :], wb_ref[...])
            store_rope(rows, acc, DIFF_QK_DIM ** -0.5 * LOG2_E)
        g_ref[...] = _nt_dot(hn_ref[...], wg_ref[...])

    @pl.when(j == 1)
    def _():
        wb_ref[...] = w_ref[...].astype(BF16)
        for r in range(n_sub):
            rows = pl.ds(r * PROJ_SUB, PROJ_SUB)
            acc = _nt_dot(hn_ref[rows, :], wb_ref[...])
            store_rope(rows, acc, 1.0)

    @pl.when(j >= PROJ_ROPE_TILES)
    def _():
        wb_ref[...] = w_ref[...].astype(BF16)
        for r in range(n_sub):
            rows = pl.ds(r * PROJ_SUB, PROJ_SUB)
            acc = _nt_dot(hn_ref[rows, :], wb_ref[...])
            for cb in range(n_cb):
                p_ref[cb, rows, :] = acc[:, cb * LANES:(cb + 1) * LANES].astype(BF16)


def _proj_call(xf, shift, scale, norm_w, w_main, w_gate, cos, sin):
    m_tiles = TOKENS // PROJ_TM
    n_tiles = N_MAIN // PROJ_TN
    per_seq = SEQ // PROJ_TM
    n_cb = PROJ_TN // LANES
    return pl.pallas_call(
        _proj_kernel,
        grid=(m_tiles, n_tiles),
        in_specs=[
            pl.BlockSpec((PROJ_TM, D_MODEL), lambda i, j: (i, 0)),
            pl.BlockSpec((1, 1, D_MODEL), lambda i, j: (i // per_seq, 0, 0)),
            pl.BlockSpec((1, 1, D_MODEL), lambda i, j: (i // per_seq, 0, 0)),
            pl.BlockSpec((1, D_MODEL), lambda i, j: (0, 0)),
            pl.BlockSpec((PROJ_TN, D_MODEL), lambda i, j: (j, 0)),
            pl.BlockSpec((LANES, D_MODEL), lambda i, j: (0, 0)),
            pl.BlockSpec((PROJ_TM, LANES), lambda i, j: (i % per_seq, 0)),
            pl.BlockSpec((PROJ_TM, LANES), lambda i, j: (i % per_seq, 0)),
        ],
        out_specs=[
            pl.BlockSpec((n_cb, PROJ_TM, LANES), lambda i, j: (j, i, 0)),
            pl.BlockSpec((PROJ_TM, LANES), lambda i, j: (i, 0)),
        ],
        out_shape=[
            jax.ShapeDtypeStruct((N_COLBLK, TOKENS, LANES), BF16),
            jax.ShapeDtypeStruct((TOKENS, LANES), F32),
        ],
        scratch_shapes=[
            pltpu.VMEM((PROJ_TM, D_MODEL), BF16),
            pltpu.VMEM((PROJ_TN, D_MODEL), BF16),
        ],
        compiler_params=pltpu.CompilerParams(
            dimension_semantics=("arbitrary", "arbitrary"), vmem_limit_bytes=PROJ_VMEM_LIMIT),
        name="norm_in_proj",
    )(xf, shift, scale, norm_w, w_main, w_gate, cos, sin)


ATT_T = 256
ATT_NBLK = SEQ // ATT_T
ATT_LEAD = 11
ATT_ONES_ROWS = 16


def _attn_kernel(lq1_ref, lk1_ref, lq2_ref, lk2_ref, dnw_ref, q_ref, k_ref, v_ref, dg_ref,
                 o_ref, vt_ref):
    lam = (jnp.exp(jnp.sum(lq1_ref[...] * lk1_ref[...], keepdims=True))
           - jnp.exp(jnp.sum(lq2_ref[...] * lk2_ref[...], keepdims=True))
           + LAMBDA_INIT)

    for jb in range(ATT_NBLK):
        cols = pl.ds(jb * ATT_T, ATT_T)
        vb = v_ref[0, jb * ATT_T:(jb + 1) * ATT_T, :].astype(F32)
        vt_ref[0:DIFF_HEAD_DIM, cols] = vb.T.astype(BF16)
    vt_ref[DIFF_HEAD_DIM:, :] = jnp.ones((ATT_ONES_ROWS, SEQ), BF16)

    lane = lax.broadcasted_iota(jnp.int32, (ATT_T, LANES), 1)
    comp_a = lane < DIFF_QK_DIM
    krow = lax.broadcasted_iota(jnp.int32, (ATT_T, ATT_T), 0) // CHUNK
    qcol = lax.broadcasted_iota(jnp.int32, (ATT_T, ATT_T), 1) // CHUNK
    diag_mask = krow <= qcol

    diag_mask2 = jnp.concatenate([diag_mask, diag_mask], axis=1)

    units = [(qi, j) for qi in range(ATT_NBLK) for j in range(qi + 1)]
    blocks = [{"m": None, "pv": None, "s": {}} for _ in range(ATT_NBLK)]

    def block_queries(qi):
        q = q_ref[0, pl.ds(qi * ATT_T, ATT_T), :]
        zero = jnp.zeros_like(q)
        return jnp.concatenate([jnp.where(comp_a, q, zero), jnp.where(comp_a, zero, q)], axis=0)

    def score_unit(qi, j):
        st = blocks[qi]
        if j == 0:
            st["q2"] = block_queries(qi)
        s = _nt_dot(k_ref[0, pl.ds(j * ATT_T, ATT_T), :], st["q2"])
        if j == qi:
            s = jnp.where(diag_mask2, s, NEG_BIG)
        st["s"][j] = s
        mj = jnp.max(s, axis=0, keepdims=True)
        st["m"] = mj if st["m"] is None else jnp.maximum(st["m"], mj)

    def value_unit(qi, j):
        st = blocks[qi]
        p = jnp.exp2(st["s"].pop(j) - st["m"])
        pvj = jnp.dot(vt_ref[:, pl.ds(j * ATT_T, ATT_T)], p.astype(BF16),
                      preferred_element_type=F32)
        st["pv"] = pvj if st["pv"] is None else st["pv"] + pvj
        if j == qi:
            finish(qi, st["pv"])

    def finish(qi, pv_l):
        rows = pl.ds(qi * ATT_T, ATT_T)
        l = pv_l[DIFF_HEAD_DIM:DIFF_HEAD_DIM + 1, :]
        pv = pv_l[0:DIFF_HEAD_DIM, :] * (1.0 / l)
        out_t = pv[:, :ATT_T] - lam * pv[:, ATT_T:]
        out = out_t.T
        ms = jnp.mean(out * out, axis=-1, keepdims=True)
        y = out * lax.rsqrt(ms + NORM_EPS) * dnw_ref[...] * (1.0 - LAMBDA_INIT)
        g = dg_ref[0, rows, :].astype(F32)
        o_ref[0, rows, :] = (y * (g * jax.nn.sigmoid(g))).astype(BF16)

    for t in range(len(units) + ATT_LEAD):
        if t < len(units):
            score_unit(*units[t])
        if t >= ATT_LEAD:
            value_unit(*units[t - ATT_LEAD])


def _attn_call(p, lq1, lk1, lq2, lk2, dnw):
    vec = lambda n: pl.BlockSpec((1, n), lambda b, h: (0, 0))
    slab = lambda cb0: pl.BlockSpec((1, SEQ, LANES), lambda b, h: (cb0 + h, b, 0))
    return pl.pallas_call(
        _attn_kernel,
        grid=(BATCH, DIFF_HEADS),
        in_specs=[vec(DIFF_QK_DIM)] * 4 + [vec(DIFF_HEAD_DIM),
                  slab(CB_DQ), slab(CB_DK), slab(CB_DV), slab(CB_DG)],
        out_specs=pl.BlockSpec((1, SEQ, LANES), lambda b, h: (h, b, 0)),
        out_shape=jax.ShapeDtypeStruct((DIFF_HEADS, TOKENS, LANES), BF16),
        scratch_shapes=[
            pltpu.VMEM((DIFF_HEAD_DIM + ATT_ONES_ROWS, SEQ), BF16),
        ],
        compiler_params=pltpu.CompilerParams(
            dimension_semantics=("arbitrary", "arbitrary"), vmem_limit_bytes=VMEM_LIMIT),
        name="diff_attn",
    )(lq1, lk1, lq2, lk2, dnw, p, p, p, p)


GLA_BLK = 256
GLA_NCHUNK = SEQ // CHUNK


def _split_hi_lo(x):
    hi = x.astype(BF16)
    lo = (x - hi.astype(F32)).astype(BF16)
    return hi, lo


def _gla_kernel(g_ref, w2_ref, b2_ref, gw_ref, q_ref, k_ref, v_ref, gg_ref, o_ref,
                kdec_ref, tot_ref, kv_ref, st_ref):
    z = jnp.dot(g_ref[...].astype(BF16), w2_ref[...], preferred_element_type=F32) + b2_ref[...]
    log_a = (jnp.minimum(z, 0.0) - jnp.log(1.0 + jnp.exp(-jnp.abs(z)))) * (1.0 / GLA_GATE_TAU)

    r = lax.broadcasted_iota(jnp.int32, (GLA_BLK, GLA_BLK), 0)
    c = lax.broadcasted_iota(jnp.int32, (GLA_BLK, GLA_BLK), 1)
    after = jnp.where(((r // CHUNK) == (c // CHUNK)) & (c > r), 1.0, 0.0).astype(BF16)
    for blk in range(SEQ // GLA_BLK):
        rows = pl.ds(blk * GLA_BLK, GLA_BLK)
        la = log_a[blk * GLA_BLK:(blk + 1) * GLA_BLK, :]
        hi, lo = _split_hi_lo(la)
        both = jnp.dot(after, jnp.concatenate([hi, lo], axis=1), preferred_element_type=F32)
        suffix = both[:, :GLA_K_DIM] + both[:, GLA_K_DIM:]
        kdec_ref[rows, :] = (k_ref[0, rows, :].astype(F32) * jnp.exp(suffix)).astype(BF16)
        tot_ref[rows, :] = suffix + la

    dec_rows = jnp.exp(tot_ref[pl.ds(0, GLA_NCHUNK, stride=CHUNK), :])
    dec_cols = jnp.concatenate(
        [dec_rows, jnp.zeros((GLA_K_DIM - GLA_NCHUNK, GLA_K_DIM), F32)], axis=0).T

    for ci in range(GLA_NCHUNK):
        rows = pl.ds(ci * CHUNK, CHUNK)
        v_c = jnp.concatenate([v_ref[0, rows, :], v_ref[1, rows, :]], axis=1)
        kv_ref[ci] = _tn_dot(kdec_ref[rows, :], v_c)

    state = jnp.zeros((GLA_K_DIM, GLA_V_DIM), F32)
    for ci in range(GLA_NCHUNK):
        state = state * dec_cols[:, ci:ci + 1] + kv_ref[ci]
        st_ref[ci] = state.astype(BF16)

    gw = gw_ref[...]
    for ci in range(GLA_NCHUNK):
        rows = pl.ds(ci * CHUNK, CHUNK)
        o = jnp.dot(q_ref[0, rows, :], st_ref[ci], preferred_element_type=F32)
        o = o * (GLA_K_DIM ** -0.5)
        ms = jnp.mean(o * o, axis=-1, keepdims=True)
        y = o * lax.rsqrt(ms + NORM_EPS) * gw
        gate = jnp.concatenate([gg_ref[0, rows, :], gg_ref[1, rows, :]], axis=1).astype(F32)
        y = y * (gate * jax.nn.sigmoid(gate))
        o_ref[0, rows, :] = y[:, :LANES].astype(BF16)
        o_ref[1, rows, :] = y[:, LANES:].astype(BF16)


def _gla_call(p, g, w2_pad, b2, gw):
    one = lambda cb0: pl.BlockSpec((1, SEQ, LANES), lambda b, h: (cb0 + h, b, 0))
    two = lambda cb0: pl.BlockSpec((2, SEQ, LANES), lambda b, h: (cb0 // 2 + h, b, 0))
    return pl.pallas_call(
        _gla_kernel,
        grid=(BATCH, GLA_HEADS),
        in_specs=[
            pl.BlockSpec((SEQ, LANES), lambda b, h: (b, 0)),
            pl.BlockSpec((LANES, GLA_K_DIM), lambda b, h: (0, h)),
            pl.BlockSpec((1, GLA_K_DIM), lambda b, h: (0, h)),
            pl.BlockSpec((1, GLA_V_DIM), lambda b, h: (0, 0)),
            one(CB_GQ), one(CB_GK), two(CB_GV), two(CB_GG),
        ],
        out_specs=pl.BlockSpec((2, SEQ, LANES), lambda b, h: (h, b, 0)),
        out_shape=jax.ShapeDtypeStruct((2 * GLA_HEADS, TOKENS, LANES), BF16),
        scratch_shapes=[
            pltpu.VMEM((SEQ, GLA_K_DIM), BF16),
            pltpu.VMEM((SEQ, GLA_K_DIM), F32),
            pltpu.VMEM((GLA_NCHUNK, GLA_K_DIM, GLA_V_DIM), F32),
            pltpu.VMEM((GLA_NCHUNK, GLA_K_DIM, GLA_V_DIM), BF16),
        ],
        compiler_params=pltpu.CompilerParams(
            dimension_semantics=("arbitrary", "arbitrary"), vmem_limit_bytes=VMEM_LIMIT),
        name="gla_scan",
    )(g, w2_pad, b2, gw, p, p, p, p)


OUT_TM = 512
OUT_SUB = 256


def _out_kernel(a_ref, b_ref, w_ref, x_ref, gate_ref, fw_ref, o_ref, wb_ref):
    @pl.when(pl.program_id(0) == 0)
    def _():
        wb_ref[...] = w_ref[...].astype(BF16)

    gate = gate_ref[0]
    fw = fw_ref[...]
    for r in range(OUT_TM // OUT_SUB):
        rows = pl.ds(r * OUT_SUB, OUT_SUB)
        mix = jnp.concatenate(
            [a_ref[h, rows, :] for h in range(a_ref.shape[0])]
            + [b_ref[h, rows, :] for h in range(b_ref.shape[0])], axis=1)
        y = jnp.dot(mix, wb_ref[...], preferred_element_type=F32)
        h_res = x_ref[rows, :] + gate * y
        ms = jnp.mean(h_res * h_res, axis=-1, keepdims=True)
        o_ref[rows, :] = h_res * lax.rsqrt(ms + NORM_EPS) * fw


def _out_call(a, b, w_out, xf, gate, fw):
    per_seq = SEQ // OUT_TM
    return pl.pallas_call(
        _out_kernel,
        grid=(TOKENS // OUT_TM,),
        in_specs=[
            pl.BlockSpec((DIFF_HEADS, OUT_TM, LANES), lambda i: (0, i, 0)),
            pl.BlockSpec((2 * GLA_HEADS, OUT_TM, LANES), lambda i: (0, i, 0)),
            pl.BlockSpec((D_MODEL, D_MODEL), lambda i: (0, 0), pipeline_mode=pl.Buffered(1)),
            pl.BlockSpec((OUT_TM, D_MODEL), lambda i: (i, 0)),
            pl.BlockSpec((1, 1, D_MODEL), lambda i: (i // per_seq, 0, 0)),
            pl.BlockSpec((1, D_MODEL), lambda i: (0, 0)),
        ],
        out_specs=pl.BlockSpec((OUT_TM, D_MODEL), lambda i: (i, 0)),
        out_shape=jax.ShapeDtypeStruct((TOKENS, D_MODEL), F32),
        scratch_shapes=[pltpu.VMEM((D_MODEL, D_MODEL), BF16)],
        compiler_params=pltpu.CompilerParams(
            dimension_semantics=("arbitrary",), vmem_limit_bytes=VMEM_LIMIT),
        name="out_proj_norm",
    )(a, b, w_out, xf, gate, fw)


def _rope_tables():
    half = DIFF_QK_DIM // 2
    inv_freq = ROPE_THETA ** (-jnp.arange(0, DIFF_QK_DIM, 2, dtype=F32) / DIFF_QK_DIM)
    ang = jnp.arange(SEQ, dtype=F32)[:, None] * inv_freq[None, :]
    cos, sin = jnp.cos(ang), jnp.sin(ang)
    del half
    return jnp.tile(cos, (1, 4)), jnp.concatenate([-sin, sin, -sin, sin], axis=1)


def kernel(x, c, norm_w, w_ada, b_ada, w_in, lambda_q1, lambda_k1, lambda_q2, lambda_k2,
           diff_norm_w, gla_gate_w2, gla_gate_b, gla_norm_w, w_out, final_norm_w):
    assert x.shape == (BATCH, SEQ, D_MODEL) and w_in.shape[0] == 1
    xf = x.reshape(TOKENS, D_MODEL).astype(F32)

    c_pad = jnp.pad(c.astype(F32), ((0, 8 - BATCH), (0, 0)))
    mod = _ada_call(c_pad, w_ada[0], b_ada[0][None, :])[:BATCH]
    shift = mod[:, :D_MODEL].reshape(BATCH, 1, D_MODEL)
    scale = mod[:, D_MODEL:2 * D_MODEL].reshape(BATCH, 1, D_MODEL)
    gate = mod[:, 2 * D_MODEL:].reshape(BATCH, 1, D_MODEL)

    w_t = w_in[0].T
    w_gate = jnp.pad(w_t[N_MAIN:], ((0, LANES - GLA_GATE_RANK), (0, 0))).astype(BF16)
    cos, sin = _rope_tables()
    p, g = _proj_call(xf, shift, scale, norm_w[0][None, :], w_t, w_gate, cos, sin)

    row = lambda v: v[0][None, :].astype(F32)
    a = _attn_call(p, row(lambda_q1), row(lambda_k1), row(lambda_q2), row(lambda_k2),
                   row(diff_norm_w))

    w2_pad = jnp.pad(gla_gate_w2[0], ((0, LANES - GLA_GATE_RANK), (0, 0))).astype(BF16)
    b = _gla_call(p, g, w2_pad, row(gla_gate_b), row(gla_norm_w))

    out = _out_call(a, b, w_out[0], xf, gate, final_norm_w[None, :].astype(F32))
    return out.reshape(BATCH, SEQ, D_MODEL).astype(x.dtype)
```

```python
import functools
import math

import jax
import jax.numpy as jnp
from jax import lax
from jax.experimental import pallas as pl
from jax.experimental.pallas import tpu as pltpu

D_MODEL = 2048
BATCH = 4
SEQ = 2048
TOKENS = BATCH * SEQ
CHUNK = 64
LANES = 128

DIFF_HEADS = 8
DIFF_HEAD_DIM = 128
DIFF_QK_DIM = 64
GLA_HEADS = 4
GLA_K_DIM = 128
GLA_V_DIM = 256
GLA_GATE_RANK = 16
GLA_GATE_TAU = 16.0
ROPE_THETA = 10000.0
NORM_EPS = 1e-6
LAMBDA_INIT = 0.8 - 0.6 * math.exp(-0.3 * 0)

N_MAIN = 7168
N_COLBLK = N_MAIN // LANES
CB_DQ, CB_DK, CB_DV, CB_DG = 0, 8, 16, 24
CB_GQ, CB_GK, CB_GV, CB_GG = 32, 36, 40, 48

VMEM_LIMIT = 48 * 1024 * 1024
NEG_BIG = -1e30
LOG2_E = math.log2(math.e)

BF16 = jnp.bfloat16
F32 = jnp.float32


def _nt_dot(a, b):
    return lax.dot_general(a, b, (((1,), (1,)), ((), ())), preferred_element_type=F32)


def _tn_dot(a, b):
    return lax.dot_general(a, b, (((0,), (0,)), ((), ())), preferred_element_type=F32)


ADA_TN = 768


def _ada_kernel(c_ref, w_ref, b_ref, o_ref):
    c = c_ref[...]
    c_act = (c * jax.nn.sigmoid(c)).astype(BF16)
    w = w_ref[...].astype(BF16)
    o_ref[...] = jnp.dot(c_act, w, preferred_element_type=F32) + b_ref[...]


def _ada_call(c_pad, w_ada, b_ada):
    n = w_ada.shape[1]
    return pl.pallas_call(
        _ada_kernel,
        grid=(n // ADA_TN,),
        in_specs=[
            pl.BlockSpec((8, D_MODEL), lambda j: (0, 0)),
            pl.BlockSpec((D_MODEL, ADA_TN), lambda j: (0, j)),
            pl.BlockSpec((1, ADA_TN), lambda j: (0, j)),
        ],
        out_specs=pl.BlockSpec((8, ADA_TN), lambda j: (0, j)),
        out_shape=jax.ShapeDtypeStruct((8, n), F32),
        compiler_params=pltpu.CompilerParams(
            dimension_semantics=("arbitrary",), vmem_limit_bytes=VMEM_LIMIT),
        name="ada_mod",
    )(c_pad, w_ada, b_ada)


PROJ_TM = SEQ
PROJ_TN = 1024
PROJ_SUB = 512
PROJ_SUB_WIDE = 512
PROJ_ROPE_TILES = 2
PROJ_VMEM_LIMIT = (2 * D_MODEL * PROJ_TN * 4 + 2 * PROJ_TM * PROJ_TN * 2
                   + 2 * PROJ_SUB * D_MODEL * 4 + D_MODEL * PROJ_TN * 2 + PROJ_TM * D_MODEL * 2
                   + 6 * PROJ_TM * LANES * 4 + 6 * 1024 * 1024)


def _rot_half(x):
    lane = lax.broadcasted_iota(jnp.int32, x.shape, 1)
    first = (lane % DIFF_QK_DIM) < (DIFF_QK_DIM // 2)
    return jnp.where(first, pltpu.roll(x, LANES - 32, 1), pltpu.roll(x, 32, 1))


def _proj_kernel(x_hbm, shift_ref, scale_ref, nw_ref, w_ref, wg_ref, cos_ref, sin_ref,
                 p_ref, g_ref, hn_ref, wb_ref, xbuf_ref, xsem):
    i = pl.program_id(0)
    j = pl.program_id(1)
    n_i = pl.num_programs(0)
    n_j = pl.num_programs(1)
    n_chunks = PROJ_TM // PROJ_SUB
    n_cb = PROJ_TN // LANES

    def x_copy(tile, chunk):
        row0 = pl.multiple_of(tile * PROJ_TM + chunk * PROJ_SUB, PROJ_SUB)
        slot = chunk % 2
        return pltpu.make_async_copy(
            x_hbm.at[pl.ds(row0, PROJ_SUB), :], xbuf_ref.at[slot], xsem.at[slot])

    def norm_rows(rows, xs):
        mult = nw_ref[...] * (1.0 + scale_ref[0])
        ms = jnp.mean(xs * xs, axis=-1, keepdims=True)
        hn_ref[rows, :] = (xs * lax.rsqrt(ms + NORM_EPS) * mult + shift_ref[0]).astype(BF16)

    def store_rope(rows, acc, sc):
        cos = cos_ref[rows, :] * sc
        sin = sin_ref[rows, :] * sc
        for cb in range(n_cb):
            t = acc[:, cb * LANES:(cb + 1) * LANES]
            p_ref[cb, rows, :] = (t * cos + _rot_half(t) * sin).astype(BF16)

    @pl.when(j == 0)
    def _():
        @pl.when(i == 0)
        def _():
            x_copy(i, 0).start()

        chunk_rows = lambda r: pl.ds(r * PROJ_SUB, PROJ_SUB)
        x_copy(i, 1).start()
        x_copy(i, 0).wait()
        wb_ref[...] = w_ref[...].astype(BF16)
        norm_rows(chunk_rows(0), xbuf_ref[0])
        for r in range(n_chunks):
            if r + 1 < n_chunks:
                x_copy(i, r + 1).wait()
            if r + 2 < n_chunks:
                x_copy(i, r + 2).start()
            acc = _nt_dot(hn_ref[chunk_rows(r), :], wb_ref[...])
            if r + 1 < n_chunks:
                norm_rows(chunk_rows(r + 1), xbuf_ref[(r + 1) % 2])
            store_rope(chunk_rows(r), acc, DIFF_QK_DIM ** -0.5 * LOG2_E)
        g_ref[...] = _nt_dot(hn_ref[...], wg_ref[...])

    @pl.when(j == 1)
    def _():
        wb_ref[...] = w_ref[...].astype(BF16)
        for r in range(PROJ_TM // PROJ_SUB_WIDE):
            rows = pl.ds(r * PROJ_SUB_WIDE, PROJ_SUB_WIDE)
            acc = _nt_dot(hn_ref[rows, :], wb_ref[...])
            store_rope(rows, acc, 1.0)

    @pl.when(j >= PROJ_ROPE_TILES)
    def _():
        wb_ref[...] = w_ref[...].astype(BF16)
        for r in range(PROJ_TM // PROJ_SUB_WIDE):
            rows = pl.ds(r * PROJ_SUB_WIDE, PROJ_SUB_WIDE)
            acc = _nt_dot(hn_ref[rows, :], wb_ref[...])
            for cb in range(n_cb):
                p_ref[cb, rows, :] = acc[:, cb * LANES:(cb + 1) * LANES].astype(BF16)

    @pl.when((j == n_j - 1) & (i + 1 < n_i))
    def _():
        x_copy(i + 1, 0).start()


def _proj_call(xf, shift, scale, norm_w, w_t, w_gate, cos, sin):
    assert PROJ_TM == SEQ and N_MAIN % PROJ_TN == 0
    n_cb = PROJ_TN // LANES
    return pl.pallas_call(
        _proj_kernel,
        grid=(TOKENS // PROJ_TM, N_MAIN // PROJ_TN),
        in_specs=[
            pl.BlockSpec(memory_space=pl.ANY),
            pl.BlockSpec((1, 1, D_MODEL), lambda i, j: (i, 0, 0)),
            pl.BlockSpec((1, 1, D_MODEL), lambda i, j: (i, 0, 0)),
            pl.BlockSpec((1, D_MODEL), lambda i, j: (0, 0)),
            pl.BlockSpec((PROJ_TN, D_MODEL), lambda i, j: (j, 0)),
            pl.BlockSpec((LANES, D_MODEL), lambda i, j: (0, 0)),
            pl.BlockSpec((SEQ, LANES), lambda i, j: (0, 0)),
            pl.BlockSpec((SEQ, LANES), lambda i, j: (0, 0)),
        ],
        out_specs=[
            pl.BlockSpec((n_cb, PROJ_TM, LANES), lambda i, j: (j, i, 0)),
            pl.BlockSpec((PROJ_TM, LANES), lambda i, j: (i, 0)),
        ],
        out_shape=[
            jax.ShapeDtypeStruct((N_COLBLK, TOKENS, LANES), BF16),
            jax.ShapeDtypeStruct((TOKENS, LANES), F32),
        ],
        scratch_shapes=[
            pltpu.VMEM((PROJ_TM, D_MODEL), BF16),
            pltpu.VMEM((PROJ_TN, D_MODEL), BF16),
            pltpu.VMEM((2, PROJ_SUB, D_MODEL), F32),
            pltpu.SemaphoreType.DMA((2,)),
        ],
        compiler_params=pltpu.CompilerParams(
            dimension_semantics=("arbitrary", "arbitrary"), vmem_limit_bytes=PROJ_VMEM_LIMIT),
        name="norm_in_proj",
    )(xf, shift, scale, norm_w, w_t, w_gate, cos, sin)


ATT_T = 256
ATT_NBLK = SEQ // ATT_T
ATT_LEAD = 11
ATT_ONES_ROWS = 16


def _attn_kernel(lq1_ref, lk1_ref, lq2_ref, lk2_ref, dnw_ref, q_ref, k_ref, v_ref, dg_ref,
                 o_ref, vt_ref):
    lam = (jnp.exp(jnp.sum(lq1_ref[...] * lk1_ref[...], keepdims=True))
           - jnp.exp(jnp.sum(lq2_ref[...] * lk2_ref[...], keepdims=True))
           + LAMBDA_INIT)

    for jb in range(ATT_NBLK):
        cols = pl.ds(jb * ATT_T, ATT_T)
        vb = v_ref[0, jb * ATT_T:(jb + 1) * ATT_T, :].astype(F32)
        vt_ref[0:DIFF_HEAD_DIM, cols] = vb.T.astype(BF16)
    vt_ref[DIFF_HEAD_DIM:, :] = jnp.ones((ATT_ONES_ROWS, SEQ), BF16)

    lane = lax.broadcasted_iota(jnp.int32, (ATT_T, LANES), 1)
    comp_a = lane < DIFF_QK_DIM
    krow = lax.broadcasted_iota(jnp.int32, (ATT_T, ATT_T), 0) // CHUNK
    qcol = lax.broadcasted_iota(jnp.int32, (ATT_T, ATT_T), 1) // CHUNK
    diag_mask = krow <= qcol

    diag_mask2 = jnp.concatenate([diag_mask, diag_mask], axis=1)

    units = [(qi, j) for qi in range(ATT_NBLK) for j in range(qi + 1)]
    blocks = [{"m": None, "pv": None, "s": {}} for _ in range(ATT_NBLK)]

    def block_queries(qi):
        q = q_ref[0, pl.ds(qi * ATT_T, ATT_T), :]
        zero = jnp.zeros_like(q)
        return jnp.concatenate([jnp.where(comp_a, q, zero), jnp.where(comp_a, zero, q)], axis=0)

    def score_unit(qi, j):
        st = blocks[qi]
        if j == 0:
            st["q2"] = block_queries(qi)
        s = _nt_dot(k_ref[0, pl.ds(j * ATT_T, ATT_T), :], st["q2"])
        if j == qi:
            s = jnp.where(diag_mask2, s, NEG_BIG)
        st["s"][j] = s
        mj = jnp.max(s, axis=0, keepdims=True)
        st["m"] = mj if st["m"] is None else jnp.maximum(st["m"], mj)

    def value_unit(qi, j):
        st = blocks[qi]
        p = jnp.exp2(st["s"].pop(j) - st["m"])
        pvj = jnp.dot(vt_ref[:, pl.ds(j * ATT_T, ATT_T)], p.astype(BF16),
                      preferred_element_type=F32)
        st["pv"] = pvj if st["pv"] is None else st["pv"] + pvj
        if j == qi:
            finish(qi, st["pv"])

    def finish(qi, pv_l):
        rows = pl.ds(qi * ATT_T, ATT_T)
        l = pv_l[DIFF_HEAD_DIM:DIFF_HEAD_DIM + 1, :]
        pv = pv_l[0:DIFF_HEAD_DIM, :] * (1.0 / l)
        out_t = pv[:, :ATT_T] - lam * pv[:, ATT_T:]
        out = out_t.T
        ms = jnp.mean(out * out, axis=-1, keepdims=True)
        y = out * lax.rsqrt(ms + NORM_EPS) * dnw_ref[...] * (1.0 - LAMBDA_INIT)
        g = dg_ref[0, rows, :].astype(F32)
        o_ref[0, rows, :] = (y * (g * jax.nn.sigmoid(g))).astype(BF16)

    for t in range(len(units) + ATT_LEAD):
        if t < len(units):
            score_unit(*units[t])
        if t >= ATT_LEAD:
            value_unit(*units[t - ATT_LEAD])


def _attn_call(p, lq1, lk1, lq2, lk2, dnw):
    vec = lambda n: pl.BlockSpec((1, n), lambda b, h: (0, 0))
    slab = lambda cb0: pl.BlockSpec((1, SEQ, LANES), lambda b, h: (cb0 + h, b, 0))
    return pl.pallas_call(
        _attn_kernel,
        grid=(BATCH, DIFF_HEADS),
        in_specs=[vec(DIFF_QK_DIM)] * 4 + [vec(DIFF_HEAD_DIM),
                  slab(CB_DQ), slab(CB_DK), slab(CB_DV), slab(CB_DG)],
        out_specs=pl.BlockSpec((1, SEQ, LANES), lambda b, h: (h, b, 0)),
        out_shape=jax.ShapeDtypeStruct((DIFF_HEADS, TOKENS, LANES), BF16),
        scratch_shapes=[
            pltpu.VMEM((DIFF_HEAD_DIM + ATT_ONES_ROWS, SEQ), BF16),
        ],
        compiler_params=pltpu.CompilerParams(
            dimension_semantics=("arbitrary", "arbitrary"), vmem_limit_bytes=VMEM_LIMIT),
        name="diff_attn",
    )(lq1, lk1, lq2, lk2, dnw, p, p, p, p)


GLA_BLK = 256
GLA_NCHUNK = SEQ // CHUNK


def _split_hi_lo(x):
    hi = x.astype(BF16)
    lo = (x - hi.astype(F32)).astype(BF16)
    return hi, lo


def _gla_kernel(g_ref, w2_ref, b2_ref, gw_ref, q_ref, k_ref, v_ref, gg_ref, o_ref,
                kdec_ref, tot_ref, kv_ref, st_ref):
    z = jnp.dot(g_ref[...].astype(BF16), w2_ref[...], preferred_element_type=F32) + b2_ref[...]
    log_a = (jnp.minimum(z, 0.0) - jnp.log(1.0 + jnp.exp(-jnp.abs(z)))) * (1.0 / GLA_GATE_TAU)

    r = lax.broadcasted_iota(jnp.int32, (GLA_BLK, GLA_BLK), 0)
    c = lax.broadcasted_iota(jnp.int32, (GLA_BLK, GLA_BLK), 1)
    after = jnp.where(((r // CHUNK) == (c // CHUNK)) & (c > r), 1.0, 0.0).astype(BF16)
    for blk in range(SEQ // GLA_BLK):
        rows = pl.ds(blk * GLA_BLK, GLA_BLK)
        la = log_a[blk * GLA_BLK:(blk + 1) * GLA_BLK, :]
        hi, lo = _split_hi_lo(la)
        both = jnp.dot(after, jnp.concatenate([hi, lo], axis=1), preferred_element_type=F32)
        suffix = both[:, :GLA_K_DIM] + both[:, GLA_K_DIM:]
        kdec_ref[rows, :] = (k_ref[0, rows, :].astype(F32) * jnp.exp(suffix)).astype(BF16)
        tot_ref[rows, :] = suffix + la

    dec_rows = jnp.exp(tot_ref[pl.ds(0, GLA_NCHUNK, stride=CHUNK), :])
    dec_cols = jnp.concatenate(
        [dec_rows, jnp.zeros((GLA_K_DIM - GLA_NCHUNK, GLA_K_DIM), F32)], axis=0).T

    for ci in range(GLA_NCHUNK):
        rows = pl.ds(ci * CHUNK, CHUNK)
        v_c = jnp.concatenate([v_ref[0, rows, :], v_ref[1, rows, :]], axis=1)
        kv_ref[ci] = _tn_dot(kdec_ref[rows, :], v_c)

    state = jnp.zeros((GLA_K_DIM, GLA_V_DIM), F32)
    for ci in range(GLA_NCHUNK):
        state = state * dec_cols[:, ci:ci + 1] + kv_ref[ci]
        st_ref[ci] = state.astype(BF16)

    gw = gw_ref[...]
    for ci in range(GLA_NCHUNK):
        rows = pl.ds(ci * CHUNK, CHUNK)
        o = jnp.dot(q_ref[0, rows, :], st_ref[ci], preferred_element_type=F32)
        o = o * (GLA_K_DIM ** -0.5)
        ms = jnp.mean(o * o, axis=-1, keepdims=True)
        y = o * lax.rsqrt(ms + NORM_EPS) * gw
        gate = jnp.concatenate([gg_ref[0, rows, :], gg_ref[1, rows, :]], axis=1).astype(F32)
        y = y * (gate * jax.nn.sigmoid(gate))
        o_ref[0, rows, :] = y[:, :LANES].astype(BF16)
        o_ref[1, rows, :] = y[:, LANES:].astype(BF16)


def _gla_call(p, g, w2_pad, b2, gw):
    one = lambda cb0: pl.BlockSpec((1, SEQ, LANES), lambda b, h: (cb0 + h, b, 0))
    two = lambda cb0: pl.BlockSpec((2, SEQ, LANES), lambda b, h: (cb0 // 2 + h, b, 0))
    return pl.pallas_call(
        _gla_kernel,
        grid=(BATCH, GLA_HEADS),
        in_specs=[
            pl.BlockSpec((SEQ, LANES), lambda b, h: (b, 0)),
            pl.BlockSpec((LANES, GLA_K_DIM), lambda b, h: (0, h)),
            pl.BlockSpec((1, GLA_K_DIM), lambda b, h: (0, h)),
            pl.BlockSpec((1, GLA_V_DIM), lambda b, h: (0, 0)),
            one(CB_GQ), one(CB_GK), two(CB_GV), two(CB_GG),
        ],
        out_specs=pl.BlockSpec((2, SEQ, LANES), lambda b, h: (h, b, 0)),
        out_shape=jax.ShapeDtypeStruct((2 * GLA_HEADS, TOKENS, LANES), BF16),
        scratch_shapes=[
            pltpu.VMEM((SEQ, GLA_K_DIM), BF16),
            pltpu.VMEM((SEQ, GLA_K_DIM), F32),
            pltpu.VMEM((GLA_NCHUNK, GLA_K_DIM, GLA_V_DIM), F32),
            pltpu.VMEM((GLA_NCHUNK, GLA_K_DIM, GLA_V_DIM), BF16),
        ],
        compiler_params=pltpu.CompilerParams(
            dimension_semantics=("arbitrary", "arbitrary"), vmem_limit_bytes=VMEM_LIMIT),
        name="gla_scan",
    )(g, w2_pad, b2, gw, p, p, p, p)


OUT_TM = 512
OUT_SUB = 256


def _out_kernel(a_ref, b_ref, w_ref, x_ref, gate_ref, fw_ref, o_ref, wb_ref):
    @pl.when(pl.program_id(0) == 0)
    def _():
        wb_ref[...] = w_ref[...].astype(BF16)

    gate = gate_ref[0]
    fw = fw_ref[...]
    for r in range(OUT_TM // OUT_SUB):
        rows = pl.ds(r * OUT_SUB, OUT_SUB)
        mix = jnp.concatenate(
            [a_ref[h, rows, :] for h in range(a_ref.shape[0])]
            + [b_ref[h, rows, :] for h in range(b_ref.shape[0])], axis=1)
        y = jnp.dot(mix, wb_ref[...], preferred_element_type=F32)
        h_res = x_ref[rows, :] + gate * y
        ms = jnp.mean(h_res * h_res, axis=-1, keepdims=True)
        o_ref[rows, :] = h_res * lax.rsqrt(ms + NORM_EPS) * fw


def _out_call(a, b, w_out, xf, gate, fw):
    per_seq = SEQ // OUT_TM
    return pl.pallas_call(
        _out_kernel,
        grid=(TOKENS // OUT_TM,),
        in_specs=[
            pl.BlockSpec((DIFF_HEADS, OUT_TM, LANES), lambda i: (0, i, 0)),
            pl.BlockSpec((2 * GLA_HEADS, OUT_TM, LANES), lambda i: (0, i, 0)),
            pl.BlockSpec((D_MODEL, D_MODEL), lambda i: (0, 0), pipeline_mode=pl.Buffered(1)),
            pl.BlockSpec((OUT_TM, D_MODEL), lambda i: (i, 0)),
            pl.BlockSpec((1, 1, D_MODEL), lambda i: (i // per_seq, 0, 0)),
            pl.BlockSpec((1, D_MODEL), lambda i: (0, 0)),
        ],
        out_specs=pl.BlockSpec((OUT_TM, D_MODEL), lambda i: (i, 0)),
        out_shape=jax.ShapeDtypeStruct((TOKENS, D_MODEL), F32),
        scratch_shapes=[pltpu.VMEM((D_MODEL, D_MODEL), BF16)],
        compiler_params=pltpu.CompilerParams(
            dimension_semantics=("arbitrary",), vmem_limit_bytes=VMEM_LIMIT),
        name="out_proj_norm",
    )(a, b, w_out, xf, gate, fw)


def _rope_tables():
    half = DIFF_QK_DIM // 2
    inv_freq = ROPE_THETA ** (-jnp.arange(0, DIFF_QK_DIM, 2, dtype=F32) / DIFF_QK_DIM)
    ang = jnp.arange(SEQ, dtype=F32)[:, None] * inv_freq[None, :]
    cos, sin = jnp.cos(ang), jnp.sin(ang)
    del half
    return jnp.tile(cos, (1, 4)), jnp.concatenate([-sin, sin, -sin, sin], axis=1)


def kernel(x, c, norm_w, w_ada, b_ada, w_in, lambda_q1, lambda_k1, lambda_q2, lambda_k2,
           diff_norm_w, gla_gate_w2, gla_gate_b, gla_norm_w, w_out, final_norm_w):
    assert x.shape == (BATCH, SEQ, D_MODEL) and w_in.shape[0] == 1
    xf = x.reshape(TOKENS, D_MODEL).astype(F32)

    c_pad = jnp.pad(c.astype(F32), ((0, 8 - BATCH), (0, 0)))
    mod = _ada_call(c_pad, w_ada[0], b_ada[0][None, :])[:BATCH]
    shift = mod[:, :D_MODEL].reshape(BATCH, 1, D_MODEL)
    scale = mod[:, D_MODEL:2 * D_MODEL].reshape(BATCH, 1, D_MODEL)
    gate = mod[:, 2 * D_MODEL:].reshape(BATCH, 1, D_MODEL)

    w_t = w_in[0].T
    w_gate = jnp.pad(w_t[N_MAIN:], ((0, LANES - GLA_GATE_RANK), (0, 0))).astype(BF16)
    cos, sin = _rope_tables()
    p, g = _proj_call(xf, shift, scale, norm_w[0][None, :], w_t, w_gate, cos, sin)

    row = lambda v: v[0][None, :].astype(F32)
    a = _attn_call(p, row(lambda_q1), row(lambda_k1), row(lambda_q2), row(lambda_k2),
                   row(diff_norm_w))

    w2_pad = jnp.pad(gla_gate_w2[0], ((0, LANES - GLA_GATE_RANK), (0, 0))).astype(BF16)
    b = _gla_call(p, g, w2_pad, row(gla_gate_b), row(gla_norm_w))

    out = _out_call(a, b, w_out[0], xf, gate, final_norm_w[None, :].astype(F32))
    return out.reshape(BATCH, SEQ, D_MODEL).astype(x.dtype)
```

```python
import functools
import math

import jax
import jax.numpy as jnp
from jax import lax
from jax.experimental import pallas as pl
from jax.experimental.pallas import tpu as pltpu

D_MODEL = 2048
BATCH = 4
SEQ = 2048
TOKENS = BATCH * SEQ
CHUNK = 64
LANES = 128

DIFF_HEADS = 8
DIFF_HEAD_DIM = 128
DIFF_QK_DIM = 64
GLA_HEADS = 4
GLA_K_DIM = 128
GLA_V_DIM = 256
GLA_GATE_RANK = 16
GLA_GATE_TAU = 16.0
ROPE_THETA = 10000.0
NORM_EPS = 1e-6
LAMBDA_INIT = 0.8 - 0.6 * math.exp(-0.3 * 0)

N_MAIN = 7168
N_COLBLK = N_MAIN // LANES
CB_DQ, CB_DK, CB_DV, CB_DG = 0, 8, 16, 24
CB_GQ, CB_GK, CB_GV, CB_GG = 32, 36, 40, 48

VMEM_LIMIT = 48 * 1024 * 1024
NEG_BIG = -1e30
LOG2_E = math.log2(math.e)

BF16 = jnp.bfloat16
F32 = jnp.float32


def _nt_dot(a, b):
    return lax.dot_general(a, b, (((1,), (1,)), ((), ())), preferred_element_type=F32)


def _tn_dot(a, b):
    return lax.dot_general(a, b, (((0,), (0,)), ((), ())), preferred_element_type=F32)


ADA_TN = 768


def _ada_kernel(c_ref, w_ref, b_ref, o_ref):
    c = c_ref[...]
    c_act = (c * jax.nn.sigmoid(c)).astype(BF16)
    w = w_ref[...].astype(BF16)
    o_ref[...] = jnp.dot(c_act, w, preferred_element_type=F32) + b_ref[...]


def _ada_call(c_pad, w_ada, b_ada):
    n = w_ada.shape[1]
    return pl.pallas_call(
        _ada_kernel,
        grid=(n // ADA_TN,),
        in_specs=[
            pl.BlockSpec((8, D_MODEL), lambda j: (0, 0)),
            pl.BlockSpec((D_MODEL, ADA_TN), lambda j: (0, j)),
            pl.BlockSpec((1, ADA_TN), lambda j: (0, j)),
        ],
        out_specs=pl.BlockSpec((8, ADA_TN), lambda j: (0, j)),
        out_shape=jax.ShapeDtypeStruct((8, n), F32),
        compiler_params=pltpu.CompilerParams(
            dimension_semantics=("arbitrary",), vmem_limit_bytes=VMEM_LIMIT),
        name="ada_mod",
    )(c_pad, w_ada, b_ada)


PROJ_TM = 1024
PROJ_TN = 1024
PROJ_SUB = 256
PROJ_ROPE_TILES = 2
PROJ_VMEM_LIMIT = (2 * PROJ_TM * D_MODEL * 4 + 2 * D_MODEL * PROJ_TN * 4
                   + D_MODEL * PROJ_TN * 2 + PROJ_TM * D_MODEL * 2
                   + 2 * PROJ_TM * PROJ_TN * 2 + 12 * 1024 * 1024)


def _rot_half(x):
    lane = lax.broadcasted_iota(jnp.int32, x.shape, 1)
    first = (lane % DIFF_QK_DIM) < (DIFF_QK_DIM // 2)
    return jnp.where(first, pltpu.roll(x, LANES - 32, 1), pltpu.roll(x, 32, 1))


def _proj_kernel(x_ref, shift_ref, scale_ref, nw_ref, w_ref, wg_ref, cos_ref, sin_ref,
                 p_ref, g_ref, hn_ref, wb_ref):
    j = pl.program_id(1)
    n_sub = PROJ_TM // PROJ_SUB
    n_cb = PROJ_TN // LANES

    def norm_rows(rows):
        mult = nw_ref[...] * (1.0 + scale_ref[0])
        xs = x_ref[rows, :]
        ms = jnp.mean(xs * xs, axis=-1, keepdims=True)
        hn_ref[rows, :] = (xs * lax.rsqrt(ms + NORM_EPS) * mult + shift_ref[0]).astype(BF16)

    def store_rope(rows, acc, sc):
        cos = cos_ref[rows, :] * sc
        sin = sin_ref[rows, :] * sc
        for cb in range(n_cb):
            t = acc[:, cb * LANES:(cb + 1) * LANES]
            p_ref[cb, rows, :] = (t * cos + _rot_half(t) * sin).astype(BF16)

    @pl.when(j == 0)
    def _():
        wb_ref[...] = w_ref[...].astype(BF16)
        for r in range(n_sub):
            rows = pl.ds(r * PROJ_SUB, PROJ_SUB)
            norm_rows(rows)
            acc = _nt_dot(hn_ref[rows, :], wb_ref[...])
            store_rope(rows, acc, DIFF_QK_DIM ** -0.5 * LOG2_E)
        g_ref[...] = _nt_dot(hn_ref[...], wg_ref[...])

    @pl.when(j == 1)
    def _():
        wb_ref[...] = w_ref[...].astype(BF16)
        for r in range(n_sub):
            rows = pl.ds(r * PROJ_SUB, PROJ_SUB)
            acc = _nt_dot(hn_ref[rows, :], wb_ref[...])
            store_rope(rows, acc, 1.0)

    @pl.when(j >= PROJ_ROPE_TILES)
    def _():
        wb_ref[...] = w_ref[...].astype(BF16)
        for r in range(n_sub):
            rows = pl.ds(r * PROJ_SUB, PROJ_SUB)
            acc = _nt_dot(hn_ref[rows, :], wb_ref[...])
            for cb in range(n_cb):
                p_ref[cb, rows, :] = acc[:, cb * LANES:(cb + 1) * LANES].astype(BF16)


def _proj_call(xf, shift, scale, norm_w, w_t, w_gate, cos, sin):
    m_tiles = TOKENS // PROJ_TM
    n_tiles = N_MAIN // PROJ_TN
    per_seq = SEQ // PROJ_TM
    n_cb = PROJ_TN // LANES
    return pl.pallas_call(
        _proj_kernel,
        grid=(m_tiles, n_tiles),
        in_specs=[
            pl.BlockSpec((PROJ_TM, D_MODEL), lambda i, j: (i, 0)),
            pl.BlockSpec((1, 1, D_MODEL), lambda i, j: (i // per_seq, 0, 0)),
            pl.BlockSpec((1, 1, D_MODEL), lambda i, j: (i // per_seq, 0, 0)),
            pl.BlockSpec((1, D_MODEL), lambda i, j: (0, 0)),
            pl.BlockSpec((PROJ_TN, D_MODEL), lambda i, j: (j, 0)),
            pl.BlockSpec((LANES, D_MODEL), lambda i, j: (0, 0)),
            pl.BlockSpec((PROJ_TM, LANES), lambda i, j: (i % per_seq, 0)),
            pl.BlockSpec((PROJ_TM, LANES), lambda i, j: (i % per_seq, 0)),
        ],
        out_specs=[
            pl.BlockSpec((n_cb, PROJ_TM, LANES), lambda i, j: (j, i, 0)),
            pl.BlockSpec((PROJ_TM, LANES), lambda i, j: (i, 0)),
        ],
        out_shape=[
            jax.ShapeDtypeStruct((N_COLBLK, TOKENS, LANES), BF16),
            jax.ShapeDtypeStruct((TOKENS, LANES), F32),
        ],
        scratch_shapes=[
            pltpu.VMEM((PROJ_TM, D_MODEL), BF16),
            pltpu.VMEM((PROJ_TN, D_MODEL), BF16),
        ],
        compiler_params=pltpu.CompilerParams(
            dimension_semantics=("arbitrary", "arbitrary"), vmem_limit_bytes=PROJ_VMEM_LIMIT),
        name="norm_in_proj",
    )(xf, shift, scale, norm_w, w_t, w_gate, cos, sin)


ATT_T = 256
ATT_NBLK = SEQ // ATT_T
ATT_LEAD = 24
assert ATT_LEAD >= ATT_NBLK - 1
ATT_ONES_ROWS = 16
ATT_HEADS_PER_STEP = 2


def _attn_kernel(lq1_ref, lk1_ref, lq2_ref, lk2_ref, dnw_ref, q_ref, k_ref, v_ref, dg_ref,
                 o_ref, vt_ref):
    lam = (jnp.exp(jnp.sum(lq1_ref[...] * lk1_ref[...], keepdims=True))
           - jnp.exp(jnp.sum(lq2_ref[...] * lk2_ref[...], keepdims=True))
           + LAMBDA_INIT)
    out_gain = dnw_ref[...] * (1.0 - LAMBDA_INIT)

    for hd in range(ATT_HEADS_PER_STEP):
        for jb in range(ATT_NBLK):
            cols = pl.ds(jb * ATT_T, ATT_T)
            vb = v_ref[hd, jb * ATT_T:(jb + 1) * ATT_T, :].astype(F32)
            vt_ref[hd, 0:DIFF_HEAD_DIM, cols] = vb.T.astype(BF16)
        vt_ref[hd, DIFF_HEAD_DIM:, :] = jnp.ones((ATT_ONES_ROWS, SEQ), BF16)

    lane = lax.broadcasted_iota(jnp.int32, (ATT_T, LANES), 1)
    comp_a = lane < DIFF_QK_DIM
    krow = lax.broadcasted_iota(jnp.int32, (ATT_T, ATT_T), 0) // CHUNK
    qcol = lax.broadcasted_iota(jnp.int32, (ATT_T, ATT_T), 1) // CHUNK
    diag_mask = krow <= qcol
    diag_mask2 = jnp.concatenate([diag_mask, diag_mask], axis=1)

    units = [(hd, qi, j) for hd in range(ATT_HEADS_PER_STEP)
             for qi in range(ATT_NBLK) for j in range(qi + 1)]
    blocks = {(hd, qi): {"m": None, "pv": None, "s": {}}
              for hd in range(ATT_HEADS_PER_STEP) for qi in range(ATT_NBLK)}

    def block_queries(hd, qi):
        q = q_ref[hd, pl.ds(qi * ATT_T, ATT_T), :]
        zero = jnp.zeros_like(q)
        return jnp.concatenate([jnp.where(comp_a, q, zero), jnp.where(comp_a, zero, q)], axis=0)

    def score_unit(hd, qi, j):
        st = blocks[hd, qi]
        if j == 0:
            st["q2"] = block_queries(hd, qi)
        s = _nt_dot(k_ref[hd, pl.ds(j * ATT_T, ATT_T), :], st["q2"])
        if j == qi:
            s = jnp.where(diag_mask2, s, NEG_BIG)
        st["s"][j] = s
        mj = jnp.max(s, axis=0, keepdims=True)
        st["m"] = mj if st["m"] is None else jnp.maximum(st["m"], mj)

    def value_unit(hd, qi, j):
        st = blocks[hd, qi]
        p = jnp.exp2(st["s"].pop(j) - st["m"])
        pvj = jnp.dot(vt_ref[hd, :, pl.ds(j * ATT_T, ATT_T)], p.astype(BF16),
                      preferred_element_type=F32)
        st["pv"] = pvj if st["pv"] is None else st["pv"] + pvj
        if j == qi:
            finish(hd, qi, st["pv"])

    def finish(hd, qi, pv_l):
        rows = pl.ds(qi * ATT_T, ATT_T)
        l = pv_l[DIFF_HEAD_DIM:DIFF_HEAD_DIM + 1, :]
        pv = pv_l[0:DIFF_HEAD_DIM, :] * (1.0 / l)
        out_t = pv[:, :ATT_T] - lam * pv[:, ATT_T:]
        out = out_t.T
        ms = jnp.mean(out * out, axis=-1, keepdims=True)
        y = out * lax.rsqrt(ms + NORM_EPS) * out_gain
        g = dg_ref[hd, rows, :].astype(F32)
        o_ref[hd, rows, :] = (y * (g * jax.nn.sigmoid(g))).astype(BF16)

    for t in range(len(units) + ATT_LEAD):
        if t < len(units):
            score_unit(*units[t])
        if t >= ATT_LEAD:
            value_unit(*units[t - ATT_LEAD])


def _attn_call(p, lq1, lk1, lq2, lk2, dnw):
    hp = ATT_HEADS_PER_STEP
    vec = lambda n: pl.BlockSpec((1, n), lambda b, h: (0, 0))
    slab = lambda cb0: pl.BlockSpec((hp, SEQ, LANES), lambda b, h: (cb0 // hp + h, b, 0))
    return pl.pallas_call(
        _attn_kernel,
        grid=(BATCH, DIFF_HEADS // hp),
        in_specs=[vec(DIFF_QK_DIM)] * 4 + [vec(DIFF_HEAD_DIM),
                  slab(CB_DQ), slab(CB_DK), slab(CB_DV), slab(CB_DG)],
        out_specs=pl.BlockSpec((hp, SEQ, LANES), lambda b, h: (h, b, 0)),
        out_shape=jax.ShapeDtypeStruct((DIFF_HEADS, TOKENS, LANES), BF16),
        scratch_shapes=[
            pltpu.VMEM((hp, DIFF_HEAD_DIM + ATT_ONES_ROWS, SEQ), BF16),
        ],
        compiler_params=pltpu.CompilerParams(
            dimension_semantics=("arbitrary", "arbitrary"), vmem_limit_bytes=VMEM_LIMIT),
        name="diff_attn",
    )(lq1, lk1, lq2, lk2, dnw, p, p, p, p)


GLA_BLK = 256
GLA_NCHUNK = SEQ // CHUNK


def _split_hi_lo(x):
    hi = x.astype(BF16)
    lo = (x - hi.astype(F32)).astype(BF16)
    return hi, lo


def _gla_kernel(g_ref, w2_ref, b2_ref, gw_ref, q_ref, k_ref, v_ref, gg_ref, o_ref,
                kdec_ref, tot_ref, kv_ref, st_ref):
    z = jnp.dot(g_ref[...].astype(BF16), w2_ref[...], preferred_element_type=F32) + b2_ref[...]
    log_a = (jnp.minimum(z, 0.0) - jnp.log(1.0 + jnp.exp(-jnp.abs(z)))) * (1.0 / GLA_GATE_TAU)

    r = lax.broadcasted_iota(jnp.int32, (GLA_BLK, GLA_BLK), 0)
    c = lax.broadcasted_iota(jnp.int32, (GLA_BLK, GLA_BLK), 1)
    after = jnp.where(((r // CHUNK) == (c // CHUNK)) & (c > r), 1.0, 0.0).astype(BF16)
    for blk in range(SEQ // GLA_BLK):
        rows = pl.ds(blk * GLA_BLK, GLA_BLK)
        la = log_a[blk * GLA_BLK:(blk + 1) * GLA_BLK, :]
        hi, lo = _split_hi_lo(la)
        both = jnp.dot(after, jnp.concatenate([hi, lo], axis=1), preferred_element_type=F32)
        suffix = both[:, :GLA_K_DIM] + both[:, GLA_K_DIM:]
        kdec_ref[rows, :] = (k_ref[0, rows, :].astype(F32) * jnp.exp(suffix)).astype(BF16)
        tot_ref[rows, :] = suffix + la

    dec_rows = jnp.exp(tot_ref[pl.ds(0, GLA_NCHUNK, stride=CHUNK), :])
    dec_cols = jnp.concatenate(
        [dec_rows, jnp.zeros((GLA_K_DIM - GLA_NCHUNK, GLA_K_DIM), F32)], axis=0).T

    for ci in range(GLA_NCHUNK):
        rows = pl.ds(ci * CHUNK, CHUNK)
        v_c = jnp.concatenate([v_ref[0, rows, :], v_ref[1, rows, :]], axis=1)
        kv_ref[ci] = _tn_dot(kdec_ref[rows, :], v_c)

    state = jnp.zeros((GLA_K_DIM, GLA_V_DIM), F32)
    for ci in range(GLA_NCHUNK):
        state = state * dec_cols[:, ci:ci + 1] + kv_ref[ci]
        st_ref[ci] = state.astype(BF16)

    gw = gw_ref[...]
    for ci in range(GLA_NCHUNK):
        rows = pl.ds(ci * CHUNK, CHUNK)
        o = jnp.dot(q_ref[0, rows, :], st_ref[ci], preferred_element_type=F32)
        o = o * (GLA_K_DIM ** -0.5)
        ms = jnp.mean(o * o, axis=-1, keepdims=True)
        y = o * lax.rsqrt(ms + NORM_EPS) * gw
        gate = jnp.concatenate([gg_ref[0, rows, :], gg_ref[1, rows, :]], axis=1).astype(F32)
        y = y * (gate * jax.nn.sigmoid(gate))
        o_ref[0, rows, :] = y[:, :LANES].astype(BF16)
        o_ref[1, rows, :] = y[:, LANES:].astype(BF16)


def _gla_call(p, g, w2_pad, b2, gw):
    one = lambda cb0: pl.BlockSpec((1, SEQ, LANES), lambda b, h: (cb0 + h, b, 0))
    two = lambda cb0: pl.BlockSpec((2, SEQ, LANES), lambda b, h: (cb0 // 2 + h, b, 0))
    return pl.pallas_call(
        _gla_kernel,
        grid=(BATCH, GLA_HEADS),
        in_specs=[
            pl.BlockSpec((SEQ, LANES), lambda b, h: (b, 0)),
            pl.BlockSpec((LANES, GLA_K_DIM), lambda b, h: (0, h)),
            pl.BlockSpec((1, GLA_K_DIM), lambda b, h: (0, h)),
            pl.BlockSpec((1, GLA_V_DIM), lambda b, h: (0, 0)),
            one(CB_GQ), one(CB_GK), two(CB_GV), two(CB_GG),
        ],
        out_specs=pl.BlockSpec((2, SEQ, LANES), lambda b, h: (h, b, 0)),
        out_shape=jax.ShapeDtypeStruct((2 * GLA_HEADS, TOKENS, LANES), BF16),
        scratch_shapes=[
            pltpu.VMEM((SEQ, GLA_K_DIM), BF16),
            pltpu.VMEM((SEQ, GLA_K_DIM), F32),
            pltpu.VMEM((GLA_NCHUNK, GLA_K_DIM, GLA_V_DIM), F32),
            pltpu.VMEM((GLA_NCHUNK, GLA_K_DIM, GLA_V_DIM), BF16),
        ],
        compiler_params=pltpu.CompilerParams(
            dimension_semantics=("arbitrary", "arbitrary"), vmem_limit_bytes=VMEM_LIMIT),
        name="gla_scan",
    )(g, w2_pad, b2, gw, p, p, p, p)


OUT_TM = 512
OUT_SUB = 256


def _out_kernel(a_ref, b_ref, w_ref, x_ref, gate_ref, fw_ref, o_ref, wb_ref):
    @pl.when(pl.program_id(0) == 0)
    def _():
        wb_ref[...] = w_ref[...].astype(BF16)

    gate = gate_ref[0]
    fw = fw_ref[...]
    for r in range(OUT_TM // OUT_SUB):
        rows = pl.ds(r * OUT_SUB, OUT_SUB)
        mix = jnp.concatenate(
            [a_ref[h, rows, :] for h in range(a_ref.shape[0])]
            + [b_ref[h, rows, :] for h in range(b_ref.shape[0])], axis=1)
        y = jnp.dot(mix, wb_ref[...], preferred_element_type=F32)
        h_res = x_ref[rows, :] + gate * y
        ms = jnp.mean(h_res * h_res, axis=-1, keepdims=True)
        o_ref[rows, :] = h_res * lax.rsqrt(ms + NORM_EPS) * fw


def _out_call(a, b, w_out, xf, gate, fw):
    per_seq = SEQ // OUT_TM
    return pl.pallas_call(
        _out_kernel,
        grid=(TOKENS // OUT_TM,),
        in_specs=[
            pl.BlockSpec((DIFF_HEADS, OUT_TM, LANES), lambda i: (0, i, 0)),
            pl.BlockSpec((2 * GLA_HEADS, OUT_TM, LANES), lambda i: (0, i, 0)),
            pl.BlockSpec((D_MODEL, D_MODEL), lambda i: (0, 0), pipeline_mode=pl.Buffered(1)),
            pl.BlockSpec((OUT_TM, D_MODEL), lambda i: (i, 0)),
            pl.BlockSpec((1, 1, D_MODEL), lambda i: (i // per_seq, 0, 0)),
            pl.BlockSpec((1, D_MODEL), lambda i: (0, 0)),
        ],
        out_specs=pl.BlockSpec((OUT_TM, D_MODEL), lambda i: (i, 0)),
        out_shape=jax.ShapeDtypeStruct((TOKENS, D_MODEL), F32),
        scratch_shapes=[pltpu.VMEM((D_MODEL, D_MODEL), BF16)],
        compiler_params=pltpu.CompilerParams(
            dimension_semantics=("arbitrary",), vmem_limit_bytes=VMEM_LIMIT),
        name="out_proj_norm",
    )(a, b, w_out, xf, gate, fw)


def _rope_tables():
    inv_freq = ROPE_THETA ** (-jnp.arange(0, DIFF_QK_DIM, 2, dtype=F32) / DIFF_QK_DIM)
    ang = jnp.arange(SEQ, dtype=F32)[:, None] * inv_freq[None, :]
    cos, sin = jnp.cos(ang), jnp.sin(ang)
    return jnp.tile(cos, (1, 4)), jnp.concatenate([-sin, sin, -sin, sin], axis=1)


def kernel(x, c, norm_w, w_ada, b_ada, w_in, lambda_q1, lambda_k1, lambda_q2, lambda_k2,
           diff_norm_w, gla_gate_w2, gla_gate_b, gla_norm_w, w_out, final_norm_w):
    assert x.shape == (BATCH, SEQ, D_MODEL) and w_in.shape[0] == 1
    xf = x.reshape(TOKENS, D_MODEL).astype(F32)

    c_pad = jnp.pad(c.astype(F32), ((0, 8 - BATCH), (0, 0)))
    mod = _ada_call(c_pad, w_ada[0], b_ada[0][None, :])[:BATCH]
    shift = mod[:, :D_MODEL].reshape(BATCH, 1, D_MODEL)
    scale = mod[:, D_MODEL:2 * D_MODEL].reshape(BATCH, 1, D_MODEL)
    gate = mod[:, 2 * D_MODEL:].reshape(BATCH, 1, D_MODEL)

    w_t = w_in[0].T
    w_gate = jnp.pad(w_t[N_MAIN:], ((0, LANES - GLA_GATE_RANK), (0, 0))).astype(BF16)
    cos, sin = _rope_tables()
    p, g = _proj_call(xf, shift, scale, norm_w[0][None, :], w_t, w_gate, cos, sin)

    row = lambda v: v[0][None, :].astype(F32)
    a = _attn_call(p, row(lambda_q1), row(lambda_k1), row(lambda_q2), row(lambda_k2),
                   row(diff_norm_w))

    w2_pad = jnp.pad(gla_gate_w2[0], ((0, LANES - GLA_GATE_RANK), (0, 0))).astype(BF16)
    b = _gla_call(p, g, w2_pad, row(gla_gate_b), row(gla_norm_w))

    out = _out_call(a, b, w_out[0], xf, gate, final_norm_w[None, :].astype(F32))
    return out.reshape(BATCH, SEQ, D_MODEL).astype(x.dtype)
```

```python
import functools
import math

import jax
import jax.numpy as jnp
from jax import lax
from jax.experimental import pallas as pl
from jax.experimental.pallas import tpu as pltpu

D_MODEL = 2048
BATCH = 4
SEQ = 2048
TOKENS = BATCH * SEQ
CHUNK = 64
LANES = 128

DIFF_HEADS = 8
DIFF_HEAD_DIM = 128
DIFF_QK_DIM = 64
GLA_HEADS = 4
GLA_K_DIM = 128
GLA_V_DIM = 256
GLA_GATE_RANK = 16
GLA_GATE_TAU = 16.0
ROPE_THETA = 10000.0
NORM_EPS = 1e-6
LAMBDA_INIT = 0.8 - 0.6 * math.exp(-0.3 * 0)

N_MAIN = 7168
N_COLBLK = N_MAIN // LANES
CB_DQ, CB_DK, CB_DV, CB_DG = 0, 8, 16, 24
CB_GQ, CB_GK, CB_GV, CB_GG = 32, 36, 40, 48

VMEM_LIMIT = 48 * 1024 * 1024
NEG_BIG = -1e30
LOG2_E = math.log2(math.e)

BF16 = jnp.bfloat16
F32 = jnp.float32


def _nt_dot(a, b):
    return lax.dot_general(a, b, (((1,), (1,)), ((), ())), preferred_element_type=F32)


def _tn_dot(a, b):
    return lax.dot_general(a, b, (((0,), (0,)), ((), ())), preferred_element_type=F32)


ADA_TN = 768


def _ada_kernel(c_ref, w_ref, b_ref, o_ref):
    c = c_ref[...]
    c_act = (c * jax.nn.sigmoid(c)).astype(BF16)
    w = w_ref[...].astype(BF16)
    o_ref[...] = jnp.dot(c_act, w, preferred_element_type=F32) + b_ref[...]


def _ada_call(c_pad, w_ada, b_ada):
    n = w_ada.shape[1]
    return pl.pallas_call(
        _ada_kernel,
        grid=(n // ADA_TN,),
        in_specs=[
            pl.BlockSpec((8, D_MODEL), lambda j: (0, 0)),
            pl.BlockSpec((D_MODEL, ADA_TN), lambda j: (0, j)),
            pl.BlockSpec((1, ADA_TN), lambda j: (0, j)),
        ],
        out_specs=pl.BlockSpec((8, ADA_TN), lambda j: (0, j)),
        out_shape=jax.ShapeDtypeStruct((8, n), F32),
        compiler_params=pltpu.CompilerParams(
            dimension_semantics=("arbitrary",), vmem_limit_bytes=VMEM_LIMIT),
        name="ada_mod",
    )(c_pad, w_ada, b_ada)


PROJ_TM = 1024
PROJ_TN = 1024
PROJ_SUB = 256
PROJ_ROPE_TILES = 2
PROJ_VMEM_LIMIT = (2 * PROJ_TM * D_MODEL * 4 + 2 * D_MODEL * PROJ_TN * 4
                   + PROJ_TM * D_MODEL * 2 + 2 * PROJ_TM * PROJ_TN * 2 + 16 * 1024 * 1024)


def _rot_half(x):
    lane = lax.broadcasted_iota(jnp.int32, x.shape, 1)
    first = (lane % DIFF_QK_DIM) < (DIFF_QK_DIM // 2)
    return jnp.where(first, pltpu.roll(x, LANES - 32, 1), pltpu.roll(x, 32, 1))


def _proj_kernel(x_ref, shift_ref, scale_ref, nw_ref, w_ref, wg_ref, cos_ref, sin_ref,
                 p_ref, g_ref, hn_ref):
    j = pl.program_id(1)
    n_sub = PROJ_TM // PROJ_SUB
    n_cb = PROJ_TN // LANES

    def norm_rows(rows):
        mult = nw_ref[...] * (1.0 + scale_ref[0])
        xs = x_ref[rows, :]
        ms = jnp.mean(xs * xs, axis=-1, keepdims=True)
        hn_ref[rows, :] = (xs * lax.rsqrt(ms + NORM_EPS) * mult + shift_ref[0]).astype(BF16)

    def store_rope(rows, acc, sc):
        cos = cos_ref[rows, :] * sc
        sin = sin_ref[rows, :] * sc
        for cb in range(n_cb):
            t = acc[:, cb * LANES:(cb + 1) * LANES]
            p_ref[cb, rows, :] = (t * cos + _rot_half(t) * sin).astype(BF16)

    @pl.when(j == 0)
    def _():
        for r in range(n_sub):
            rows = pl.ds(r * PROJ_SUB, PROJ_SUB)
            norm_rows(rows)
            acc = _nt_dot(hn_ref[rows, :], w_ref[...])
            store_rope(rows, acc, DIFF_QK_DIM ** -0.5 * LOG2_E)
        g_ref[...] = _nt_dot(hn_ref[...], wg_ref[...])

    @pl.when(j == 1)
    def _():
        for r in range(n_sub):
            rows = pl.ds(r * PROJ_SUB, PROJ_SUB)
            acc = _nt_dot(hn_ref[rows, :], w_ref[...])
            store_rope(rows, acc, 1.0)

    @pl.when(j >= PROJ_ROPE_TILES)
    def _():
        for r in range(n_sub):
            rows = pl.ds(r * PROJ_SUB, PROJ_SUB)
            acc = _nt_dot(hn_ref[rows, :], w_ref[...])
            for cb in range(n_cb):
                p_ref[cb, rows, :] = acc[:, cb * LANES:(cb + 1) * LANES].astype(BF16)


def _proj_call(xf, shift, scale, norm_w, w_t, w_gate, cos, sin):
    m_tiles = TOKENS // PROJ_TM
    n_tiles = N_MAIN // PROJ_TN
    per_seq = SEQ // PROJ_TM
    n_cb = PROJ_TN // LANES
    return pl.pallas_call(
        _proj_kernel,
        grid=(m_tiles, n_tiles),
        in_specs=[
            pl.BlockSpec((PROJ_TM, D_MODEL), lambda i, j: (i, 0)),
            pl.BlockSpec((1, 1, D_MODEL), lambda i, j: (i // per_seq, 0, 0)),
            pl.BlockSpec((1, 1, D_MODEL), lambda i, j: (i // per_seq, 0, 0)),
            pl.BlockSpec((1, D_MODEL), lambda i, j: (0, 0)),
            pl.BlockSpec((PROJ_TN, D_MODEL), lambda i, j: (j, 0)),
            pl.BlockSpec((LANES, D_MODEL), lambda i, j: (0, 0)),
            pl.BlockSpec((PROJ_TM, LANES), lambda i, j: (i % per_seq, 0)),
            pl.BlockSpec((PROJ_TM, LANES), lambda i, j: (i % per_seq, 0)),
        ],
        out_specs=[
            pl.BlockSpec((n_cb, PROJ_TM, LANES), lambda i, j: (j, i, 0)),
            pl.BlockSpec((PROJ_TM, LANES), lambda i, j: (i, 0)),
        ],
        out_shape=[
            jax.ShapeDtypeStruct((N_COLBLK, TOKENS, LANES), BF16),
            jax.ShapeDtypeStruct((TOKENS, LANES), F32),
        ],
        scratch_shapes=[pltpu.VMEM((PROJ_TM, D_MODEL), BF16)],
        compiler_params=pltpu.CompilerParams(
            dimension_semantics=("arbitrary", "arbitrary"), vmem_limit_bytes=PROJ_VMEM_LIMIT),
        name="norm_in_proj",
    )(xf, shift, scale, norm_w, w_t, w_gate, cos, sin)


ATT_T = 256
ATT_NBLK = SEQ // ATT_T
ATT_LEAD = 24
assert ATT_LEAD >= ATT_NBLK - 1
ATT_ONES_ROWS = 16
ATT_HEADS_PER_STEP = 2


def _attn_kernel(lq1_ref, lk1_ref, lq2_ref, lk2_ref, dnw_ref, q_ref, k_ref, v_ref, dg_ref,
                 o_ref, vt_ref):
    lam = (jnp.exp(jnp.sum(lq1_ref[...] * lk1_ref[...], keepdims=True))
           - jnp.exp(jnp.sum(lq2_ref[...] * lk2_ref[...], keepdims=True))
           + LAMBDA_INIT)
    out_gain = dnw_ref[...] * (1.0 - LAMBDA_INIT)

    for hd in range(ATT_HEADS_PER_STEP):
        for jb in range(ATT_NBLK):
            cols = pl.ds(jb * ATT_T, ATT_T)
            vb = v_ref[hd, jb * ATT_T:(jb + 1) * ATT_T, :].astype(F32)
            vt_ref[hd, 0:DIFF_HEAD_DIM, cols] = vb.T.astype(BF16)
        vt_ref[hd, DIFF_HEAD_DIM:, :] = jnp.ones((ATT_ONES_ROWS, SEQ), BF16)

    lane = lax.broadcasted_iota(jnp.int32, (ATT_T, LANES), 1)
    comp_a = lane < DIFF_QK_DIM
    krow = lax.broadcasted_iota(jnp.int32, (ATT_T, ATT_T), 0) // CHUNK
    qcol = lax.broadcasted_iota(jnp.int32, (ATT_T, ATT_T), 1) // CHUNK
    diag_mask = krow <= qcol
    diag_mask2 = jnp.concatenate([diag_mask, diag_mask], axis=1)

    units = [(hd, qi, j) for hd in range(ATT_HEADS_PER_STEP)
             for qi in range(ATT_NBLK) for j in range(qi + 1)]
    blocks = {(hd, qi): {"m": None, "pv": None, "s": {}}
              for hd in range(ATT_HEADS_PER_STEP) for qi in range(ATT_NBLK)}

    def block_queries(hd, qi):
        q = q_ref[hd, pl.ds(qi * ATT_T, ATT_T), :]
        zero = jnp.zeros_like(q)
        return jnp.concatenate([jnp.where(comp_a, q, zero), jnp.where(comp_a, zero, q)], axis=0)

    def score_unit(hd, qi, j):
        st = blocks[hd, qi]
        if j == 0:
            st["q2"] = block_queries(hd, qi)
        s = _nt_dot(k_ref[hd, pl.ds(j * ATT_T, ATT_T), :], st["q2"])
        if j == qi:
            s = jnp.where(diag_mask2, s, NEG_BIG)
        st["s"][j] = s
        mj = jnp.max(s, axis=0, keepdims=True)
        st["m"] = mj if st["m"] is None else jnp.maximum(st["m"], mj)

    def value_unit(hd, qi, j):
        st = blocks[hd, qi]
        p = jnp.exp2(st["s"].pop(j) - st["m"])
        pvj = jnp.dot(vt_ref[hd, :, pl.ds(j * ATT_T, ATT_T)], p.astype(BF16),
                      preferred_element_type=F32)
        st["pv"] = pvj if st["pv"] is None else st["pv"] + pvj
        if j == qi:
            finish(hd, qi, st["pv"])

    def finish(hd, qi, pv_l):
        rows = pl.ds(qi * ATT_T, ATT_T)
        l = pv_l[DIFF_HEAD_DIM:DIFF_HEAD_DIM + 1, :]
        pv = pv_l[0:DIFF_HEAD_DIM, :] * (1.0 / l)
        out_t = pv[:, :ATT_T] - lam * pv[:, ATT_T:]
        out = out_t.T
        ms = jnp.mean(out * out, axis=-1, keepdims=True)
        y = out * lax.rsqrt(ms + NORM_EPS) * out_gain
        g = dg_ref[hd, rows, :].astype(F32)
        o_ref[hd, rows, :] = (y * (g * jax.nn.sigmoid(g))).astype(BF16)

    for t in range(len(units) + ATT_LEAD):
        if t < len(units):
            score_unit(*units[t])
        if t >= ATT_LEAD:
            value_unit(*units[t - ATT_LEAD])


def _attn_call(p, lq1, lk1, lq2, lk2, dnw):
    hp = ATT_HEADS_PER_STEP
    vec = lambda n: pl.BlockSpec((1, n), lambda b, h: (0, 0))
    slab = lambda cb0: pl.BlockSpec((hp, SEQ, LANES), lambda b, h: (cb0 // hp + h, b, 0))
    return pl.pallas_call(
        _attn_kernel,
        grid=(BATCH, DIFF_HEADS // hp),
        in_specs=[vec(DIFF_QK_DIM)] * 4 + [vec(DIFF_HEAD_DIM),
                  slab(CB_DQ), slab(CB_DK), slab(CB_DV), slab(CB_DG)],
        out_specs=pl.BlockSpec((hp, SEQ, LANES), lambda b, h: (h, b, 0)),
        out_shape=jax.ShapeDtypeStruct((DIFF_HEADS, TOKENS, LANES), BF16),
        scratch_shapes=[
            pltpu.VMEM((hp, DIFF_HEAD_DIM + ATT_ONES_ROWS, SEQ), BF16),
        ],
        compiler_params=pltpu.CompilerParams(
            dimension_semantics=("arbitrary", "arbitrary"), vmem_limit_bytes=VMEM_LIMIT),
        name="diff_attn",
    )(lq1, lk1, lq2, lk2, dnw, p, p, p, p)


GLA_BLK = 256
GLA_NCHUNK = SEQ // CHUNK


def _split_hi_lo(x):
    hi = x.astype(BF16)
    lo = (x - hi.astype(F32)).astype(BF16)
    return hi, lo


def _gla_kernel(g_ref, w2_ref, b2_ref, gw_ref, q_ref, k_ref, v_ref, gg_ref, o_ref,
                kdec_ref, tot_ref, kv_ref, st_ref):
    z = jnp.dot(g_ref[...].astype(BF16), w2_ref[...], preferred_element_type=F32) + b2_ref[...]
    log_a = (jnp.minimum(z, 0.0) - jnp.log(1.0 + jnp.exp(-jnp.abs(z)))) * (1.0 / GLA_GATE_TAU)

    r = lax.broadcasted_iota(jnp.int32, (GLA_BLK, GLA_BLK), 0)
    c = lax.broadcasted_iota(jnp.int32, (GLA_BLK, GLA_BLK), 1)
    after = jnp.where(((r // CHUNK) == (c // CHUNK)) & (c > r), 1.0, 0.0).astype(BF16)
    for blk in range(SEQ // GLA_BLK):
        rows = pl.ds(blk * GLA_BLK, GLA_BLK)
        la = log_a[blk * GLA_BLK:(blk + 1) * GLA_BLK, :]
        hi, lo = _split_hi_lo(la)
        both = jnp.dot(after, jnp.concatenate([hi, lo], axis=1), preferred_element_type=F32)
        suffix = both[:, :GLA_K_DIM] + both[:, GLA_K_DIM:]
        kdec_ref[rows, :] = (k_ref[0, rows, :].astype(F32) * jnp.exp(suffix)).astype(BF16)
        tot_ref[rows, :] = suffix + la

    dec_rows = jnp.exp(tot_ref[pl.ds(0, GLA_NCHUNK, stride=CHUNK), :])
    dec_cols = jnp.concatenate(
        [dec_rows, jnp.zeros((GLA_K_DIM - GLA_NCHUNK, GLA_K_DIM), F32)], axis=0).T

    for ci in range(GLA_NCHUNK):
        rows = pl.ds(ci * CHUNK, CHUNK)
        v_c = jnp.concatenate([v_ref[0, rows, :], v_ref[1, rows, :]], axis=1)
        kv_ref[ci] = _tn_dot(kdec_ref[rows, :], v_c)

    state = jnp.zeros((GLA_K_DIM, GLA_V_DIM), F32)
    for ci in range(GLA_NCHUNK):
        state = state * dec_cols[:, ci:ci + 1] + kv_ref[ci]
        st_ref[ci] = state.astype(BF16)

    gw = gw_ref[...]
    qs = GLA_K_DIM ** -0.5
    for ci in range(GLA_NCHUNK):
        rows = pl.ds(ci * CHUNK, CHUNK)
        o = jnp.dot(q_ref[0, rows, :], st_ref[ci], preferred_element_type=F32)
        ms = jnp.mean(o * o, axis=-1, keepdims=True) * (qs * qs)
        y = o * (qs * lax.rsqrt(ms + NORM_EPS)) * gw
        gate = jnp.concatenate([gg_ref[0, rows, :], gg_ref[1, rows, :]], axis=1).astype(F32)
        y = y * (gate * jax.nn.sigmoid(gate))
        o_ref[0, rows, :] = y[:, :LANES].astype(BF16)
        o_ref[1, rows, :] = y[:, LANES:].astype(BF16)


def _gla_call(p, g, w2_pad, b2, gw):
    one = lambda cb0: pl.BlockSpec((1, SEQ, LANES), lambda b, h: (cb0 + h, b, 0))
    two = lambda cb0: pl.BlockSpec((2, SEQ, LANES), lambda b, h: (cb0 // 2 + h, b, 0))
    return pl.pallas_call(
        _gla_kernel,
        grid=(BATCH, GLA_HEADS),
        in_specs=[
            pl.BlockSpec((SEQ, LANES), lambda b, h: (b, 0)),
            pl.BlockSpec((LANES, GLA_K_DIM), lambda b, h: (0, h)),
            pl.BlockSpec((1, GLA_K_DIM), lambda b, h: (0, h)),
            pl.BlockSpec((1, GLA_V_DIM), lambda b, h: (0, 0)),
            one(CB_GQ), one(CB_GK), two(CB_GV), two(CB_GG),
        ],
        out_specs=pl.BlockSpec((2, SEQ, LANES), lambda b, h: (h, b, 0)),
        out_shape=jax.ShapeDtypeStruct((2 * GLA_HEADS, TOKENS, LANES), BF16),
        scratch_shapes=[
            pltpu.VMEM((SEQ, GLA_K_DIM), BF16),
            pltpu.VMEM((SEQ, GLA_K_DIM), F32),
            pltpu.VMEM((GLA_NCHUNK, GLA_K_DIM, GLA_V_DIM), F32),
            pltpu.VMEM((GLA_NCHUNK, GLA_K_DIM, GLA_V_DIM), BF16),
        ],
        compiler_params=pltpu.CompilerParams(
            dimension_semantics=("arbitrary", "arbitrary"), vmem_limit_bytes=VMEM_LIMIT),
        name="gla_scan",
    )(g, w2_pad, b2, gw, p, p, p, p)


OUT_TM = 512
OUT_SUB = 256


def _out_kernel(a_ref, b_ref, w_ref, x_ref, gate_ref, fw_ref, o_ref, wb_ref):
    @pl.when(pl.program_id(0) == 0)
    def _():
        wb_ref[...] = w_ref[...].astype(BF16)

    gate = gate_ref[0]
    fw = fw_ref[...]
    for r in range(OUT_TM // OUT_SUB):
        rows = pl.ds(r * OUT_SUB, OUT_SUB)
        mix = jnp.concatenate(
            [a_ref[h, rows, :] for h in range(a_ref.shape[0])]
            + [b_ref[h, rows, :] for h in range(b_ref.shape[0])], axis=1)
        y = jnp.dot(mix, wb_ref[...], preferred_element_type=F32)
        h_res = x_ref[rows, :] + gate * y
        ms = jnp.mean(h_res * h_res, axis=-1, keepdims=True)
        o_ref[rows, :] = h_res * lax.rsqrt(ms + NORM_EPS) * fw


def _out_call(a, b, w_out, xf, gate, fw):
    per_seq = SEQ // OUT_TM
    return pl.pallas_call(
        _out_kernel,
        grid=(TOKENS // OUT_TM,),
        in_specs=[
            pl.BlockSpec((DIFF_HEADS, OUT_TM, LANES), lambda i: (0, i, 0)),
            pl.BlockSpec((2 * GLA_HEADS, OUT_TM, LANES), lambda i: (0, i, 0)),
            pl.BlockSpec((D_MODEL, D_MODEL), lambda i: (0, 0), pipeline_mode=pl.Buffered(1)),
            pl.BlockSpec((OUT_TM, D_MODEL), lambda i: (i, 0)),
            pl.BlockSpec((1, 1, D_MODEL), lambda i: (i // per_seq, 0, 0)),
            pl.BlockSpec((1, D_MODEL), lambda i: (0, 0)),
        ],
        out_specs=pl.BlockSpec((OUT_TM, D_MODEL), lambda i: (i, 0)),
        out_shape=jax.ShapeDtypeStruct((TOKENS, D_MODEL), F32),
        scratch_shapes=[pltpu.VMEM((D_MODEL, D_MODEL), BF16)],
        compiler_params=pltpu.CompilerParams(
            dimension_semantics=("arbitrary",), vmem_limit_bytes=VMEM_LIMIT),
        name="out_proj_norm",
    )(a, b, w_out, xf, gate, fw)


def _rope_tables():
    inv_freq = ROPE_THETA ** (-jnp.arange(0, DIFF_QK_DIM, 2, dtype=F32) / DIFF_QK_DIM)
    ang = jnp.arange(SEQ, dtype=F32)[:, None] * inv_freq[None, :]
    cos, sin = jnp.cos(ang), jnp.sin(ang)
    return jnp.tile(cos, (1, 4)), jnp.concatenate([-sin, sin, -sin, sin], axis=1)


def kernel(x, c, norm_w, w_ada, b_ada, w_in, lambda_q1, lambda_k1, lambda_q2, lambda_k2,
           diff_norm_w, gla_gate_w2, gla_gate_b, gla_norm_w, w_out, final_norm_w):
    assert x.shape == (BATCH, SEQ, D_MODEL) and w_in.shape[0] == 1
    xf = x.reshape(TOKENS, D_MODEL).astype(F32)

    c_pad = jnp.pad(c.astype(F32), ((0, 8 - BATCH), (0, 0)))
    mod = _ada_call(c_pad, w_ada[0], b_ada[0][None, :])[:BATCH]
    shift = mod[:, :D_MODEL].reshape(BATCH, 1, D_MODEL)
    scale = mod[:, D_MODEL:2 * D_MODEL].reshape(BATCH, 1, D_MODEL)
    gate = mod[:, 2 * D_MODEL:].reshape(BATCH, 1, D_MODEL)

    w_t = w_in[0].T
    w_gate = jnp.pad(w_t[N_MAIN:], ((0, LANES - GLA_GATE_RANK), (0, 0))).astype(BF16)
    cos, sin = _rope_tables()
    p, g = _proj_call(xf, shift, scale, norm_w[0][None, :], w_t, w_gate, cos, sin)

    row = lambda v: v[0][None, :].astype(F32)
    a = _attn_call(p, row(lambda_q1), row(lambda_k1), row(lambda_q2), row(lambda_k2),
                   row(diff_norm_w))

    w2_pad = jnp.pad(gla_gate_w2[0], ((0, LANES - GLA_GATE_RANK), (0, 0))).astype(BF16)
    b = _gla_call(p, g, w2_pad, row(gla_gate_b), row(gla_norm_w))

    out = _out_call(a, b, w_out[0], xf, gate, final_norm_w[None, :].astype(F32))
    return out.reshape(BATCH, SEQ, D_MODEL).astype(x.dtype)
```

```python
import functools
import math

import jax
import jax.numpy as jnp
from jax import lax
from jax.experimental import pallas as pl
from jax.experimental.pallas import tpu as pltpu

D_MODEL = 2048
BATCH = 4
SEQ = 2048
TOKENS = BATCH * SEQ
CHUNK = 64
LANES = 128

DIFF_HEADS = 8
DIFF_HEAD_DIM = 128
DIFF_QK_DIM = 64
GLA_HEADS = 4
GLA_K_DIM = 128
GLA_V_DIM = 256
GLA_GATE_RANK = 16
GLA_GATE_TAU = 16.0
ROPE_THETA = 10000.0
NORM_EPS = 1e-6
LAMBDA_INIT = 0.8 - 0.6 * math.exp(-0.3 * 0)

N_MAIN = 7168
N_COLBLK = N_MAIN // LANES
CB_DQ, CB_DK, CB_DV, CB_DG = 0, 8, 16, 24
CB_GQ, CB_GK, CB_GV, CB_GG = 32, 36, 40, 48

VMEM_LIMIT = 48 * 1024 * 1024
NEG_BIG = -1e30
LOG2_E = math.log2(math.e)

BF16 = jnp.bfloat16
F32 = jnp.float32


def _nt_dot(a, b):
    return lax.dot_general(a, b, (((1,), (1,)), ((), ())), preferred_element_type=F32)


def _tn_dot(a, b):
    return lax.dot_general(a, b, (((0,), (0,)), ((), ())), preferred_element_type=F32)


ADA_TN = 768


def _ada_kernel(c_ref, w_ref, b_ref, o_ref):
    c = c_ref[...]
    c_act = (c * jax.nn.sigmoid(c)).astype(BF16)
    w = w_ref[...].astype(BF16)
    o_ref[...] = jnp.dot(c_act, w, preferred_element_type=F32) + b_ref[...]


def _ada_call(c_pad, w_ada, b_ada):
    n = w_ada.shape[1]
    return pl.pallas_call(
        _ada_kernel,
        grid=(n // ADA_TN,),
        in_specs=[
            pl.BlockSpec((8, D_MODEL), lambda j: (0, 0)),
            pl.BlockSpec((D_MODEL, ADA_TN), lambda j: (0, j)),
            pl.BlockSpec((1, ADA_TN), lambda j: (0, j)),
        ],
        out_specs=pl.BlockSpec((8, ADA_TN), lambda j: (0, j)),
        out_shape=jax.ShapeDtypeStruct((8, n), F32),
        compiler_params=pltpu.CompilerParams(
            dimension_semantics=("arbitrary",), vmem_limit_bytes=VMEM_LIMIT),
        name="ada_mod",
    )(c_pad, w_ada, b_ada)


PROJ_TM = 1024
PROJ_TN = 1024
PROJ_SUB = 512
PROJ_ROPE_TILES = 2
PROJ_VMEM_LIMIT = (2 * PROJ_TM * D_MODEL * 4 + 2 * D_MODEL * PROJ_TN * 4
                   + PROJ_TM * D_MODEL * 2 + 2 * PROJ_TM * PROJ_TN * 2 + 16 * 1024 * 1024)


def _rot_half(x):
    lane = lax.broadcasted_iota(jnp.int32, x.shape, 1)
    first = (lane % DIFF_QK_DIM) < (DIFF_QK_DIM // 2)
    return jnp.where(first, pltpu.roll(x, LANES - 32, 1), pltpu.roll(x, 32, 1))


def _proj_kernel(x_ref, shift_ref, scale_ref, nw_ref, w_ref, wg_ref, cos_ref, sin_ref,
                 p_ref, g_ref, hn_ref):
    j = pl.program_id(1)
    n_sub = PROJ_TM // PROJ_SUB
    n_cb = PROJ_TN // LANES

    def norm_rows(rows):
        mult = nw_ref[...] * (1.0 + scale_ref[0])
        xs = x_ref[rows, :]
        ms = jnp.mean(xs * xs, axis=-1, keepdims=True)
        hn_ref[rows, :] = (xs * lax.rsqrt(ms + NORM_EPS) * mult + shift_ref[0]).astype(BF16)

    def store_rope(rows, acc, sc):
        cos = cos_ref[rows, :] * sc
        sin = sin_ref[rows, :] * sc
        for cb in range(n_cb):
            t = acc[:, cb * LANES:(cb + 1) * LANES]
            p_ref[cb, rows, :] = (t * cos + _rot_half(t) * sin).astype(BF16)

    @pl.when(j == 0)
    def _():
        for r in range(n_sub):
            rows = pl.ds(r * PROJ_SUB, PROJ_SUB)
            norm_rows(rows)
            acc = _nt_dot(hn_ref[rows, :], w_ref[...])
            store_rope(rows, acc, DIFF_QK_DIM ** -0.5 * LOG2_E)
        g_ref[...] = _nt_dot(hn_ref[...], wg_ref[...])

    @pl.when(j == 1)
    def _():
        for r in range(n_sub):
            rows = pl.ds(r * PROJ_SUB, PROJ_SUB)
            acc = _nt_dot(hn_ref[rows, :], w_ref[...])
            store_rope(rows, acc, 1.0)

    @pl.when(j >= PROJ_ROPE_TILES)
    def _():
        for r in range(n_sub):
            rows = pl.ds(r * PROJ_SUB, PROJ_SUB)
            acc = _nt_dot(hn_ref[rows, :], w_ref[...])
            for cb in range(n_cb):
                p_ref[cb, rows, :] = acc[:, cb * LANES:(cb + 1) * LANES].astype(BF16)


def _proj_call(xf, shift, scale, norm_w, w_t, w_gate, cos, sin):
    m_tiles = TOKENS // PROJ_TM
    n_tiles = N_MAIN // PROJ_TN
    per_seq = SEQ // PROJ_TM
    n_cb = PROJ_TN // LANES
    return pl.pallas_call(
        _proj_kernel,
        grid=(m_tiles, n_tiles),
        in_specs=[
            pl.BlockSpec((PROJ_TM, D_MODEL), lambda i, j: (i, 0)),
            pl.BlockSpec((1, 1, D_MODEL), lambda i, j: (i // per_seq, 0, 0)),
            pl.BlockSpec((1, 1, D_MODEL), lambda i, j: (i // per_seq, 0, 0)),
            pl.BlockSpec((1, D_MODEL), lambda i, j: (0, 0)),
            pl.BlockSpec((PROJ_TN, D_MODEL), lambda i, j: (j, 0)),
            pl.BlockSpec((LANES, D_MODEL), lambda i, j: (0, 0)),
            pl.BlockSpec((PROJ_TM, LANES), lambda i, j: (i % per_seq, 0)),
            pl.BlockSpec((PROJ_TM, LANES), lambda i, j: (i % per_seq, 0)),
        ],
        out_specs=[
            pl.BlockSpec((n_cb, PROJ_TM, LANES), lambda i, j: (j, i, 0)),
            pl.BlockSpec((PROJ_TM, LANES), lambda i, j: (i, 0)),
        ],
        out_shape=[
            jax.ShapeDtypeStruct((N_COLBLK, TOKENS, LANES), BF16),
            jax.ShapeDtypeStruct((TOKENS, LANES), F32),
        ],
        scratch_shapes=[pltpu.VMEM((PROJ_TM, D_MODEL), BF16)],
        compiler_params=pltpu.CompilerParams(
            dimension_semantics=("arbitrary", "arbitrary"), vmem_limit_bytes=PROJ_VMEM_LIMIT),
        name="norm_in_proj",
    )(xf, shift, scale, norm_w, w_t, w_gate, cos, sin)


ATT_T = 256
ATT_NBLK = SEQ // ATT_T
ATT_LEAD = 24
assert ATT_LEAD >= ATT_NBLK - 1
ATT_ONES_ROWS = 16
ATT_HEADS_PER_STEP = 2


def _attn_kernel(lq1_ref, lk1_ref, lq2_ref, lk2_ref, dnw_ref, q_ref, k_ref, v_ref, dg_ref,
                 o_ref, vt_ref):
    lam = (jnp.exp(jnp.sum(lq1_ref[...] * lk1_ref[...], keepdims=True))
           - jnp.exp(jnp.sum(lq2_ref[...] * lk2_ref[...], keepdims=True))
           + LAMBDA_INIT)
    out_gain = dnw_ref[...] * (1.0 - LAMBDA_INIT)

    for hd in range(ATT_HEADS_PER_STEP):
        for jb in range(ATT_NBLK):
            cols = pl.ds(jb * ATT_T, ATT_T)
            vb = v_ref[hd, jb * ATT_T:(jb + 1) * ATT_T, :].astype(F32)
            vt_ref[hd, 0:DIFF_HEAD_DIM, cols] = vb.T.astype(BF16)
        vt_ref[hd, DIFF_HEAD_DIM:, :] = jnp.ones((ATT_ONES_ROWS, SEQ), BF16)

    lane = lax.broadcasted_iota(jnp.int32, (ATT_T, LANES), 1)
    comp_a = lane < DIFF_QK_DIM
    krow = lax.broadcasted_iota(jnp.int32, (ATT_T, ATT_T), 0) // CHUNK
    qcol = lax.broadcasted_iota(jnp.int32, (ATT_T, ATT_T), 1) // CHUNK
    diag_mask = krow <= qcol
    diag_mask2 = jnp.concatenate([diag_mask, diag_mask], axis=1)

    units = [(hd, qi, j) for hd in range(ATT_HEADS_PER_STEP)
             for qi in range(ATT_NBLK) for j in range(qi + 1)]
    blocks = {(hd, qi): {"m": None, "pv": None, "s": {}}
              for hd in range(ATT_HEADS_PER_STEP) for qi in range(ATT_NBLK)}

    def block_queries(hd, qi):
        q = q_ref[hd, pl.ds(qi * ATT_T, ATT_T), :]
        zero = jnp.zeros_like(q)
        return jnp.concatenate([jnp.where(comp_a, q, zero), jnp.where(comp_a, zero, q)], axis=0)

    def score_unit(hd, qi, j):
        st = blocks[hd, qi]
        if j == 0:
            st["q2"] = block_queries(hd, qi)
        s = _nt_dot(k_ref[hd, pl.ds(j * ATT_T, ATT_T), :], st["q2"])
        if j == qi:
            s = jnp.where(diag_mask2, s, NEG_BIG)
        st["s"][j] = s
        mj = jnp.max(s, axis=0, keepdims=True)
        st["m"] = mj if st["m"] is None else jnp.maximum(st["m"], mj)

    def value_unit(hd, qi, j):
        st = blocks[hd, qi]
        p = jnp.exp2(st["s"].pop(j) - st["m"])
        pvj = jnp.dot(vt_ref[hd, :, pl.ds(j * ATT_T, ATT_T)], p.astype(BF16),
                      preferred_element_type=F32)
        st["pv"] = pvj if st["pv"] is None else st["pv"] + pvj
        if j == qi:
            finish(hd, qi, st["pv"])

    def finish(hd, qi, pv_l):
        rows = pl.ds(qi * ATT_T, ATT_T)
        l = pv_l[DIFF_HEAD_DIM:DIFF_HEAD_DIM + 1, :]
        pv = pv_l[0:DIFF_HEAD_DIM, :] * (1.0 / l)
        out_t = pv[:, :ATT_T] - lam * pv[:, ATT_T:]
        out = out_t.T
        ms = jnp.mean(out * out, axis=-1, keepdims=True)
        y = out * lax.rsqrt(ms + NORM_EPS) * out_gain
        g = dg_ref[hd, rows, :].astype(F32)
        o_ref[hd, rows, :] = (y * (g * jax.nn.sigmoid(g))).astype(BF16)

    for t in range(len(units) + ATT_LEAD):
        if t < len(units):
            score_unit(*units[t])
        if t >= ATT_LEAD:
            value_unit(*units[t - ATT_LEAD])


def _attn_call(p, lq1, lk1, lq2, lk2, dnw):
    hp = ATT_HEADS_PER_STEP
    vec = lambda n: pl.BlockSpec((1, n), lambda b, h: (0, 0))
    slab = lambda cb0: pl.BlockSpec((hp, SEQ, LANES), lambda b, h: (cb0 // hp + h, b, 0))
    return pl.pallas_call(
        _attn_kernel,
        grid=(BATCH, DIFF_HEADS // hp),
        in_specs=[vec(DIFF_QK_DIM)] * 4 + [vec(DIFF_HEAD_DIM),
                  slab(CB_DQ), slab(CB_DK), slab(CB_DV), slab(CB_DG)],
        out_specs=pl.BlockSpec((hp, SEQ, LANES), lambda b, h: (h, b, 0)),
        out_shape=jax.ShapeDtypeStruct((DIFF_HEADS, TOKENS, LANES), BF16),
        scratch_shapes=[
            pltpu.VMEM((hp, DIFF_HEAD_DIM + ATT_ONES_ROWS, SEQ), BF16),
        ],
        compiler_params=pltpu.CompilerParams(
            dimension_semantics=("arbitrary", "arbitrary"), vmem_limit_bytes=VMEM_LIMIT),
        name="diff_attn",
    )(lq1, lk1, lq2, lk2, dnw, p, p, p, p)


GLA_BLK = 256
GLA_NCHUNK = SEQ // CHUNK


def _split_hi_lo(x):
    hi = x.astype(BF16)
    lo = (x - hi.astype(F32)).astype(BF16)
    return hi, lo


def _gla_kernel(g_ref, w2_ref, b2_ref, gw_ref, q_ref, k_ref, v_ref, gg_ref, o_ref,
                kdec_ref, tot_ref, kv_ref, st_ref):
    z = jnp.dot(g_ref[...].astype(BF16), w2_ref[...], preferred_element_type=F32) + b2_ref[...]
    log_a = (jnp.minimum(z, 0.0) - jnp.log(1.0 + jnp.exp(-jnp.abs(z)))) * (1.0 / GLA_GATE_TAU)

    r = lax.broadcasted_iota(jnp.int32, (GLA_BLK, GLA_BLK), 0)
    c = lax.broadcasted_iota(jnp.int32, (GLA_BLK, GLA_BLK), 1)
    after = jnp.where(((r // CHUNK) == (c // CHUNK)) & (c > r), 1.0, 0.0).astype(BF16)
    for blk in range(SEQ // GLA_BLK):
        rows = pl.ds(blk * GLA_BLK, GLA_BLK)
        la = log_a[blk * GLA_BLK:(blk + 1) * GLA_BLK, :]
        hi, lo = _split_hi_lo(la)
        both = jnp.dot(after, jnp.concatenate([hi, lo], axis=1), preferred_element_type=F32)
        suffix = both[:, :GLA_K_DIM] + both[:, GLA_K_DIM:]
        kdec_ref[rows, :] = (k_ref[0, rows, :].astype(F32) * jnp.exp(suffix)).astype(BF16)
        tot_ref[rows, :] = suffix + la

    dec_rows = jnp.exp(tot_ref[pl.ds(0, GLA_NCHUNK, stride=CHUNK), :])
    dec_cols = jnp.concatenate(
        [dec_rows, jnp.zeros((GLA_K_DIM - GLA_NCHUNK, GLA_K_DIM), F32)], axis=0).T

    for ci in range(GLA_NCHUNK):
        rows = pl.ds(ci * CHUNK, CHUNK)
        v_c = jnp.concatenate([v_ref[0, rows, :], v_ref[1, rows, :]], axis=1)
        kv_ref[ci] = _tn_dot(kdec_ref[rows, :], v_c)

    state = jnp.zeros((GLA_K_DIM, GLA_V_DIM), F32)
    for ci in range(GLA_NCHUNK):
        state = state * dec_cols[:, ci:ci + 1] + kv_ref[ci]
        st_ref[ci] = state.astype(BF16)

    gw = gw_ref[...]
    qs = GLA_K_DIM ** -0.5
    for ci in range(GLA_NCHUNK):
        rows = pl.ds(ci * CHUNK, CHUNK)
        o = jnp.dot(q_ref[0, rows, :], st_ref[ci], preferred_element_type=F32)
        ms = jnp.mean(o * o, axis=-1, keepdims=True) * (qs * qs)
        y = o * (qs * lax.rsqrt(ms + NORM_EPS)) * gw
        gate = jnp.concatenate([gg_ref[0, rows, :], gg_ref[1, rows, :]], axis=1).astype(F32)
        y = y * (gate * jax.nn.sigmoid(gate))
        o_ref[0, rows, :] = y[:, :LANES].astype(BF16)
        o_ref[1, rows, :] = y[:, LANES:].astype(BF16)


def _gla_call(p, g, w2_pad, b2, gw):
    one = lambda cb0: pl.BlockSpec((1, SEQ, LANES), lambda b, h: (cb0 + h, b, 0))
    two = lambda cb0: pl.BlockSpec((2, SEQ, LANES), lambda b, h: (cb0 // 2 + h, b, 0))
    return pl.pallas_call(
        _gla_kernel,
        grid=(BATCH, GLA_HEADS),
        in_specs=[
            pl.BlockSpec((SEQ, LANES), lambda b, h: (b, 0)),
            pl.BlockSpec((LANES, GLA_K_DIM), lambda b, h: (0, h)),
            pl.BlockSpec((1, GLA_K_DIM), lambda b, h: (0, h)),
            pl.BlockSpec((1, GLA_V_DIM), lambda b, h: (0, 0)),
            one(CB_GQ), one(CB_GK), two(CB_GV), two(CB_GG),
        ],
        out_specs=pl.BlockSpec((2, SEQ, LANES), lambda b, h: (h, b, 0)),
        out_shape=jax.ShapeDtypeStruct((2 * GLA_HEADS, TOKENS, LANES), BF16),
        scratch_shapes=[
            pltpu.VMEM((SEQ, GLA_K_DIM), BF16),
            pltpu.VMEM((SEQ, GLA_K_DIM), F32),
            pltpu.VMEM((GLA_NCHUNK, GLA_K_DIM, GLA_V_DIM), F32),
            pltpu.VMEM((GLA_NCHUNK, GLA_K_DIM, GLA_V_DIM), BF16),
        ],
        compiler_params=pltpu.CompilerParams(
            dimension_semantics=("arbitrary", "arbitrary"), vmem_limit_bytes=VMEM_LIMIT),
        name="gla_scan",
    )(g, w2_pad, b2, gw, p, p, p, p)


OUT_TM = 512
OUT_SPLITS = (256, 128, 128)
assert sum(OUT_SPLITS) == OUT_TM


def _out_kernel(a_ref, b_ref, w_ref, x_ref, gate_ref, fw_ref, o_ref, wb_ref):
    @pl.when(pl.program_id(0) == 0)
    def _():
        wb_ref[...] = w_ref[...].astype(BF16)

    gate = gate_ref[0]
    fw = fw_ref[...]
    row0 = 0
    for n_rows in OUT_SPLITS:
        rows = pl.ds(row0, n_rows)
        row0 += n_rows
        mix = jnp.concatenate(
            [a_ref[h, rows, :] for h in range(a_ref.shape[0])]
            + [b_ref[h, rows, :] for h in range(b_ref.shape[0])], axis=1)
        y = jnp.dot(mix, wb_ref[...], preferred_element_type=F32)
        h_res = x_ref[rows, :] + gate * y
        ms = jnp.mean(h_res * h_res, axis=-1, keepdims=True)
        o_ref[rows, :] = h_res * lax.rsqrt(ms + NORM_EPS) * fw


def _out_call(a, b, w_out, xf, gate, fw):
    per_seq = SEQ // OUT_TM
    return pl.pallas_call(
        _out_kernel,
        grid=(TOKENS // OUT_TM,),
        in_specs=[
            pl.BlockSpec((DIFF_HEADS, OUT_TM, LANES), lambda i: (0, i, 0)),
            pl.BlockSpec((2 * GLA_HEADS, OUT_TM, LANES), lambda i: (0, i, 0)),
            pl.BlockSpec((D_MODEL, D_MODEL), lambda i: (0, 0), pipeline_mode=pl.Buffered(1)),
            pl.BlockSpec((OUT_TM, D_MODEL), lambda i: (i, 0)),
            pl.BlockSpec((1, 1, D_MODEL), lambda i: (i // per_seq, 0, 0)),
            pl.BlockSpec((1, D_MODEL), lambda i: (0, 0)),
        ],
        out_specs=pl.BlockSpec((OUT_TM, D_MODEL), lambda i: (i, 0)),
        out_shape=jax.ShapeDtypeStruct((TOKENS, D_MODEL), F32),
        scratch_shapes=[pltpu.VMEM((D_MODEL, D_MODEL), BF16)],
        compiler_params=pltpu.CompilerParams(
            dimension_semantics=("arbitrary",), vmem_limit_bytes=VMEM_LIMIT),
        name="out_proj_norm",
    )(a, b, w_out, xf, gate, fw)


def _rope_tables():
    inv_freq = ROPE_THETA ** (-jnp.arange(0, DIFF_QK_DIM, 2, dtype=F32) / DIFF_QK_DIM)
    ang = jnp.arange(SEQ, dtype=F32)[:, None] * inv_freq[None, :]
    cos, sin = jnp.cos(ang), jnp.sin(ang)
    return jnp.tile(cos, (1, 4)), jnp.concatenate([-sin, sin, -sin, sin], axis=1)


def kernel(x, c, norm_w, w_ada, b_ada, w_in, lambda_q1, lambda_k1, lambda_q2, lambda_k2,
           diff_norm_w, gla_gate_w2, gla_gate_b, gla_norm_w, w_out, final_norm_w):
    assert x.shape == (BATCH, SEQ, D_MODEL) and w_in.shape[0] == 1
    xf = x.reshape(TOKENS, D_MODEL).astype(F32)

    c_pad = jnp.pad(c.astype(F32), ((0, 8 - BATCH), (0, 0)))
    mod = _ada_call(c_pad, w_ada[0], b_ada[0][None, :])[:BATCH]
    shift = mod[:, :D_MODEL].reshape(BATCH, 1, D_MODEL)
    scale = mod[:, D_MODEL:2 * D_MODEL].reshape(BATCH, 1, D_MODEL)
    gate = mod[:, 2 * D_MODEL:].reshape(BATCH, 1, D_MODEL)

    w_t = w_in[0].T
    w_gate = jnp.pad(w_t[N_MAIN:], ((0, LANES - GLA_GATE_RANK), (0, 0))).astype(BF16)
    cos, sin = _rope_tables()
    p, g = _proj_call(xf, shift, scale, norm_w[0][None, :], w_t, w_gate, cos, sin)

    row = lambda v: v[0][None, :].astype(F32)
    a = _attn_call(p, row(lambda_q1), row(lambda_k1), row(lambda_q2), row(lambda_k2),
                   row(diff_norm_w))

    w2_pad = jnp.pad(gla_gate_w2[0], ((0, LANES - GLA_GATE_RANK), (0, 0))).astype(BF16)
    b = _gla_call(p, g, w2_pad, row(gla_gate_b), row(gla_norm_w))

    out = _out_call(a, b, w_out[0], xf, gate, final_norm_w[None, :].astype(F32))
    return out.reshape(BATCH, SEQ, D_MODEL).astype(x.dtype)
```

```python
import functools
import math

import jax
import jax.numpy as jnp
import numpy as np
from jax import lax
from jax.experimental import pallas as pl
from jax.experimental.pallas import tpu as pltpu

D_MODEL = 2048
BATCH = 4
SEQ = 2048
TOKENS = BATCH * SEQ
CHUNK = 64
LANES = 128

DIFF_HEADS = 8
DIFF_HEAD_DIM = 128
DIFF_QK_DIM = 64
GLA_HEADS = 4
GLA_K_DIM = 128
GLA_V_DIM = 256
GLA_GATE_RANK = 16
GLA_GATE_TAU = 16.0
ROPE_THETA = 10000.0
NORM_EPS = 1e-6
LAMBDA_INIT = 0.8 - 0.6 * math.exp(-0.3 * 0)

N_MAIN = 7168
N_COLBLK = N_MAIN // LANES
CB_DQ, CB_DK, CB_DV, CB_DG = 0, 8, 16, 24
CB_GQ, CB_GK, CB_GV, CB_GG = 32, 36, 40, 48

VMEM_LIMIT = 48 * 1024 * 1024
NEG_BIG = -1e30
LOG2_E = math.log2(math.e)

BF16 = jnp.bfloat16
F32 = jnp.float32


def _nt_dot(a, b):
    return lax.dot_general(a, b, (((1,), (1,)), ((), ())), preferred_element_type=F32)


def _silu(x):
    h = 0.5 * x
    return h + h * jnp.tanh(h)


def _tn_dot(a, b):
    return lax.dot_general(a, b, (((0,), (0,)), ((), ())), preferred_element_type=F32)


ADA_TN = 768


def _ada_kernel(c_ref, w_ref, b_ref, o_ref):
    c = c_ref[...]
    c_act = (c * jax.nn.sigmoid(c)).astype(BF16)
    w = w_ref[...].astype(BF16)
    o_ref[...] = jnp.dot(c_act, w, preferred_element_type=F32) + b_ref[...]


def _ada_call(c_pad, w_ada, b_ada):
    n = w_ada.shape[1]
    return pl.pallas_call(
        _ada_kernel,
        grid=(n // ADA_TN,),
        in_specs=[
            pl.BlockSpec((8, D_MODEL), lambda j: (0, 0)),
            pl.BlockSpec((D_MODEL, ADA_TN), lambda j: (0, j)),
            pl.BlockSpec((1, ADA_TN), lambda j: (0, j)),
        ],
        out_specs=pl.BlockSpec((8, ADA_TN), lambda j: (0, j)),
        out_shape=jax.ShapeDtypeStruct((8, n), F32),
        compiler_params=pltpu.CompilerParams(
            dimension_semantics=("arbitrary",), vmem_limit_bytes=VMEM_LIMIT),
        name="ada_mod",
    )(c_pad, w_ada, b_ada)


PROJ_TM = 1024
PROJ_TN = 1024
PROJ_SUB = 256
PROJ_ROPE_TILES = 2
PROJ_VMEM_LIMIT = (2 * PROJ_TM * D_MODEL * 4 + 2 * D_MODEL * PROJ_TN * 4
                   + PROJ_TM * D_MODEL * 2 + 2 * PROJ_TM * PROJ_TN * 2 + 16 * 1024 * 1024)


def _rot_half(x):
    lane = lax.broadcasted_iota(jnp.int32, x.shape, 1)
    first = (lane % DIFF_QK_DIM) < (DIFF_QK_DIM // 2)
    return jnp.where(first, pltpu.roll(x, LANES - 32, 1), pltpu.roll(x, 32, 1))


def _proj_kernel(x_ref, shift_ref, scale_ref, nw_ref, w_ref, wg_ref, cos_ref, sin_ref,
                 p_ref, g_ref, hn_ref):
    j = pl.program_id(1)
    n_sub = PROJ_TM // PROJ_SUB
    n_cb = PROJ_TN // LANES

    def norm_rows(rows):
        mult = nw_ref[...] * (1.0 + scale_ref[0])
        xs = x_ref[rows, :]
        ms = jnp.mean(xs * xs, axis=-1, keepdims=True)
        hn_ref[rows, :] = (xs * lax.rsqrt(ms + NORM_EPS) * mult + shift_ref[0]).astype(BF16)

    def store_rope(rows, acc, sc):
        cos = cos_ref[rows, :] * sc
        sin = sin_ref[rows, :] * sc
        for cb in range(n_cb):
            t = acc[:, cb * LANES:(cb + 1) * LANES]
            p_ref[cb, rows, :] = (t * cos + _rot_half(t) * sin).astype(BF16)

    @pl.when(j == 0)
    def _():
        for r in range(n_sub):
            rows = pl.ds(r * PROJ_SUB, PROJ_SUB)
            norm_rows(rows)
            acc = _nt_dot(hn_ref[rows, :], w_ref[...])
            store_rope(rows, acc, DIFF_QK_DIM ** -0.5 * LOG2_E)
        g_ref[...] = _nt_dot(hn_ref[...], wg_ref[...])

    @pl.when(j == 1)
    def _():
        for r in range(n_sub):
            rows = pl.ds(r * PROJ_SUB, PROJ_SUB)
            acc = _nt_dot(hn_ref[rows, :], w_ref[...])
            store_rope(rows, acc, 1.0)

    @pl.when(j >= PROJ_ROPE_TILES)
    def _():
        for r in range(n_sub):
            rows = pl.ds(r * PROJ_SUB, PROJ_SUB)
            acc = _nt_dot(hn_ref[rows, :], w_ref[...])
            for cb in range(n_cb):
                p_ref[cb, rows, :] = acc[:, cb * LANES:(cb + 1) * LANES].astype(BF16)


def _proj_call(xf, shift, scale, norm_w, w_t, w_gate, cos, sin):
    m_tiles = TOKENS // PROJ_TM
    n_tiles = N_MAIN // PROJ_TN
    per_seq = SEQ // PROJ_TM
    n_cb = PROJ_TN // LANES
    return pl.pallas_call(
        _proj_kernel,
        grid=(m_tiles, n_tiles),
        in_specs=[
            pl.BlockSpec((PROJ_TM, D_MODEL), lambda i, j: (i, 0)),
            pl.BlockSpec((1, 1, D_MODEL), lambda i, j: (i // per_seq, 0, 0)),
            pl.BlockSpec((1, 1, D_MODEL), lambda i, j: (i // per_seq, 0, 0)),
            pl.BlockSpec((1, D_MODEL), lambda i, j: (0, 0)),
            pl.BlockSpec((PROJ_TN, D_MODEL), lambda i, j: (j, 0)),
            pl.BlockSpec((LANES, D_MODEL), lambda i, j: (0, 0)),
            pl.BlockSpec((PROJ_TM, LANES), lambda i, j: (i % per_seq, 0)),
            pl.BlockSpec((PROJ_TM, LANES), lambda i, j: (i % per_seq, 0)),
        ],
        out_specs=[
            pl.BlockSpec((n_cb, PROJ_TM, LANES), lambda i, j: (j, i, 0)),
            pl.BlockSpec((PROJ_TM, LANES), lambda i, j: (i, 0)),
        ],
        out_shape=[
            jax.ShapeDtypeStruct((N_COLBLK, TOKENS, LANES), BF16),
            jax.ShapeDtypeStruct((TOKENS, LANES), F32),
        ],
        scratch_shapes=[pltpu.VMEM((PROJ_TM, D_MODEL), BF16)],
        compiler_params=pltpu.CompilerParams(
            dimension_semantics=("arbitrary", "arbitrary"), vmem_limit_bytes=PROJ_VMEM_LIMIT),
        name="norm_in_proj",
    )(xf, shift, scale, norm_w, w_t, w_gate, cos, sin)


ATT_T = 256
ATT_NBLK = SEQ // ATT_T
ATT_LEAD = 24
assert ATT_LEAD >= ATT_NBLK - 1
ATT_ONES_ROWS = 16
ATT_HEADS_PER_STEP = 2


def _attn_kernel(lq1_ref, lk1_ref, lq2_ref, lk2_ref, dnw_ref, q_ref, k_ref, v_ref, dg_ref,
                 o_ref, vt_ref):
    lam = (jnp.exp(jnp.sum(lq1_ref[...] * lk1_ref[...], keepdims=True))
           - jnp.exp(jnp.sum(lq2_ref[...] * lk2_ref[...], keepdims=True))
           + LAMBDA_INIT)
    out_gain = dnw_ref[...] * (1.0 - LAMBDA_INIT)

    for hd in range(ATT_HEADS_PER_STEP):
        for jb in range(ATT_NBLK):
            cols = pl.ds(jb * ATT_T, ATT_T)
            vb = v_ref[hd, jb * ATT_T:(jb + 1) * ATT_T, :].astype(F32)
            vt_ref[hd, 0:DIFF_HEAD_DIM, cols] = vb.T.astype(BF16)
        vt_ref[hd, DIFF_HEAD_DIM:, :] = jnp.ones((ATT_ONES_ROWS, SEQ), BF16)

    lane = lax.broadcasted_iota(jnp.int32, (ATT_T, LANES), 1)
    comp_a = lane < DIFF_QK_DIM
    krow = lax.broadcasted_iota(jnp.int32, (ATT_T, ATT_T), 0) // CHUNK
    qcol = lax.broadcasted_iota(jnp.int32, (ATT_T, ATT_T), 1) // CHUNK
    diag_mask = krow <= qcol
    diag_mask2 = jnp.concatenate([diag_mask, diag_mask], axis=1)

    units = [(hd, qi, j) for hd in range(ATT_HEADS_PER_STEP)
             for qi in range(ATT_NBLK) for j in range(qi + 1)]
    blocks = {(hd, qi): {"m": None, "pv": None, "s": {}}
              for hd in range(ATT_HEADS_PER_STEP) for qi in range(ATT_NBLK)}

    def block_queries(hd, qi):
        q = q_ref[hd, pl.ds(qi * ATT_T, ATT_T), :]
        zero = jnp.zeros_like(q)
        return jnp.concatenate([jnp.where(comp_a, q, zero), jnp.where(comp_a, zero, q)], axis=0)

    def score_unit(hd, qi, j):
        st = blocks[hd, qi]
        if j == 0:
            st["q2"] = block_queries(hd, qi)
        s = _nt_dot(k_ref[hd, pl.ds(j * ATT_T, ATT_T), :], st["q2"])
        if j == qi:
            s = jnp.where(diag_mask2, s, NEG_BIG)
        st["s"][j] = s
        mj = jnp.max(s, axis=0, keepdims=True)
        st["m"] = mj if st["m"] is None else jnp.maximum(st["m"], mj)

    def value_unit(hd, qi, j):
        st = blocks[hd, qi]
        p = jnp.exp2(st["s"].pop(j) - st["m"])
        pvj = jnp.dot(vt_ref[hd, :, pl.ds(j * ATT_T, ATT_T)], p.astype(BF16),
                      preferred_element_type=F32)
        st["pv"] = pvj if st["pv"] is None else st["pv"] + pvj
        if j == qi:
            finish(hd, qi, st["pv"])

    def finish(hd, qi, pv_l):
        rows = pl.ds(qi * ATT_T, ATT_T)
        l = pv_l[DIFF_HEAD_DIM:DIFF_HEAD_DIM + 1, :]
        pv = pv_l[0:DIFF_HEAD_DIM, :] * (1.0 / l)
        out_t = pv[:, :ATT_T] - lam * pv[:, ATT_T:]
        out = out_t.T
        ms = jnp.mean(out * out, axis=-1, keepdims=True)
        y = out * lax.rsqrt(ms + NORM_EPS) * out_gain
        g = dg_ref[hd, rows, :].astype(F32)
        o_ref[hd, rows, :] = (y * _silu(g)).astype(BF16)

    for t in range(len(units) + ATT_LEAD):
        if t < len(units):
            score_unit(*units[t])
        if t >= ATT_LEAD:
            value_unit(*units[t - ATT_LEAD])


def _attn_call(p, lq1, lk1, lq2, lk2, dnw):
    hp = ATT_HEADS_PER_STEP
    vec = lambda n: pl.BlockSpec((1, n), lambda b, h: (0, 0))
    slab = lambda cb0: pl.BlockSpec((hp, SEQ, LANES), lambda b, h: (cb0 // hp + h, b, 0))
    return pl.pallas_call(
        _attn_kernel,
        grid=(BATCH, DIFF_HEADS // hp),
        in_specs=[vec(DIFF_QK_DIM)] * 4 + [vec(DIFF_HEAD_DIM),
                  slab(CB_DQ), slab(CB_DK), slab(CB_DV), slab(CB_DG)],
        out_specs=pl.BlockSpec((hp, SEQ, LANES), lambda b, h: (h, b, 0)),
        out_shape=jax.ShapeDtypeStruct((DIFF_HEADS, TOKENS, LANES), BF16),
        scratch_shapes=[
            pltpu.VMEM((hp, DIFF_HEAD_DIM + ATT_ONES_ROWS, SEQ), BF16),
        ],
        compiler_params=pltpu.CompilerParams(
            dimension_semantics=("arbitrary", "arbitrary"), vmem_limit_bytes=VMEM_LIMIT),
        name="diff_attn",
    )(lq1, lk1, lq2, lk2, dnw, p, p, p, p)


GLA_BLK = 256
GLA_NCHUNK = SEQ // CHUNK


def _split_hi_lo(x):
    hi = x.astype(BF16)
    lo = (x - hi.astype(F32)).astype(BF16)
    return hi, lo


def _gla_kernel(g_ref, w2_ref, b2_ref, gw_ref, q_ref, k_ref, v_ref, gg_ref, o_ref,
                kdec_ref, tot_ref, kv_ref, st_ref):
    z = jnp.dot(g_ref[...].astype(BF16), w2_ref[...], preferred_element_type=F32) + b2_ref[...]
    log_a = (jnp.minimum(z, 0.0) - jnp.log(1.0 + jnp.exp(-jnp.abs(z)))) * (1.0 / GLA_GATE_TAU)

    r = lax.broadcasted_iota(jnp.int32, (GLA_BLK, GLA_BLK), 0)
    c = lax.broadcasted_iota(jnp.int32, (GLA_BLK, GLA_BLK), 1)
    after = jnp.where(((r // CHUNK) == (c // CHUNK)) & (c > r), 1.0, 0.0).astype(BF16)
    for blk in range(SEQ // GLA_BLK):
        rows = pl.ds(blk * GLA_BLK, GLA_BLK)
        la = log_a[blk * GLA_BLK:(blk + 1) * GLA_BLK, :]
        hi, lo = _split_hi_lo(la)
        both = jnp.dot(after, jnp.concatenate([hi, lo], axis=1), preferred_element_type=F32)
        suffix = both[:, :GLA_K_DIM] + both[:, GLA_K_DIM:]
        kdec_ref[rows, :] = (k_ref[0, rows, :].astype(F32) * jnp.exp(suffix)).astype(BF16)
        tot_ref[rows, :] = suffix + la

    dec_rows = jnp.exp(tot_ref[pl.ds(0, GLA_NCHUNK, stride=CHUNK), :])
    dec_cols = jnp.concatenate(
        [dec_rows, jnp.zeros((GLA_K_DIM - GLA_NCHUNK, GLA_K_DIM), F32)], axis=0).T

    for ci in range(GLA_NCHUNK):
        rows = pl.ds(ci * CHUNK, CHUNK)
        v_c = jnp.concatenate([v_ref[0, rows, :], v_ref[1, rows, :]], axis=1)
        kv_ref[ci] = _tn_dot(kdec_ref[rows, :], v_c)

    state = jnp.zeros((GLA_K_DIM, GLA_V_DIM), F32)
    for ci in range(GLA_NCHUNK):
        state = state * dec_cols[:, ci:ci + 1] + kv_ref[ci]
        st_ref[ci] = state.astype(BF16)

    gw = gw_ref[...]
    qs = GLA_K_DIM ** -0.5
    for ci in range(GLA_NCHUNK):
        rows = pl.ds(ci * CHUNK, CHUNK)
        o = jnp.dot(q_ref[0, rows, :], st_ref[ci], preferred_element_type=F32)
        ms = jnp.mean(o * o, axis=-1, keepdims=True) * (qs * qs)
        y = o * (qs * lax.rsqrt(ms + NORM_EPS)) * gw
        gate = jnp.concatenate([gg_ref[0, rows, :], gg_ref[1, rows, :]], axis=1).astype(F32)
        y = y * _silu(gate)
        o_ref[0, rows, :] = y[:, :LANES].astype(BF16)
        o_ref[1, rows, :] = y[:, LANES:].astype(BF16)


def _gla_call(p, g, w2_pad, b2, gw):
    one = lambda cb0: pl.BlockSpec((1, SEQ, LANES), lambda b, h: (cb0 + h, b, 0))
    two = lambda cb0: pl.BlockSpec((2, SEQ, LANES), lambda b, h: (cb0 // 2 + h, b, 0))
    return pl.pallas_call(
        _gla_kernel,
        grid=(BATCH, GLA_HEADS),
        in_specs=[
            pl.BlockSpec((SEQ, LANES), lambda b, h: (b, 0)),
            pl.BlockSpec((LANES, GLA_K_DIM), lambda b, h: (0, h)),
            pl.BlockSpec((1, GLA_K_DIM), lambda b, h: (0, h)),
            pl.BlockSpec((1, GLA_V_DIM), lambda b, h: (0, 0)),
            one(CB_GQ), one(CB_GK), two(CB_GV), two(CB_GG),
        ],
        out_specs=pl.BlockSpec((2, SEQ, LANES), lambda b, h: (h, b, 0)),
        out_shape=jax.ShapeDtypeStruct((2 * GLA_HEADS, TOKENS, LANES), BF16),
        scratch_shapes=[
            pltpu.VMEM((SEQ, GLA_K_DIM), BF16),
            pltpu.VMEM((SEQ, GLA_K_DIM), F32),
            pltpu.VMEM((GLA_NCHUNK, GLA_K_DIM, GLA_V_DIM), F32),
            pltpu.VMEM((GLA_NCHUNK, GLA_K_DIM, GLA_V_DIM), BF16),
        ],
        compiler_params=pltpu.CompilerParams(
            dimension_semantics=("arbitrary", "arbitrary"), vmem_limit_bytes=VMEM_LIMIT),
        name="gla_scan",
    )(g, w2_pad, b2, gw, p, p, p, p)


OUT_TM = 512
OUT_SPLITS = (256, 256)
assert sum(OUT_SPLITS) == OUT_TM


def _out_kernel(a_ref, b_ref, w_ref, x_ref, gate_ref, fw_ref, o_ref, wb_ref):
    @pl.when(pl.program_id(0) == 0)
    def _():
        wb_ref[...] = w_ref[...].astype(BF16)

    gate = gate_ref[0]
    fw = fw_ref[...]
    row0 = 0
    for n_rows in OUT_SPLITS:
        rows = pl.ds(row0, n_rows)
        row0 += n_rows
        mix = jnp.concatenate(
            [a_ref[h, rows, :] for h in range(a_ref.shape[0])]
            + [b_ref[h, rows, :] for h in range(b_ref.shape[0])], axis=1)
        y = jnp.dot(mix, wb_ref[...], preferred_element_type=F32)
        h_res = x_ref[rows, :] + gate * y
        ms = jnp.mean(h_res * h_res, axis=-1, keepdims=True)
        o_ref[rows, :] = h_res * lax.rsqrt(ms + NORM_EPS) * fw


def _out_call(a, b, w_out, xf, gate, fw):
    per_seq = SEQ // OUT_TM
    return pl.pallas_call(
        _out_kernel,
        grid=(TOKENS // OUT_TM,),
        in_specs=[
            pl.BlockSpec((DIFF_HEADS, OUT_TM, LANES), lambda i: (0, i, 0)),
            pl.BlockSpec((2 * GLA_HEADS, OUT_TM, LANES), lambda i: (0, i, 0)),
            pl.BlockSpec((D_MODEL, D_MODEL), lambda i: (0, 0), pipeline_mode=pl.Buffered(1)),
            pl.BlockSpec((OUT_TM, D_MODEL), lambda i: (i, 0)),
            pl.BlockSpec((1, 1, D_MODEL), lambda i: (i // per_seq, 0, 0)),
            pl.BlockSpec((1, D_MODEL), lambda i: (0, 0)),
        ],
        out_specs=pl.BlockSpec((OUT_TM, D_MODEL), lambda i: (i, 0)),
        out_shape=jax.ShapeDtypeStruct((TOKENS, D_MODEL), F32),
        scratch_shapes=[pltpu.VMEM((D_MODEL, D_MODEL), BF16)],
        compiler_params=pltpu.CompilerParams(
            dimension_semantics=("arbitrary",), vmem_limit_bytes=VMEM_LIMIT),
        name="out_proj_norm",
    )(a, b, w_out, xf, gate, fw)


def _rope_tables():
    inv_freq = ROPE_THETA ** (-np.arange(0, DIFF_QK_DIM, 2, dtype=np.float64) / DIFF_QK_DIM)
    ang = np.arange(SEQ, dtype=np.float64)[:, None] * inv_freq[None, :]
    cos, sin = np.cos(ang), np.sin(ang)
    cos_t = np.tile(cos, (1, 4)).astype(np.float32)
    sin_t = np.concatenate([-sin, sin, -sin, sin], axis=1).astype(np.float32)
    return jnp.asarray(cos_t), jnp.asarray(sin_t)


def kernel(x, c, norm_w, w_ada, b_ada, w_in, lambda_q1, lambda_k1, lambda_q2, lambda_k2,
           diff_norm_w, gla_gate_w2, gla_gate_b, gla_norm_w, w_out, final_norm_w):
    assert x.shape == (BATCH, SEQ, D_MODEL) and w_in.shape[0] == 1
    xf = x.reshape(TOKENS, D_MODEL).astype(F32)

    c_pad = jnp.pad(c.astype(F32), ((0, 8 - BATCH), (0, 0)))
    mod = _ada_call(c_pad, w_ada[0], b_ada[0][None, :])[:BATCH]
    shift = mod[:, :D_MODEL].reshape(BATCH, 1, D_MODEL)
    scale = mod[:, D_MODEL:2 * D_MODEL].reshape(BATCH, 1, D_MODEL)
    gate = mod[:, 2 * D_MODEL:].reshape(BATCH, 1, D_MODEL)

    w_t = w_in[0].T
    w_gate = jnp.pad(w_t[N_MAIN:], ((0, LANES - GLA_GATE_RANK), (0, 0))).astype(BF16)
    cos, sin = _rope_tables()
    p, g = _proj_call(xf, shift, scale, norm_w[0][None, :], w_t, w_gate, cos, sin)

    row = lambda v: v[0][None, :].astype(F32)
    a = _attn_call(p, row(lambda_q1), row(lambda_k1), row(lambda_q2), row(lambda_k2),
                   row(diff_norm_w))

    w2_pad = jnp.pad(gla_gate_w2[0], ((0, LANES - GLA_GATE_RANK), (0, 0))).astype(BF16)
    b = _gla_call(p, g, w2_pad, row(gla_gate_b), row(gla_norm_w))

    out = _out_call(a, b, w_out[0], xf, gate, final_norm_w[None, :].astype(F32))
    return out.reshape(BATCH, SEQ, D_MODEL).astype(x.dtype)
```

```python
import functools
import math

import jax
import jax.numpy as jnp
import numpy as np
from jax import lax
from jax.experimental import pallas as pl
from jax.experimental.pallas import tpu as pltpu

D_MODEL = 2048
BATCH = 4
SEQ = 2048
TOKENS = BATCH * SEQ
CHUNK = 64
LANES = 128

DIFF_HEADS = 8
DIFF_HEAD_DIM = 128
DIFF_QK_DIM = 64
GLA_HEADS = 4
GLA_K_DIM = 128
GLA_V_DIM = 256
GLA_GATE_RANK = 16
GLA_GATE_TAU = 16.0
ROPE_THETA = 10000.0
NORM_EPS = 1e-6
LAMBDA_INIT = 0.8 - 0.6 * math.exp(-0.3 * 0)

N_MAIN = 7168
N_COLBLK = N_MAIN // LANES
CB_DQ, CB_DK, CB_DV, CB_DG = 0, 8, 16, 24
CB_GQ, CB_GK, CB_GV, CB_GG = 32, 36, 40, 48

VMEM_LIMIT = 48 * 1024 * 1024
NEG_BIG = -1e30
LOG2_E = math.log2(math.e)

BF16 = jnp.bfloat16
F32 = jnp.float32


def _nt_dot(a, b):
    return lax.dot_general(a, b, (((1,), (1,)), ((), ())), preferred_element_type=F32)


def _silu(x):
    h = 0.5 * x
    return h + h * jnp.tanh(h)


def _tn_dot(a, b):
    return lax.dot_general(a, b, (((0,), (0,)), ((), ())), preferred_element_type=F32)


ADA_TN = 768


def _ada_kernel(c_ref, w_ref, b_ref, o_ref):
    c = c_ref[...]
    c_act = (c * jax.nn.sigmoid(c)).astype(BF16)
    w = w_ref[...].astype(BF16)
    o_ref[...] = jnp.dot(c_act, w, preferred_element_type=F32) + b_ref[...]


def _ada_call(c_pad, w_ada, b_ada):
    n = w_ada.shape[1]
    return pl.pallas_call(
        _ada_kernel,
        grid=(n // ADA_TN,),
        in_specs=[
            pl.BlockSpec((8, D_MODEL), lambda j: (0, 0)),
            pl.BlockSpec((D_MODEL, ADA_TN), lambda j: (0, j)),
            pl.BlockSpec((1, ADA_TN), lambda j: (0, j)),
        ],
        out_specs=pl.BlockSpec((8, ADA_TN), lambda j: (0, j)),
        out_shape=jax.ShapeDtypeStruct((8, n), F32),
        compiler_params=pltpu.CompilerParams(
            dimension_semantics=("arbitrary",), vmem_limit_bytes=VMEM_LIMIT),
        name="ada_mod",
    )(c_pad, w_ada, b_ada)


PROJ_TM = 1024
PROJ_TN = 1024
PROJ_SUB = 256
PROJ_ROPE_TILES = 2
PROJ_VMEM_LIMIT = (2 * PROJ_TM * D_MODEL * 4 + 2 * D_MODEL * PROJ_TN * 4
                   + PROJ_TM * D_MODEL * 2 + 2 * PROJ_TM * PROJ_TN * 2 + 16 * 1024 * 1024)


def _rot_half(x):
    lane = lax.broadcasted_iota(jnp.int32, x.shape, 1)
    first = (lane % DIFF_QK_DIM) < (DIFF_QK_DIM // 2)
    return jnp.where(first, pltpu.roll(x, LANES - 32, 1), pltpu.roll(x, 32, 1))


def _proj_kernel(x_ref, shift_ref, scale_ref, nw_ref, w_ref, wg_ref, cos_ref, sin_ref,
                 p_ref, g_ref, hn_ref):
    j = pl.program_id(1)
    n_sub = PROJ_TM // PROJ_SUB
    n_cb = PROJ_TN // LANES

    def norm_rows(rows):
        mult = nw_ref[...] * (1.0 + scale_ref[0])
        xs = x_ref[rows, :]
        ms = jnp.mean(xs * xs, axis=-1, keepdims=True)
        hn_ref[rows, :] = (xs * lax.rsqrt(ms + NORM_EPS) * mult + shift_ref[0]).astype(BF16)

    def store_rope(rows, acc, sc):
        cos = cos_ref[rows, :] * sc
        sin = sin_ref[rows, :] * sc
        for cb in range(n_cb):
            t = acc[:, cb * LANES:(cb + 1) * LANES]
            p_ref[cb, rows, :] = (t * cos + _rot_half(t) * sin).astype(BF16)

    @pl.when(j == 0)
    def _():
        for r in range(n_sub):
            rows = pl.ds(r * PROJ_SUB, PROJ_SUB)
            norm_rows(rows)
            acc = _nt_dot(hn_ref[rows, :], w_ref[...])
            store_rope(rows, acc, DIFF_QK_DIM ** -0.5 * LOG2_E)
        g_ref[...] = _nt_dot(hn_ref[...], wg_ref[...])

    @pl.when(j == 1)
    def _():
        for r in range(n_sub):
            rows = pl.ds(r * PROJ_SUB, PROJ_SUB)
            acc = _nt_dot(hn_ref[rows, :], w_ref[...])
            store_rope(rows, acc, 1.0)

    @pl.when(j >= PROJ_ROPE_TILES)
    def _():
        for r in range(n_sub):
            rows = pl.ds(r * PROJ_SUB, PROJ_SUB)
            acc = _nt_dot(hn_ref[rows, :], w_ref[...])
            for cb in range(n_cb):
                p_ref[cb, rows, :] = acc[:, cb * LANES:(cb + 1) * LANES].astype(BF16)


def _proj_call(xf, shift, scale, norm_w, w_t, w_gate, cos, sin):
    m_tiles = TOKENS // PROJ_TM
    n_tiles = N_MAIN // PROJ_TN
    per_seq = SEQ // PROJ_TM
    n_cb = PROJ_TN // LANES
    return pl.pallas_call(
        _proj_kernel,
        grid=(m_tiles, n_tiles),
        in_specs=[
            pl.BlockSpec((PROJ_TM, D_MODEL), lambda i, j: (i, 0)),
            pl.BlockSpec((1, 1, D_MODEL), lambda i, j: (i // per_seq, 0, 0)),
            pl.BlockSpec((1, 1, D_MODEL), lambda i, j: (i // per_seq, 0, 0)),
            pl.BlockSpec((1, D_MODEL), lambda i, j: (0, 0)),
            pl.BlockSpec((PROJ_TN, D_MODEL), lambda i, j: (j, 0)),
            pl.BlockSpec((LANES, D_MODEL), lambda i, j: (0, 0)),
            pl.BlockSpec((PROJ_TM, LANES), lambda i, j: (i % per_seq, 0)),
            pl.BlockSpec((PROJ_TM, LANES), lambda i, j: (i % per_seq, 0)),
        ],
        out_specs=[
            pl.BlockSpec((n_cb, PROJ_TM, LANES), lambda i, j: (j, i, 0)),
            pl.BlockSpec((PROJ_TM, LANES), lambda i, j: (i, 0)),
        ],
        out_shape=[
            jax.ShapeDtypeStruct((N_COLBLK, TOKENS, LANES), BF16),
            jax.ShapeDtypeStruct((TOKENS, LANES), F32),
        ],
        scratch_shapes=[pltpu.VMEM((PROJ_TM, D_MODEL), BF16)],
        compiler_params=pltpu.CompilerParams(
            dimension_semantics=("arbitrary", "arbitrary"), vmem_limit_bytes=PROJ_VMEM_LIMIT),
        name="norm_in_proj",
    )(xf, shift, scale, norm_w, w_t, w_gate, cos, sin)


ATT_T = 256
ATT_NBLK = SEQ // ATT_T
ATT_LEAD = 24
assert ATT_LEAD >= ATT_NBLK - 1
ATT_ONES_ROWS = 16
ATT_HEADS_PER_STEP = 2
ATT_SLOTS = ATT_LEAD + 4


def _attn_kernel(lq1_ref, lk1_ref, lq2_ref, lk2_ref, dnw_ref, q_ref, k_ref, v_ref, dg_ref,
                 o_ref, vt_ref, s_ref):
    lam = (jnp.exp(jnp.sum(lq1_ref[...] * lk1_ref[...], keepdims=True))
           - jnp.exp(jnp.sum(lq2_ref[...] * lk2_ref[...], keepdims=True))
           + LAMBDA_INIT)
    out_gain = dnw_ref[...] * (1.0 - LAMBDA_INIT)

    for hd in range(ATT_HEADS_PER_STEP):
        for jb in range(ATT_NBLK):
            cols = pl.ds(jb * ATT_T, ATT_T)
            vb = v_ref[hd, jb * ATT_T:(jb + 1) * ATT_T, :].astype(F32)
            vt_ref[hd, 0:DIFF_HEAD_DIM, cols] = vb.T.astype(BF16)
        vt_ref[hd, DIFF_HEAD_DIM:, :] = jnp.ones((ATT_ONES_ROWS, SEQ), BF16)

    lane = lax.broadcasted_iota(jnp.int32, (ATT_T, LANES), 1)
    comp_a = lane < DIFF_QK_DIM
    krow = lax.broadcasted_iota(jnp.int32, (ATT_T, ATT_T), 0) // CHUNK
    qcol = lax.broadcasted_iota(jnp.int32, (ATT_T, ATT_T), 1) // CHUNK
    diag_mask = krow <= qcol
    diag_mask2 = jnp.concatenate([diag_mask, diag_mask], axis=1)

    units = [(hd, qi, j) for hd in range(ATT_HEADS_PER_STEP)
             for qi in range(ATT_NBLK) for j in range(qi + 1)]
    blocks = {(hd, qi): {"m": None, "p": []}
              for hd in range(ATT_HEADS_PER_STEP) for qi in range(ATT_NBLK)}

    def block_queries(hd, qi):
        q = q_ref[hd, pl.ds(qi * ATT_T, ATT_T), :]
        zero = jnp.zeros_like(q)
        return jnp.concatenate([jnp.where(comp_a, q, zero), jnp.where(comp_a, zero, q)], axis=0)

    dyn0 = lax.shift_right_logical(pl.program_id(0), 20)

    def score_unit(t, hd, qi, j):
        st = blocks[hd, qi]
        if j == 0:
            st["q2"] = block_queries(hd, qi)
        s = _nt_dot(k_ref[hd, pl.ds(j * ATT_T, ATT_T), :], st["q2"])
        if j == qi:
            s = jnp.where(diag_mask2, s, NEG_BIG)
        s_ref[dyn0 + t % ATT_SLOTS] = s
        mj = jnp.max(s, axis=0, keepdims=True)
        st["m"] = mj if st["m"] is None else jnp.maximum(st["m"], mj)

    def value_unit(t, hd, qi, j):
        st = blocks[hd, qi]
        p = jnp.exp2(s_ref[dyn0 + t % ATT_SLOTS] - st["m"])
        st["p"].append(p.astype(BF16))
        if j == qi:
            p_all = jnp.concatenate(st["p"], axis=0) if qi else st["p"][0]
            pv_l = jnp.dot(vt_ref[hd, :, 0:(qi + 1) * ATT_T], p_all,
                           preferred_element_type=F32)
            finish(hd, qi, pv_l)

    def finish(hd, qi, pv_l):
        rows = pl.ds(qi * ATT_T, ATT_T)
        l = pv_l[DIFF_HEAD_DIM:DIFF_HEAD_DIM + 1, :]
        pv = pv_l[0:DIFF_HEAD_DIM, :] * (1.0 / l)
        out_t = pv[:, :ATT_T] - lam * pv[:, ATT_T:]
        out = out_t.T
        ms = jnp.mean(out * out, axis=-1, keepdims=True)
        y = out * lax.rsqrt(ms + NORM_EPS) * out_gain
        g = dg_ref[hd, rows, :].astype(F32)
        o_ref[hd, rows, :] = (y * _silu(g)).astype(BF16)

    for t in range(len(units) + ATT_LEAD):
        if t < len(units):
            score_unit(t, *units[t])
        if t >= ATT_LEAD:
            value_unit(t - ATT_LEAD, *units[t - ATT_LEAD])


def _attn_call(p, lq1, lk1, lq2, lk2, dnw):
    hp = ATT_HEADS_PER_STEP
    vec = lambda n: pl.BlockSpec((1, n), lambda b, h: (0, 0))
    slab = lambda cb0: pl.BlockSpec((hp, SEQ, LANES), lambda b, h: (cb0 // hp + h, b, 0))
    return pl.pallas_call(
        _attn_kernel,
        grid=(BATCH, DIFF_HEADS // hp),
        in_specs=[vec(DIFF_QK_DIM)] * 4 + [vec(DIFF_HEAD_DIM),
                  slab(CB_DQ), slab(CB_DK), slab(CB_DV), slab(CB_DG)],
        out_specs=pl.BlockSpec((hp, SEQ, LANES), lambda b, h: (h, b, 0)),
        out_shape=jax.ShapeDtypeStruct((DIFF_HEADS, TOKENS, LANES), BF16),
        scratch_shapes=[
            pltpu.VMEM((hp, DIFF_HEAD_DIM + ATT_ONES_ROWS, SEQ), BF16),
            pltpu.VMEM((ATT_SLOTS, ATT_T, 2 * ATT_T), F32),
        ],
        compiler_params=pltpu.CompilerParams(
            dimension_semantics=("arbitrary", "arbitrary"), vmem_limit_bytes=VMEM_LIMIT),
        name="diff_attn",
    )(lq1, lk1, lq2, lk2, dnw, p, p, p, p)


GLA_BLK = 256
GLA_NCHUNK = SEQ // CHUNK


def _split_hi_lo(x):
    hi = x.astype(BF16)
    lo = (x - hi.astype(F32)).astype(BF16)
    return hi, lo


def _gla_kernel(g_ref, w2_ref, b2_ref, gw_ref, q_ref, k_ref, v_ref, gg_ref, o_ref,
                kdec_ref, tot_ref, kv_ref, st_ref):
    z = jnp.dot(g_ref[...].astype(BF16), w2_ref[...], preferred_element_type=F32) + b2_ref[...]
    log_a = (jnp.minimum(z, 0.0) - jnp.log(1.0 + jnp.exp(-jnp.abs(z)))) * (1.0 / GLA_GATE_TAU)

    r = lax.broadcasted_iota(jnp.int32, (GLA_BLK, GLA_BLK), 0)
    c = lax.broadcasted_iota(jnp.int32, (GLA_BLK, GLA_BLK), 1)
    after = jnp.where(((r // CHUNK) == (c // CHUNK)) & (c > r), 1.0, 0.0).astype(BF16)
    for blk in range(SEQ // GLA_BLK):
        rows = pl.ds(blk * GLA_BLK, GLA_BLK)
        la = log_a[blk * GLA_BLK:(blk + 1) * GLA_BLK, :]
        hi, lo = _split_hi_lo(la)
        both = jnp.dot(after, jnp.concatenate([hi, lo], axis=1), preferred_element_type=F32)
        suffix = both[:, :GLA_K_DIM] + both[:, GLA_K_DIM:]
        kdec_ref[rows, :] = (k_ref[0, rows, :].astype(F32) * jnp.exp(suffix)).astype(BF16)
        tot_ref[rows, :] = suffix + la

    dec_rows = jnp.exp(tot_ref[pl.ds(0, GLA_NCHUNK, stride=CHUNK), :])
    dec_cols = jnp.concatenate(
        [dec_rows, jnp.zeros((GLA_K_DIM - GLA_NCHUNK, GLA_K_DIM), F32)], axis=0).T

    for ci in range(GLA_NCHUNK):
        rows = pl.ds(ci * CHUNK, CHUNK)
        v_c = jnp.concatenate([v_ref[0, rows, :], v_ref[1, rows, :]], axis=1)
        kv_ref[ci] = _tn_dot(kdec_ref[rows, :], v_c)

    state = jnp.zeros((GLA_K_DIM, GLA_V_DIM), F32)
    for ci in range(GLA_NCHUNK):
        state = state * dec_cols[:, ci:ci + 1] + kv_ref[ci]
        st_ref[ci] = state.astype(BF16)

    gw = gw_ref[...]
    qs = GLA_K_DIM ** -0.5
    for ci in range(GLA_NCHUNK):
        rows = pl.ds(ci * CHUNK, CHUNK)
        o = jnp.dot(q_ref[0, rows, :], st_ref[ci], preferred_element_type=F32)
        ms = jnp.mean(o * o, axis=-1, keepdims=True) * (qs * qs)
        y = o * (qs * lax.rsqrt(ms + NORM_EPS)) * gw
        gate = jnp.concatenate([gg_ref[0, rows, :], gg_ref[1, rows, :]], axis=1).astype(F32)
        y = y * _silu(gate)
        o_ref[0, rows, :] = y[:, :LANES].astype(BF16)
        o_ref[1, rows, :] = y[:, LANES:].astype(BF16)


def _gla_call(p, g, w2_pad, b2, gw):
    one = lambda cb0: pl.BlockSpec((1, SEQ, LANES), lambda b, h: (cb0 + h, b, 0))
    two = lambda cb0: pl.BlockSpec((2, SEQ, LANES), lambda b, h: (cb0 // 2 + h, b, 0))
    return pl.pallas_call(
        _gla_kernel,
        grid=(BATCH, GLA_HEADS),
        in_specs=[
            pl.BlockSpec((SEQ, LANES), lambda b, h: (b, 0)),
            pl.BlockSpec((LANES, GLA_K_DIM), lambda b, h: (0, h)),
            pl.BlockSpec((1, GLA_K_DIM), lambda b, h: (0, h)),
            pl.BlockSpec((1, GLA_V_DIM), lambda b, h: (0, 0)),
            one(CB_GQ), one(CB_GK), two(CB_GV), two(CB_GG),
        ],
        out_specs=pl.BlockSpec((2, SEQ, LANES), lambda b, h: (h, b, 0)),
        out_shape=jax.ShapeDtypeStruct((2 * GLA_HEADS, TOKENS, LANES), BF16),
        scratch_shapes=[
            pltpu.VMEM((SEQ, GLA_K_DIM), BF16),
            pltpu.VMEM((SEQ, GLA_K_DIM), F32),
            pltpu.VMEM((GLA_NCHUNK, GLA_K_DIM, GLA_V_DIM), F32),
            pltpu.VMEM((GLA_NCHUNK, GLA_K_DIM, GLA_V_DIM), BF16),
        ],
        compiler_params=pltpu.CompilerParams(
            dimension_semantics=("arbitrary", "arbitrary"), vmem_limit_bytes=VMEM_LIMIT),
        name="gla_scan",
    )(g, w2_pad, b2, gw, p, p, p, p)


OUT_TM = 512
OUT_SPLITS = (256, 256)
assert sum(OUT_SPLITS) == OUT_TM


def _out_kernel(a_ref, b_ref, w_ref, x_ref, gate_ref, fw_ref, o_ref, wb_ref):
    @pl.when(pl.program_id(0) == 0)
    def _():
        wb_ref[...] = w_ref[...].astype(BF16)

    gate = gate_ref[0]
    fw = fw_ref[...]
    row0 = 0
    for n_rows in OUT_SPLITS:
        rows = pl.ds(row0, n_rows)
        row0 += n_rows
        mix = jnp.concatenate(
            [a_ref[h, rows, :] for h in range(a_ref.shape[0])]
            + [b_ref[h, rows, :] for h in range(b_ref.shape[0])], axis=1)
        y = jnp.dot(mix, wb_ref[...], preferred_element_type=F32)
        h_res = x_ref[rows, :] + gate * y
        ms = jnp.mean(h_res * h_res, axis=-1, keepdims=True)
        o_ref[rows, :] = h_res * lax.rsqrt(ms + NORM_EPS) * fw


def _out_call(a, b, w_out, xf, gate, fw):
    per_seq = SEQ // OUT_TM
    return pl.pallas_call(
        _out_kernel,
        grid=(TOKENS // OUT_TM,),
        in_specs=[
            pl.BlockSpec((DIFF_HEADS, OUT_TM, LANES), lambda i: (0, i, 0)),
            pl.BlockSpec((2 * GLA_HEADS, OUT_TM, LANES), lambda i: (0, i, 0)),
            pl.BlockSpec((D_MODEL, D_MODEL), lambda i: (0, 0), pipeline_mode=pl.Buffered(1)),
            pl.BlockSpec((OUT_TM, D_MODEL), lambda i: (i, 0)),
            pl.BlockSpec((1, 1, D_MODEL), lambda i: (i // per_seq, 0, 0)),
            pl.BlockSpec((1, D_MODEL), lambda i: (0, 0)),
        ],
        out_specs=pl.BlockSpec((OUT_TM, D_MODEL), lambda i: (i, 0)),
        out_shape=jax.ShapeDtypeStruct((TOKENS, D_MODEL), F32),
        scratch_shapes=[pltpu.VMEM((D_MODEL, D_MODEL), BF16)],
        compiler_params=pltpu.CompilerParams(
            dimension_semantics=("arbitrary",), vmem_limit_bytes=VMEM_LIMIT),
        name="out_proj_norm",
    )(a, b, w_out, xf, gate, fw)


def _rope_tables():
    inv_freq = ROPE_THETA ** (-np.arange(0, DIFF_QK_DIM, 2, dtype=np.float64) / DIFF_QK_DIM)
    ang = np.arange(SEQ, dtype=np.float64)[:, None] * inv_freq[None, :]
    cos, sin = np.cos(ang), np.sin(ang)
    cos_t = np.tile(cos, (1, 4)).astype(np.float32)
    sin_t = np.concatenate([-sin, sin, -sin, sin], axis=1).astype(np.float32)
    return jnp.asarray(cos_t), jnp.asarray(sin_t)


def kernel(x, c, norm_w, w_ada, b_ada, w_in, lambda_q1, lambda_k1, lambda_q2, lambda_k2,
           diff_norm_w, gla_gate_w2, gla_gate_b, gla_norm_w, w_out, final_norm_w):
    assert x.shape == (BATCH, SEQ, D_MODEL) and w_in.shape[0] == 1
    xf = x.reshape(TOKENS, D_MODEL).astype(F32)

    c_pad = jnp.pad(c.astype(F32), ((0, 8 - BATCH), (0, 0)))
    mod = _ada_call(c_pad, w_ada[0], b_ada[0][None, :])[:BATCH]
    shift = mod[:, :D_MODEL].reshape(BATCH, 1, D_MODEL)
    scale = mod[:, D_MODEL:2 * D_MODEL].reshape(BATCH, 1, D_MODEL)
    gate = mod[:, 2 * D_MODEL:].reshape(BATCH, 1, D_MODEL)

    w_t = w_in[0].T
    w_gate = jnp.pad(w_t[N_MAIN:], ((0, LANES - GLA_GATE_RANK), (0, 0))).astype(BF16)
    cos, sin = _rope_tables()
    p, g = _proj_call(xf, shift, scale, norm_w[0][None, :], w_t, w_gate, cos, sin)

    row = lambda v: v[0][None, :].astype(F32)
    a = _attn_call(p, row(lambda_q1), row(lambda_k1), row(lambda_q2), row(lambda_k2),
                   row(diff_norm_w))

    w2_pad = jnp.pad(gla_gate_w2[0], ((0, LANES - GLA_GATE_RANK), (0, 0))).astype(BF16)
    b = _gla_call(p, g, w2_pad, row(gla_gate_b), row(gla_norm_w))

    out = _out_call(a, b, w_out[0], xf, gate, final_norm_w[None, :].astype(F32))
    return out.reshape(BATCH, SEQ, D_MODEL).astype(x.dtype)
```

```python
import functools
import math

import jax
import jax.numpy as jnp
import numpy as np
from jax import lax
from jax.experimental import pallas as pl
from jax.experimental.pallas import tpu as pltpu

D_MODEL = 2048
BATCH = 4
SEQ = 2048
TOKENS = BATCH * SEQ
CHUNK = 64
LANES = 128

DIFF_HEADS = 8
DIFF_HEAD_DIM = 128
DIFF_QK_DIM = 64
GLA_HEADS = 4
GLA_K_DIM = 128
GLA_V_DIM = 256
GLA_GATE_RANK = 16
GLA_GATE_TAU = 16.0
ROPE_THETA = 10000.0
NORM_EPS = 1e-6
LAMBDA_INIT = 0.8 - 0.6 * math.exp(-0.3 * 0)

N_MAIN = 7168
N_COLBLK = N_MAIN // LANES
CB_DQ, CB_DK, CB_DV, CB_DG = 0, 8, 16, 24
CB_GQ, CB_GK, CB_GV, CB_GG = 32, 36, 40, 48

VMEM_LIMIT = 48 * 1024 * 1024
NEG_BIG = -1e30
LOG2_E = math.log2(math.e)

BF16 = jnp.bfloat16
F32 = jnp.float32


def _nt_dot(a, b):
    return lax.dot_general(a, b, (((1,), (1,)), ((), ())), preferred_element_type=F32)


def _silu(x):
    h = 0.5 * x
    return h + h * jnp.tanh(h)


def _tn_dot(a, b):
    return lax.dot_general(a, b, (((0,), (0,)), ((), ())), preferred_element_type=F32)


ADA_TN = 768


def _ada_kernel(c_ref, w_ref, b_ref, o_ref):
    c = c_ref[...]
    c_act = (c * jax.nn.sigmoid(c)).astype(BF16)
    w = w_ref[...].astype(BF16)
    o_ref[...] = jnp.dot(c_act, w, preferred_element_type=F32) + b_ref[...]


def _ada_call(c_pad, w_ada, b_ada):
    n = w_ada.shape[1]
    return pl.pallas_call(
        _ada_kernel,
        grid=(n // ADA_TN,),
        in_specs=[
            pl.BlockSpec((8, D_MODEL), lambda j: (0, 0)),
            pl.BlockSpec((D_MODEL, ADA_TN), lambda j: (0, j)),
            pl.BlockSpec((1, ADA_TN), lambda j: (0, j)),
        ],
        out_specs=pl.BlockSpec((8, ADA_TN), lambda j: (0, j)),
        out_shape=jax.ShapeDtypeStruct((8, n), F32),
        compiler_params=pltpu.CompilerParams(
            dimension_semantics=("arbitrary",), vmem_limit_bytes=VMEM_LIMIT),
        name="ada_mod",
    )(c_pad, w_ada, b_ada)


PROJ_TM = 1024
PROJ_TN = 1024
PROJ_SUB = 256
PROJ_ROPE_TILES = 2
PROJ_VMEM_LIMIT = (2 * PROJ_TM * D_MODEL * 4 + 2 * D_MODEL * PROJ_TN * 4
                   + PROJ_TM * D_MODEL * 2 + 2 * PROJ_TM * PROJ_TN * 2 + 16 * 1024 * 1024)


def _rot_half(x):
    lane = lax.broadcasted_iota(jnp.int32, x.shape, 1)
    first = (lane % DIFF_QK_DIM) < (DIFF_QK_DIM // 2)
    return jnp.where(first, pltpu.roll(x, LANES - 32, 1), pltpu.roll(x, 32, 1))


def _proj_kernel(x_ref, shift_ref, scale_ref, nw_ref, w_ref, wg_ref, cos_ref, sin_ref,
                 p_ref, g_ref, hn_ref):
    j = pl.program_id(1)
    n_sub = PROJ_TM // PROJ_SUB
    n_cb = PROJ_TN // LANES

    def norm_rows(rows):
        mult = nw_ref[...] * (1.0 + scale_ref[0])
        xs = x_ref[rows, :]
        ms = jnp.mean(xs * xs, axis=-1, keepdims=True)
        hn_ref[rows, :] = (xs * lax.rsqrt(ms + NORM_EPS) * mult + shift_ref[0]).astype(BF16)

    def store_rope(rows, acc, sc):
        cos = cos_ref[rows, :] * sc
        sin = sin_ref[rows, :] * sc
        for cb in range(n_cb):
            t = acc[:, cb * LANES:(cb + 1) * LANES]
            p_ref[cb, rows, :] = (t * cos + _rot_half(t) * sin).astype(BF16)

    @pl.when(j == 0)
    def _():
        for r in range(n_sub):
            rows = pl.ds(r * PROJ_SUB, PROJ_SUB)
            norm_rows(rows)
            acc = _nt_dot(hn_ref[rows, :], w_ref[...])
            store_rope(rows, acc, DIFF_QK_DIM ** -0.5 * LOG2_E)
        g_ref[...] = _nt_dot(hn_ref[...], wg_ref[...]).astype(BF16)

    @pl.when(j == 1)
    def _():
        for r in range(n_sub):
            rows = pl.ds(r * PROJ_SUB, PROJ_SUB)
            acc = _nt_dot(hn_ref[rows, :], w_ref[...])
            store_rope(rows, acc, 1.0)

    @pl.when(j >= PROJ_ROPE_TILES)
    def _():
        for r in range(n_sub):
            rows = pl.ds(r * PROJ_SUB, PROJ_SUB)
            acc = _nt_dot(hn_ref[rows, :], w_ref[...])
            for cb in range(n_cb):
                p_ref[cb, rows, :] = acc[:, cb * LANES:(cb + 1) * LANES].astype(BF16)


def _proj_call(xf, shift, scale, norm_w, w_t, w_gate, cos, sin):
    m_tiles = TOKENS // PROJ_TM
    n_tiles = N_MAIN // PROJ_TN
    per_seq = SEQ // PROJ_TM
    n_cb = PROJ_TN // LANES
    return pl.pallas_call(
        _proj_kernel,
        grid=(m_tiles, n_tiles),
        in_specs=[
            pl.BlockSpec((PROJ_TM, D_MODEL), lambda i, j: (i, 0)),
            pl.BlockSpec((1, 1, D_MODEL), lambda i, j: (i // per_seq, 0, 0)),
            pl.BlockSpec((1, 1, D_MODEL), lambda i, j: (i // per_seq, 0, 0)),
            pl.BlockSpec((1, D_MODEL), lambda i, j: (0, 0)),
            pl.BlockSpec((PROJ_TN, D_MODEL), lambda i, j: (j, 0)),
            pl.BlockSpec((LANES, D_MODEL), lambda i, j: (0, 0)),
            pl.BlockSpec((PROJ_TM, LANES), lambda i, j: (i % per_seq, 0)),
            pl.BlockSpec((PROJ_TM, LANES), lambda i, j: (i % per_seq, 0)),
        ],
        out_specs=[
            pl.BlockSpec((n_cb, PROJ_TM, LANES), lambda i, j: (j, i, 0)),
            pl.BlockSpec((PROJ_TM, LANES), lambda i, j: (i, 0)),
        ],
        out_shape=[
            jax.ShapeDtypeStruct((N_COLBLK, TOKENS, LANES), BF16),
            jax.ShapeDtypeStruct((TOKENS, LANES), BF16),
        ],
        scratch_shapes=[pltpu.VMEM((PROJ_TM, D_MODEL), BF16)],
        compiler_params=pltpu.CompilerParams(
            dimension_semantics=("arbitrary", "arbitrary"), vmem_limit_bytes=PROJ_VMEM_LIMIT),
        name="norm_in_proj",
    )(xf, shift, scale, norm_w, w_t, w_gate, cos, sin)


ATT_T = 256
ATT_NBLK = SEQ // ATT_T
ATT_LEAD = 24
assert ATT_LEAD >= ATT_NBLK - 1
ATT_ONES_ROWS = 16
ATT_HEADS_PER_STEP = 2
ATT_SLOTS = ATT_LEAD + 4


def _attn_kernel(lq1_ref, lk1_ref, lq2_ref, lk2_ref, dnw_ref, q_ref, k_ref, v_ref, dg_ref,
                 o_ref, vt_ref, s_ref):
    lam = (jnp.exp(jnp.sum(lq1_ref[...] * lk1_ref[...], keepdims=True))
           - jnp.exp(jnp.sum(lq2_ref[...] * lk2_ref[...], keepdims=True))
           + LAMBDA_INIT)
    out_gain = dnw_ref[...] * (1.0 - LAMBDA_INIT)

    for hd in range(ATT_HEADS_PER_STEP):
        for jb in range(ATT_NBLK):
            cols = pl.ds(jb * ATT_T, ATT_T)
            vb = v_ref[hd, jb * ATT_T:(jb + 1) * ATT_T, :].astype(F32)
            vt_ref[hd, 0:DIFF_HEAD_DIM, cols] = vb.T.astype(BF16)
        vt_ref[hd, DIFF_HEAD_DIM:, :] = jnp.ones((ATT_ONES_ROWS, SEQ), BF16)

    lane = lax.broadcasted_iota(jnp.int32, (ATT_T, LANES), 1)
    comp_a = lane < DIFF_QK_DIM
    krow = lax.broadcasted_iota(jnp.int32, (ATT_T, ATT_T), 0) // CHUNK
    qcol = lax.broadcasted_iota(jnp.int32, (ATT_T, ATT_T), 1) // CHUNK
    diag_mask = krow <= qcol
    diag_mask2 = jnp.concatenate([diag_mask, diag_mask], axis=1)

    units = [(hd, qi, j) for hd in range(ATT_HEADS_PER_STEP)
             for qi in range(ATT_NBLK) for j in range(qi + 1)]
    blocks = {(hd, qi): {"m": None, "p": []}
              for hd in range(ATT_HEADS_PER_STEP) for qi in range(ATT_NBLK)}

    def block_queries(hd, qi):
        q = q_ref[hd, pl.ds(qi * ATT_T, ATT_T), :]
        zero = jnp.zeros_like(q)
        return jnp.concatenate([jnp.where(comp_a, q, zero), jnp.where(comp_a, zero, q)], axis=0)

    dyn0 = lax.shift_right_logical(pl.program_id(0), 20)

    def score_unit(t, hd, qi, j):
        st = blocks[hd, qi]
        if j == 0:
            st["q2"] = block_queries(hd, qi)
        s = _nt_dot(k_ref[hd, pl.ds(j * ATT_T, ATT_T), :], st["q2"])
        if j == qi:
            s = jnp.where(diag_mask2, s, NEG_BIG)
        s_ref[dyn0 + t % ATT_SLOTS] = s
        mj = jnp.max(s, axis=0, keepdims=True)
        st["m"] = mj if st["m"] is None else jnp.maximum(st["m"], mj)

    def value_unit(t, hd, qi, j):
        st = blocks[hd, qi]
        p = jnp.exp2(s_ref[dyn0 + t % ATT_SLOTS] - st["m"])
        st["p"].append(p.astype(BF16))
        if j == qi:
            p_all = jnp.concatenate(st["p"], axis=0) if qi else st["p"][0]
            pv_l = jnp.dot(vt_ref[hd, :, 0:(qi + 1) * ATT_T], p_all,
                           preferred_element_type=F32)
            finish(hd, qi, pv_l)

    def finish(hd, qi, pv_l):
        rows = pl.ds(qi * ATT_T, ATT_T)
        l = pv_l[DIFF_HEAD_DIM:DIFF_HEAD_DIM + 1, :]
        pv = pv_l[0:DIFF_HEAD_DIM, :] * (1.0 / l)
        out_t = pv[:, :ATT_T] - lam * pv[:, ATT_T:]
        out = out_t.T
        ms = jnp.mean(out * out, axis=-1, keepdims=True)
        y = out * lax.rsqrt(ms + NORM_EPS) * out_gain
        g = dg_ref[hd, rows, :].astype(F32)
        o_ref[hd, rows, :] = (y * _silu(g)).astype(BF16)

    for t in range(len(units) + ATT_LEAD):
        if t < len(units):
            score_unit(t, *units[t])
        if t >= ATT_LEAD:
            value_unit(t - ATT_LEAD, *units[t - ATT_LEAD])


def _attn_call(p, lq1, lk1, lq2, lk2, dnw):
    hp = ATT_HEADS_PER_STEP
    vec = lambda n: pl.BlockSpec((1, n), lambda b, h: (0, 0))
    slab = lambda cb0: pl.BlockSpec((hp, SEQ, LANES), lambda b, h: (cb0 // hp + h, b, 0))
    return pl.pallas_call(
        _attn_kernel,
        grid=(BATCH, DIFF_HEADS // hp),
        in_specs=[vec(DIFF_QK_DIM)] * 4 + [vec(DIFF_HEAD_DIM),
                  slab(CB_DQ), slab(CB_DK), slab(CB_DV), slab(CB_DG)],
        out_specs=pl.BlockSpec((hp, SEQ, LANES), lambda b, h: (h, b, 0)),
        out_shape=jax.ShapeDtypeStruct((DIFF_HEADS, TOKENS, LANES), BF16),
        scratch_shapes=[
            pltpu.VMEM((hp, DIFF_HEAD_DIM + ATT_ONES_ROWS, SEQ), BF16),
            pltpu.VMEM((ATT_SLOTS, ATT_T, 2 * ATT_T), F32),
        ],
        compiler_params=pltpu.CompilerParams(
            dimension_semantics=("arbitrary", "arbitrary"), vmem_limit_bytes=VMEM_LIMIT),
        name="diff_attn",
    )(lq1, lk1, lq2, lk2, dnw, p, p, p, p)


GLA_BLK = 256
GLA_NCHUNK = SEQ // CHUNK
GLA_HEADS_PER_STEP = 2


def _split_hi_lo(x):
    hi = x.astype(BF16)
    lo = (x - hi.astype(F32)).astype(BF16)
    return hi, lo


def _gla_kernel(g_ref, w2_ref, b2_ref, gw_ref, q_ref, k_ref, v_ref, gg_ref, o_ref,
                kdec_ref, tot_ref, kv_ref, st_ref):
    for hd in range(GLA_HEADS_PER_STEP):
        _gla_head(g_ref, w2_ref.at[hd], b2_ref.at[hd], gw_ref, q_ref.at[hd], k_ref.at[hd],
                  v_ref.at[hd], gg_ref.at[hd], o_ref.at[hd], kdec_ref, tot_ref, kv_ref, st_ref)


def _gla_head(g_ref, w2_ref, b2_ref, gw_ref, q_ref, k_ref, v_ref, gg_ref, o_ref,
              kdec_ref, tot_ref, kv_ref, st_ref):
    z = jnp.dot(g_ref[...], w2_ref[...], preferred_element_type=F32) + b2_ref[...]
    log_a = (jnp.minimum(z, 0.0) - jnp.log(1.0 + jnp.exp(-jnp.abs(z)))) * (1.0 / GLA_GATE_TAU)

    r = lax.broadcasted_iota(jnp.int32, (GLA_BLK, GLA_BLK), 0)
    c = lax.broadcasted_iota(jnp.int32, (GLA_BLK, GLA_BLK), 1)
    after = jnp.where(((r // CHUNK) == (c // CHUNK)) & (c > r), 1.0, 0.0).astype(BF16)
    for blk in range(SEQ // GLA_BLK):
        rows = pl.ds(blk * GLA_BLK, GLA_BLK)
        la = log_a[blk * GLA_BLK:(blk + 1) * GLA_BLK, :]
        hi, lo = _split_hi_lo(la)
        both = jnp.dot(after, jnp.concatenate([hi, lo], axis=1), preferred_element_type=F32)
        suffix = both[:, :GLA_K_DIM] + both[:, GLA_K_DIM:]
        kdec_ref[rows, :] = (k_ref[rows, :].astype(F32) * jnp.exp(suffix)).astype(BF16)
        tot_ref[rows, :] = suffix + la

    dec_rows = jnp.exp(tot_ref[pl.ds(0, GLA_NCHUNK, stride=CHUNK), :])
    dec_cols = jnp.concatenate(
        [dec_rows, jnp.zeros((GLA_K_DIM - GLA_NCHUNK, GLA_K_DIM), F32)], axis=0).T

    dyn0 = lax.shift_right_logical(pl.program_id(0), 20)
    for ci in range(GLA_NCHUNK):
        rows = pl.ds(ci * CHUNK, CHUNK)
        v_c = jnp.concatenate([v_ref[0, rows, :], v_ref[1, rows, :]], axis=1)
        kv_ref[dyn0 + ci] = _tn_dot(kdec_ref[rows, :], v_c)

    state = jnp.zeros((GLA_K_DIM, GLA_V_DIM), F32)
    for ci in range(GLA_NCHUNK):
        state = state * dec_cols[:, ci:ci + 1] + kv_ref[dyn0 + ci]
        st_ref[ci] = state.astype(BF16)

    gw = gw_ref[...]
    qs = GLA_K_DIM ** -0.5
    for ci in range(GLA_NCHUNK):
        rows = pl.ds(ci * CHUNK, CHUNK)
        o = jnp.dot(q_ref[rows, :], st_ref[ci], preferred_element_type=F32)
        ms = jnp.mean(o * o, axis=-1, keepdims=True) * (qs * qs)
        y = o * (qs * lax.rsqrt(ms + NORM_EPS)) * gw
        gate = jnp.concatenate([gg_ref[0, rows, :], gg_ref[1, rows, :]], axis=1).astype(F32)
        y = y * _silu(gate)
        o_ref[0, rows, :] = y[:, :LANES].astype(BF16)
        o_ref[1, rows, :] = y[:, LANES:].astype(BF16)


def _gla_call(p, g, w2_heads, b2_heads, gw):
    hp = GLA_HEADS_PER_STEP
    qk = lambda cb0: pl.BlockSpec((hp, SEQ, LANES), lambda b, h: (cb0 // hp + h, b, 0))
    p4 = p.reshape(N_COLBLK // 2, 2, TOKENS, LANES)
    vg = lambda cb0: pl.BlockSpec((hp, 2, SEQ, LANES), lambda b, h: (cb0 // 2 // hp + h, 0, b, 0))
    out = pl.pallas_call(
        _gla_kernel,
        grid=(BATCH, GLA_HEADS // hp),
        in_specs=[
            pl.BlockSpec((SEQ, LANES), lambda b, h: (b, 0)),
            pl.BlockSpec((hp, LANES, GLA_K_DIM), lambda b, h: (h, 0, 0)),
            pl.BlockSpec((hp, 1, GLA_K_DIM), lambda b, h: (h, 0, 0)),
            pl.BlockSpec((1, GLA_V_DIM), lambda b, h: (0, 0)),
            qk(CB_GQ), qk(CB_GK), vg(CB_GV), vg(CB_GG),
        ],
        out_specs=pl.BlockSpec((hp, 2, SEQ, LANES), lambda b, h: (h, 0, b, 0)),
        out_shape=jax.ShapeDtypeStruct((GLA_HEADS, 2, TOKENS, LANES), BF16),
        scratch_shapes=[
            pltpu.VMEM((SEQ, GLA_K_DIM), BF16),
            pltpu.VMEM((SEQ, GLA_K_DIM), F32),
            pltpu.VMEM((GLA_NCHUNK, GLA_K_DIM, GLA_V_DIM), F32),
            pltpu.VMEM((GLA_NCHUNK, GLA_K_DIM, GLA_V_DIM), BF16),
        ],
        compiler_params=pltpu.CompilerParams(
            dimension_semantics=("arbitrary", "arbitrary"), vmem_limit_bytes=VMEM_LIMIT),
        name="gla_scan",
    )(g, w2_heads, b2_heads, gw, p, p, p4, p4)
    return out.reshape(2 * GLA_HEADS, TOKENS, LANES)


OUT_TM = 512
OUT_SPLITS = (256, 256)
assert sum(OUT_SPLITS) == OUT_TM


def _out_kernel(a_ref, b_ref, w_ref, x_ref, gate_ref, fw_ref, o_ref, wb_ref):
    @pl.when(pl.program_id(0) == 0)
    def _():
        wb_ref[...] = w_ref[...].astype(BF16)

    gate = gate_ref[0]
    fw = fw_ref[...]
    row0 = 0
    for n_rows in OUT_SPLITS:
        rows = pl.ds(row0, n_rows)
        row0 += n_rows
        mix = jnp.concatenate(
            [a_ref[h, rows, :] for h in range(a_ref.shape[0])]
            + [b_ref[h, rows, :] for h in range(b_ref.shape[0])], axis=1)
        y = jnp.dot(mix, wb_ref[...], preferred_element_type=F32)
        h_res = x_ref[rows, :] + gate * y
        ms = jnp.mean(h_res * h_res, axis=-1, keepdims=True)
        o_ref[rows, :] = h_res * lax.rsqrt(ms + NORM_EPS) * fw


def _out_call(a, b, w_out, xf, gate, fw):
    per_seq = SEQ // OUT_TM
    return pl.pallas_call(
        _out_kernel,
        grid=(TOKENS // OUT_TM,),
        in_specs=[
            pl.BlockSpec((DIFF_HEADS, OUT_TM, LANES), lambda i: (0, i, 0)),
            pl.BlockSpec((2 * GLA_HEADS, OUT_TM, LANES), lambda i: (0, i, 0)),
            pl.BlockSpec((D_MODEL, D_MODEL), lambda i: (0, 0), pipeline_mode=pl.Buffered(1)),
            pl.BlockSpec((OUT_TM, D_MODEL), lambda i: (i, 0)),
            pl.BlockSpec((1, 1, D_MODEL), lambda i: (i // per_seq, 0, 0)),
            pl.BlockSpec((1, D_MODEL), lambda i: (0, 0)),
        ],
        out_specs=pl.BlockSpec((OUT_TM, D_MODEL), lambda i: (i, 0)),
        out_shape=jax.ShapeDtypeStruct((TOKENS, D_MODEL), F32),
        scratch_shapes=[pltpu.VMEM((D_MODEL, D_MODEL), BF16)],
        compiler_params=pltpu.CompilerParams(
            dimension_semantics=("arbitrary",), vmem_limit_bytes=VMEM_LIMIT),
        name="out_proj_norm",
    )(a, b, w_out, xf, gate, fw)


def _rope_tables():
    inv_freq = ROPE_THETA ** (-np.arange(0, DIFF_QK_DIM, 2, dtype=np.float64) / DIFF_QK_DIM)
    ang = np.arange(SEQ, dtype=np.float64)[:, None] * inv_freq[None, :]
    cos, sin = np.cos(ang), np.sin(ang)
    cos_t = np.tile(cos, (1, 4)).astype(np.float32)
    sin_t = np.concatenate([-sin, sin, -sin, sin], axis=1).astype(np.float32)
    return jnp.asarray(cos_t), jnp.asarray(sin_t)


def kernel(x, c, norm_w, w_ada, b_ada, w_in, lambda_q1, lambda_k1, lambda_q2, lambda_k2,
           diff_norm_w, gla_gate_w2, gla_gate_b, gla_norm_w, w_out, final_norm_w):
    assert x.shape == (BATCH, SEQ, D_MODEL) and w_in.shape[0] == 1
    xf = x.reshape(TOKENS, D_MODEL).astype(F32)

    c_pad = jnp.pad(c.astype(F32), ((0, 8 - BATCH), (0, 0)))
    mod = _ada_call(c_pad, w_ada[0], b_ada[0][None, :])[:BATCH]
    shift = mod[:, :D_MODEL].reshape(BATCH, 1, D_MODEL)
    scale = mod[:, D_MODEL:2 * D_MODEL].reshape(BATCH, 1, D_MODEL)
    gate = mod[:, 2 * D_MODEL:].reshape(BATCH, 1, D_MODEL)

    w_t = w_in[0].T
    w_gate = jnp.pad(w_t[N_MAIN:], ((0, LANES - GLA_GATE_RANK), (0, 0))).astype(BF16)
    cos, sin = _rope_tables()
    p, g = _proj_call(xf, shift, scale, norm_w[0][None, :], w_t, w_gate, cos, sin)

    row = lambda v: v[0][None, :].astype(F32)
    a = _attn_call(p, row(lambda_q1), row(lambda_k1), row(lambda_q2), row(lambda_k2),
                   row(diff_norm_w))

    w2_heads = jnp.pad(gla_gate_w2[0], ((0, LANES - GLA_GATE_RANK), (0, 0))).astype(BF16)
    w2_heads = w2_heads.reshape(LANES, GLA_HEADS, GLA_K_DIM).transpose(1, 0, 2)
    b2_heads = gla_gate_b[0].astype(F32).reshape(GLA_HEADS, 1, GLA_K_DIM)
    b = _gla_call(p, g, w2_heads, b2_heads, row(gla_norm_w))

    out = _out_call(a, b, w_out[0], xf, gate, final_norm_w[None, :].astype(F32))
    return out.reshape(BATCH, SEQ, D_MODEL).astype(x.dtype)
```

```python
import functools
import math

import jax
import jax.numpy as jnp
import numpy as np
from jax import lax
from jax.experimental import pallas as pl
from jax.experimental.pallas import tpu as pltpu

D_MODEL = 2048
BATCH = 4
SEQ = 2048
TOKENS = BATCH * SEQ
CHUNK = 64
LANES = 128

DIFF_HEADS = 8
DIFF_HEAD_DIM = 128
DIFF_QK_DIM = 64
GLA_HEADS = 4
GLA_K_DIM = 128
GLA_V_DIM = 256
GLA_GATE_RANK = 16
GLA_GATE_TAU = 16.0
ROPE_THETA = 10000.0
NORM_EPS = 1e-6
LAMBDA_INIT = 0.8 - 0.6 * math.exp(-0.3 * 0)

N_MAIN = 7168
N_COLBLK = N_MAIN // LANES
CB_DQ, CB_DK, CB_DV, CB_DG = 0, 8, 16, 24
CB_GQ, CB_GK, CB_GV, CB_GG = 32, 36, 40, 48

VMEM_LIMIT = 48 * 1024 * 1024
NEG_BIG = -1e30
LOG2_E = math.log2(math.e)

BF16 = jnp.bfloat16
F32 = jnp.float32


def _nt_dot(a, b):
    return lax.dot_general(a, b, (((1,), (1,)), ((), ())), preferred_element_type=F32)


def _silu(x):
    h = 0.5 * x
    return h + h * jnp.tanh(h)


def _tn_dot(a, b):
    return lax.dot_general(a, b, (((0,), (0,)), ((), ())), preferred_element_type=F32)


ADA_TN = 1024


def _ada_block(c_ref, w_ref, b_ref):
    c = c_ref[...]
    c_act = (c * jax.nn.sigmoid(c)).astype(BF16)
    return jnp.dot(c_act, w_ref[...].astype(BF16), preferred_element_type=F32) + b_ref[...]


def _ada_kernel(c_ref, w_ref, b_ref, o_ref):
    o_ref[...] = _ada_block(c_ref, w_ref, b_ref)


def _ada_call(c_pad, w_ada, b_ada, n):
    return pl.pallas_call(
        _ada_kernel,
        grid=(n // ADA_TN,),
        in_specs=[
            pl.BlockSpec((8, D_MODEL), lambda j: (0, 0)),
            pl.BlockSpec((D_MODEL, ADA_TN), lambda j: (0, j)),
            pl.BlockSpec((1, ADA_TN), lambda j: (0, j)),
        ],
        out_specs=pl.BlockSpec((8, ADA_TN), lambda j: (0, j)),
        out_shape=jax.ShapeDtypeStruct((8, n), F32),
        compiler_params=pltpu.CompilerParams(
            dimension_semantics=("arbitrary",), vmem_limit_bytes=VMEM_LIMIT),
        name="ada_mod",
    )(c_pad, w_ada, b_ada)


PROJ_TM = 1024
PROJ_TN = 1024
PROJ_SUB = 256
PROJ_ROPE_TILES = 2
PROJ_VMEM_LIMIT = (2 * PROJ_TM * D_MODEL * 4 + 2 * D_MODEL * PROJ_TN * 4
                   + PROJ_TM * D_MODEL * 2 + 2 * PROJ_TM * PROJ_TN * 2 + 16 * 1024 * 1024)


def _rot_half(x):
    lane = lax.broadcasted_iota(jnp.int32, x.shape, 1)
    first = (lane % DIFF_QK_DIM) < (DIFF_QK_DIM // 2)
    return jnp.where(first, pltpu.roll(x, LANES - 32, 1), pltpu.roll(x, 32, 1))


def _proj_kernel(x_ref, shift_ref, scale_ref, nw_ref, w_ref, wg_ref, cos_ref, sin_ref,
                 p_ref, g_ref, hn_ref):
    j = pl.program_id(1)
    n_sub = PROJ_TM // PROJ_SUB
    n_cb = PROJ_TN // LANES

    def norm_rows(rows):
        mult = nw_ref[...] * (1.0 + scale_ref[0])
        xs = x_ref[rows, :]
        ms = jnp.mean(xs * xs, axis=-1, keepdims=True)
        hn_ref[rows, :] = (xs * lax.rsqrt(ms + NORM_EPS) * mult + shift_ref[0]).astype(BF16)

    def store_rope(rows, acc, sc):
        cos = cos_ref[rows, :] * sc
        sin = sin_ref[rows, :] * sc
        for cb in range(n_cb):
            t = acc[:, cb * LANES:(cb + 1) * LANES]
            p_ref[cb, rows, :] = (t * cos + _rot_half(t) * sin).astype(BF16)

    @pl.when(j == 0)
    def _():
        for r in range(n_sub):
            rows = pl.ds(r * PROJ_SUB, PROJ_SUB)
            norm_rows(rows)
            acc = _nt_dot(hn_ref[rows, :], w_ref[...])
            store_rope(rows, acc, DIFF_QK_DIM ** -0.5 * LOG2_E)
        g_ref[...] = _nt_dot(hn_ref[...], wg_ref[...]).astype(BF16)

    @pl.when(j == 1)
    def _():
        for r in range(n_sub):
            rows = pl.ds(r * PROJ_SUB, PROJ_SUB)
            acc = _nt_dot(hn_ref[rows, :], w_ref[...])
            store_rope(rows, acc, 1.0)

    @pl.when(j >= PROJ_ROPE_TILES)
    def _():
        for r in range(n_sub):
            rows = pl.ds(r * PROJ_SUB, PROJ_SUB)
            acc = _nt_dot(hn_ref[rows, :], w_ref[...])
            for cb in range(n_cb):
                p_ref[cb, rows, :] = acc[:, cb * LANES:(cb + 1) * LANES].astype(BF16)


def _proj_call(xf, shift, scale, norm_w, w_t, w_gate, cos, sin):
    m_tiles = TOKENS // PROJ_TM
    n_tiles = N_MAIN // PROJ_TN
    per_seq = SEQ // PROJ_TM
    n_cb = PROJ_TN // LANES
    return pl.pallas_call(
        _proj_kernel,
        grid=(m_tiles, n_tiles),
        in_specs=[
            pl.BlockSpec((PROJ_TM, D_MODEL), lambda i, j: (i, 0)),
            pl.BlockSpec((1, 1, D_MODEL), lambda i, j: (i // per_seq, 0, 0)),
            pl.BlockSpec((1, 1, D_MODEL), lambda i, j: (i // per_seq, 0, 0)),
            pl.BlockSpec((1, D_MODEL), lambda i, j: (0, 0)),
            pl.BlockSpec((PROJ_TN, D_MODEL), lambda i, j: (j, 0)),
            pl.BlockSpec((LANES, D_MODEL), lambda i, j: (0, 0)),
            pl.BlockSpec((PROJ_TM, LANES), lambda i, j: (i % per_seq, 0)),
            pl.BlockSpec((PROJ_TM, LANES), lambda i, j: (i % per_seq, 0)),
        ],
        out_specs=[
            pl.BlockSpec((n_cb, PROJ_TM, LANES), lambda i, j: (j, i, 0)),
            pl.BlockSpec((PROJ_TM, LANES), lambda i, j: (i, 0)),
        ],
        out_shape=[
            jax.ShapeDtypeStruct((N_COLBLK, TOKENS, LANES), BF16),
            jax.ShapeDtypeStruct((TOKENS, LANES), BF16),
        ],
        scratch_shapes=[pltpu.VMEM((PROJ_TM, D_MODEL), BF16)],
        compiler_params=pltpu.CompilerParams(
            dimension_semantics=("arbitrary", "arbitrary"), vmem_limit_bytes=PROJ_VMEM_LIMIT),
        name="norm_in_proj",
    )(xf, shift, scale, norm_w, w_t, w_gate, cos, sin)


ATT_T = 256
ATT_NBLK = SEQ // ATT_T
ATT_LEAD = 24
assert ATT_LEAD >= ATT_NBLK - 1
ATT_ONES_ROWS = 16
ATT_HEADS_PER_STEP = 2
ATT_SLOTS = ATT_LEAD + 4


def _attn_kernel(lq1_ref, lk1_ref, lq2_ref, lk2_ref, dnw_ref, q_ref, k_ref, v_ref, dg_ref,
                 c_ref, wa_ref, ba_ref, o_ref, gate_ref, vt_ref, s_ref):
    gate_ref[...] = _ada_block(c_ref, wa_ref, ba_ref)

    lam = (jnp.exp(jnp.sum(lq1_ref[...] * lk1_ref[...], keepdims=True))
           - jnp.exp(jnp.sum(lq2_ref[...] * lk2_ref[...], keepdims=True))
           + LAMBDA_INIT)
    out_gain = dnw_ref[...] * (1.0 - LAMBDA_INIT)

    for hd in range(ATT_HEADS_PER_STEP):
        for jb in range(ATT_NBLK):
            cols = pl.ds(jb * ATT_T, ATT_T)
            vb = v_ref[hd, jb * ATT_T:(jb + 1) * ATT_T, :].astype(F32)
            vt_ref[hd, 0:DIFF_HEAD_DIM, cols] = vb.T.astype(BF16)
        vt_ref[hd, DIFF_HEAD_DIM:, :] = jnp.ones((ATT_ONES_ROWS, SEQ), BF16)

    lane = lax.broadcasted_iota(jnp.int32, (ATT_T, LANES), 1)
    comp_a = lane < DIFF_QK_DIM
    krow = lax.broadcasted_iota(jnp.int32, (ATT_T, ATT_T), 0) // CHUNK
    qcol = lax.broadcasted_iota(jnp.int32, (ATT_T, ATT_T), 1) // CHUNK
    diag_mask = krow <= qcol
    diag_mask2 = jnp.concatenate([diag_mask, diag_mask], axis=1)

    units = [(hd, qi, j) for hd in range(ATT_HEADS_PER_STEP)
             for qi in range(ATT_NBLK) for j in range(qi + 1)]
    blocks = {(hd, qi): {"m": None, "p": []}
              for hd in range(ATT_HEADS_PER_STEP) for qi in range(ATT_NBLK)}

    def block_queries(hd, qi):
        q = q_ref[hd, pl.ds(qi * ATT_T, ATT_T), :]
        zero = jnp.zeros_like(q)
        return jnp.concatenate([jnp.where(comp_a, q, zero), jnp.where(comp_a, zero, q)], axis=0)

    dyn0 = lax.shift_right_logical(pl.program_id(0), 20)

    def score_unit(t, hd, qi, j):
        st = blocks[hd, qi]
        if j == 0:
            st["q2"] = block_queries(hd, qi)
        s = _nt_dot(k_ref[hd, pl.ds(j * ATT_T, ATT_T), :], st["q2"])
        if j == qi:
            s = jnp.where(diag_mask2, s, NEG_BIG)
        s_ref[dyn0 + t % ATT_SLOTS] = s
        mj = jnp.max(s, axis=0, keepdims=True)
        st["m"] = mj if st["m"] is None else jnp.maximum(st["m"], mj)

    def value_unit(t, hd, qi, j):
        st = blocks[hd, qi]
        p = jnp.exp2(s_ref[dyn0 + t % ATT_SLOTS] - st["m"])
        st["p"].append(p.astype(BF16))
        if j == qi:
            p_all = jnp.concatenate(st["p"], axis=0) if qi else st["p"][0]
            pv_l = jnp.dot(vt_ref[hd, :, 0:(qi + 1) * ATT_T], p_all,
                           preferred_element_type=F32)
            finish(hd, qi, pv_l)

    def finish(hd, qi, pv_l):
        rows = pl.ds(qi * ATT_T, ATT_T)
        l = pv_l[DIFF_HEAD_DIM:DIFF_HEAD_DIM + 1, :]
        pv = pv_l[0:DIFF_HEAD_DIM, :] * (1.0 / l)
        out_t = pv[:, :ATT_T] - lam * pv[:, ATT_T:]
        out = out_t.T
        ms = jnp.mean(out * out, axis=-1, keepdims=True)
        y = out * lax.rsqrt(ms + NORM_EPS) * out_gain
        g = dg_ref[hd, rows, :].astype(F32)
        o_ref[hd, rows, :] = (y * _silu(g)).astype(BF16)

    for t in range(len(units) + ATT_LEAD):
        if t < len(units):
            score_unit(t, *units[t])
        if t >= ATT_LEAD:
            value_unit(t - ATT_LEAD, *units[t - ATT_LEAD])


def _attn_call(p, lq1, lk1, lq2, lk2, dnw, c_pad, w_ada, b_ada):
    hp = ATT_HEADS_PER_STEP
    steps_per_batch = DIFF_HEADS // hp
    assert BATCH * steps_per_batch * LANES == D_MODEL
    gate_cb0 = 2 * D_MODEL // LANES
    vec = lambda n: pl.BlockSpec((1, n), lambda b, h: (0, 0))
    slab = lambda cb0: pl.BlockSpec((hp, SEQ, LANES), lambda b, h: (cb0 // hp + h, b, 0))
    gate_blk = lambda rows, cb0: pl.BlockSpec(
        (rows, LANES), lambda b, h: (0, cb0 + b * steps_per_batch + h))
    return pl.pallas_call(
        _attn_kernel,
        grid=(BATCH, steps_per_batch),
        in_specs=[vec(DIFF_QK_DIM)] * 4 + [vec(DIFF_HEAD_DIM),
                  slab(CB_DQ), slab(CB_DK), slab(CB_DV), slab(CB_DG),
                  pl.BlockSpec((8, D_MODEL), lambda b, h: (0, 0)),
                  gate_blk(D_MODEL, gate_cb0), gate_blk(1, gate_cb0)],
        out_specs=[pl.BlockSpec((hp, SEQ, LANES), lambda b, h: (h, b, 0)), gate_blk(8, 0)],
        out_shape=[jax.ShapeDtypeStruct((DIFF_HEADS, TOKENS, LANES), BF16),
                   jax.ShapeDtypeStruct((8, D_MODEL), F32)],
        scratch_shapes=[
            pltpu.VMEM((hp, DIFF_HEAD_DIM + ATT_ONES_ROWS, SEQ), BF16),
            pltpu.VMEM((ATT_SLOTS, ATT_T, 2 * ATT_T), F32),
        ],
        compiler_params=pltpu.CompilerParams(
            dimension_semantics=("arbitrary", "arbitrary"), vmem_limit_bytes=VMEM_LIMIT),
        name="diff_attn",
    )(lq1, lk1, lq2, lk2, dnw, p, p, p, p, c_pad, w_ada, b_ada)


GLA_BLK = 256
GLA_NCHUNK = SEQ // CHUNK
GLA_HEADS_PER_STEP = 2


def _split_hi_lo(x):
    hi = x.astype(BF16)
    lo = (x - hi.astype(F32)).astype(BF16)
    return hi, lo


def _gla_kernel(g_ref, w2_ref, b2_ref, gw_ref, q_ref, k_ref, v_ref, gg_ref, o_ref,
                kdec_ref, tot_ref, kv_ref, st_ref):
    for hd in range(GLA_HEADS_PER_STEP):
        _gla_head(g_ref, w2_ref.at[hd], b2_ref.at[hd], gw_ref, q_ref.at[hd], k_ref.at[hd],
                  v_ref.at[hd], gg_ref.at[hd], o_ref.at[hd], kdec_ref, tot_ref, kv_ref, st_ref)


def _gla_head(g_ref, w2_ref, b2_ref, gw_ref, q_ref, k_ref, v_ref, gg_ref, o_ref,
              kdec_ref, tot_ref, kv_ref, st_ref):
    z = jnp.dot(g_ref[...], w2_ref[...], preferred_element_type=F32) + b2_ref[...]
    log_a = (jnp.minimum(z, 0.0) - jnp.log(1.0 + jnp.exp(-jnp.abs(z)))) * (1.0 / GLA_GATE_TAU)

    r = lax.broadcasted_iota(jnp.int32, (GLA_BLK, GLA_BLK), 0)
    c = lax.broadcasted_iota(jnp.int32, (GLA_BLK, GLA_BLK), 1)
    after = jnp.where(((r // CHUNK) == (c // CHUNK)) & (c > r), 1.0, 0.0).astype(BF16)
    for blk in range(SEQ // GLA_BLK):
        rows = pl.ds(blk * GLA_BLK, GLA_BLK)
        la = log_a[blk * GLA_BLK:(blk + 1) * GLA_BLK, :]
        hi, lo = _split_hi_lo(la)
        both = jnp.dot(after, jnp.concatenate([hi, lo], axis=1), preferred_element_type=F32)
        suffix = both[:, :GLA_K_DIM] + both[:, GLA_K_DIM:]
        kdec_ref[rows, :] = (k_ref[rows, :].astype(F32) * jnp.exp(suffix)).astype(BF16)
        tot_ref[rows, :] = suffix + la

    dec_rows = jnp.exp(tot_ref[pl.ds(0, GLA_NCHUNK, stride=CHUNK), :])
    dec_cols = jnp.concatenate(
        [dec_rows, jnp.zeros((GLA_K_DIM - GLA_NCHUNK, GLA_K_DIM), F32)], axis=0).T

    dyn0 = lax.shift_right_logical(pl.program_id(0), 20)
    for ci in range(GLA_NCHUNK):
        rows = pl.ds(ci * CHUNK, CHUNK)
        v_c = jnp.concatenate([v_ref[0, rows, :], v_ref[1, rows, :]], axis=1)
        kv_ref[dyn0 + ci] = _tn_dot(kdec_ref[rows, :], v_c)

    state = jnp.zeros((GLA_K_DIM, GLA_V_DIM), F32)
    for ci in range(GLA_NCHUNK):
        state = state * dec_cols[:, ci:ci + 1] + kv_ref[dyn0 + ci]
        st_ref[ci] = state.astype(BF16)

    gw = gw_ref[...]
    qs = GLA_K_DIM ** -0.5
    for ci in range(GLA_NCHUNK):
        rows = pl.ds(ci * CHUNK, CHUNK)
        o = jnp.dot(q_ref[rows, :], st_ref[ci], preferred_element_type=F32)
        ms = jnp.mean(o * o, axis=-1, keepdims=True) * (qs * qs)
        y = o * (qs * lax.rsqrt(ms + NORM_EPS)) * gw
        gate = jnp.concatenate([gg_ref[0, rows, :], gg_ref[1, rows, :]], axis=1).astype(F32)
        y = y * _silu(gate)
        o_ref[0, rows, :] = y[:, :LANES].astype(BF16)
        o_ref[1, rows, :] = y[:, LANES:].astype(BF16)


def _gla_call(p, g, w2_heads, b2_heads, gw):
    hp = GLA_HEADS_PER_STEP
    qk = lambda cb0: pl.BlockSpec((hp, SEQ, LANES), lambda b, h: (cb0 // hp + h, b, 0))
    p4 = p.reshape(N_COLBLK // 2, 2, TOKENS, LANES)
    vg = lambda cb0: pl.BlockSpec((hp, 2, SEQ, LANES), lambda b, h: (cb0 // 2 // hp + h, 0, b, 0))
    out = pl.pallas_call(
        _gla_kernel,
        grid=(BATCH, GLA_HEADS // hp),
        in_specs=[
            pl.BlockSpec((SEQ, LANES), lambda b, h: (b, 0)),
            pl.BlockSpec((hp, LANES, GLA_K_DIM), lambda b, h: (h, 0, 0)),
            pl.BlockSpec((hp, 1, GLA_K_DIM), lambda b, h: (h, 0, 0)),
            pl.BlockSpec((1, GLA_V_DIM), lambda b, h: (0, 0)),
            qk(CB_GQ), qk(CB_GK), vg(CB_GV), vg(CB_GG),
        ],
        out_specs=pl.BlockSpec((hp, 2, SEQ, LANES), lambda b, h: (h, 0, b, 0)),
        out_shape=jax.ShapeDtypeStruct((GLA_HEADS, 2, TOKENS, LANES), BF16),
        scratch_shapes=[
            pltpu.VMEM((SEQ, GLA_K_DIM), BF16),
            pltpu.VMEM((SEQ, GLA_K_DIM), F32),
            pltpu.VMEM((GLA_NCHUNK, GLA_K_DIM, GLA_V_DIM), F32),
            pltpu.VMEM((GLA_NCHUNK, GLA_K_DIM, GLA_V_DIM), BF16),
        ],
        compiler_params=pltpu.CompilerParams(
            dimension_semantics=("arbitrary", "arbitrary"), vmem_limit_bytes=VMEM_LIMIT),
        name="gla_scan",
    )(g, w2_heads, b2_heads, gw, p, p, p4, p4)
    return out.reshape(2 * GLA_HEADS, TOKENS, LANES)


OUT_TM = 512
OUT_SPLITS = (256, 256)
assert sum(OUT_SPLITS) == OUT_TM


def _out_kernel(a_ref, b_ref, w_ref, x_ref, gate_ref, fw_ref, o_ref, wb_ref):
    @pl.when(pl.program_id(0) == 0)
    def _():
        wb_ref[...] = w_ref[...].astype(BF16)

    gate = gate_ref[0]
    fw = fw_ref[...]
    row0 = 0
    for n_rows in OUT_SPLITS:
        rows = pl.ds(row0, n_rows)
        row0 += n_rows
        mix = jnp.concatenate(
            [a_ref[h, rows, :] for h in range(a_ref.shape[0])]
            + [b_ref[h, rows, :] for h in range(b_ref.shape[0])], axis=1)
        y = jnp.dot(mix, wb_ref[...], preferred_element_type=F32)
        h_res = x_ref[rows, :] + gate * y
        ms = jnp.mean(h_res * h_res, axis=-1, keepdims=True)
        o_ref[rows, :] = h_res * lax.rsqrt(ms + NORM_EPS) * fw


def _out_call(a, b, w_out, xf, gate, fw):
    per_seq = SEQ // OUT_TM
    return pl.pallas_call(
        _out_kernel,
        grid=(TOKENS // OUT_TM,),
        in_specs=[
            pl.BlockSpec((DIFF_HEADS, OUT_TM, LANES), lambda i: (0, i, 0)),
            pl.BlockSpec((2 * GLA_HEADS, OUT_TM, LANES), lambda i: (0, i, 0)),
            pl.BlockSpec((D_MODEL, D_MODEL), lambda i: (0, 0), pipeline_mode=pl.Buffered(1)),
            pl.BlockSpec((OUT_TM, D_MODEL), lambda i: (i, 0)),
            pl.BlockSpec((1, 1, D_MODEL), lambda i: (i // per_seq, 0, 0)),
            pl.BlockSpec((1, D_MODEL), lambda i: (0, 0)),
        ],
        out_specs=pl.BlockSpec((OUT_TM, D_MODEL), lambda i: (i, 0)),
        out_shape=jax.ShapeDtypeStruct((TOKENS, D_MODEL), F32),
        scratch_shapes=[pltpu.VMEM((D_MODEL, D_MODEL), BF16)],
        compiler_params=pltpu.CompilerParams(
            dimension_semantics=("arbitrary",), vmem_limit_bytes=VMEM_LIMIT),
        name="out_proj_norm",
    )(a, b, w_out, xf, gate, fw)


def _rope_tables():
    inv_freq = ROPE_THETA ** (-np.arange(0, DIFF_QK_DIM, 2, dtype=np.float64) / DIFF_QK_DIM)
    ang = np.arange(SEQ, dtype=np.float64)[:, None] * inv_freq[None, :]
    cos, sin = np.cos(ang), np.sin(ang)
    cos_t = np.tile(cos, (1, 4)).astype(np.float32)
    sin_t = np.concatenate([-sin, sin, -sin, sin], axis=1).astype(np.float32)
    return jnp.asarray(cos_t), jnp.asarray(sin_t)


def kernel(x, c, norm_w, w_ada, b_ada, w_in, lambda_q1, lambda_k1, lambda_q2, lambda_k2,
           diff_norm_w, gla_gate_w2, gla_gate_b, gla_norm_w, w_out, final_norm_w):
    assert x.shape == (BATCH, SEQ, D_MODEL) and w_in.shape[0] == 1
    xf = x.reshape(TOKENS, D_MODEL).astype(F32)

    c_pad = jnp.pad(c.astype(F32), ((0, 8 - BATCH), (0, 0)))
    b_ada_row = b_ada[0][None, :]
    mod = _ada_call(c_pad, w_ada[0], b_ada_row, 2 * D_MODEL)[:BATCH]
    shift = mod[:, :D_MODEL].reshape(BATCH, 1, D_MODEL)
    scale = mod[:, D_MODEL:].reshape(BATCH, 1, D_MODEL)

    w_t = w_in[0].T
    w_gate = jnp.pad(w_t[N_MAIN:], ((0, LANES - GLA_GATE_RANK), (0, 0))).astype(BF16)
    cos, sin = _rope_tables()
    p, g = _proj_call(xf, shift, scale, norm_w[0][None, :], w_t, w_gate, cos, sin)

    row = lambda v: v[0][None, :].astype(F32)
    a, gate8 = _attn_call(p, row(lambda_q1), row(lambda_k1), row(lambda_q2), row(lambda_k2),
                          row(diff_norm_w), c_pad, w_ada[0], b_ada_row)
    gate = gate8[:BATCH].reshape(BATCH, 1, D_MODEL)

    w2_heads = jnp.pad(gla_gate_w2[0], ((0, LANES - GLA_GATE_RANK), (0, 0))).astype(BF16)
    w2_heads = w2_heads.reshape(LANES, GLA_HEADS, GLA_K_DIM).transpose(1, 0, 2)
    b2_heads = gla_gate_b[0].astype(F32).reshape(GLA_HEADS, 1, GLA_K_DIM)
    b = _gla_call(p, g, w2_heads, b2_heads, row(gla_norm_w))

    out = _out_call(a, b, w_out[0], xf, gate, final_norm_w[None, :].astype(F32))
    return out.reshape(BATCH, SEQ, D_MODEL).astype(x.dtype)
```

```python
import functools
import math

import jax
import jax.numpy as jnp
import numpy as np
from jax import lax
from jax.experimental import pallas as pl
from jax.experimental.pallas import tpu as pltpu

D_MODEL = 2048
BATCH = 4
SEQ = 2048
TOKENS = BATCH * SEQ
CHUNK = 64
LANES = 128

DIFF_HEADS = 8
DIFF_HEAD_DIM = 128
DIFF_QK_DIM = 64
GLA_HEADS = 4
GLA_K_DIM = 128
GLA_V_DIM = 256
GLA_GATE_RANK = 16
GLA_GATE_TAU = 16.0
ROPE_THETA = 10000.0
NORM_EPS = 1e-6
LAMBDA_INIT = 0.8 - 0.6 * math.exp(-0.3 * 0)

N_MAIN = 7168
N_COLBLK = N_MAIN // LANES
CB_DQ, CB_DK, CB_DV, CB_DG = 0, 8, 16, 24
CB_GQ, CB_GK, CB_GV, CB_GG = 32, 36, 40, 48

VMEM_LIMIT = 48 * 1024 * 1024
NEG_BIG = -1e30
LOG2_E = math.log2(math.e)

BF16 = jnp.bfloat16
F32 = jnp.float32


def _nt_dot(a, b):
    return lax.dot_general(a, b, (((1,), (1,)), ((), ())), preferred_element_type=F32)


def _silu(x):
    h = 0.5 * x
    return h + h * jnp.tanh(h)


def _tn_dot(a, b):
    return lax.dot_general(a, b, (((0,), (0,)), ((), ())), preferred_element_type=F32)


ADA_TN = 1024


def _ada_block(c_ref, w_ref, b_ref):
    c = c_ref[...]
    c_act = (c * jax.nn.sigmoid(c)).astype(BF16)
    return jnp.dot(c_act, w_ref[...].astype(BF16), preferred_element_type=F32) + b_ref[...]


def _ada_kernel(c_ref, w_ref, b_ref, o_ref):
    o_ref[...] = _ada_block(c_ref, w_ref, b_ref)


def _ada_call(c_pad, w_ada, b_ada, n):
    return pl.pallas_call(
        _ada_kernel,
        grid=(n // ADA_TN,),
        in_specs=[
            pl.BlockSpec((8, D_MODEL), lambda j: (0, 0)),
            pl.BlockSpec((D_MODEL, ADA_TN), lambda j: (0, j)),
            pl.BlockSpec((1, ADA_TN), lambda j: (0, j)),
        ],
        out_specs=pl.BlockSpec((8, ADA_TN), lambda j: (0, j)),
        out_shape=jax.ShapeDtypeStruct((8, n), F32),
        compiler_params=pltpu.CompilerParams(
            dimension_semantics=("arbitrary",), vmem_limit_bytes=VMEM_LIMIT),
        name="ada_mod",
    )(c_pad, w_ada, b_ada)


PROJ_TM = 1024
PROJ_TN = 1024
PROJ_SUB = 256
PROJ_ROPE_TILES = 2
PROJ_VMEM_LIMIT = (2 * PROJ_TM * D_MODEL * 4 + 2 * D_MODEL * PROJ_TN * 4
                   + PROJ_TM * D_MODEL * 2 + 2 * PROJ_TM * PROJ_TN * 2 + 16 * 1024 * 1024)


def _rot_half(x):
    lane = lax.broadcasted_iota(jnp.int32, x.shape, 1)
    first = (lane % DIFF_QK_DIM) < (DIFF_QK_DIM // 2)
    return jnp.where(first, pltpu.roll(x, LANES - 32, 1), pltpu.roll(x, 32, 1))


def _proj_kernel(x_ref, shift_ref, scale_ref, nw_ref, w_ref, wg_ref, cos_ref, sin_ref,
                 p_ref, g_ref, hn_ref):
    j = pl.program_id(1)
    n_sub = PROJ_TM // PROJ_SUB
    n_cb = PROJ_TN // LANES

    def norm_rows(rows):
        mult = nw_ref[...] * (1.0 + scale_ref[0])
        xs = x_ref[rows, :]
        ms = jnp.mean(xs * xs, axis=-1, keepdims=True)
        hn_ref[rows, :] = (xs * lax.rsqrt(ms + NORM_EPS) * mult + shift_ref[0]).astype(BF16)

    def store_rope(rows, acc, sc):
        cos = cos_ref[rows, :] * sc
        sin = sin_ref[rows, :] * sc
        for cb in range(n_cb):
            t = acc[:, cb * LANES:(cb + 1) * LANES]
            p_ref[cb, rows, :] = (t * cos + _rot_half(t) * sin).astype(BF16)

    @pl.when(j == 0)
    def _():
        for r in range(n_sub):
            rows = pl.ds(r * PROJ_SUB, PROJ_SUB)
            norm_rows(rows)
            acc = _nt_dot(hn_ref[rows, :], w_ref[...])
            store_rope(rows, acc, DIFF_QK_DIM ** -0.5 * LOG2_E)
        g_ref[...] = _nt_dot(hn_ref[...], wg_ref[...]).astype(BF16)

    @pl.when(j == 1)
    def _():
        for r in range(n_sub):
            rows = pl.ds(r * PROJ_SUB, PROJ_SUB)
            acc = _nt_dot(hn_ref[rows, :], w_ref[...])
            store_rope(rows, acc, 1.0)

    @pl.when(j >= PROJ_ROPE_TILES)
    def _():
        for r in range(n_sub):
            rows = pl.ds(r * PROJ_SUB, PROJ_SUB)
            acc = _nt_dot(hn_ref[rows, :], w_ref[...])
            for cb in range(n_cb):
                p_ref[cb, rows, :] = acc[:, cb * LANES:(cb + 1) * LANES].astype(BF16)


def _proj_call(xf, shift, scale, norm_w, w_t, w_gate, cos, sin):
    m_tiles = TOKENS // PROJ_TM
    n_tiles = N_MAIN // PROJ_TN
    per_seq = SEQ // PROJ_TM
    n_cb = PROJ_TN // LANES
    return pl.pallas_call(
        _proj_kernel,
        grid=(m_tiles, n_tiles),
        in_specs=[
            pl.BlockSpec((PROJ_TM, D_MODEL), lambda i, j: (i, 0)),
            pl.BlockSpec((1, 1, D_MODEL), lambda i, j: (i // per_seq, 0, 0)),
            pl.BlockSpec((1, 1, D_MODEL), lambda i, j: (i // per_seq, 0, 0)),
            pl.BlockSpec((1, D_MODEL), lambda i, j: (0, 0)),
            pl.BlockSpec((PROJ_TN, D_MODEL), lambda i, j: (j, 0)),
            pl.BlockSpec((LANES, D_MODEL), lambda i, j: (0, 0)),
            pl.BlockSpec((PROJ_TM, LANES), lambda i, j: (i % per_seq, 0)),
            pl.BlockSpec((PROJ_TM, LANES), lambda i, j: (i % per_seq, 0)),
        ],
        out_specs=[
            pl.BlockSpec((n_cb, PROJ_TM, LANES), lambda i, j: (j, i, 0)),
            pl.BlockSpec((PROJ_TM, LANES), lambda i, j: (i, 0)),
        ],
        out_shape=[
            jax.ShapeDtypeStruct((N_COLBLK, TOKENS, LANES), BF16),
            jax.ShapeDtypeStruct((TOKENS, LANES), BF16),
        ],
        scratch_shapes=[pltpu.VMEM((PROJ_TM, D_MODEL), BF16)],
        compiler_params=pltpu.CompilerParams(
            dimension_semantics=("arbitrary", "arbitrary"), vmem_limit_bytes=PROJ_VMEM_LIMIT),
        name="norm_in_proj",
    )(xf, shift, scale, norm_w, w_t, w_gate, cos, sin)


ATT_T = 256
ATT_NBLK = SEQ // ATT_T
ATT_LEAD = 24
assert ATT_LEAD >= ATT_NBLK - 1
ATT_ONES_ROWS = 16
ATT_HEADS_PER_STEP = 2
ATT_SLOTS = ATT_LEAD + 4


def _attn_kernel(lq1_ref, lk1_ref, lq2_ref, lk2_ref, dnw_ref, q_ref, k_ref, v_ref, dg_ref,
                 c_ref, wa_ref, ba_ref, wo_ref, o_ref, gate_ref, wob_ref, vt_ref, s_ref):
    gate_ref[...] = _ada_block(c_ref, wa_ref, ba_ref)
    wob_ref[...] = wo_ref[...].astype(BF16)

    lam = (jnp.exp(jnp.sum(lq1_ref[...] * lk1_ref[...], keepdims=True))
           - jnp.exp(jnp.sum(lq2_ref[...] * lk2_ref[...], keepdims=True))
           + LAMBDA_INIT)
    out_gain = dnw_ref[...] * (1.0 - LAMBDA_INIT)

    for hd in range(ATT_HEADS_PER_STEP):
        for jb in range(ATT_NBLK):
            cols = pl.ds(jb * ATT_T, ATT_T)
            vb = v_ref[hd, jb * ATT_T:(jb + 1) * ATT_T, :].astype(F32)
            vt_ref[hd, 0:DIFF_HEAD_DIM, cols] = vb.T.astype(BF16)
        vt_ref[hd, DIFF_HEAD_DIM:, :] = jnp.ones((ATT_ONES_ROWS, SEQ), BF16)

    lane = lax.broadcasted_iota(jnp.int32, (ATT_T, LANES), 1)
    comp_a = lane < DIFF_QK_DIM
    krow = lax.broadcasted_iota(jnp.int32, (ATT_T, ATT_T), 0) // CHUNK
    qcol = lax.broadcasted_iota(jnp.int32, (ATT_T, ATT_T), 1) // CHUNK
    diag_mask = krow <= qcol
    diag_mask2 = jnp.concatenate([diag_mask, diag_mask], axis=1)

    units = [(hd, qi, j) for hd in range(ATT_HEADS_PER_STEP)
             for qi in range(ATT_NBLK) for j in range(qi + 1)]
    blocks = {(hd, qi): {"m": None, "p": []}
              for hd in range(ATT_HEADS_PER_STEP) for qi in range(ATT_NBLK)}

    def block_queries(hd, qi):
        q = q_ref[hd, pl.ds(qi * ATT_T, ATT_T), :]
        zero = jnp.zeros_like(q)
        return jnp.concatenate([jnp.where(comp_a, q, zero), jnp.where(comp_a, zero, q)], axis=0)

    dyn0 = lax.shift_right_logical(pl.program_id(0), 20)

    def score_unit(t, hd, qi, j):
        st = blocks[hd, qi]
        if j == 0:
            st["q2"] = block_queries(hd, qi)
        s = _nt_dot(k_ref[hd, pl.ds(j * ATT_T, ATT_T), :], st["q2"])
        if j == qi:
            s = jnp.where(diag_mask2, s, NEG_BIG)
        s_ref[dyn0 + t % ATT_SLOTS] = s
        mj = jnp.max(s, axis=0, keepdims=True)
        st["m"] = mj if st["m"] is None else jnp.maximum(st["m"], mj)

    def value_unit(t, hd, qi, j):
        st = blocks[hd, qi]
        p = jnp.exp2(s_ref[dyn0 + t % ATT_SLOTS] - st["m"])
        st["p"].append(p.astype(BF16))
        if j == qi:
            p_all = jnp.concatenate(st["p"], axis=0) if qi else st["p"][0]
            pv_l = jnp.dot(vt_ref[hd, :, 0:(qi + 1) * ATT_T], p_all,
                           preferred_element_type=F32)
            finish(hd, qi, pv_l)

    def finish(hd, qi, pv_l):
        rows = pl.ds(qi * ATT_T, ATT_T)
        l = pv_l[DIFF_HEAD_DIM:DIFF_HEAD_DIM + 1, :]
        pv = pv_l[0:DIFF_HEAD_DIM, :] * (1.0 / l)
        out_t = pv[:, :ATT_T] - lam * pv[:, ATT_T:]
        out = out_t.T
        ms = jnp.mean(out * out, axis=-1, keepdims=True)
        y = out * lax.rsqrt(ms + NORM_EPS) * out_gain
        g = dg_ref[hd, rows, :].astype(F32)
        o_ref[hd, rows, :] = (y * _silu(g)).astype(BF16)

    for t in range(len(units) + ATT_LEAD):
        if t < len(units):
            score_unit(t, *units[t])
        if t >= ATT_LEAD:
            value_unit(t - ATT_LEAD, *units[t - ATT_LEAD])


def _attn_call(p, lq1, lk1, lq2, lk2, dnw, c_pad, w_ada, b_ada, w_out):
    hp = ATT_HEADS_PER_STEP
    steps_per_batch = DIFF_HEADS // hp
    assert BATCH * steps_per_batch * LANES == D_MODEL
    wo_rows = pl.BlockSpec((LANES, D_MODEL), lambda b, h: (b * steps_per_batch + h, 0))
    gate_cb0 = 2 * D_MODEL // LANES
    vec = lambda n: pl.BlockSpec((1, n), lambda b, h: (0, 0))
    slab = lambda cb0: pl.BlockSpec((hp, SEQ, LANES), lambda b, h: (cb0 // hp + h, b, 0))
    gate_blk = lambda rows, cb0: pl.BlockSpec(
        (rows, LANES), lambda b, h: (0, cb0 + b * steps_per_batch + h))
    return pl.pallas_call(
        _attn_kernel,
        grid=(BATCH, steps_per_batch),
        in_specs=[vec(DIFF_QK_DIM)] * 4 + [vec(DIFF_HEAD_DIM),
                  slab(CB_DQ), slab(CB_DK), slab(CB_DV), slab(CB_DG),
                  pl.BlockSpec((8, D_MODEL), lambda b, h: (0, 0)),
                  gate_blk(D_MODEL, gate_cb0), gate_blk(1, gate_cb0), wo_rows],
        out_specs=[pl.BlockSpec((hp, SEQ, LANES), lambda b, h: (h, b, 0)), gate_blk(8, 0),
                   wo_rows],
        out_shape=[jax.ShapeDtypeStruct((DIFF_HEADS, TOKENS, LANES), BF16),
                   jax.ShapeDtypeStruct((8, D_MODEL), F32),
                   jax.ShapeDtypeStruct((D_MODEL, D_MODEL), BF16)],
        scratch_shapes=[
            pltpu.VMEM((hp, DIFF_HEAD_DIM + ATT_ONES_ROWS, SEQ), BF16),
            pltpu.VMEM((ATT_SLOTS, ATT_T, 2 * ATT_T), F32),
        ],
        compiler_params=pltpu.CompilerParams(
            dimension_semantics=("arbitrary", "arbitrary"), vmem_limit_bytes=VMEM_LIMIT),
        name="diff_attn",
    )(lq1, lk1, lq2, lk2, dnw, p, p, p, p, c_pad, w_ada, b_ada, w_out)


GLA_BLK = 256
GLA_NCHUNK = SEQ // CHUNK
GLA_HEADS_PER_STEP = 2


def _split_hi_lo(x):
    hi = x.astype(BF16)
    lo = (x - hi.astype(F32)).astype(BF16)
    return hi, lo


def _gla_kernel(g_ref, w2_ref, b2_ref, gw_ref, q_ref, k_ref, v_ref, gg_ref, o_ref,
                kdec_ref, tot_ref, kv_ref, st_ref):
    for hd in range(GLA_HEADS_PER_STEP):
        _gla_head(g_ref, w2_ref.at[hd], b2_ref.at[hd], gw_ref, q_ref.at[hd], k_ref.at[hd],
                  v_ref.at[hd], gg_ref.at[hd], o_ref.at[hd], kdec_ref, tot_ref, kv_ref, st_ref)


def _gla_head(g_ref, w2_ref, b2_ref, gw_ref, q_ref, k_ref, v_ref, gg_ref, o_ref,
              kdec_ref, tot_ref, kv_ref, st_ref):
    z = jnp.dot(g_ref[...], w2_ref[...], preferred_element_type=F32) + b2_ref[...]
    log_a = (jnp.minimum(z, 0.0) - jnp.log(1.0 + jnp.exp(-jnp.abs(z)))) * (1.0 / GLA_GATE_TAU)

    r = lax.broadcasted_iota(jnp.int32, (GLA_BLK, GLA_BLK), 0)
    c = lax.broadcasted_iota(jnp.int32, (GLA_BLK, GLA_BLK), 1)
    after = jnp.where(((r // CHUNK) == (c // CHUNK)) & (c > r), 1.0, 0.0).astype(BF16)
    for blk in range(SEQ // GLA_BLK):
        rows = pl.ds(blk * GLA_BLK, GLA_BLK)
        la = log_a[blk * GLA_BLK:(blk + 1) * GLA_BLK, :]
        hi, lo = _split_hi_lo(la)
        both = jnp.dot(after, jnp.concatenate([hi, lo], axis=1), preferred_element_type=F32)
        suffix = both[:, :GLA_K_DIM] + both[:, GLA_K_DIM:]
        kdec_ref[rows, :] = (k_ref[rows, :].astype(F32) * jnp.exp(suffix)).astype(BF16)
        tot_ref[rows, :] = suffix + la

    dec_rows = jnp.exp(tot_ref[pl.ds(0, GLA_NCHUNK, stride=CHUNK), :])
    dec_cols = jnp.concatenate(
        [dec_rows, jnp.zeros((GLA_K_DIM - GLA_NCHUNK, GLA_K_DIM), F32)], axis=0).T

    dyn0 = lax.shift_right_logical(pl.program_id(0), 20)
    for ci in range(GLA_NCHUNK):
        rows = pl.ds(ci * CHUNK, CHUNK)
        v_c = jnp.concatenate([v_ref[0, rows, :], v_ref[1, rows, :]], axis=1)
        kv_ref[dyn0 + ci] = _tn_dot(kdec_ref[rows, :], v_c)

    state = jnp.zeros((GLA_K_DIM, GLA_V_DIM), F32)
    for ci in range(GLA_NCHUNK):
        state = state * dec_cols[:, ci:ci + 1] + kv_ref[dyn0 + ci]
        st_ref[ci] = state.astype(BF16)

    gw = gw_ref[...]
    qs = GLA_K_DIM ** -0.5
    for ci in range(GLA_NCHUNK):
        rows = pl.ds(ci * CHUNK, CHUNK)
        o = jnp.dot(q_ref[rows, :], st_ref[ci], preferred_element_type=F32)
        ms = jnp.mean(o * o, axis=-1, keepdims=True) * (qs * qs)
        y = o * (qs * lax.rsqrt(ms + NORM_EPS)) * gw
        gate = jnp.concatenate([gg_ref[0, rows, :], gg_ref[1, rows, :]], axis=1).astype(F32)
        y = y * _silu(gate)
        o_ref[0, rows, :] = y[:, :LANES].astype(BF16)
        o_ref[1, rows, :] = y[:, LANES:].astype(BF16)


def _gla_call(p, g, w2_heads, b2_heads, gw):
    hp = GLA_HEADS_PER_STEP
    qk = lambda cb0: pl.BlockSpec((hp, SEQ, LANES), lambda b, h: (cb0 // hp + h, b, 0))
    p4 = p.reshape(N_COLBLK // 2, 2, TOKENS, LANES)
    vg = lambda cb0: pl.BlockSpec((hp, 2, SEQ, LANES), lambda b, h: (cb0 // 2 // hp + h, 0, b, 0))
    out = pl.pallas_call(
        _gla_kernel,
        grid=(BATCH, GLA_HEADS // hp),
        in_specs=[
            pl.BlockSpec((SEQ, LANES), lambda b, h: (b, 0)),
            pl.BlockSpec((hp, LANES, GLA_K_DIM), lambda b, h: (h, 0, 0)),
            pl.BlockSpec((hp, 1, GLA_K_DIM), lambda b, h: (h, 0, 0)),
            pl.BlockSpec((1, GLA_V_DIM), lambda b, h: (0, 0)),
            qk(CB_GQ), qk(CB_GK), vg(CB_GV), vg(CB_GG),
        ],
        out_specs=pl.BlockSpec((hp, 2, SEQ, LANES), lambda b, h: (h, 0, b, 0)),
        out_shape=jax.ShapeDtypeStruct((GLA_HEADS, 2, TOKENS, LANES), BF16),
        scratch_shapes=[
            pltpu.VMEM((SEQ, GLA_K_DIM), BF16),
            pltpu.VMEM((SEQ, GLA_K_DIM), F32),
            pltpu.VMEM((GLA_NCHUNK, GLA_K_DIM, GLA_V_DIM), F32),
            pltpu.VMEM((GLA_NCHUNK, GLA_K_DIM, GLA_V_DIM), BF16),
        ],
        compiler_params=pltpu.CompilerParams(
            dimension_semantics=("arbitrary", "arbitrary"), vmem_limit_bytes=VMEM_LIMIT),
        name="gla_scan",
    )(g, w2_heads, b2_heads, gw, p, p, p4, p4)
    return out.reshape(2 * GLA_HEADS, TOKENS, LANES)


OUT_TM = 1024
OUT_SPLITS = (256, 256, 256, 256)
assert sum(OUT_SPLITS) == OUT_TM
OUT_VMEM_LIMIT = (4 * OUT_TM * D_MODEL * 4 + 2 * OUT_TM * D_MODEL * 2 + D_MODEL * D_MODEL * 2
                  + 8 * 1024 * 1024)


def _out_kernel(a_ref, b_ref, wb_ref, x_ref, gate_ref, fw_ref, o_ref):
    gate = gate_ref[0]
    fw = fw_ref[...]
    row0 = 0
    for n_rows in OUT_SPLITS:
        rows = pl.ds(row0, n_rows)
        row0 += n_rows
        mix = jnp.concatenate(
            [a_ref[h, rows, :] for h in range(a_ref.shape[0])]
            + [b_ref[h, rows, :] for h in range(b_ref.shape[0])], axis=1)
        y = jnp.dot(mix, wb_ref[...], preferred_element_type=F32)
        h_res = x_ref[rows, :] + gate * y
        ms = jnp.mean(h_res * h_res, axis=-1, keepdims=True)
        o_ref[rows, :] = h_res * lax.rsqrt(ms + NORM_EPS) * fw


def _out_call(a, b, w_out, xf, gate, fw):
    per_seq = SEQ // OUT_TM
    return pl.pallas_call(
        _out_kernel,
        grid=(TOKENS // OUT_TM,),
        in_specs=[
            pl.BlockSpec((DIFF_HEADS, OUT_TM, LANES), lambda i: (0, i, 0)),
            pl.BlockSpec((2 * GLA_HEADS, OUT_TM, LANES), lambda i: (0, i, 0)),
            pl.BlockSpec((D_MODEL, D_MODEL), lambda i: (0, 0), pipeline_mode=pl.Buffered(1)),
            pl.BlockSpec((OUT_TM, D_MODEL), lambda i: (i, 0)),
            pl.BlockSpec((1, 1, D_MODEL), lambda i: (i // per_seq, 0, 0)),
            pl.BlockSpec((1, D_MODEL), lambda i: (0, 0)),
        ],
        out_specs=pl.BlockSpec((OUT_TM, D_MODEL), lambda i: (i, 0)),
        out_shape=jax.ShapeDtypeStruct((TOKENS, D_MODEL), F32),
        compiler_params=pltpu.CompilerParams(
            dimension_semantics=("arbitrary",), vmem_limit_bytes=OUT_VMEM_LIMIT),
        name="out_proj_norm",
    )(a, b, w_out, xf, gate, fw)


def _rope_tables():
    inv_freq = ROPE_THETA ** (-np.arange(0, DIFF_QK_DIM, 2, dtype=np.float64) / DIFF_QK_DIM)
    ang = np.arange(SEQ, dtype=np.float64)[:, None] * inv_freq[None, :]
    cos, sin = np.cos(ang), np.sin(ang)
    cos_t = np.tile(cos, (1, 4)).astype(np.float32)
    sin_t = np.concatenate([-sin, sin, -sin, sin], axis=1).astype(np.float32)
    return jnp.asarray(cos_t), jnp.asarray(sin_t)


def kernel(x, c, norm_w, w_ada, b_ada, w_in, lambda_q1, lambda_k1, lambda_q2, lambda_k2,
           diff_norm_w, gla_gate_w2, gla_gate_b, gla_norm_w, w_out, final_norm_w):
    assert x.shape == (BATCH, SEQ, D_MODEL) and w_in.shape[0] == 1
    xf = x.reshape(TOKENS, D_MODEL).astype(F32)

    c_pad = jnp.pad(c.astype(F32), ((0, 8 - BATCH), (0, 0)))
    b_ada_row = b_ada[0][None, :]
    mod = _ada_call(c_pad, w_ada[0], b_ada_row, 2 * D_MODEL)[:BATCH]
    shift = mod[:, :D_MODEL].reshape(BATCH, 1, D_MODEL)
    scale = mod[:, D_MODEL:].reshape(BATCH, 1, D_MODEL)

    w_t = w_in[0].T
    w_gate = jnp.pad(w_t[N_MAIN:], ((0, LANES - GLA_GATE_RANK), (0, 0))).astype(BF16)
    cos, sin = _rope_tables()
    p, g = _proj_call(xf, shift, scale, norm_w[0][None, :], w_t, w_gate, cos, sin)

    row = lambda v: v[0][None, :].astype(F32)
    a, gate8, w_out_bf16 = _attn_call(
        p, row(lambda_q1), row(lambda_k1), row(lambda_q2), row(lambda_k2), row(diff_norm_w),
        c_pad, w_ada[0], b_ada_row, w_out[0])
    gate = gate8[:BATCH].reshape(BATCH, 1, D_MODEL)

    w2_heads = jnp.pad(gla_gate_w2[0], ((0, LANES - GLA_GATE_RANK), (0, 0))).astype(BF16)
    w2_heads = w2_heads.reshape(LANES, GLA_HEADS, GLA_K_DIM).transpose(1, 0, 2)
    b2_heads = gla_gate_b[0].astype(F32).reshape(GLA_HEADS, 1, GLA_K_DIM)
    b = _gla_call(p, g, w2_heads, b2_heads, row(gla_norm_w))

    out = _out_call(a, b, w_out_bf16, xf, gate, final_norm_w[None, :].astype(F32))
    return out.reshape(BATCH, SEQ, D_MODEL).astype(x.dtype)
```

```python
import functools
import math

import jax
import jax.numpy as jnp
import numpy as np
from jax import lax
from jax.experimental import pallas as pl
from jax.experimental.pallas import tpu as pltpu

D_MODEL = 2048
BATCH = 4
SEQ = 2048
TOKENS = BATCH * SEQ
CHUNK = 64
LANES = 128

DIFF_HEADS = 8
DIFF_HEAD_DIM = 128
DIFF_QK_DIM = 64
GLA_HEADS = 4
GLA_K_DIM = 128
GLA_V_DIM = 256
GLA_GATE_RANK = 16
GLA_GATE_TAU = 16.0
ROPE_THETA = 10000.0
NORM_EPS = 1e-6
LAMBDA_INIT = 0.8 - 0.6 * math.exp(-0.3 * 0)

N_MAIN = 7168
N_COLBLK = N_MAIN // LANES
CB_DQ, CB_DK, CB_DV, CB_DG = 0, 8, 16, 24
CB_GQ, CB_GK, CB_GV, CB_GG = 32, 36, 40, 48

VMEM_LIMIT = 48 * 1024 * 1024
NEG_BIG = -1e30
LOG2_E = math.log2(math.e)

BF16 = jnp.bfloat16
F32 = jnp.float32


def _nt_dot(a, b):
    return lax.dot_general(a, b, (((1,), (1,)), ((), ())), preferred_element_type=F32)


def _silu(x):
    h = 0.5 * x
    return h + h * jnp.tanh(h)


def _tn_dot(a, b):
    return lax.dot_general(a, b, (((0,), (0,)), ((), ())), preferred_element_type=F32)


ADA_TN = 1024


def _ada_block(c_ref, w_ref, b_ref):
    c = c_ref[...]
    c_act = (c * jax.nn.sigmoid(c)).astype(BF16)
    return jnp.dot(c_act, w_ref[...].astype(BF16), preferred_element_type=F32) + b_ref[...]


def _ada_kernel(c_ref, w_ref, b_ref, o_ref):
    o_ref[...] = _ada_block(c_ref, w_ref, b_ref)


def _ada_call(c_pad, w_ada, b_ada, n):
    return pl.pallas_call(
        _ada_kernel,
        grid=(n // ADA_TN,),
        in_specs=[
            pl.BlockSpec((8, D_MODEL), lambda j: (0, 0)),
            pl.BlockSpec((D_MODEL, ADA_TN), lambda j: (0, j)),
            pl.BlockSpec((1, ADA_TN), lambda j: (0, j)),
        ],
        out_specs=pl.BlockSpec((8, ADA_TN), lambda j: (0, j)),
        out_shape=jax.ShapeDtypeStruct((8, n), F32),
        compiler_params=pltpu.CompilerParams(
            dimension_semantics=("arbitrary",), vmem_limit_bytes=VMEM_LIMIT),
        name="ada_mod",
    )(c_pad, w_ada, b_ada)


PROJ_TM = 1024
PROJ_TN = 1024
PROJ_SUB = 256
PROJ_ROPE_TILES = 2
PROJ_VMEM_LIMIT = (2 * PROJ_TM * D_MODEL * 4 + 2 * D_MODEL * PROJ_TN * 4
                   + PROJ_TM * D_MODEL * 2 + 2 * PROJ_TM * PROJ_TN * 2 + 16 * 1024 * 1024)


def _rot_half(x):
    lane = lax.broadcasted_iota(jnp.int32, x.shape, 1)
    first = (lane % DIFF_QK_DIM) < (DIFF_QK_DIM // 2)
    return jnp.where(first, pltpu.roll(x, LANES - 32, 1), pltpu.roll(x, 32, 1))


def _proj_kernel(x_ref, shift_ref, scale_ref, nw_ref, w_ref, wg_ref, cos_ref, sin_ref,
                 p_ref, g_ref, hn_ref):
    j = pl.program_id(1)
    n_sub = PROJ_TM // PROJ_SUB
    n_cb = PROJ_TN // LANES

    def norm_rows(rows):
        mult = nw_ref[...] * (1.0 + scale_ref[0])
        xs = x_ref[rows, :]
        ms = jnp.mean(xs * xs, axis=-1, keepdims=True)
        hn_ref[rows, :] = (xs * lax.rsqrt(ms + NORM_EPS) * mult + shift_ref[0]).astype(BF16)

    def store_rope(rows, acc, sc):
        cos = cos_ref[rows, :] * sc
        sin = sin_ref[rows, :] * sc
        for cb in range(n_cb):
            t = acc[:, cb * LANES:(cb + 1) * LANES]
            p_ref[cb, rows, :] = (t * cos + _rot_half(t) * sin).astype(BF16)

    @pl.when(j == 0)
    def _():
        for r in range(n_sub):
            rows = pl.ds(r * PROJ_SUB, PROJ_SUB)
            norm_rows(rows)
            acc = _nt_dot(hn_ref[rows, :], w_ref[...])
            store_rope(rows, acc, DIFF_QK_DIM ** -0.5 * LOG2_E)
        g_ref[...] = _nt_dot(hn_ref[...], wg_ref[...]).astype(BF16)

    @pl.when(j == 1)
    def _():
        for r in range(n_sub):
            rows = pl.ds(r * PROJ_SUB, PROJ_SUB)
            acc = _nt_dot(hn_ref[rows, :], w_ref[...])
            store_rope(rows, acc, 1.0)

    @pl.when(j >= PROJ_ROPE_TILES)
    def _():
        for r in range(n_sub):
            rows = pl.ds(r * PROJ_SUB, PROJ_SUB)
            acc = _nt_dot(hn_ref[rows, :], w_ref[...])
            for cb in range(n_cb):
                p_ref[cb, rows, :] = acc[:, cb * LANES:(cb + 1) * LANES].astype(BF16)


def _proj_call(xf, shift, scale, norm_w, w_t, w_gate, cos, sin):
    m_tiles = TOKENS // PROJ_TM
    n_tiles = N_MAIN // PROJ_TN
    per_seq = SEQ // PROJ_TM
    n_cb = PROJ_TN // LANES
    return pl.pallas_call(
        _proj_kernel,
        grid=(m_tiles, n_tiles),
        in_specs=[
            pl.BlockSpec((PROJ_TM, D_MODEL), lambda i, j: (i, 0)),
            pl.BlockSpec((1, 1, D_MODEL), lambda i, j: (i // per_seq, 0, 0)),
            pl.BlockSpec((1, 1, D_MODEL), lambda i, j: (i // per_seq, 0, 0)),
            pl.BlockSpec((1, D_MODEL), lambda i, j: (0, 0)),
            pl.BlockSpec((PROJ_TN, D_MODEL), lambda i, j: (j, 0)),
            pl.BlockSpec((LANES, D_MODEL), lambda i, j: (0, 0)),
            pl.BlockSpec((PROJ_TM, LANES), lambda i, j: (i % per_seq, 0)),
            pl.BlockSpec((PROJ_TM, LANES), lambda i, j: (i % per_seq, 0)),
        ],
        out_specs=[
            pl.BlockSpec((n_cb, PROJ_TM, LANES), lambda i, j: (j, i, 0)),
            pl.BlockSpec((PROJ_TM, LANES), lambda i, j: (i, 0)),
        ],
        out_shape=[
            jax.ShapeDtypeStruct((N_COLBLK, TOKENS, LANES), BF16),
            jax.ShapeDtypeStruct((TOKENS, LANES), BF16),
        ],
        scratch_shapes=[pltpu.VMEM((PROJ_TM, D_MODEL), BF16)],
        compiler_params=pltpu.CompilerParams(
            dimension_semantics=("arbitrary", "arbitrary"), vmem_limit_bytes=PROJ_VMEM_LIMIT),
        name="norm_in_proj",
    )(xf, shift, scale, norm_w, w_t, w_gate, cos, sin)


ATT_T = 256
ATT_NBLK = SEQ // ATT_T
ATT_LEAD = 9
assert ATT_LEAD >= ATT_NBLK - 1
ATT_ONES_ROWS = 16
ATT_HEADS_PER_STEP = 2
ATT_SLOTS = ATT_LEAD + 4


def _attn_kernel(lq1_ref, lk1_ref, lq2_ref, lk2_ref, dnw_ref, q_ref, k_ref, v_ref, dg_ref,
                 c_ref, wa_ref, ba_ref, wo_ref, o_ref, gate_ref, wob_ref, vt_ref, s_ref):
    gate_ref[...] = _ada_block(c_ref, wa_ref, ba_ref)
    wob_ref[...] = wo_ref[...].astype(BF16)

    lam = (jnp.exp(jnp.sum(lq1_ref[...] * lk1_ref[...], keepdims=True))
           - jnp.exp(jnp.sum(lq2_ref[...] * lk2_ref[...], keepdims=True))
           + LAMBDA_INIT)
    out_gain = dnw_ref[...] * (1.0 - LAMBDA_INIT)

    for hd in range(ATT_HEADS_PER_STEP):
        for jb in range(ATT_NBLK):
            cols = pl.ds(jb * ATT_T, ATT_T)
            vb = v_ref[hd, jb * ATT_T:(jb + 1) * ATT_T, :].astype(F32)
            vt_ref[hd, 0:DIFF_HEAD_DIM, cols] = vb.T.astype(BF16)
        vt_ref[hd, DIFF_HEAD_DIM:, :] = jnp.ones((ATT_ONES_ROWS, SEQ), BF16)

    lane = lax.broadcasted_iota(jnp.int32, (ATT_T, LANES), 1)
    comp_a = lane < DIFF_QK_DIM
    krow = lax.broadcasted_iota(jnp.int32, (ATT_T, ATT_T), 0) // CHUNK
    qcol = lax.broadcasted_iota(jnp.int32, (ATT_T, ATT_T), 1) // CHUNK
    diag_mask = krow <= qcol
    diag_mask2 = jnp.concatenate([diag_mask, diag_mask], axis=1)

    units = [(hd, qi, j) for hd in range(ATT_HEADS_PER_STEP)
             for qi in range(ATT_NBLK) for j in range(qi + 1)]
    blocks = {(hd, qi): {"m": None, "p": []}
              for hd in range(ATT_HEADS_PER_STEP) for qi in range(ATT_NBLK)}

    def block_queries(hd, qi):
        q = q_ref[hd, pl.ds(qi * ATT_T, ATT_T), :]
        zero = jnp.zeros_like(q)
        return jnp.concatenate([jnp.where(comp_a, q, zero), jnp.where(comp_a, zero, q)], axis=0)

    dyn0 = lax.shift_right_logical(pl.program_id(0), 20)

    def score_unit(t, hd, qi, j):
        st = blocks[hd, qi]
        if j == 0:
            st["q2"] = block_queries(hd, qi)
        s = _nt_dot(k_ref[hd, pl.ds(j * ATT_T, ATT_T), :], st["q2"])
        if j == qi:
            s = jnp.where(diag_mask2, s, NEG_BIG)
        s_ref[dyn0 + t % ATT_SLOTS] = s
        mj = jnp.max(s, axis=0, keepdims=True)
        st["m"] = mj if st["m"] is None else jnp.maximum(st["m"], mj)

    def value_unit(t, hd, qi, j):
        st = blocks[hd, qi]
        p = jnp.exp2(s_ref[dyn0 + t % ATT_SLOTS] - st["m"])
        pvj = jnp.dot(vt_ref[hd, :, j * ATT_T:(j + 1) * ATT_T], p.astype(BF16),
                      preferred_element_type=F32)
        st["pv"] = pvj if j == 0 else st["pv"] + pvj
        if j == qi:
            finish(hd, qi, st["pv"])

    def finish(hd, qi, pv_l):
        rows = pl.ds(qi * ATT_T, ATT_T)
        l = pv_l[DIFF_HEAD_DIM:DIFF_HEAD_DIM + 1, :]
        pv = pv_l[0:DIFF_HEAD_DIM, :] * (1.0 / l)
        out_t = pv[:, :ATT_T] - lam * pv[:, ATT_T:]
        out = out_t.T
        ms = jnp.mean(out * out, axis=-1, keepdims=True)
        y = out * lax.rsqrt(ms + NORM_EPS) * out_gain
        g = dg_ref[hd, rows, :].astype(F32)
        o_ref[hd, rows, :] = (y * _silu(g)).astype(BF16)

    for t in range(len(units) + ATT_LEAD):
        if t < len(units):
            score_unit(t, *units[t])
        if t >= ATT_LEAD:
            value_unit(t - ATT_LEAD, *units[t - ATT_LEAD])


def _attn_call(p, lq1, lk1, lq2, lk2, dnw, c_pad, w_ada, b_ada, w_out):
    hp = ATT_HEADS_PER_STEP
    steps_per_batch = DIFF_HEADS // hp
    side_w = D_MODEL // (BATCH * steps_per_batch)
    assert side_w % LANES == 0
    wo_rows = pl.BlockSpec((side_w, D_MODEL), lambda b, h: (b * steps_per_batch + h, 0))
    gate_cb0 = 2 * D_MODEL // side_w
    vec = lambda n: pl.BlockSpec((1, n), lambda b, h: (0, 0))
    slab = lambda cb0: pl.BlockSpec((hp, SEQ, LANES), lambda b, h: (cb0 // hp + h, b, 0))
    gate_blk = lambda rows, cb0: pl.BlockSpec(
        (rows, side_w), lambda b, h: (0, cb0 + b * steps_per_batch + h))
    return pl.pallas_call(
        _attn_kernel,
        grid=(BATCH, steps_per_batch),
        in_specs=[vec(DIFF_QK_DIM)] * 4 + [vec(DIFF_HEAD_DIM),
                  slab(CB_DQ), slab(CB_DK), slab(CB_DV), slab(CB_DG),
                  pl.BlockSpec((8, D_MODEL), lambda b, h: (0, 0)),
                  gate_blk(D_MODEL, gate_cb0), gate_blk(1, gate_cb0), wo_rows],
        out_specs=[pl.BlockSpec((hp, SEQ, LANES), lambda b, h: (h, b, 0)), gate_blk(8, 0),
                   wo_rows],
        out_shape=[jax.ShapeDtypeStruct((DIFF_HEADS, TOKENS, LANES), BF16),
                   jax.ShapeDtypeStruct((8, D_MODEL), F32),
                   jax.ShapeDtypeStruct((D_MODEL, D_MODEL), BF16)],
        scratch_shapes=[
            pltpu.VMEM((hp, DIFF_HEAD_DIM + ATT_ONES_ROWS, SEQ), BF16),
            pltpu.VMEM((ATT_SLOTS, ATT_T, 2 * ATT_T), F32),
        ],
        compiler_params=pltpu.CompilerParams(
            dimension_semantics=("arbitrary", "arbitrary"), vmem_limit_bytes=VMEM_LIMIT),
        name="diff_attn",
    )(lq1, lk1, lq2, lk2, dnw, p, p, p, p, c_pad, w_ada, b_ada, w_out)


GLA_BLK = 256
GLA_NCHUNK = SEQ // CHUNK
GLA_HEADS_PER_STEP = 2


def _split_hi_lo(x):
    hi = x.astype(BF16)
    lo = (x - hi.astype(F32)).astype(BF16)
    return hi, lo


def _gla_kernel(g_ref, w2_ref, b2_ref, gw_ref, q_ref, k_ref, v_ref, gg_ref, o_ref,
                kdec_ref, tot_ref, kv_ref, st_ref):
    for hd in range(GLA_HEADS_PER_STEP):
        _gla_head(g_ref, w2_ref.at[hd], b2_ref.at[hd], gw_ref, q_ref.at[hd], k_ref.at[hd],
                  v_ref.at[hd], gg_ref.at[hd], o_ref.at[hd], kdec_ref, tot_ref, kv_ref, st_ref)


def _gla_head(g_ref, w2_ref, b2_ref, gw_ref, q_ref, k_ref, v_ref, gg_ref, o_ref,
              kdec_ref, tot_ref, kv_ref, st_ref):
    z = jnp.dot(g_ref[...], w2_ref[...], preferred_element_type=F32) + b2_ref[...]
    log_a = (jnp.minimum(z, 0.0) - jnp.log(1.0 + jnp.exp(-jnp.abs(z)))) * (1.0 / GLA_GATE_TAU)

    r = lax.broadcasted_iota(jnp.int32, (GLA_BLK, GLA_BLK), 0)
    c = lax.broadcasted_iota(jnp.int32, (GLA_BLK, GLA_BLK), 1)
    after = jnp.where(((r // CHUNK) == (c // CHUNK)) & (c > r), 1.0, 0.0).astype(BF16)
    for blk in range(SEQ // GLA_BLK):
        rows = pl.ds(blk * GLA_BLK, GLA_BLK)
        la = log_a[blk * GLA_BLK:(blk + 1) * GLA_BLK, :]
        hi, lo = _split_hi_lo(la)
        both = jnp.dot(after, jnp.concatenate([hi, lo], axis=1), preferred_element_type=F32)
        suffix = both[:, :GLA_K_DIM] + both[:, GLA_K_DIM:]
        kdec_ref[rows, :] = (k_ref[rows, :].astype(F32) * jnp.exp(suffix)).astype(BF16)
        tot_ref[rows, :] = suffix + la

    dec_rows = jnp.exp(tot_ref[pl.ds(0, GLA_NCHUNK, stride=CHUNK), :])
    dec_cols = jnp.concatenate(
        [dec_rows, jnp.zeros((GLA_K_DIM - GLA_NCHUNK, GLA_K_DIM), F32)], axis=0).T

    dyn0 = lax.shift_right_logical(pl.program_id(0), 20)
    for ci in range(GLA_NCHUNK):
        rows = pl.ds(ci * CHUNK, CHUNK)
        v_c = jnp.concatenate([v_ref[0, rows, :], v_ref[1, rows, :]], axis=1)
        kv_ref[dyn0 + ci] = _tn_dot(kdec_ref[rows, :], v_c)

    state = jnp.zeros((GLA_K_DIM, GLA_V_DIM), F32)
    for ci in range(GLA_NCHUNK):
        state = state * dec_cols[:, ci:ci + 1] + kv_ref[dyn0 + ci]
        st_ref[ci] = state.astype(BF16)

    gw = gw_ref[...]
    qs = GLA_K_DIM ** -0.5
    for ci in range(GLA_NCHUNK):
        rows = pl.ds(ci * CHUNK, CHUNK)
        o = jnp.dot(q_ref[rows, :], st_ref[ci], preferred_element_type=F32)
        ms = jnp.mean(o * o, axis=-1, keepdims=True) * (qs * qs)
        y = o * (qs * lax.rsqrt(ms + NORM_EPS)) * gw
        gate = jnp.concatenate([gg_ref[0, rows, :], gg_ref[1, rows, :]], axis=1).astype(F32)
        y = y * _silu(gate)
        o_ref[0, rows, :] = y[:, :LANES].astype(BF16)
        o_ref[1, rows, :] = y[:, LANES:].astype(BF16)


def _gla_call(p, g, w2_heads, b2_heads, gw):
    hp = GLA_HEADS_PER_STEP
    qk = lambda cb0: pl.BlockSpec((hp, SEQ, LANES), lambda b, h: (cb0 // hp + h, b, 0))
    p4 = p.reshape(N_COLBLK // 2, 2, TOKENS, LANES)
    vg = lambda cb0: pl.BlockSpec((hp, 2, SEQ, LANES), lambda b, h: (cb0 // 2 // hp + h, 0, b, 0))
    out = pl.pallas_call(
        _gla_kernel,
        grid=(BATCH, GLA_HEADS // hp),
        in_specs=[
            pl.BlockSpec((SEQ, LANES), lambda b, h: (b, 0)),
            pl.BlockSpec((hp, LANES, GLA_K_DIM), lambda b, h: (h, 0, 0)),
            pl.BlockSpec((hp, 1, GLA_K_DIM), lambda b, h: (h, 0, 0)),
            pl.BlockSpec((1, GLA_V_DIM), lambda b, h: (0, 0)),
            qk(CB_GQ), qk(CB_GK), vg(CB_GV), vg(CB_GG),
        ],
        out_specs=pl.BlockSpec((hp, 2, SEQ, LANES), lambda b, h: (h, 0, b, 0)),
        out_shape=jax.ShapeDtypeStruct((GLA_HEADS, 2, TOKENS, LANES), BF16),
        scratch_shapes=[
            pltpu.VMEM((SEQ, GLA_K_DIM), BF16),
            pltpu.VMEM((SEQ, GLA_K_DIM), F32),
            pltpu.VMEM((GLA_NCHUNK, GLA_K_DIM, GLA_V_DIM), F32),
            pltpu.VMEM((GLA_NCHUNK, GLA_K_DIM, GLA_V_DIM), BF16),
        ],
        compiler_params=pltpu.CompilerParams(
            dimension_semantics=("arbitrary", "arbitrary"), vmem_limit_bytes=VMEM_LIMIT),
        name="gla_scan",
    )(g, w2_heads, b2_heads, gw, p, p, p4, p4)
    return out.reshape(2 * GLA_HEADS, TOKENS, LANES)


OUT_TM = 1024
OUT_SPLITS = (256, 256, 256, 256)
assert sum(OUT_SPLITS) == OUT_TM
OUT_VMEM_LIMIT = (4 * OUT_TM * D_MODEL * 4 + 2 * OUT_TM * D_MODEL * 2 + D_MODEL * D_MODEL * 2
                  + 8 * 1024 * 1024)


def _out_kernel(a_ref, b_ref, wb_ref, x_ref, gate_ref, fw_ref, o_ref):
    gate = gate_ref[0]
    fw = fw_ref[...]
    row0 = 0
    for n_rows in OUT_SPLITS:
        rows = pl.ds(row0, n_rows)
        row0 += n_rows
        mix = jnp.concatenate(
            [a_ref[h, rows, :] for h in range(a_ref.shape[0])]
            + [b_ref[h, rows, :] for h in range(b_ref.shape[0])], axis=1)
        y = jnp.dot(mix, wb_ref[...], preferred_element_type=F32)
        h_res = x_ref[rows, :] + gate * y
        ms = jnp.mean(h_res * h_res, axis=-1, keepdims=True)
        o_ref[rows, :] = h_res * lax.rsqrt(ms + NORM_EPS) * fw


def _out_call(a, b, w_out, xf, gate, fw):
    per_seq = SEQ // OUT_TM
    return pl.pallas_call(
        _out_kernel,
        grid=(TOKENS // OUT_TM,),
        in_specs=[
            pl.BlockSpec((DIFF_HEADS, OUT_TM, LANES), lambda i: (0, i, 0)),
            pl.BlockSpec((2 * GLA_HEADS, OUT_TM, LANES), lambda i: (0, i, 0)),
            pl.BlockSpec((D_MODEL, D_MODEL), lambda i: (0, 0), pipeline_mode=pl.Buffered(1)),
            pl.BlockSpec((OUT_TM, D_MODEL), lambda i: (i, 0)),
            pl.BlockSpec((1, 1, D_MODEL), lambda i: (i // per_seq, 0, 0)),
            pl.BlockSpec((1, D_MODEL), lambda i: (0, 0)),
        ],
        out_specs=pl.BlockSpec((OUT_TM, D_MODEL), lambda i: (i, 0)),
        out_shape=jax.ShapeDtypeStruct((TOKENS, D_MODEL), F32),
        compiler_params=pltpu.CompilerParams(
            dimension_semantics=("arbitrary",), vmem_limit_bytes=OUT_VMEM_LIMIT),
        name="out_proj_norm",
    )(a, b, w_out, xf, gate, fw)


def _rope_tables():
    inv_freq = ROPE_THETA ** (-np.arange(0, DIFF_QK_DIM, 2, dtype=np.float64) / DIFF_QK_DIM)
    ang = np.arange(SEQ, dtype=np.float64)[:, None] * inv_freq[None, :]
    cos, sin = np.cos(ang), np.sin(ang)
    cos_t = np.tile(cos, (1, 4)).astype(np.float32)
    sin_t = np.concatenate([-sin, sin, -sin, sin], axis=1).astype(np.float32)
    return jnp.asarray(cos_t), jnp.asarray(sin_t)


def kernel(x, c, norm_w, w_ada, b_ada, w_in, lambda_q1, lambda_k1, lambda_q2, lambda_k2,
           diff_norm_w, gla_gate_w2, gla_gate_b, gla_norm_w, w_out, final_norm_w):
    assert x.shape == (BATCH, SEQ, D_MODEL) and w_in.shape[0] == 1
    xf = x.reshape(TOKENS, D_MODEL).astype(F32)

    c_pad = jnp.pad(c.astype(F32), ((0, 8 - BATCH), (0, 0)))
    b_ada_row = b_ada[0][None, :]
    mod = _ada_call(c_pad, w_ada[0], b_ada_row, 2 * D_MODEL)[:BATCH]
    shift = mod[:, :D_MODEL].reshape(BATCH, 1, D_MODEL)
    scale = mod[:, D_MODEL:].reshape(BATCH, 1, D_MODEL)

    w_t = w_in[0].T
    w_gate = jnp.pad(w_t[N_MAIN:], ((0, LANES - GLA_GATE_RANK), (0, 0))).astype(BF16)
    cos, sin = _rope_tables()
    p, g = _proj_call(xf, shift, scale, norm_w[0][None, :], w_t, w_gate, cos, sin)

    row = lambda v: v[0][None, :].astype(F32)
    a, gate8, w_out_bf16 = _attn_call(
        p, row(lambda_q1), row(lambda_k1), row(lambda_q2), row(lambda_k2), row(diff_norm_w),
        c_pad, w_ada[0], b_ada_row, w_out[0])
    gate = gate8[:BATCH].reshape(BATCH, 1, D_MODEL)

    w2_heads = jnp.pad(gla_gate_w2[0], ((0, LANES - GLA_GATE_RANK), (0, 0))).astype(BF16)
    w2_heads = w2_heads.reshape(LANES, GLA_HEADS, GLA_K_DIM).transpose(1, 0, 2)
    b2_heads = gla_gate_b[0].astype(F32).reshape(GLA_HEADS, 1, GLA_K_DIM)
    b = _gla_call(p, g, w2_heads, b2_heads, row(gla_norm_w))

    out = _out_call(a, b, w_out_bf16, xf, gate, final_norm_w[None, :].astype(F32))
    return out.reshape(BATCH, SEQ, D_MODEL).astype(x.dtype)
```

```python
import math

import jax
import jax.numpy as jnp
import numpy as np
from jax import lax
from jax.experimental import pallas as pl
from jax.experimental.pallas import tpu as pltpu

D_MODEL = 2048
BATCH = 4
SEQ = 2048
TOKENS = BATCH * SEQ
CHUNK = 64
LANES = 128

DIFF_HEADS = 8
DIFF_HEAD_DIM = 128
DIFF_QK_DIM = 64
GLA_HEADS = 4
GLA_K_DIM = 128
GLA_V_DIM = 256
GLA_GATE_RANK = 16
GLA_GATE_TAU = 16.0
ROPE_THETA = 10000.0
NORM_EPS = 1e-6
LAMBDA_INIT = 0.8 - 0.6 * math.exp(-0.3 * 0)

N_MAIN = 7168
N_COLBLK = N_MAIN // LANES
CB_DQ, CB_DK, CB_DV, CB_DG = 0, 8, 16, 24
CB_GQ, CB_GK, CB_GV, CB_GG = 32, 36, 40, 48

VMEM_LIMIT = 48 * 1024 * 1024
NEG_BIG = -1e30
LOG2_E = math.log2(math.e)

BF16 = jnp.bfloat16
F32 = jnp.float32


def _nt_dot(a, b):
    return lax.dot_general(a, b, (((1,), (1,)), ((), ())), preferred_element_type=F32)


def _silu(x):
    h = 0.5 * x
    return h + h * jnp.tanh(h)


def _tn_dot(a, b):
    return lax.dot_general(a, b, (((0,), (0,)), ((), ())), preferred_element_type=F32)


ADA_TN = 512


def _ada_block(c_ref, w_ref, b_ref):
    c = c_ref[...]
    c_act = (c * jax.nn.sigmoid(c)).astype(BF16)
    return jnp.dot(c_act, w_ref[...].astype(BF16), preferred_element_type=F32) + b_ref[...]


def _ada_kernel(c_ref, w_ref, b_ref, o_ref):
    o_ref[...] = _ada_block(c_ref, w_ref, b_ref)


def _ada_call(c_pad, w_ada, b_ada, n):
    return pl.pallas_call(
        _ada_kernel,
        grid=(n // ADA_TN,),
        in_specs=[
            pl.BlockSpec((8, D_MODEL), lambda j: (0, 0)),
            pl.BlockSpec((D_MODEL, ADA_TN), lambda j: (0, j)),
            pl.BlockSpec((1, ADA_TN), lambda j: (0, j)),
        ],
        out_specs=pl.BlockSpec((8, ADA_TN), lambda j: (0, j)),
        out_shape=jax.ShapeDtypeStruct((8, n), F32),
        compiler_params=pltpu.CompilerParams(
            dimension_semantics=("arbitrary",), vmem_limit_bytes=VMEM_LIMIT),
        name="ada_mod",
    )(c_pad, w_ada, b_ada)


PROJ_TM = 1024
PROJ_TN = 1024
PROJ_SUB = 256
PROJ_ROPE_TILES = 2
PROJ_VMEM_LIMIT = (2 * PROJ_TM * D_MODEL * 4 + 2 * D_MODEL * PROJ_TN * 4
                   + PROJ_TM * D_MODEL * 2 + 2 * PROJ_TM * PROJ_TN * 2 + 16 * 1024 * 1024)


def _rot_half(x):
    lane = lax.broadcasted_iota(jnp.int32, x.shape, 1)
    first = (lane % DIFF_QK_DIM) < (DIFF_QK_DIM // 2)
    return jnp.where(first, pltpu.roll(x, LANES - 32, 1), pltpu.roll(x, 32, 1))


def _proj_kernel(x_ref, shift_ref, scale_ref, nw_ref, w_ref, wg_ref, cos_ref, sin_ref,
                 p_ref, g_ref, hn_ref):
    j = pl.program_id(1)
    n_sub = PROJ_TM // PROJ_SUB
    n_cb = PROJ_TN // LANES

    def norm_rows(rows):
        mult = nw_ref[...] * (1.0 + scale_ref[0])
        xs = x_ref[rows, :]
        ms = jnp.mean(xs * xs, axis=-1, keepdims=True)
        hn_ref[rows, :] = (xs * lax.rsqrt(ms + NORM_EPS) * mult + shift_ref[0]).astype(BF16)

    def store_rope(rows, acc, sc):
        cos = cos_ref[rows, :] * sc
        sin = sin_ref[rows, :] * sc
        for cb in range(n_cb):
            t = acc[:, cb * LANES:(cb + 1) * LANES]
            p_ref[cb, rows, :] = (t * cos + _rot_half(t) * sin).astype(BF16)

    @pl.when(j == 0)
    def _():
        for r in range(n_sub):
            rows = pl.ds(r * PROJ_SUB, PROJ_SUB)
            norm_rows(rows)
            acc = _nt_dot(hn_ref[rows, :], w_ref[...])
            store_rope(rows, acc, DIFF_QK_DIM ** -0.5 * LOG2_E)
        g_ref[...] = _nt_dot(hn_ref[...], wg_ref[...]).astype(BF16)

    @pl.when(j == 1)
    def _():
        for r in range(n_sub):
            rows = pl.ds(r * PROJ_SUB, PROJ_SUB)
            acc = _nt_dot(hn_ref[rows, :], w_ref[...])
            store_rope(rows, acc, 1.0)

    @pl.when(j >= PROJ_ROPE_TILES)
    def _():
        for r in range(n_sub):
            rows = pl.ds(r * PROJ_SUB, PROJ_SUB)
            acc = _nt_dot(hn_ref[rows, :], w_ref[...])
            for cb in range(n_cb):
                p_ref[cb, rows, :] = acc[:, cb * LANES:(cb + 1) * LANES].astype(BF16)


def _proj_call(xf, shift, scale, norm_w, w_t, w_gate, cos, sin):
    m_tiles = TOKENS // PROJ_TM
    n_tiles = N_MAIN // PROJ_TN
    per_seq = SEQ // PROJ_TM
    n_cb = PROJ_TN // LANES
    return pl.pallas_call(
        _proj_kernel,
        grid=(m_tiles, n_tiles),
        in_specs=[
            pl.BlockSpec((PROJ_TM, D_MODEL), lambda i, j: (i, 0)),
            pl.BlockSpec((1, 1, D_MODEL), lambda i, j: (i // per_seq, 0, 0)),
            pl.BlockSpec((1, 1, D_MODEL), lambda i, j: (i // per_seq, 0, 0)),
            pl.BlockSpec((1, D_MODEL), lambda i, j: (0, 0)),
            pl.BlockSpec((PROJ_TN, D_MODEL), lambda i, j: (j, 0)),
            pl.BlockSpec((LANES, D_MODEL), lambda i, j: (0, 0)),
            pl.BlockSpec((PROJ_TM, LANES), lambda i, j: (i % per_seq, 0)),
            pl.BlockSpec((PROJ_TM, LANES), lambda i, j: (i % per_seq, 0)),
        ],
        out_specs=[
            pl.BlockSpec((n_cb, PROJ_TM, LANES), lambda i, j: (j, i, 0)),
            pl.BlockSpec((PROJ_TM, LANES), lambda i, j: (i, 0)),
        ],
        out_shape=[
            jax.ShapeDtypeStruct((N_COLBLK, TOKENS, LANES), BF16),
            jax.ShapeDtypeStruct((TOKENS, LANES), BF16),
        ],
        scratch_shapes=[pltpu.VMEM((PROJ_TM, D_MODEL), BF16)],
        compiler_params=pltpu.CompilerParams(
            dimension_semantics=("arbitrary", "arbitrary"), vmem_limit_bytes=PROJ_VMEM_LIMIT),
        name="norm_in_proj",
    )(xf, shift, scale, norm_w, w_t, w_gate, cos, sin)


ATT_T = 256
ATT_NBLK = SEQ // ATT_T
ATT_LEAD = 9
assert ATT_LEAD >= ATT_NBLK - 1
ATT_ONES_ROWS = 16
ATT_HEADS_PER_STEP = 2
ATT_SLOTS = ATT_LEAD + 4


def _attn_kernel(lq1_ref, lk1_ref, lq2_ref, lk2_ref, dnw_ref, q_ref, k_ref, v_ref, dg_ref,
                 c_ref, wa_ref, ba_ref, wo_ref, o_ref, gate_ref, wob_ref, vt_ref, s_ref):
    gate_ref[...] = _ada_block(c_ref, wa_ref, ba_ref)
    wob_ref[...] = wo_ref[...].astype(BF16)

    lam = (jnp.exp(jnp.sum(lq1_ref[...] * lk1_ref[...], keepdims=True))
           - jnp.exp(jnp.sum(lq2_ref[...] * lk2_ref[...], keepdims=True))
           + LAMBDA_INIT)
    out_gain = dnw_ref[...] * (1.0 - LAMBDA_INIT)

    for hd in range(ATT_HEADS_PER_STEP):
        for jb in range(ATT_NBLK):
            cols = pl.ds(jb * ATT_T, ATT_T)
            vb = v_ref[hd, jb * ATT_T:(jb + 1) * ATT_T, :].astype(F32)
            vt_ref[hd, 0:DIFF_HEAD_DIM, cols] = vb.T.astype(BF16)
        vt_ref[hd, DIFF_HEAD_DIM:, :] = jnp.ones((ATT_ONES_ROWS, SEQ), BF16)

    lane = lax.broadcasted_iota(jnp.int32, (ATT_T, LANES), 1)
    comp_a = lane < DIFF_QK_DIM
    krow = lax.broadcasted_iota(jnp.int32, (ATT_T, ATT_T), 0) // CHUNK
    qcol = lax.broadcasted_iota(jnp.int32, (ATT_T, ATT_T), 1) // CHUNK
    diag_mask = krow <= qcol
    diag_mask2 = jnp.concatenate([diag_mask, diag_mask], axis=1)

    units = [(hd, qi, j) for hd in range(ATT_HEADS_PER_STEP)
             for qi in range(ATT_NBLK) for j in range(qi + 1)]
    blocks = {(hd, qi): {"m": None}
              for hd in range(ATT_HEADS_PER_STEP) for qi in range(ATT_NBLK)}

    def block_queries(hd, qi):
        q = q_ref[hd, pl.ds(qi * ATT_T, ATT_T), :]
        zero = jnp.zeros_like(q)
        return jnp.concatenate([jnp.where(comp_a, q, zero), jnp.where(comp_a, zero, q)], axis=0)

    dyn0 = lax.shift_right_logical(pl.program_id(0), 20)

    def score_unit(t, hd, qi, j):
        st = blocks[hd, qi]
        if j == 0:
            st["q2"] = block_queries(hd, qi)
        s = _nt_dot(k_ref[hd, pl.ds(j * ATT_T, ATT_T), :], st["q2"])
        if j == qi:
            s = jnp.where(diag_mask2, s, NEG_BIG)
        s_ref[dyn0 + t % ATT_SLOTS] = s
        mj = jnp.max(s, axis=0, keepdims=True)
        st["m"] = mj if st["m"] is None else jnp.maximum(st["m"], mj)

    def value_unit(t, hd, qi, j):
        st = blocks[hd, qi]
        p = jnp.exp2(s_ref[dyn0 + t % ATT_SLOTS] - st["m"])
        pvj = jnp.dot(vt_ref[hd, :, j * ATT_T:(j + 1) * ATT_T], p.astype(BF16),
                      preferred_element_type=F32)
        st["pv"] = pvj if j == 0 else st["pv"] + pvj
        if j == qi:
            finish(hd, qi, st["pv"])

    def finish(hd, qi, pv_l):
        rows = pl.ds(qi * ATT_T, ATT_T)
        l = pv_l[DIFF_HEAD_DIM:DIFF_HEAD_DIM + 1, :]
        pv = pv_l[0:DIFF_HEAD_DIM, :] * (1.0 / l)
        out_t = pv[:, :ATT_T] - lam * pv[:, ATT_T:]
        out = out_t.T
        ms = jnp.mean(out * out, axis=-1, keepdims=True)
        y = out * lax.rsqrt(ms + NORM_EPS) * out_gain
        g = dg_ref[hd, rows, :].astype(F32)
        o_ref[hd, rows, :] = (y * _silu(g)).astype(BF16)

    for t in range(len(units) + ATT_LEAD):
        if t < len(units):
            score_unit(t, *units[t])
        if t >= ATT_LEAD:
            value_unit(t - ATT_LEAD, *units[t - ATT_LEAD])


def _attn_call(p, lq1, lk1, lq2, lk2, dnw, c_pad, w_ada, b_ada, w_out):
    hp = ATT_HEADS_PER_STEP
    steps_per_batch = DIFF_HEADS // hp
    side_w = D_MODEL // (BATCH * steps_per_batch)
    assert side_w % LANES == 0
    wo_rows = pl.BlockSpec((side_w, D_MODEL), lambda b, h: (b * steps_per_batch + h, 0))
    gate_cb0 = 2 * D_MODEL // side_w
    vec = lambda n: pl.BlockSpec((1, n), lambda b, h: (0, 0))
    slab = lambda cb0: pl.BlockSpec((hp, SEQ, LANES), lambda b, h: (cb0 // hp + h, b, 0))
    gate_blk = lambda rows, cb0: pl.BlockSpec(
        (rows, side_w), lambda b, h: (0, cb0 + b * steps_per_batch + h))
    return pl.pallas_call(
        _attn_kernel,
        grid=(BATCH, steps_per_batch),
        in_specs=[vec(DIFF_QK_DIM)] * 4 + [vec(DIFF_HEAD_DIM),
                  slab(CB_DQ), slab(CB_DK), slab(CB_DV), slab(CB_DG),
                  pl.BlockSpec((8, D_MODEL), lambda b, h: (0, 0)),
                  gate_blk(D_MODEL, gate_cb0), gate_blk(1, gate_cb0), wo_rows],
        out_specs=[pl.BlockSpec((hp, SEQ, LANES), lambda b, h: (h, b, 0)), gate_blk(8, 0),
                   wo_rows],
        out_shape=[jax.ShapeDtypeStruct((DIFF_HEADS, TOKENS, LANES), BF16),
                   jax.ShapeDtypeStruct((8, D_MODEL), F32),
                   jax.ShapeDtypeStruct((D_MODEL, D_MODEL), BF16)],
        scratch_shapes=[
            pltpu.VMEM((hp, DIFF_HEAD_DIM + ATT_ONES_ROWS, SEQ), BF16),
            pltpu.VMEM((ATT_SLOTS, ATT_T, 2 * ATT_T), F32),
        ],
        compiler_params=pltpu.CompilerParams(
            dimension_semantics=("arbitrary", "arbitrary"), vmem_limit_bytes=VMEM_LIMIT),
        name="diff_attn",
    )(lq1, lk1, lq2, lk2, dnw, p, p, p, p, c_pad, w_ada, b_ada, w_out)


GLA_BLK = 256
GLA_NCHUNK = SEQ // CHUNK
GLA_HEADS_PER_STEP = 4


def _split_hi_lo(x):
    hi = x.astype(BF16)
    lo = (x - hi.astype(F32)).astype(BF16)
    return hi, lo


def _gla_kernel(g_ref, w2_ref, b2_ref, gw_ref, q_ref, k_ref, v_ref, gg_ref, o_ref,
                kdec_ref, tot_ref, kv_ref, st_ref):
    for hd in range(GLA_HEADS_PER_STEP):
        _gla_head(g_ref, w2_ref.at[hd], b2_ref.at[hd], gw_ref, q_ref.at[hd], k_ref.at[hd],
                  v_ref.at[hd], gg_ref.at[hd], o_ref.at[hd], kdec_ref, tot_ref, kv_ref, st_ref)


def _gla_head(g_ref, w2_ref, b2_ref, gw_ref, q_ref, k_ref, v_ref, gg_ref, o_ref,
              kdec_ref, tot_ref, kv_ref, st_ref):
    z = jnp.dot(g_ref[...], w2_ref[...], preferred_element_type=F32) + b2_ref[...]
    log_a = (jnp.minimum(z, 0.0) - jnp.log(1.0 + jnp.exp(-jnp.abs(z)))) * (1.0 / GLA_GATE_TAU)

    r = lax.broadcasted_iota(jnp.int32, (GLA_BLK, GLA_BLK), 0)
    c = lax.broadcasted_iota(jnp.int32, (GLA_BLK, GLA_BLK), 1)
    after = jnp.where(((r // CHUNK) == (c // CHUNK)) & (c > r), 1.0, 0.0).astype(BF16)
    for blk in range(SEQ // GLA_BLK):
        rows = pl.ds(blk * GLA_BLK, GLA_BLK)
        la = log_a[blk * GLA_BLK:(blk + 1) * GLA_BLK, :]
        hi, lo = _split_hi_lo(la)
        both = jnp.dot(after, jnp.concatenate([hi, lo], axis=1), preferred_element_type=F32)
        suffix = both[:, :GLA_K_DIM] + both[:, GLA_K_DIM:]
        kdec_ref[rows, :] = (k_ref[rows, :].astype(F32) * jnp.exp(suffix)).astype(BF16)
        tot_ref[rows, :] = suffix + la

    dec_rows = jnp.exp(tot_ref[pl.ds(0, GLA_NCHUNK, stride=CHUNK), :])
    dec_cols = jnp.concatenate(
        [dec_rows, jnp.zeros((GLA_K_DIM - GLA_NCHUNK, GLA_K_DIM), F32)], axis=0).T

    dyn0 = lax.shift_right_logical(pl.program_id(0), 20)
    for ci in range(GLA_NCHUNK):
        rows = pl.ds(ci * CHUNK, CHUNK)
        v_c = jnp.concatenate([v_ref[0, rows, :], v_ref[1, rows, :]], axis=1)
        kv_ref[dyn0 + ci] = _tn_dot(kdec_ref[rows, :], v_c)

    state = jnp.zeros((GLA_K_DIM, GLA_V_DIM), F32)
    for ci in range(GLA_NCHUNK):
        state = state * dec_cols[:, ci:ci + 1] + kv_ref[dyn0 + ci]
        st_ref[ci] = state.astype(BF16)

    gw = gw_ref[...]
    qs = GLA_K_DIM ** -0.5
    for ci in range(GLA_NCHUNK):
        rows = pl.ds(ci * CHUNK, CHUNK)
        o = jnp.dot(q_ref[rows, :], st_ref[ci], preferred_element_type=F32)
        ms = jnp.mean(o * o, axis=-1, keepdims=True) * (qs * qs)
        y = o * (qs * lax.rsqrt(ms + NORM_EPS)) * gw
        gate = jnp.concatenate([gg_ref[0, rows, :], gg_ref[1, rows, :]], axis=1).astype(F32)
        y = y * _silu(gate)
        o_ref[0, rows, :] = y[:, :LANES].astype(BF16)
        o_ref[1, rows, :] = y[:, LANES:].astype(BF16)


def _gla_call(p, g, w2_heads, b2_heads, gw):
    hp = GLA_HEADS_PER_STEP
    qk = lambda cb0: pl.BlockSpec((hp, SEQ, LANES), lambda b, h: (cb0 // hp + h, b, 0))
    p4 = p.reshape(N_COLBLK // 2, 2, TOKENS, LANES)
    vg = lambda cb0: pl.BlockSpec((hp, 2, SEQ, LANES), lambda b, h: (cb0 // 2 // hp + h, 0, b, 0))
    out = pl.pallas_call(
        _gla_kernel,
        grid=(BATCH, GLA_HEADS // hp),
        in_specs=[
            pl.BlockSpec((SEQ, LANES), lambda b, h: (b, 0)),
            pl.BlockSpec((hp, LANES, GLA_K_DIM), lambda b, h: (h, 0, 0)),
            pl.BlockSpec((hp, 1, GLA_K_DIM), lambda b, h: (h, 0, 0)),
            pl.BlockSpec((1, GLA_V_DIM), lambda b, h: (0, 0)),
            qk(CB_GQ), qk(CB_GK), vg(CB_GV), vg(CB_GG),
        ],
        out_specs=pl.BlockSpec((hp, 2, SEQ, LANES), lambda b, h: (h, 0, b, 0)),
        out_shape=jax.ShapeDtypeStruct((GLA_HEADS, 2, TOKENS, LANES), BF16),
        scratch_shapes=[
            pltpu.VMEM((SEQ, GLA_K_DIM), BF16),
            pltpu.VMEM((SEQ, GLA_K_DIM), F32),
            pltpu.VMEM((GLA_NCHUNK, GLA_K_DIM, GLA_V_DIM), F32),
            pltpu.VMEM((GLA_NCHUNK, GLA_K_DIM, GLA_V_DIM), BF16),
        ],
        compiler_params=pltpu.CompilerParams(
            dimension_semantics=("arbitrary", "arbitrary"), vmem_limit_bytes=VMEM_LIMIT),
        name="gla_scan",
    )(g, w2_heads, b2_heads, gw, p, p, p4, p4)
    return out.reshape(2 * GLA_HEADS, TOKENS, LANES)


OUT_TM = 1024
OUT_SPLITS = (256, 256, 256, 256)
assert sum(OUT_SPLITS) == OUT_TM
OUT_VMEM_LIMIT = (4 * OUT_TM * D_MODEL * 4 + 2 * OUT_TM * D_MODEL * 2 + D_MODEL * D_MODEL * 2
                  + 8 * 1024 * 1024)


def _out_kernel(a_ref, b_ref, wb_ref, x_ref, gate_ref, fw_ref, o_ref):
    gate = gate_ref[0]
    fw = fw_ref[...]
    row0 = 0
    for n_rows in OUT_SPLITS:
        rows = pl.ds(row0, n_rows)
        row0 += n_rows
        mix = jnp.concatenate(
            [a_ref[h, rows, :] for h in range(a_ref.shape[0])]
            + [b_ref[h, rows, :] for h in range(b_ref.shape[0])], axis=1)
        y = jnp.dot(mix, wb_ref[...], preferred_element_type=F32)
        h_res = x_ref[rows, :] + gate * y
        ms = jnp.mean(h_res * h_res, axis=-1, keepdims=True)
        o_ref[rows, :] = h_res * lax.rsqrt(ms + NORM_EPS) * fw


def _out_call(a, b, w_out, xf, gate, fw):
    per_seq = SEQ // OUT_TM
    return pl.pallas_call(
        _out_kernel,
        grid=(TOKENS // OUT_TM,),
        in_specs=[
            pl.BlockSpec((DIFF_HEADS, OUT_TM, LANES), lambda i: (0, i, 0)),
            pl.BlockSpec((2 * GLA_HEADS, OUT_TM, LANES), lambda i: (0, i, 0)),
            pl.BlockSpec((D_MODEL, D_MODEL), lambda i: (0, 0), pipeline_mode=pl.Buffered(1)),
            pl.BlockSpec((OUT_TM, D_MODEL), lambda i: (i, 0)),
            pl.BlockSpec((1, 1, D_MODEL), lambda i: (i // per_seq, 0, 0)),
            pl.BlockSpec((1, D_MODEL), lambda i: (0, 0)),
        ],
        out_specs=pl.BlockSpec((OUT_TM, D_MODEL), lambda i: (i, 0)),
        out_shape=jax.ShapeDtypeStruct((TOKENS, D_MODEL), F32),
        compiler_params=pltpu.CompilerParams(
            dimension_semantics=("arbitrary",), vmem_limit_bytes=OUT_VMEM_LIMIT),
        name="out_proj_norm",
    )(a, b, w_out, xf, gate, fw)


def _rope_tables():
    inv_freq = ROPE_THETA ** (-np.arange(0, DIFF_QK_DIM, 2, dtype=np.float64) / DIFF_QK_DIM)
    ang = np.arange(SEQ, dtype=np.float64)[:, None] * inv_freq[None, :]
    cos, sin = np.cos(ang), np.sin(ang)
    cos_t = np.tile(cos, (1, 4)).astype(np.float32)
    sin_t = np.concatenate([-sin, sin, -sin, sin], axis=1).astype(np.float32)
    return jnp.asarray(cos_t), jnp.asarray(sin_t)


def kernel(x, c, norm_w, w_ada, b_ada, w_in, lambda_q1, lambda_k1, lambda_q2, lambda_k2,
           diff_norm_w, gla_gate_w2, gla_gate_b, gla_norm_w, w_out, final_norm_w):
    assert x.shape == (BATCH, SEQ, D_MODEL) and w_in.shape[0] == 1
    xf = x.reshape(TOKENS, D_MODEL).astype(F32)

    c_pad = jnp.pad(c.astype(F32), ((0, 8 - BATCH), (0, 0)))
    b_ada_row = b_ada[0][None, :]
    mod = _ada_call(c_pad, w_ada[0], b_ada_row, 2 * D_MODEL)[:BATCH]
    shift = mod[:, :D_MODEL].reshape(BATCH, 1, D_MODEL)
    scale = mod[:, D_MODEL:].reshape(BATCH, 1, D_MODEL)

    w_t = w_in[0].T
    w_gate = jnp.pad(w_t[N_MAIN:], ((0, LANES - GLA_GATE_RANK), (0, 0))).astype(BF16)
    cos, sin = _rope_tables()
    p, g = _proj_call(xf, shift, scale, norm_w[0][None, :], w_t, w_gate, cos, sin)

    row = lambda v: v[0][None, :].astype(F32)
    a, gate8, w_out_bf16 = _attn_call(
        p, row(lambda_q1), row(lambda_k1), row(lambda_q2), row(lambda_k2), row(diff_norm_w),
        c_pad, w_ada[0], b_ada_row, w_out[0])
    gate = gate8[:BATCH].reshape(BATCH, 1, D_MODEL)

    w2_heads = jnp.pad(gla_gate_w2[0], ((0, LANES - GLA_GATE_RANK), (0, 0))).astype(BF16)
    w2_heads = w2_heads.reshape(LANES, GLA_HEADS, GLA_K_DIM).transpose(1, 0, 2)
    b2_heads = gla_gate_b[0].astype(F32).reshape(GLA_HEADS, 1, GLA_K_DIM)
    b = _gla_call(p, g, w2_heads, b2_heads, row(gla_norm_w))

    out = _out_call(a, b, w_out_bf16, xf, gate, final_norm_w[None, :].astype(F32))
    return out.reshape(BATCH, SEQ, D_MODEL).astype(x.dtype)
```

```python
import math

import jax
import jax.numpy as jnp
import numpy as np
from jax import lax
from jax.experimental import pallas as pl
from jax.experimental.pallas import tpu as pltpu

D_MODEL = 2048
BATCH = 4
SEQ = 2048
TOKENS = BATCH * SEQ
CHUNK = 64
LANES = 128

DIFF_HEADS = 8
DIFF_HEAD_DIM = 128
DIFF_QK_DIM = 64
GLA_HEADS = 4
GLA_K_DIM = 128
GLA_V_DIM = 256
GLA_GATE_RANK = 16
GLA_GATE_TAU = 16.0
ROPE_THETA = 10000.0
NORM_EPS = 1e-6
LAMBDA_INIT = 0.8 - 0.6 * math.exp(-0.3 * 0)

N_MAIN = 7168
N_COLBLK = N_MAIN // LANES
CB_DQ, CB_DK, CB_DV, CB_DG = 0, 8, 16, 24
CB_GQ, CB_GK, CB_GV, CB_GG = 32, 36, 40, 48

VMEM_LIMIT = 48 * 1024 * 1024
NEG_BIG = -1e30
LOG2_E = math.log2(math.e)

BF16 = jnp.bfloat16
F32 = jnp.float32


def _nt_dot(a, b):
    return lax.dot_general(a, b, (((1,), (1,)), ((), ())), preferred_element_type=F32)


def _silu(x):
    h = 0.5 * x
    return h + h * jnp.tanh(h)


def _tn_dot(a, b):
    return lax.dot_general(a, b, (((0,), (0,)), ((), ())), preferred_element_type=F32)


ADA_TN = 1024


def _ada_block(c_ref, w_ref, b_ref):
    c = c_ref[...]
    c_act = (c * jax.nn.sigmoid(c)).astype(BF16)
    return jnp.dot(c_act, w_ref[...].astype(BF16), preferred_element_type=F32) + b_ref[...]


def _ada_kernel(c_ref, w_ref, b_ref, o_ref):
    o_ref[...] = _ada_block(c_ref, w_ref, b_ref)


def _ada_call(c_pad, w_ada, b_ada, n):
    return pl.pallas_call(
        _ada_kernel,
        grid=(n // ADA_TN,),
        in_specs=[
            pl.BlockSpec((8, D_MODEL), lambda j: (0, 0)),
            pl.BlockSpec((D_MODEL, ADA_TN), lambda j: (0, j)),
            pl.BlockSpec((1, ADA_TN), lambda j: (0, j)),
        ],
        out_specs=pl.BlockSpec((8, ADA_TN), lambda j: (0, j)),
        out_shape=jax.ShapeDtypeStruct((8, n), F32),
        compiler_params=pltpu.CompilerParams(
            dimension_semantics=("arbitrary",), vmem_limit_bytes=VMEM_LIMIT),
        name="ada_mod",
    )(c_pad, w_ada, b_ada)


PROJ_TM = 1024
PROJ_TN = 1024
PROJ_SUB = 256
PROJ_ROPE_TILES = 2
PROJ_VMEM_LIMIT = (2 * PROJ_TM * D_MODEL * 4 + 2 * D_MODEL * PROJ_TN * 4
                   + PROJ_TM * D_MODEL * 2 + 2 * PROJ_TM * PROJ_TN * 2 + 16 * 1024 * 1024)


def _rot_half(x):
    lane = lax.broadcasted_iota(jnp.int32, x.shape, 1)
    first = (lane % DIFF_QK_DIM) < (DIFF_QK_DIM // 2)
    return jnp.where(first, pltpu.roll(x, LANES - 32, 1), pltpu.roll(x, 32, 1))


def _proj_kernel(x_ref, shift_ref, scale_ref, nw_ref, w_ref, wg_ref, cos_ref, sin_ref,
                 p_ref, g_ref, hn_ref):
    j = pl.program_id(1)
    n_sub = PROJ_TM // PROJ_SUB
    n_cb = PROJ_TN // LANES

    def norm_rows(rows):
        mult = nw_ref[...] * (1.0 + scale_ref[0])
        xs = x_ref[rows, :]
        ms = jnp.mean(xs * xs, axis=-1, keepdims=True)
        hn_ref[rows, :] = (xs * lax.rsqrt(ms + NORM_EPS) * mult + shift_ref[0]).astype(BF16)

    def store_rope(rows, acc, sc):
        cos = cos_ref[rows, :] * sc
        sin = sin_ref[rows, :] * sc
        for cb in range(n_cb):
            t = acc[:, cb * LANES:(cb + 1) * LANES]
            p_ref[cb, rows, :] = (t * cos + _rot_half(t) * sin).astype(BF16)

    @pl.when(j == 0)
    def _():
        for r in range(n_sub):
            rows = pl.ds(r * PROJ_SUB, PROJ_SUB)
            norm_rows(rows)
            acc = _nt_dot(hn_ref[rows, :], w_ref[...])
            store_rope(rows, acc, DIFF_QK_DIM ** -0.5 * LOG2_E)
        g_ref[...] = _nt_dot(hn_ref[...], wg_ref[...]).astype(BF16)

    @pl.when(j == 1)
    def _():
        for r in range(n_sub):
            rows = pl.ds(r * PROJ_SUB, PROJ_SUB)
            acc = _nt_dot(hn_ref[rows, :], w_ref[...])
            store_rope(rows, acc, 1.0)

    @pl.when(j >= PROJ_ROPE_TILES)
    def _():
        for r in range(n_sub):
            rows = pl.ds(r * PROJ_SUB, PROJ_SUB)
            acc = _nt_dot(hn_ref[rows, :], w_ref[...])
            for cb in range(n_cb):
                p_ref[cb, rows, :] = acc[:, cb * LANES:(cb + 1) * LANES].astype(BF16)


def _proj_call(xf, shift, scale, norm_w, w_t, w_gate, cos, sin):
    m_tiles = TOKENS // PROJ_TM
    n_tiles = N_MAIN // PROJ_TN
    per_seq = SEQ // PROJ_TM
    n_cb = PROJ_TN // LANES
    return pl.pallas_call(
        _proj_kernel,
        grid=(m_tiles, n_tiles),
        in_specs=[
            pl.BlockSpec((PROJ_TM, D_MODEL), lambda i, j: (i, 0)),
            pl.BlockSpec((1, 1, D_MODEL), lambda i, j: (i // per_seq, 0, 0)),
            pl.BlockSpec((1, 1, D_MODEL), lambda i, j: (i // per_seq, 0, 0)),
            pl.BlockSpec((1, D_MODEL), lambda i, j: (0, 0)),
            pl.BlockSpec((PROJ_TN, D_MODEL), lambda i, j: (j, 0)),
            pl.BlockSpec((LANES, D_MODEL), lambda i, j: (0, 0)),
            pl.BlockSpec((PROJ_TM, LANES), lambda i, j: (i % per_seq, 0)),
            pl.BlockSpec((PROJ_TM, LANES), lambda i, j: (i % per_seq, 0)),
        ],
        out_specs=[
            pl.BlockSpec((n_cb, PROJ_TM, LANES), lambda i, j: (j, i, 0)),
            pl.BlockSpec((PROJ_TM, LANES), lambda i, j: (i, 0)),
        ],
        out_shape=[
            jax.ShapeDtypeStruct((N_COLBLK, TOKENS, LANES), BF16),
            jax.ShapeDtypeStruct((TOKENS, LANES), BF16),
        ],
        scratch_shapes=[pltpu.VMEM((PROJ_TM, D_MODEL), BF16)],
        compiler_params=pltpu.CompilerParams(
            dimension_semantics=("arbitrary", "arbitrary"), vmem_limit_bytes=PROJ_VMEM_LIMIT),
        name="norm_in_proj",
    )(xf, shift, scale, norm_w, w_t, w_gate, cos, sin)


ATT_T = 256
ATT_NBLK = SEQ // ATT_T
ATT_LEAD = 9
assert ATT_LEAD >= ATT_NBLK - 1
ATT_ONES_ROWS = 16
ATT_HEADS_PER_STEP = 2
ATT_SLOTS = ATT_LEAD + 4


def _attn_kernel(lq1_ref, lk1_ref, lq2_ref, lk2_ref, dnw_ref, q_ref, k_ref, v_ref, dg_ref,
                 c_ref, wa_ref, ba_ref, wo_ref, o_ref, gate_ref, wob_ref, vt_ref, s_ref):
    gate_ref[...] = _ada_block(c_ref, wa_ref, ba_ref)
    wob_ref[...] = wo_ref[...].astype(BF16)

    lam = (jnp.exp(jnp.sum(lq1_ref[...] * lk1_ref[...], keepdims=True))
           - jnp.exp(jnp.sum(lq2_ref[...] * lk2_ref[...], keepdims=True))
           + LAMBDA_INIT)
    out_gain = dnw_ref[...] * (1.0 - LAMBDA_INIT)

    for hd in range(ATT_HEADS_PER_STEP):
        for jb in range(ATT_NBLK):
            cols = pl.ds(jb * ATT_T, ATT_T)
            vb = v_ref[hd, jb * ATT_T:(jb + 1) * ATT_T, :].astype(F32)
            vt_ref[hd, 0:DIFF_HEAD_DIM, cols] = vb.T.astype(BF16)
        vt_ref[hd, DIFF_HEAD_DIM:, :] = jnp.ones((ATT_ONES_ROWS, SEQ), BF16)

    lane = lax.broadcasted_iota(jnp.int32, (ATT_T, LANES), 1)
    comp_a = lane < DIFF_QK_DIM
    krow = lax.broadcasted_iota(jnp.int32, (ATT_T, ATT_T), 0) // CHUNK
    qcol = lax.broadcasted_iota(jnp.int32, (ATT_T, ATT_T), 1) // CHUNK
    diag_mask = krow <= qcol
    diag_mask2 = jnp.concatenate([diag_mask, diag_mask], axis=1)

    units = [(hd, qi, j) for hd in range(ATT_HEADS_PER_STEP)
             for qi in range(ATT_NBLK) for j in range(qi + 1)]
    blocks = {(hd, qi): {"m": None}
              for hd in range(ATT_HEADS_PER_STEP) for qi in range(ATT_NBLK)}

    def block_queries(hd, qi):
        q = q_ref[hd, pl.ds(qi * ATT_T, ATT_T), :]
        zero = jnp.zeros_like(q)
        return jnp.concatenate([jnp.where(comp_a, q, zero), jnp.where(comp_a, zero, q)], axis=0)

    dyn0 = lax.shift_right_logical(pl.program_id(0), 20)

    def score_unit(t, hd, qi, j):
        st = blocks[hd, qi]
        if j == 0:
            st["q2"] = block_queries(hd, qi)
        s = _nt_dot(k_ref[hd, pl.ds(j * ATT_T, ATT_T), :], st["q2"])
        if j == qi:
            s = jnp.where(diag_mask2, s, NEG_BIG)
        s_ref[dyn0 + t % ATT_SLOTS] = s
        mj = jnp.max(s, axis=0, keepdims=True)
        st["m"] = mj if st["m"] is None else jnp.maximum(st["m"], mj)

    def value_unit(t, hd, qi, j):
        st = blocks[hd, qi]
        p = jnp.exp2(s_ref[dyn0 + t % ATT_SLOTS] - st["m"])
        pvj = jnp.dot(vt_ref[hd, :, j * ATT_T:(j + 1) * ATT_T], p.astype(BF16),
                      preferred_element_type=F32)
        st["pv"] = pvj if j == 0 else st["pv"] + pvj
        if j == qi:
            finish(hd, qi, st["pv"])

    def finish(hd, qi, pv_l):
        rows = pl.ds(qi * ATT_T, ATT_T)
        l = pv_l[DIFF_HEAD_DIM:DIFF_HEAD_DIM + 1, :]
        pv = pv_l[0:DIFF_HEAD_DIM, :] * (1.0 / l)
        out_t = pv[:, :ATT_T] - lam * pv[:, ATT_T:]
        out = out_t.T
        ms = jnp.mean(out * out, axis=-1, keepdims=True)
        y = out * lax.rsqrt(ms + NORM_EPS) * out_gain
        g = dg_ref[hd, rows, :].astype(F32)
        o_ref[hd, rows, :] = (y * _silu(g)).astype(BF16)

    for t in range(len(units) + ATT_LEAD):
        if t < len(units):
            score_unit(t, *units[t])
        if t >= ATT_LEAD:
            value_unit(t - ATT_LEAD, *units[t - ATT_LEAD])


def _attn_call(p, lq1, lk1, lq2, lk2, dnw, c_pad, w_ada, b_ada, w_out):
    hp = ATT_HEADS_PER_STEP
    steps_per_batch = DIFF_HEADS // hp
    side_w = D_MODEL // (BATCH * steps_per_batch)
    assert side_w % LANES == 0
    wo_rows = pl.BlockSpec((side_w, D_MODEL), lambda b, h: (b * steps_per_batch + h, 0))
    gate_cb0 = 2 * D_MODEL // side_w
    vec = lambda n: pl.BlockSpec((1, n), lambda b, h: (0, 0))
    slab = lambda cb0: pl.BlockSpec((hp, SEQ, LANES), lambda b, h: (cb0 // hp + h, b, 0))
    gate_blk = lambda rows, cb0: pl.BlockSpec(
        (rows, side_w), lambda b, h: (0, cb0 + b * steps_per_batch + h))
    return pl.pallas_call(
        _attn_kernel,
        grid=(BATCH, steps_per_batch),
        in_specs=[vec(DIFF_QK_DIM)] * 4 + [vec(DIFF_HEAD_DIM),
                  slab(CB_DQ), slab(CB_DK), slab(CB_DV), slab(CB_DG),
                  pl.BlockSpec((8, D_MODEL), lambda b, h: (0, 0)),
                  gate_blk(D_MODEL, gate_cb0), gate_blk(1, gate_cb0), wo_rows],
        out_specs=[pl.BlockSpec((hp, SEQ, LANES), lambda b, h: (h, b, 0)), gate_blk(8, 0),
                   wo_rows],
        out_shape=[jax.ShapeDtypeStruct((DIFF_HEADS, TOKENS, LANES), BF16),
                   jax.ShapeDtypeStruct((8, D_MODEL), F32),
                   jax.ShapeDtypeStruct((D_MODEL, D_MODEL), BF16)],
        scratch_shapes=[
            pltpu.VMEM((hp, DIFF_HEAD_DIM + ATT_ONES_ROWS, SEQ), BF16),
            pltpu.VMEM((ATT_SLOTS, ATT_T, 2 * ATT_T), F32),
        ],
        compiler_params=pltpu.CompilerParams(
            dimension_semantics=("arbitrary", "arbitrary"), vmem_limit_bytes=VMEM_LIMIT),
        name="diff_attn",
    )(lq1, lk1, lq2, lk2, dnw, p, p, p, p, c_pad, w_ada, b_ada, w_out)


GLA_BLK = 256
GLA_NCHUNK = SEQ // CHUNK
GLA_HEADS_PER_STEP = 2


def _split_hi_lo(x):
    hi = x.astype(BF16)
    lo = (x - hi.astype(F32)).astype(BF16)
    return hi, lo


def _gla_kernel(g_ref, w2_ref, b2_ref, gw_ref, q_ref, k_ref, v_ref, gg_ref, o_ref,
                kdec_ref, tot_ref, kv_ref, st_ref):
    for hd in range(GLA_HEADS_PER_STEP):
        _gla_head(g_ref, w2_ref.at[hd], b2_ref.at[hd], gw_ref, q_ref.at[hd], k_ref.at[hd],
                  v_ref.at[hd], gg_ref.at[hd], o_ref.at[hd], kdec_ref, tot_ref, kv_ref, st_ref)


def _gla_head(g_ref, w2_ref, b2_ref, gw_ref, q_ref, k_ref, v_ref, gg_ref, o_ref,
              kdec_ref, tot_ref, kv_ref, st_ref):
    z = jnp.dot(g_ref[...], w2_ref[...], preferred_element_type=F32) + b2_ref[...]
    log_a = (jnp.minimum(z, 0.0) - jnp.log(1.0 + jnp.exp(-jnp.abs(z)))) * (1.0 / GLA_GATE_TAU)

    r = lax.broadcasted_iota(jnp.int32, (GLA_BLK, GLA_BLK), 0)
    c = lax.broadcasted_iota(jnp.int32, (GLA_BLK, GLA_BLK), 1)
    after = jnp.where(((r // CHUNK) == (c // CHUNK)) & (c > r), 1.0, 0.0).astype(BF16)
    for blk in range(SEQ // GLA_BLK):
        rows = pl.ds(blk * GLA_BLK, GLA_BLK)
        la = log_a[blk * GLA_BLK:(blk + 1) * GLA_BLK, :]
        hi, lo = _split_hi_lo(la)
        both = jnp.dot(after, jnp.concatenate([hi, lo], axis=1), preferred_element_type=F32)
        suffix = both[:, :GLA_K_DIM] + both[:, GLA_K_DIM:]
        kdec_ref[rows, :] = (k_ref[rows, :].astype(F32) * jnp.exp(suffix)).astype(BF16)
        tot_ref[rows, :] = suffix + la

    dec_rows = jnp.exp(tot_ref[pl.ds(0, GLA_NCHUNK, stride=CHUNK), :])
    dec_cols = jnp.concatenate(
        [dec_rows, jnp.zeros((GLA_K_DIM - GLA_NCHUNK, GLA_K_DIM), F32)], axis=0).T

    dyn0 = lax.shift_right_logical(pl.program_id(0), 20)
    for ci in range(GLA_NCHUNK):
        rows = pl.ds(ci * CHUNK, CHUNK)
        v_c = jnp.concatenate([v_ref[0, rows, :], v_ref[1, rows, :]], axis=1)
        kv_ref[dyn0 + ci] = _tn_dot(kdec_ref[rows, :], v_c)

    state = jnp.zeros((GLA_K_DIM, GLA_V_DIM), F32)
    for ci in range(GLA_NCHUNK):
        state = state * dec_cols[:, ci:ci + 1] + kv_ref[dyn0 + ci]
        st_ref[ci] = state.astype(BF16)

    gw = gw_ref[...]
    qs = GLA_K_DIM ** -0.5
    for ci in range(GLA_NCHUNK):
        rows = pl.ds(ci * CHUNK, CHUNK)
        o = jnp.dot(q_ref[rows, :], st_ref[ci], preferred_element_type=F32)
        ms = jnp.mean(o * o, axis=-1, keepdims=True) * (qs * qs)
        y = o * (qs * lax.rsqrt(ms + NORM_EPS)) * gw
        gate = jnp.concatenate([gg_ref[0, rows, :], gg_ref[1, rows, :]], axis=1).astype(F32)
        y = y * _silu(gate)
        o_ref[0, rows, :] = y[:, :LANES].astype(BF16)
        o_ref[1, rows, :] = y[:, LANES:].astype(BF16)


def _gla_call(p, g, w2_heads, b2_heads, gw):
    hp = GLA_HEADS_PER_STEP
    qk = lambda cb0: pl.BlockSpec((hp, SEQ, LANES), lambda b, h: (cb0 // hp + h, b, 0))
    p4 = p.reshape(N_COLBLK // 2, 2, TOKENS, LANES)
    vg = lambda cb0: pl.BlockSpec((hp, 2, SEQ, LANES), lambda b, h: (cb0 // 2 // hp + h, 0, b, 0))
    out = pl.pallas_call(
        _gla_kernel,
        grid=(BATCH, GLA_HEADS // hp),
        in_specs=[
            pl.BlockSpec((SEQ, LANES), lambda b, h: (b, 0)),
            pl.BlockSpec((hp, LANES, GLA_K_DIM), lambda b, h: (h, 0, 0)),
            pl.BlockSpec((hp, 1, GLA_K_DIM), lambda b, h: (h, 0, 0)),
            pl.BlockSpec((1, GLA_V_DIM), lambda b, h: (0, 0)),
            qk(CB_GQ), qk(CB_GK), vg(CB_GV), vg(CB_GG),
        ],
        out_specs=pl.BlockSpec((hp, 2, SEQ, LANES), lambda b, h: (h, 0, b, 0)),
        out_shape=jax.ShapeDtypeStruct((GLA_HEADS, 2, TOKENS, LANES), BF16),
        scratch_shapes=[
            pltpu.VMEM((SEQ, GLA_K_DIM), BF16),
            pltpu.VMEM((SEQ, GLA_K_DIM), F32),
            pltpu.VMEM((GLA_NCHUNK, GLA_K_DIM, GLA_V_DIM), F32),
            pltpu.VMEM((GLA_NCHUNK, GLA_K_DIM, GLA_V_DIM), BF16),
        ],
        compiler_params=pltpu.CompilerParams(
            dimension_semantics=("arbitrary", "arbitrary"), vmem_limit_bytes=VMEM_LIMIT),
        name="gla_scan",
    )(g, w2_heads, b2_heads, gw, p, p, p4, p4)
    return out.reshape(2 * GLA_HEADS, TOKENS, LANES)


OUT_TM = 1024
OUT_SPLITS = (256, 256, 256, 256)
assert sum(OUT_SPLITS) == OUT_TM
OUT_VMEM_LIMIT = (4 * OUT_TM * D_MODEL * 4 + 2 * OUT_TM * D_MODEL * 2 + D_MODEL * D_MODEL * 2
                  + 8 * 1024 * 1024)


def _out_kernel(a_ref, b_ref, wb_ref, x_ref, gate_ref, fw_ref, o_ref):
    gate = gate_ref[0]
    fw = fw_ref[...]
    row0 = 0
    for n_rows in OUT_SPLITS:
        rows = pl.ds(row0, n_rows)
        row0 += n_rows
        mix = jnp.concatenate(
            [a_ref[h, rows, :] for h in range(a_ref.shape[0])]
            + [b_ref[h, rows, :] for h in range(b_ref.shape[0])], axis=1)
        y = jnp.dot(mix, wb_ref[...], preferred_element_type=F32)
        h_res = x_ref[rows, :] + gate * y
        ms = jnp.mean(h_res * h_res, axis=-1, keepdims=True)
        o_ref[rows, :] = h_res * lax.rsqrt(ms + NORM_EPS) * fw


def _out_call(a, b, w_out, xf, gate, fw):
    per_seq = SEQ // OUT_TM
    return pl.pallas_call(
        _out_kernel,
        grid=(TOKENS // OUT_TM,),
        in_specs=[
            pl.BlockSpec((DIFF_HEADS, OUT_TM, LANES), lambda i: (0, i, 0)),
            pl.BlockSpec((2 * GLA_HEADS, OUT_TM, LANES), lambda i: (0, i, 0)),
            pl.BlockSpec((D_MODEL, D_MODEL), lambda i: (0, 0), pipeline_mode=pl.Buffered(1)),
            pl.BlockSpec((OUT_TM, D_MODEL), lambda i: (i, 0)),
            pl.BlockSpec((1, 1, D_MODEL), lambda i: (i // per_seq, 0, 0)),
            pl.BlockSpec((1, D_MODEL), lambda i: (0, 0)),
        ],
        out_specs=pl.BlockSpec((OUT_TM, D_MODEL), lambda i: (i, 0)),
        out_shape=jax.ShapeDtypeStruct((TOKENS, D_MODEL), F32),
        compiler_params=pltpu.CompilerParams(
            dimension_semantics=("arbitrary",), vmem_limit_bytes=OUT_VMEM_LIMIT),
        name="out_proj_norm",
    )(a, b, w_out, xf, gate, fw)


def _rope_tables():
    inv_freq = ROPE_THETA ** (-np.arange(0, DIFF_QK_DIM, 2, dtype=np.float64) / DIFF_QK_DIM)
    ang = np.arange(SEQ, dtype=np.float64)[:, None] * inv_freq[None, :]
    cos, sin = np.cos(ang), np.sin(ang)
    cos_t = np.tile(cos, (1, 4)).astype(np.float32)
    sin_t = np.concatenate([-sin, sin, -sin, sin], axis=1).astype(np.float32)
    return jnp.asarray(cos_t), jnp.asarray(sin_t)


def kernel(x, c, norm_w, w_ada, b_ada, w_in, lambda_q1, lambda_k1, lambda_q2, lambda_k2,
           diff_norm_w, gla_gate_w2, gla_gate_b, gla_norm_w, w_out, final_norm_w):
    assert x.shape == (BATCH, SEQ, D_MODEL) and w_in.shape[0] == 1
    xf = x.reshape(TOKENS, D_MODEL).astype(F32)

    c_pad = jnp.pad(c.astype(F32), ((0, 8 - BATCH), (0, 0)))
    b_ada_row = b_ada[0][None, :]
    mod = _ada_call(c_pad, w_ada[0], b_ada_row, 2 * D_MODEL)[:BATCH]
    shift = mod[:, :D_MODEL].reshape(BATCH, 1, D_MODEL)
    scale = mod[:, D_MODEL:].reshape(BATCH, 1, D_MODEL)

    w_t = w_in[0].T
    w_gate = jnp.pad(w_t[N_MAIN:], ((0, LANES - GLA_GATE_RANK), (0, 0))).astype(BF16)
    cos, sin = _rope_tables()
    p, g = _proj_call(xf, shift, scale, norm_w[0][None, :], w_t, w_gate, cos, sin)

    row = lambda v: v[0][None, :].astype(F32)
    a, gate8, w_out_bf16 = _attn_call(
        p, row(lambda_q1), row(lambda_k1), row(lambda_q2), row(lambda_k2), row(diff_norm_w),
        c_pad, w_ada[0], b_ada_row, w_out[0])
    gate = gate8[:BATCH].reshape(BATCH, 1, D_MODEL)

    w2_heads = jnp.pad(gla_gate_w2[0], ((0, LANES - GLA_GATE_RANK), (0, 0))).astype(BF16)
    w2_heads = w2_heads.reshape(LANES, GLA_HEADS, GLA_K_DIM).transpose(1, 0, 2)
    b2_heads = gla_gate_b[0].astype(F32).reshape(GLA_HEADS, 1, GLA_K_DIM)
    b = _gla_call(p, g, w2_heads, b2_heads, row(gla_norm_w))

    out = _out_call(a, b, w_out_bf16, xf, gate, final_norm_w[None, :].astype(F32))
    return out.reshape(BATCH, SEQ, D_MODEL).astype(x.dtype)
```

```python
import math

import jax
import jax.numpy as jnp
import numpy as np
from jax import lax
from jax.experimental import pallas as pl
from jax.experimental.pallas import tpu as pltpu

D_MODEL = 2048
BATCH = 4
SEQ = 2048
TOKENS = BATCH * SEQ
CHUNK = 64
LANES = 128

DIFF_HEADS = 8
DIFF_HEAD_DIM = 128
DIFF_QK_DIM = 64
GLA_HEADS = 4
GLA_K_DIM = 128
GLA_V_DIM = 256
GLA_GATE_RANK = 16
GLA_GATE_TAU = 16.0
ROPE_THETA = 10000.0
NORM_EPS = 1e-6
LAMBDA_INIT = 0.8 - 0.6 * math.exp(-0.3 * 0)

N_MAIN = 7168
N_COLBLK = N_MAIN // LANES
CB_DQ, CB_DK, CB_DV, CB_DG = 0, 8, 16, 24
CB_GQ, CB_GK, CB_GV, CB_GG = 32, 36, 40, 48

VMEM_LIMIT = 48 * 1024 * 1024
NEG_BIG = -1e30
LOG2_E = math.log2(math.e)

BF16 = jnp.bfloat16
F32 = jnp.float32


def _nt_dot(a, b):
    return lax.dot_general(a, b, (((1,), (1,)), ((), ())), preferred_element_type=F32)


def _silu(x):
    h = 0.5 * x
    return h + h * jnp.tanh(h)


def _tn_dot(a, b):
    return lax.dot_general(a, b, (((0,), (0,)), ((), ())), preferred_element_type=F32)


ADA_TN = 1024


def _ada_block(c_ref, w_ref, b_ref):
    c = c_ref[...]
    c_act = (c * jax.nn.sigmoid(c)).astype(BF16)
    return jnp.dot(c_act, w_ref[...].astype(BF16), preferred_element_type=F32) + b_ref[...]


def _ada_kernel(c_ref, w_ref, b_ref, o_ref):
    o_ref[...] = _ada_block(c_ref, w_ref, b_ref)


def _ada_call(c_pad, w_ada, b_ada, n):
    return pl.pallas_call(
        _ada_kernel,
        grid=(n // ADA_TN,),
        in_specs=[
            pl.BlockSpec((8, D_MODEL), lambda j: (0, 0)),
            pl.BlockSpec((D_MODEL, ADA_TN), lambda j: (0, j)),
            pl.BlockSpec((1, ADA_TN), lambda j: (0, j)),
        ],
        out_specs=pl.BlockSpec((8, ADA_TN), lambda j: (0, j)),
        out_shape=jax.ShapeDtypeStruct((8, n), F32),
        compiler_params=pltpu.CompilerParams(
            dimension_semantics=("arbitrary",), vmem_limit_bytes=VMEM_LIMIT),
        name="ada_mod",
    )(c_pad, w_ada, b_ada)


PROJ_TM = 1024
PROJ_TN = 1024
PROJ_SUB = 256
PROJ_ROPE_TILES = 2
PROJ_ROW_TILES = SEQ // PROJ_TM
PROJ_VMEM_LIMIT = (2 * PROJ_TM * D_MODEL * 4 + 2 * D_MODEL * PROJ_TN * 4
                   + (1 + PROJ_ROW_TILES) * PROJ_TM * D_MODEL * 2
                   + 2 * PROJ_TM * PROJ_TN * 2 + 10 * 1024 * 1024)


def _rot_half(x):
    lane = lax.broadcasted_iota(jnp.int32, x.shape, 1)
    first = (lane % DIFF_QK_DIM) < (DIFF_QK_DIM // 2)
    return jnp.where(first, pltpu.roll(x, LANES - 32, 1), pltpu.roll(x, 32, 1))


def _proj_kernel(x_ref, shift_ref, scale_ref, nw_ref, w_ref, wg_ref, cos_ref, sin_ref,
                 p_ref, g_ref, hn_ref, hn_all_ref):
    j = pl.program_id(1)
    r = pl.program_id(2)
    n_sub = PROJ_TM // PROJ_SUB
    n_cb = PROJ_TN // LANES
    pos0 = pl.multiple_of(r * PROJ_TM, PROJ_TM)

    def norm_rows(rows):
        mult = nw_ref[...] * (1.0 + scale_ref[0])
        xs = x_ref[rows, :]
        ms = jnp.mean(xs * xs, axis=-1, keepdims=True)
        hn_ref[rows, :] = (xs * lax.rsqrt(ms + NORM_EPS) * mult + shift_ref[0]).astype(BF16)

    def store_rope(sub, acc, sc):
        rows = pl.ds(sub * PROJ_SUB, PROJ_SUB)
        pos = pl.ds(pos0 + sub * PROJ_SUB, PROJ_SUB)
        cos = cos_ref[pos, :] * sc
        sin = sin_ref[pos, :] * sc
        for cb in range(n_cb):
            t = acc[:, cb * LANES:(cb + 1) * LANES]
            p_ref[cb, rows, :] = (t * cos + _rot_half(t) * sin).astype(BF16)

    @pl.when(j == 0)
    def _():
        for sub in range(n_sub):
            rows = pl.ds(sub * PROJ_SUB, PROJ_SUB)
            norm_rows(rows)
            acc = _nt_dot(hn_ref[rows, :], w_ref[...])
            store_rope(sub, acc, DIFF_QK_DIM ** -0.5 * LOG2_E)
        g_ref[...] = _nt_dot(hn_ref[...], wg_ref[...]).astype(BF16)
        hn_all_ref[r] = hn_ref[...]

    @pl.when(j == 1)
    def _():
        for sub in range(n_sub):
            rows = pl.ds(sub * PROJ_SUB, PROJ_SUB)
            acc = _nt_dot(hn_all_ref[r, rows, :], w_ref[...])
            store_rope(sub, acc, 1.0)

    @pl.when(j >= PROJ_ROPE_TILES)
    def _():
        for sub in range(n_sub):
            rows = pl.ds(sub * PROJ_SUB, PROJ_SUB)
            acc = _nt_dot(hn_all_ref[r, rows, :], w_ref[...])
            for cb in range(n_cb):
                p_ref[cb, rows, :] = acc[:, cb * LANES:(cb + 1) * LANES].astype(BF16)


def _proj_call(xf, shift, scale, norm_w, w_t, w_gate, cos, sin):
    n_tiles = N_MAIN // PROJ_TN
    rt = PROJ_ROW_TILES
    n_cb = PROJ_TN // LANES
    x_tile = lambda b, j, r: (jnp.where(j == 0, b * rt + r, b * rt + rt - 1), 0)
    return pl.pallas_call(
        _proj_kernel,
        grid=(BATCH, n_tiles, rt),
        in_specs=[
            pl.BlockSpec((PROJ_TM, D_MODEL), x_tile),
            pl.BlockSpec((1, 1, D_MODEL), lambda b, j, r: (b, 0, 0)),
            pl.BlockSpec((1, 1, D_MODEL), lambda b, j, r: (b, 0, 0)),
            pl.BlockSpec((1, D_MODEL), lambda b, j, r: (0, 0)),
            pl.BlockSpec((PROJ_TN, D_MODEL), lambda b, j, r: (j, 0)),
            pl.BlockSpec((LANES, D_MODEL), lambda b, j, r: (0, 0)),
            pl.BlockSpec((SEQ, LANES), lambda b, j, r: (0, 0)),
            pl.BlockSpec((SEQ, LANES), lambda b, j, r: (0, 0)),
        ],
        out_specs=[
            pl.BlockSpec((n_cb, PROJ_TM, LANES), lambda b, j, r: (j, b * rt + r, 0)),
            pl.BlockSpec((PROJ_TM, LANES), x_tile),
        ],
        out_shape=[
            jax.ShapeDtypeStruct((N_COLBLK, TOKENS, LANES), BF16),
            jax.ShapeDtypeStruct((TOKENS, LANES), BF16),
        ],
        scratch_shapes=[
            pltpu.VMEM((PROJ_TM, D_MODEL), BF16),
            pltpu.VMEM((rt, PROJ_TM, D_MODEL), BF16),
        ],
        compiler_params=pltpu.CompilerParams(
            dimension_semantics=("arbitrary", "arbitrary", "arbitrary"),
            vmem_limit_bytes=PROJ_VMEM_LIMIT),
        name="norm_in_proj",
    )(xf, shift, scale, norm_w, w_t, w_gate, cos, sin)


ATT_T = 256
ATT_NBLK = SEQ // ATT_T
ATT_LEAD = 9
assert ATT_LEAD >= ATT_NBLK - 1
ATT_ONES_ROWS = 16
ATT_HEADS_PER_STEP = 2
ATT_SLOTS = ATT_LEAD + 4


def _attn_kernel(lq1_ref, lk1_ref, lq2_ref, lk2_ref, dnw_ref, q_ref, k_ref, v_ref, dg_ref,
                 c_ref, wa_ref, ba_ref, wo_ref, o_ref, gate_ref, wob_ref, vt_ref, s_ref):
    gate_ref[...] = _ada_block(c_ref, wa_ref, ba_ref)
    wob_ref[...] = wo_ref[...].astype(BF16)

    lam = (jnp.exp(jnp.sum(lq1_ref[...] * lk1_ref[...], keepdims=True))
           - jnp.exp(jnp.sum(lq2_ref[...] * lk2_ref[...], keepdims=True))
           + LAMBDA_INIT)
    out_gain = dnw_ref[...] * (1.0 - LAMBDA_INIT)

    for hd in range(ATT_HEADS_PER_STEP):
        for jb in range(ATT_NBLK):
            cols = pl.ds(jb * ATT_T, ATT_T)
            vb = v_ref[hd, jb * ATT_T:(jb + 1) * ATT_T, :].astype(F32)
            vt_ref[hd, 0:DIFF_HEAD_DIM, cols] = vb.T.astype(BF16)
        vt_ref[hd, DIFF_HEAD_DIM:, :] = jnp.ones((ATT_ONES_ROWS, SEQ), BF16)

    lane = lax.broadcasted_iota(jnp.int32, (ATT_T, LANES), 1)
    comp_a = lane < DIFF_QK_DIM
    krow = lax.broadcasted_iota(jnp.int32, (ATT_T, ATT_T), 0) // CHUNK
    qcol = lax.broadcasted_iota(jnp.int32, (ATT_T, ATT_T), 1) // CHUNK
    diag_mask = krow <= qcol
    diag_mask2 = jnp.concatenate([diag_mask, diag_mask], axis=1)

    units = [(hd, qi, j) for hd in range(ATT_HEADS_PER_STEP)
             for qi in range(ATT_NBLK) for j in range(qi + 1)]
    blocks = {(hd, qi): {"m": None}
              for hd in range(ATT_HEADS_PER_STEP) for qi in range(ATT_NBLK)}

    def block_queries(hd, qi):
        q = q_ref[hd, pl.ds(qi * ATT_T, ATT_T), :]
        zero = jnp.zeros_like(q)
        return jnp.concatenate([jnp.where(comp_a, q, zero), jnp.where(comp_a, zero, q)], axis=0)

    dyn0 = lax.shift_right_logical(pl.program_id(0), 20)

    def score_unit(t, hd, qi, j):
        st = blocks[hd, qi]
        if j == 0:
            st["q2"] = block_queries(hd, qi)
        s = _nt_dot(k_ref[hd, pl.ds(j * ATT_T, ATT_T), :], st["q2"])
        if j == qi:
            s = jnp.where(diag_mask2, s, NEG_BIG)
        s_ref[dyn0 + t % ATT_SLOTS] = s
        mj = jnp.max(s, axis=0, keepdims=True)
        st["m"] = mj if st["m"] is None else jnp.maximum(st["m"], mj)

    def value_unit(t, hd, qi, j):
        st = blocks[hd, qi]
        p = jnp.exp2(s_ref[dyn0 + t % ATT_SLOTS] - st["m"])
        pvj = jnp.dot(vt_ref[hd, :, j * ATT_T:(j + 1) * ATT_T], p.astype(BF16),
                      preferred_element_type=F32)
        st["pv"] = pvj if j == 0 else st["pv"] + pvj
        if j == qi:
            finish(hd, qi, st["pv"])

    def finish(hd, qi, pv_l):
        rows = pl.ds(qi * ATT_T, ATT_T)
        l = pv_l[DIFF_HEAD_DIM:DIFF_HEAD_DIM + 1, :]
        pv = pv_l[0:DIFF_HEAD_DIM, :] * (1.0 / l)
        out_t = pv[:, :ATT_T] - lam * pv[:, ATT_T:]
        out = out_t.T
        ms = jnp.mean(out * out, axis=-1, keepdims=True)
        y = out * lax.rsqrt(ms + NORM_EPS) * out_gain
        g = dg_ref[hd, rows, :].astype(F32)
        o_ref[hd, rows, :] = (y * _silu(g)).astype(BF16)

    for t in range(len(units) + ATT_LEAD):
        if t < len(units):
            score_unit(t, *units[t])
        if t >= ATT_LEAD:
            value_unit(t - ATT_LEAD, *units[t - ATT_LEAD])


def _attn_call(p, lq1, lk1, lq2, lk2, dnw, c_pad, w_ada, b_ada, w_out):
    hp = ATT_HEADS_PER_STEP
    steps_per_batch = DIFF_HEADS // hp
    side_w = D_MODEL // (BATCH * steps_per_batch)
    assert side_w % LANES == 0
    wo_rows = pl.BlockSpec((side_w, D_MODEL), lambda b, h: (b * steps_per_batch + h, 0))
    gate_cb0 = 2 * D_MODEL // side_w
    vec = lambda n: pl.BlockSpec((1, n), lambda b, h: (0, 0))
    slab = lambda cb0: pl.BlockSpec((hp, SEQ, LANES), lambda b, h: (cb0 // hp + h, b, 0))
    gate_blk = lambda rows, cb0: pl.BlockSpec(
        (rows, side_w), lambda b, h: (0, cb0 + b * steps_per_batch + h))
    return pl.pallas_call(
        _attn_kernel,
        grid=(BATCH, steps_per_batch),
        in_specs=[vec(DIFF_QK_DIM)] * 4 + [vec(DIFF_HEAD_DIM),
                  slab(CB_DQ), slab(CB_DK), slab(CB_DV), slab(CB_DG),
                  pl.BlockSpec((8, D_MODEL), lambda b, h: (0, 0)),
                  gate_blk(D_MODEL, gate_cb0), gate_blk(1, gate_cb0), wo_rows],
        out_specs=[pl.BlockSpec((hp, SEQ, LANES), lambda b, h: (h, b, 0)), gate_blk(8, 0),
                   wo_rows],
        out_shape=[jax.ShapeDtypeStruct((DIFF_HEADS, TOKENS, LANES), BF16),
                   jax.ShapeDtypeStruct((8, D_MODEL), F32),
                   jax.ShapeDtypeStruct((D_MODEL, D_MODEL), BF16)],
        scratch_shapes=[
            pltpu.VMEM((hp, DIFF_HEAD_DIM + ATT_ONES_ROWS, SEQ), BF16),
            pltpu.VMEM((ATT_SLOTS, ATT_T, 2 * ATT_T), F32),
        ],
        compiler_params=pltpu.CompilerParams(
            dimension_semantics=("arbitrary", "arbitrary"), vmem_limit_bytes=VMEM_LIMIT),
        name="diff_attn",
    )(lq1, lk1, lq2, lk2, dnw, p, p, p, p, c_pad, w_ada, b_ada, w_out)


GLA_BLK = 256
GLA_NCHUNK = SEQ // CHUNK
GLA_HEADS_PER_STEP = 2


def _split_hi_lo(x):
    hi = x.astype(BF16)
    lo = (x - hi.astype(F32)).astype(BF16)
    return hi, lo


def _gla_kernel(g_ref, w2_ref, b2_ref, gw_ref, q_ref, k_ref, v_ref, gg_ref, o_ref,
                kdec_ref, tot_ref, kv_ref, st_ref):
    for hd in range(GLA_HEADS_PER_STEP):
        _gla_head(g_ref, w2_ref.at[hd], b2_ref.at[hd], gw_ref, q_ref.at[hd], k_ref.at[hd],
                  v_ref.at[hd], gg_ref.at[hd], o_ref.at[hd], kdec_ref, tot_ref, kv_ref, st_ref)


def _gla_head(g_ref, w2_ref, b2_ref, gw_ref, q_ref, k_ref, v_ref, gg_ref, o_ref,
              kdec_ref, tot_ref, kv_ref, st_ref):
    z = jnp.dot(g_ref[...], w2_ref[...], preferred_element_type=F32) + b2_ref[...]
    log_a = (jnp.minimum(z, 0.0) - jnp.log(1.0 + jnp.exp(-jnp.abs(z)))) * (1.0 / GLA_GATE_TAU)

    r = lax.broadcasted_iota(jnp.int32, (GLA_BLK, GLA_BLK), 0)
    c = lax.broadcasted_iota(jnp.int32, (GLA_BLK, GLA_BLK), 1)
    after = jnp.where(((r // CHUNK) == (c // CHUNK)) & (c > r), 1.0, 0.0).astype(BF16)
    for blk in range(SEQ // GLA_BLK):
        rows = pl.ds(blk * GLA_BLK, GLA_BLK)
        la = log_a[blk * GLA_BLK:(blk + 1) * GLA_BLK, :]
        hi, lo = _split_hi_lo(la)
        both = jnp.dot(after, jnp.concatenate([hi, lo], axis=1), preferred_element_type=F32)
        suffix = both[:, :GLA_K_DIM] + both[:, GLA_K_DIM:]
        kdec_ref[rows, :] = (k_ref[rows, :].astype(F32) * jnp.exp(suffix)).astype(BF16)
        tot_ref[rows, :] = suffix + la

    dec_rows = jnp.exp(tot_ref[pl.ds(0, GLA_NCHUNK, stride=CHUNK), :])
    dec_cols = jnp.concatenate(
        [dec_rows, jnp.zeros((GLA_K_DIM - GLA_NCHUNK, GLA_K_DIM), F32)], axis=0).T

    dyn0 = lax.shift_right_logical(pl.program_id(0), 20)
    for ci in range(GLA_NCHUNK):
        rows = pl.ds(ci * CHUNK, CHUNK)
        v_c = jnp.concatenate([v_ref[0, rows, :], v_ref[1, rows, :]], axis=1)
        kv_ref[dyn0 + ci] = _tn_dot(kdec_ref[rows, :], v_c)

    state = jnp.zeros((GLA_K_DIM, GLA_V_DIM), F32)
    for ci in range(GLA_NCHUNK):
        state = state * dec_cols[:, ci:ci + 1] + kv_ref[dyn0 + ci]
        st_ref[ci] = state.astype(BF16)

    gw = gw_ref[...]
    qs = GLA_K_DIM ** -0.5
    for ci in range(GLA_NCHUNK):
        rows = pl.ds(ci * CHUNK, CHUNK)
        o = jnp.dot(q_ref[rows, :], st_ref[ci], preferred_element_type=F32)
        ms = jnp.mean(o * o, axis=-1, keepdims=True) * (qs * qs)
        y = o * (qs * lax.rsqrt(ms + NORM_EPS)) * gw
        gate = jnp.concatenate([gg_ref[0, rows, :], gg_ref[1, rows, :]], axis=1).astype(F32)
        y = y * _silu(gate)
        o_ref[0, rows, :] = y[:, :LANES].astype(BF16)
        o_ref[1, rows, :] = y[:, LANES:].astype(BF16)


def _gla_call(p, g, w2_heads, b2_heads, gw):
    hp = GLA_HEADS_PER_STEP
    qk = lambda cb0: pl.BlockSpec((hp, SEQ, LANES), lambda b, h: (cb0 // hp + h, b, 0))
    p4 = p.reshape(N_COLBLK // 2, 2, TOKENS, LANES)
    vg = lambda cb0: pl.BlockSpec((hp, 2, SEQ, LANES), lambda b, h: (cb0 // 2 // hp + h, 0, b, 0))
    out = pl.pallas_call(
        _gla_kernel,
        grid=(BATCH, GLA_HEADS // hp),
        in_specs=[
            pl.BlockSpec((SEQ, LANES), lambda b, h: (b, 0)),
            pl.BlockSpec((hp, LANES, GLA_K_DIM), lambda b, h: (h, 0, 0)),
            pl.BlockSpec((hp, 1, GLA_K_DIM), lambda b, h: (h, 0, 0)),
            pl.BlockSpec((1, GLA_V_DIM), lambda b, h: (0, 0)),
            qk(CB_GQ), qk(CB_GK), vg(CB_GV), vg(CB_GG),
        ],
        out_specs=pl.BlockSpec((hp, 2, SEQ, LANES), lambda b, h: (h, 0, b, 0)),
        out_shape=jax.ShapeDtypeStruct((GLA_HEADS, 2, TOKENS, LANES), BF16),
        scratch_shapes=[
            pltpu.VMEM((SEQ, GLA_K_DIM), BF16),
            pltpu.VMEM((SEQ, GLA_K_DIM), F32),
            pltpu.VMEM((GLA_NCHUNK, GLA_K_DIM, GLA_V_DIM), F32),
            pltpu.VMEM((GLA_NCHUNK, GLA_K_DIM, GLA_V_DIM), BF16),
        ],
        compiler_params=pltpu.CompilerParams(
            dimension_semantics=("arbitrary", "arbitrary"), vmem_limit_bytes=VMEM_LIMIT),
        name="gla_scan",
    )(g, w2_heads, b2_heads, gw, p, p, p4, p4)
    return out.reshape(2 * GLA_HEADS, TOKENS, LANES)


OUT_TM = 1024
OUT_SPLITS = (256, 256, 256, 256)
assert sum(OUT_SPLITS) == OUT_TM
OUT_VMEM_LIMIT = (4 * OUT_TM * D_MODEL * 4 + 2 * OUT_TM * D_MODEL * 2 + D_MODEL * D_MODEL * 2
                  + 8 * 1024 * 1024)


def _out_kernel(a_ref, b_ref, wb_ref, x_ref, gate_ref, fw_ref, o_ref):
    gate = gate_ref[0]
    fw = fw_ref[...]
    row0 = 0
    for n_rows in OUT_SPLITS:
        rows = pl.ds(row0, n_rows)
        row0 += n_rows
        mix = jnp.concatenate(
            [a_ref[h, rows, :] for h in range(a_ref.shape[0])]
            + [b_ref[h, rows, :] for h in range(b_ref.shape[0])], axis=1)
        y = jnp.dot(mix, wb_ref[...], preferred_element_type=F32)
        h_res = x_ref[rows, :] + gate * y
        ms = jnp.mean(h_res * h_res, axis=-1, keepdims=True)
        o_ref[rows, :] = h_res * lax.rsqrt(ms + NORM_EPS) * fw


def _out_call(a, b, w_out, xf, gate, fw):
    per_seq = SEQ // OUT_TM
    return pl.pallas_call(
        _out_kernel,
        grid=(TOKENS // OUT_TM,),
        in_specs=[
            pl.BlockSpec((DIFF_HEADS, OUT_TM, LANES), lambda i: (0, i, 0)),
            pl.BlockSpec((2 * GLA_HEADS, OUT_TM, LANES), lambda i: (0, i, 0)),
            pl.BlockSpec((D_MODEL, D_MODEL), lambda i: (0, 0), pipeline_mode=pl.Buffered(1)),
            pl.BlockSpec((OUT_TM, D_MODEL), lambda i: (i, 0)),
            pl.BlockSpec((1, 1, D_MODEL), lambda i: (i // per_seq, 0, 0)),
            pl.BlockSpec((1, D_MODEL), lambda i: (0, 0)),
        ],
        out_specs=pl.BlockSpec((OUT_TM, D_MODEL), lambda i: (i, 0)),
        out_shape=jax.ShapeDtypeStruct((TOKENS, D_MODEL), F32),
        compiler_params=pltpu.CompilerParams(
            dimension_semantics=("arbitrary",), vmem_limit_bytes=OUT_VMEM_LIMIT),
        name="out_proj_norm",
    )(a, b, w_out, xf, gate, fw)


def _rope_tables():
    inv_freq = ROPE_THETA ** (-np.arange(0, DIFF_QK_DIM, 2, dtype=np.float64) / DIFF_QK_DIM)
    ang = np.arange(SEQ, dtype=np.float64)[:, None] * inv_freq[None, :]
    cos, sin = np.cos(ang), np.sin(ang)
    cos_t = np.tile(cos, (1, 4)).astype(np.float32)
    sin_t = np.concatenate([-sin, sin, -sin, sin], axis=1).astype(np.float32)
    return jnp.asarray(cos_t), jnp.asarray(sin_t)


def kernel(x, c, norm_w, w_ada, b_ada, w_in, lambda_q1, lambda_k1, lambda_q2, lambda_k2,
           diff_norm_w, gla_gate_w2, gla_gate_b, gla_norm_w, w_out, final_norm_w):
    assert x.shape == (BATCH, SEQ, D_MODEL) and w_in.shape[0] == 1
    xf = x.reshape(TOKENS, D_MODEL).astype(F32)

    c_pad = jnp.pad(c.astype(F32), ((0, 8 - BATCH), (0, 0)))
    b_ada_row = b_ada[0][None, :]
    mod = _ada_call(c_pad, w_ada[0], b_ada_row, 2 * D_MODEL)[:BATCH]
    shift = mod[:, :D_MODEL].reshape(BATCH, 1, D_MODEL)
    scale = mod[:, D_MODEL:].reshape(BATCH, 1, D_MODEL)

    w_t = w_in[0].T
    w_gate = jnp.pad(w_t[N_MAIN:], ((0, LANES - GLA_GATE_RANK), (0, 0))).astype(BF16)
    cos, sin = _rope_tables()
    p, g = _proj_call(xf, shift, scale, norm_w[0][None, :], w_t, w_gate, cos, sin)

    row = lambda v: v[0][None, :].astype(F32)
    a, gate8, w_out_bf16 = _attn_call(
        p, row(lambda_q1), row(lambda_k1), row(lambda_q2), row(lambda_k2), row(diff_norm_w),
        c_pad, w_ada[0], b_ada_row, w_out[0])
    gate = gate8[:BATCH].reshape(BATCH, 1, D_MODEL)

    w2_heads = jnp.pad(gla_gate_w2[0], ((0, LANES - GLA_GATE_RANK), (0, 0))).astype(BF16)
    w2_heads = w2_heads.reshape(LANES, GLA_HEADS, GLA_K_DIM).transpose(1, 0, 2)
    b2_heads = gla_gate_b[0].astype(F32).reshape(GLA_HEADS, 1, GLA_K_DIM)
    b = _gla_call(p, g, w2_heads, b2_heads, row(gla_norm_w))

    out = _out_call(a, b, w_out_bf16, xf, gate, final_norm_w[None, :].astype(F32))
    return out.reshape(BATCH, SEQ, D_MODEL).astype(x.dtype)
```

```python
import math

import jax
import jax.numpy as jnp
import numpy as np
from jax import lax
from jax.experimental import pallas as pl
from jax.experimental.pallas import tpu as pltpu

D_MODEL = 2048
BATCH = 4
SEQ = 2048
TOKENS = BATCH * SEQ
CHUNK = 64
LANES = 128

DIFF_HEADS = 8
DIFF_HEAD_DIM = 128
DIFF_QK_DIM = 64
GLA_HEADS = 4
GLA_K_DIM = 128
GLA_V_DIM = 256
GLA_GATE_RANK = 16
GLA_GATE_TAU = 16.0
ROPE_THETA = 10000.0
NORM_EPS = 1e-6
LAMBDA_INIT = 0.8 - 0.6 * math.exp(-0.3 * 0)

N_MAIN = 7168
N_COLBLK = N_MAIN // LANES
CB_DQ, CB_DK, CB_DV, CB_DG = 0, 8, 16, 24
CB_GQ, CB_GK, CB_GV, CB_GG = 32, 36, 40, 48

VMEM_LIMIT = 48 * 1024 * 1024
NEG_BIG = -1e30
LOG2_E = math.log2(math.e)

BF16 = jnp.bfloat16
F32 = jnp.float32


def _nt_dot(a, b):
    return lax.dot_general(a, b, (((1,), (1,)), ((), ())), preferred_element_type=F32)


def _silu(x):
    h = 0.5 * x
    return h + h * jnp.tanh(h)


def _tn_dot(a, b):
    return lax.dot_general(a, b, (((0,), (0,)), ((), ())), preferred_element_type=F32)


ADA_TN = 1024


def _ada_block(c_ref, w_ref, b_ref):
    c = c_ref[...]
    c_act = (c * jax.nn.sigmoid(c)).astype(BF16)
    return jnp.dot(c_act, w_ref[...].astype(BF16), preferred_element_type=F32) + b_ref[...]


def _ada_kernel(c_ref, w_ref, b_ref, o_ref):
    o_ref[...] = _ada_block(c_ref, w_ref, b_ref)


def _ada_call(c_pad, w_ada, b_ada, n):
    return pl.pallas_call(
        _ada_kernel,
        grid=(n // ADA_TN,),
        in_specs=[
            pl.BlockSpec((8, D_MODEL), lambda j: (0, 0)),
            pl.BlockSpec((D_MODEL, ADA_TN), lambda j: (0, j)),
            pl.BlockSpec((1, ADA_TN), lambda j: (0, j)),
        ],
        out_specs=pl.BlockSpec((8, ADA_TN), lambda j: (0, j)),
        out_shape=jax.ShapeDtypeStruct((8, n), F32),
        compiler_params=pltpu.CompilerParams(
            dimension_semantics=("arbitrary",), vmem_limit_bytes=VMEM_LIMIT),
        name="ada_mod",
    )(c_pad, w_ada, b_ada)


PROJ_TM = 1024
PROJ_TN = 1024
PROJ_SUB = 256
PROJ_ROPE_TILES = 2
PROJ_SEQ_TILES = SEQ // PROJ_TM
PROJ_ROW_TILES = 2 * PROJ_SEQ_TILES
PROJ_VMEM_LIMIT = (2 * PROJ_TM * D_MODEL * 4 + 2 * D_MODEL * PROJ_TN * 4
                   + (1 + PROJ_ROW_TILES) * PROJ_TM * D_MODEL * 2
                   + 2 * PROJ_TM * PROJ_TN * 2 + 4 * 1024 * 1024)


def _rot_half(x):
    lane = lax.broadcasted_iota(jnp.int32, x.shape, 1)
    first = (lane % DIFF_QK_DIM) < (DIFF_QK_DIM // 2)
    return jnp.where(first, pltpu.roll(x, LANES - 32, 1), pltpu.roll(x, 32, 1))


def _proj_kernel(x_ref, shift_ref, scale_ref, nw_ref, w_ref, wg_ref, cos_ref, sin_ref,
                 p_ref, g_ref, hn_ref, hn_all_ref):
    j = pl.program_id(1)
    r = pl.program_id(2)
    n_sub = PROJ_TM // PROJ_SUB
    n_cb = PROJ_TN // LANES
    pos0 = pl.multiple_of((r % PROJ_SEQ_TILES) * PROJ_TM, PROJ_TM)

    def norm_rows(rows):
        mult = nw_ref[...] * (1.0 + scale_ref[0])
        xs = x_ref[rows, :]
        ms = jnp.mean(xs * xs, axis=-1, keepdims=True)
        hn_ref[rows, :] = (xs * lax.rsqrt(ms + NORM_EPS) * mult + shift_ref[0]).astype(BF16)

    def store_rope(sub, acc, sc):
        rows = pl.ds(sub * PROJ_SUB, PROJ_SUB)
        pos = pl.ds(pos0 + sub * PROJ_SUB, PROJ_SUB)
        cos = cos_ref[pos, :] * sc
        sin = sin_ref[pos, :] * sc
        for cb in range(n_cb):
            t = acc[:, cb * LANES:(cb + 1) * LANES]
            p_ref[cb, rows, :] = (t * cos + _rot_half(t) * sin).astype(BF16)

    @pl.when(j == 0)
    def _():
        for sub in range(n_sub):
            rows = pl.ds(sub * PROJ_SUB, PROJ_SUB)
            norm_rows(rows)
            acc = _nt_dot(hn_ref[rows, :], w_ref[...])
            store_rope(sub, acc, DIFF_QK_DIM ** -0.5 * LOG2_E)
        g_ref[...] = _nt_dot(hn_ref[...], wg_ref[...]).astype(BF16)
        hn_all_ref[r] = hn_ref[...]

    @pl.when(j == 1)
    def _():
        for sub in range(n_sub):
            rows = pl.ds(sub * PROJ_SUB, PROJ_SUB)
            acc = _nt_dot(hn_all_ref[r, rows, :], w_ref[...])
            store_rope(sub, acc, 1.0)

    @pl.when(j >= PROJ_ROPE_TILES)
    def _():
        for sub in range(n_sub):
            rows = pl.ds(sub * PROJ_SUB, PROJ_SUB)
            acc = _nt_dot(hn_all_ref[r, rows, :], w_ref[...])
            for cb in range(n_cb):
                p_ref[cb, rows, :] = acc[:, cb * LANES:(cb + 1) * LANES].astype(BF16)


def _proj_call(xf, shift, scale, norm_w, w_t, w_gate, cos, sin):
    n_tiles = N_MAIN // PROJ_TN
    rt = PROJ_ROW_TILES
    n_cb = PROJ_TN // LANES
    x_tile = lambda b, j, r: (jnp.where(j == 0, b * rt + r, b * rt + rt - 1), 0)
    return pl.pallas_call(
        _proj_kernel,
        grid=(TOKENS // (rt * PROJ_TM), n_tiles, rt),
        in_specs=[
            pl.BlockSpec((PROJ_TM, D_MODEL), x_tile),
            pl.BlockSpec((1, 1, D_MODEL), lambda b, j, r: ((b * rt + r) // PROJ_SEQ_TILES, 0, 0)),
            pl.BlockSpec((1, 1, D_MODEL), lambda b, j, r: ((b * rt + r) // PROJ_SEQ_TILES, 0, 0)),
            pl.BlockSpec((1, D_MODEL), lambda b, j, r: (0, 0)),
            pl.BlockSpec((PROJ_TN, D_MODEL), lambda b, j, r: (j, 0)),
            pl.BlockSpec((LANES, D_MODEL), lambda b, j, r: (0, 0)),
            pl.BlockSpec((SEQ, LANES), lambda b, j, r: (0, 0)),
            pl.BlockSpec((SEQ, LANES), lambda b, j, r: (0, 0)),
        ],
        out_specs=[
            pl.BlockSpec((n_cb, PROJ_TM, LANES), lambda b, j, r: (j, b * rt + r, 0)),
            pl.BlockSpec((PROJ_TM, LANES), x_tile),
        ],
        out_shape=[
            jax.ShapeDtypeStruct((N_COLBLK, TOKENS, LANES), BF16),
            jax.ShapeDtypeStruct((TOKENS, LANES), BF16),
        ],
        scratch_shapes=[
            pltpu.VMEM((PROJ_TM, D_MODEL), BF16),
            pltpu.VMEM((rt, PROJ_TM, D_MODEL), BF16),
        ],
        compiler_params=pltpu.CompilerParams(
            dimension_semantics=("arbitrary", "arbitrary", "arbitrary"),
            vmem_limit_bytes=PROJ_VMEM_LIMIT),
        name="norm_in_proj",
    )(xf, shift, scale, norm_w, w_t, w_gate, cos, sin)


ATT_T = 256
ATT_NBLK = SEQ // ATT_T
ATT_LEAD = 9
assert ATT_LEAD >= ATT_NBLK - 1
ATT_ONES_ROWS = 16
ATT_HEADS_PER_STEP = 2
ATT_SLOTS = ATT_LEAD + 4


def _attn_kernel(lq1_ref, lk1_ref, lq2_ref, lk2_ref, dnw_ref, q_ref, k_ref, v_ref, dg_ref,
                 c_ref, wa_ref, ba_ref, wo_ref, o_ref, gate_ref, wob_ref, vt_ref, s_ref):
    gate_ref[...] = _ada_block(c_ref, wa_ref, ba_ref)
    wob_ref[...] = wo_ref[...].astype(BF16)

    lam = (jnp.exp(jnp.sum(lq1_ref[...] * lk1_ref[...], keepdims=True))
           - jnp.exp(jnp.sum(lq2_ref[...] * lk2_ref[...], keepdims=True))
           + LAMBDA_INIT)
    out_gain = dnw_ref[...] * (1.0 - LAMBDA_INIT)

    for hd in range(ATT_HEADS_PER_STEP):
        for jb in range(ATT_NBLK):
            cols = pl.ds(jb * ATT_T, ATT_T)
            vb = v_ref[hd, jb * ATT_T:(jb + 1) * ATT_T, :].astype(F32)
            vt_ref[hd, 0:DIFF_HEAD_DIM, cols] = vb.T.astype(BF16)
        vt_ref[hd, DIFF_HEAD_DIM:, :] = jnp.ones((ATT_ONES_ROWS, SEQ), BF16)

    lane = lax.broadcasted_iota(jnp.int32, (ATT_T, LANES), 1)
    comp_a = lane < DIFF_QK_DIM
    krow = lax.broadcasted_iota(jnp.int32, (ATT_T, ATT_T), 0) // CHUNK
    qcol = lax.broadcasted_iota(jnp.int32, (ATT_T, ATT_T), 1) // CHUNK
    diag_mask = krow <= qcol
    diag_mask2 = jnp.concatenate([diag_mask, diag_mask], axis=1)

    units = [(hd, qi, j) for hd in range(ATT_HEADS_PER_STEP)
             for qi in range(ATT_NBLK) for j in range(qi + 1)]
    blocks = {(hd, qi): {"m": None}
              for hd in range(ATT_HEADS_PER_STEP) for qi in range(ATT_NBLK)}

    def block_queries(hd, qi):
        q = q_ref[hd, pl.ds(qi * ATT_T, ATT_T), :]
        zero = jnp.zeros_like(q)
        return jnp.concatenate([jnp.where(comp_a, q, zero), jnp.where(comp_a, zero, q)], axis=0)

    dyn0 = lax.shift_right_logical(pl.program_id(0), 20)

    def score_unit(t, hd, qi, j):
        st = blocks[hd, qi]
        if j == 0:
            st["q2"] = block_queries(hd, qi)
        s = _nt_dot(k_ref[hd, pl.ds(j * ATT_T, ATT_T), :], st["q2"])
        if j == qi:
            s = jnp.where(diag_mask2, s, NEG_BIG)
        s_ref[dyn0 + t % ATT_SLOTS] = s
        mj = jnp.max(s, axis=0, keepdims=True)
        st["m"] = mj if st["m"] is None else jnp.maximum(st["m"], mj)

    def value_unit(t, hd, qi, j):
        st = blocks[hd, qi]
        p = jnp.exp2(s_ref[dyn0 + t % ATT_SLOTS] - st["m"])
        pvj = jnp.dot(vt_ref[hd, :, j * ATT_T:(j + 1) * ATT_T], p.astype(BF16),
                      preferred_element_type=F32)
        st["pv"] = pvj if j == 0 else st["pv"] + pvj
        if j == qi:
            finish(hd, qi, st["pv"])

    def finish(hd, qi, pv_l):
        rows = pl.ds(qi * ATT_T, ATT_T)
        l = pv_l[DIFF_HEAD_DIM:DIFF_HEAD_DIM + 1, :]
        pv = pv_l[0:DIFF_HEAD_DIM, :] * (1.0 / l)
        out_t = pv[:, :ATT_T] - lam * pv[:, ATT_T:]
        out = out_t.T
        ms = jnp.mean(out * out, axis=-1, keepdims=True)
        y = out * lax.rsqrt(ms + NORM_EPS) * out_gain
        g = dg_ref[hd, rows, :].astype(F32)
        o_ref[hd, rows, :] = (y * _silu(g)).astype(BF16)

    for t in range(len(units) + ATT_LEAD):
        if t < len(units):
            score_unit(t, *units[t])
        if t >= ATT_LEAD:
            value_unit(t - ATT_LEAD, *units[t - ATT_LEAD])


def _attn_call(p, lq1, lk1, lq2, lk2, dnw, c_pad, w_ada, b_ada, w_out):
    hp = ATT_HEADS_PER_STEP
    steps_per_batch = DIFF_HEADS // hp
    side_w = D_MODEL // (BATCH * steps_per_batch)
    assert side_w % LANES == 0
    wo_rows = pl.BlockSpec((side_w, D_MODEL), lambda b, h: (b * steps_per_batch + h, 0))
    gate_cb0 = 2 * D_MODEL // side_w
    vec = lambda n: pl.BlockSpec((1, n), lambda b, h: (0, 0))
    slab = lambda cb0: pl.BlockSpec((hp, SEQ, LANES), lambda b, h: (cb0 // hp + h, b, 0))
    gate_blk = lambda rows, cb0: pl.BlockSpec(
        (rows, side_w), lambda b, h: (0, cb0 + b * steps_per_batch + h))
    return pl.pallas_call(
        _attn_kernel,
        grid=(BATCH, steps_per_batch),
        in_specs=[vec(DIFF_QK_DIM)] * 4 + [vec(DIFF_HEAD_DIM),
                  slab(CB_DQ), slab(CB_DK), slab(CB_DV), slab(CB_DG),
                  pl.BlockSpec((8, D_MODEL), lambda b, h: (0, 0)),
                  gate_blk(D_MODEL, gate_cb0), gate_blk(1, gate_cb0), wo_rows],
        out_specs=[pl.BlockSpec((hp, SEQ, LANES), lambda b, h: (h, b, 0)), gate_blk(8, 0),
                   wo_rows],
        out_shape=[jax.ShapeDtypeStruct((DIFF_HEADS, TOKENS, LANES), BF16),
                   jax.ShapeDtypeStruct((8, D_MODEL), F32),
                   jax.ShapeDtypeStruct((D_MODEL, D_MODEL), BF16)],
        scratch_shapes=[
            pltpu.VMEM((hp, DIFF_HEAD_DIM + ATT_ONES_ROWS, SEQ), BF16),
            pltpu.VMEM((ATT_SLOTS, ATT_T, 2 * ATT_T), F32),
        ],
        compiler_params=pltpu.CompilerParams(
            dimension_semantics=("arbitrary", "arbitrary"), vmem_limit_bytes=VMEM_LIMIT),
        name="diff_attn",
    )(lq1, lk1, lq2, lk2, dnw, p, p, p, p, c_pad, w_ada, b_ada, w_out)


GLA_BLK = 256
GLA_NCHUNK = SEQ // CHUNK
GLA_HEADS_PER_STEP = 2


def _split_hi_lo(x):
    hi = x.astype(BF16)
    lo = (x - hi.astype(F32)).astype(BF16)
    return hi, lo


def _gla_kernel(g_ref, w2_ref, b2_ref, gw_ref, q_ref, k_ref, v_ref, gg_ref, o_ref,
                kdec_ref, tot_ref, kv_ref, st_ref):
    for hd in range(GLA_HEADS_PER_STEP):
        _gla_head(g_ref, w2_ref.at[hd], b2_ref.at[hd], gw_ref, q_ref.at[hd], k_ref.at[hd],
                  v_ref.at[hd], gg_ref.at[hd], o_ref.at[hd], kdec_ref, tot_ref, kv_ref, st_ref)


def _gla_head(g_ref, w2_ref, b2_ref, gw_ref, q_ref, k_ref, v_ref, gg_ref, o_ref,
              kdec_ref, tot_ref, kv_ref, st_ref):
    z = jnp.dot(g_ref[...], w2_ref[...], preferred_element_type=F32) + b2_ref[...]
    log_a = (jnp.minimum(z, 0.0) - jnp.log(1.0 + jnp.exp(-jnp.abs(z)))) * (1.0 / GLA_GATE_TAU)

    r = lax.broadcasted_iota(jnp.int32, (GLA_BLK, GLA_BLK), 0)
    c = lax.broadcasted_iota(jnp.int32, (GLA_BLK, GLA_BLK), 1)
    after = jnp.where(((r // CHUNK) == (c // CHUNK)) & (c > r), 1.0, 0.0).astype(BF16)
    for blk in range(SEQ // GLA_BLK):
        rows = pl.ds(blk * GLA_BLK, GLA_BLK)
        la = log_a[blk * GLA_BLK:(blk + 1) * GLA_BLK, :]
        hi, lo = _split_hi_lo(la)
        both = jnp.dot(after, jnp.concatenate([hi, lo], axis=1), preferred_element_type=F32)
        suffix = both[:, :GLA_K_DIM] + both[:, GLA_K_DIM:]
        kdec_ref[rows, :] = (k_ref[rows, :].astype(F32) * jnp.exp(suffix)).astype(BF16)
        tot_ref[rows, :] = suffix + la

    dec_rows = jnp.exp(tot_ref[pl.ds(0, GLA_NCHUNK, stride=CHUNK), :])
    dec_cols = jnp.concatenate(
        [dec_rows, jnp.zeros((GLA_K_DIM - GLA_NCHUNK, GLA_K_DIM), F32)], axis=0).T

    dyn0 = lax.shift_right_logical(pl.program_id(0), 20)
    for ci in range(GLA_NCHUNK):
        rows = pl.ds(ci * CHUNK, CHUNK)
        v_c = jnp.concatenate([v_ref[0, rows, :], v_ref[1, rows, :]], axis=1)
        kv_ref[dyn0 + ci] = _tn_dot(kdec_ref[rows, :], v_c)

    state = jnp.zeros((GLA_K_DIM, GLA_V_DIM), F32)
    for ci in range(GLA_NCHUNK):
        state = state * dec_cols[:, ci:ci + 1] + kv_ref[dyn0 + ci]
        st_ref[ci] = state.astype(BF16)

    gw = gw_ref[...]
    qs = GLA_K_DIM ** -0.5
    for ci in range(GLA_NCHUNK):
        rows = pl.ds(ci * CHUNK, CHUNK)
        o = jnp.dot(q_ref[rows, :], st_ref[ci], preferred_element_type=F32)
        ms = jnp.mean(o * o, axis=-1, keepdims=True) * (qs * qs)
        y = o * (qs * lax.rsqrt(ms + NORM_EPS)) * gw
        gate = jnp.concatenate([gg_ref[0, rows, :], gg_ref[1, rows, :]], axis=1).astype(F32)
        y = y * _silu(gate)
        o_ref[0, rows, :] = y[:, :LANES].astype(BF16)
        o_ref[1, rows, :] = y[:, LANES:].astype(BF16)


def _gla_call(p, g, w2_heads, b2_heads, gw):
    hp = GLA_HEADS_PER_STEP
    qk = lambda cb0: pl.BlockSpec((hp, SEQ, LANES), lambda b, h: (cb0 // hp + h, b, 0))
    p4 = p.reshape(N_COLBLK // 2, 2, TOKENS, LANES)
    vg = lambda cb0: pl.BlockSpec((hp, 2, SEQ, LANES), lambda b, h: (cb0 // 2 // hp + h, 0, b, 0))
    out = pl.pallas_call(
        _gla_kernel,
        grid=(BATCH, GLA_HEADS // hp),
        in_specs=[
            pl.BlockSpec((SEQ, LANES), lambda b, h: (b, 0)),
            pl.BlockSpec((hp, LANES, GLA_K_DIM), lambda b, h: (h, 0, 0)),
            pl.BlockSpec((hp, 1, GLA_K_DIM), lambda b, h: (h, 0, 0)),
            pl.BlockSpec((1, GLA_V_DIM), lambda b, h: (0, 0)),
            qk(CB_GQ), qk(CB_GK), vg(CB_GV), vg(CB_GG),
        ],
        out_specs=pl.BlockSpec((hp, 2, SEQ, LANES), lambda b, h: (h, 0, b, 0)),
        out_shape=jax.ShapeDtypeStruct((GLA_HEADS, 2, TOKENS, LANES), BF16),
        scratch_shapes=[
            pltpu.VMEM((SEQ, GLA_K_DIM), BF16),
            pltpu.VMEM((SEQ, GLA_K_DIM), F32),
            pltpu.VMEM((GLA_NCHUNK, GLA_K_DIM, GLA_V_DIM), F32),
            pltpu.VMEM((GLA_NCHUNK, GLA_K_DIM, GLA_V_DIM), BF16),
        ],
        compiler_params=pltpu.CompilerParams(
            dimension_semantics=("arbitrary", "arbitrary"), vmem_limit_bytes=VMEM_LIMIT),
        name="gla_scan",
    )(g, w2_heads, b2_heads, gw, p, p, p4, p4)
    return out.reshape(2 * GLA_HEADS, TOKENS, LANES)


OUT_TM = 1024
OUT_SPLITS = (256, 256, 256, 256)
assert sum(OUT_SPLITS) == OUT_TM
OUT_VMEM_LIMIT = (4 * OUT_TM * D_MODEL * 4 + 2 * OUT_TM * D_MODEL * 2 + D_MODEL * D_MODEL * 2
                  + 8 * 1024 * 1024)


def _out_kernel(a_ref, b_ref, wb_ref, x_ref, gate_ref, fw_ref, o_ref):
    gate = gate_ref[0]
    fw = fw_ref[...]
    row0 = 0
    for n_rows in OUT_SPLITS:
        rows = pl.ds(row0, n_rows)
        row0 += n_rows
        mix = jnp.concatenate(
            [a_ref[h, rows, :] for h in range(a_ref.shape[0])]
            + [b_ref[h, rows, :] for h in range(b_ref.shape[0])], axis=1)
        y = jnp.dot(mix, wb_ref[...], preferred_element_type=F32)
        h_res = x_ref[rows, :] + gate * y
        ms = jnp.mean(h_res * h_res, axis=-1, keepdims=True)
        o_ref[rows, :] = h_res * lax.rsqrt(ms + NORM_EPS) * fw


def _out_call(a, b, w_out, xf, gate, fw):
    per_seq = SEQ // OUT_TM
    return pl.pallas_call(
        _out_kernel,
        grid=(TOKENS // OUT_TM,),
        in_specs=[
            pl.BlockSpec((DIFF_HEADS, OUT_TM, LANES), lambda i: (0, i, 0)),
            pl.BlockSpec((2 * GLA_HEADS, OUT_TM, LANES), lambda i: (0, i, 0)),
            pl.BlockSpec((D_MODEL, D_MODEL), lambda i: (0, 0), pipeline_mode=pl.Buffered(1)),
            pl.BlockSpec((OUT_TM, D_MODEL), lambda i: (i, 0)),
            pl.BlockSpec((1, 1, D_MODEL), lambda i: (i // per_seq, 0, 0)),
            pl.BlockSpec((1, D_MODEL), lambda i: (0, 0)),
        ],
        out_specs=pl.BlockSpec((OUT_TM, D_MODEL), lambda i: (i, 0)),
        out_shape=jax.ShapeDtypeStruct((TOKENS, D_MODEL), F32),
        compiler_params=pltpu.CompilerParams(
            dimension_semantics=("arbitrary",), vmem_limit_bytes=OUT_VMEM_LIMIT),
        name="out_proj_norm",
    )(a, b, w_out, xf, gate, fw)


def _rope_tables():
    inv_freq = ROPE_THETA ** (-np.arange(0, DIFF_QK_DIM, 2, dtype=np.float64) / DIFF_QK_DIM)
    ang = np.arange(SEQ, dtype=np.float64)[:, None] * inv_freq[None, :]
    cos, sin = np.cos(ang), np.sin(ang)
    cos_t = np.tile(cos, (1, 4)).astype(np.float32)
    sin_t = np.concatenate([-sin, sin, -sin, sin], axis=1).astype(np.float32)
    return jnp.asarray(cos_t), jnp.asarray(sin_t)


def kernel(x, c, norm_w, w_ada, b_ada, w_in, lambda_q1, lambda_k1, lambda_q2, lambda_k2,
           diff_norm_w, gla_gate_w2, gla_gate_b, gla_norm_w, w_out, final_norm_w):
    assert x.shape == (BATCH, SEQ, D_MODEL) and w_in.shape[0] == 1
    xf = x.reshape(TOKENS, D_MODEL).astype(F32)

    c_pad = jnp.pad(c.astype(F32), ((0, 8 - BATCH), (0, 0)))
    b_ada_row = b_ada[0][None, :]
    mod = _ada_call(c_pad, w_ada[0], b_ada_row, 2 * D_MODEL)[:BATCH]
    shift = mod[:, :D_MODEL].reshape(BATCH, 1, D_MODEL)
    scale = mod[:, D_MODEL:].reshape(BATCH, 1, D_MODEL)

    w_t = w_in[0].T
    w_gate = jnp.pad(w_t[N_MAIN:], ((0, LANES - GLA_GATE_RANK), (0, 0))).astype(BF16)
    cos, sin = _rope_tables()
    p, g = _proj_call(xf, shift, scale, norm_w[0][None, :], w_t, w_gate, cos, sin)

    row = lambda v: v[0][None, :].astype(F32)
    a, gate8, w_out_bf16 = _attn_call(
        p, row(lambda_q1), row(lambda_k1), row(lambda_q2), row(lambda_k2), row(diff_norm_w),
        c_pad, w_ada[0], b_ada_row, w_out[0])
    gate = gate8[:BATCH].reshape(BATCH, 1, D_MODEL)

    w2_heads = jnp.pad(gla_gate_w2[0], ((0, LANES - GLA_GATE_RANK), (0, 0))).astype(BF16)
    w2_heads = w2_heads.reshape(LANES, GLA_HEADS, GLA_K_DIM).transpose(1, 0, 2)
    b2_heads = gla_gate_b[0].astype(F32).reshape(GLA_HEADS, 1, GLA_K_DIM)
    b = _gla_call(p, g, w2_heads, b2_heads, row(gla_norm_w))

    out = _out_call(a, b, w_out_bf16, xf, gate, final_norm_w[None, :].astype(F32))
    return out.reshape(BATCH, SEQ, D_MODEL).astype(x.dtype)
```

```python
import math

import jax
import jax.numpy as jnp
import numpy as np
from jax import lax
from jax.experimental import pallas as pl
from jax.experimental.pallas import tpu as pltpu

D_MODEL = 2048
BATCH = 4
SEQ = 2048
TOKENS = BATCH * SEQ
CHUNK = 64
LANES = 128

DIFF_HEADS = 8
DIFF_HEAD_DIM = 128
DIFF_QK_DIM = 64
GLA_HEADS = 4
GLA_K_DIM = 128
GLA_V_DIM = 256
GLA_GATE_RANK = 16
GLA_GATE_TAU = 16.0
ROPE_THETA = 10000.0
NORM_EPS = 1e-6
LAMBDA_INIT = 0.8 - 0.6 * math.exp(-0.3 * 0)

N_MAIN = 7168
N_COLBLK = N_MAIN // LANES
CB_DQ, CB_DK, CB_DV, CB_DG = 0, 8, 16, 24
CB_GQ, CB_GK, CB_GV, CB_GG = 32, 36, 40, 48

VMEM_LIMIT = 48 * 1024 * 1024
NEG_BIG = -1e30
LOG2_E = math.log2(math.e)

BF16 = jnp.bfloat16
F32 = jnp.float32


def _nt_dot(a, b):
    return lax.dot_general(a, b, (((1,), (1,)), ((), ())), preferred_element_type=F32)


def _silu(x):
    h = 0.5 * x
    return h + h * jnp.tanh(h)


def _tn_dot(a, b):
    return lax.dot_general(a, b, (((0,), (0,)), ((), ())), preferred_element_type=F32)


ADA_TN = 1024


def _ada_block(c_ref, w_ref, b_ref):
    c = c_ref[...]
    c_act = (c * jax.nn.sigmoid(c)).astype(BF16)
    return jnp.dot(c_act, w_ref[...].astype(BF16), preferred_element_type=F32) + b_ref[...]


def _ada_kernel(c_ref, w_ref, b_ref, o_ref):
    o_ref[...] = _ada_block(c_ref, w_ref, b_ref)


def _ada_call(c_pad, w_ada, b_ada, n):
    return pl.pallas_call(
        _ada_kernel,
        grid=(n // ADA_TN,),
        in_specs=[
            pl.BlockSpec((8, D_MODEL), lambda j: (0, 0)),
            pl.BlockSpec((D_MODEL, ADA_TN), lambda j: (0, j)),
            pl.BlockSpec((1, ADA_TN), lambda j: (0, j)),
        ],
        out_specs=pl.BlockSpec((8, ADA_TN), lambda j: (0, j)),
        out_shape=jax.ShapeDtypeStruct((8, n), F32),
        compiler_params=pltpu.CompilerParams(
            dimension_semantics=("arbitrary",), vmem_limit_bytes=VMEM_LIMIT),
        name="ada_mod",
    )(c_pad, w_ada, b_ada)


PROJ_TM = 1024
PROJ_TN = 1024
PROJ_SUB = 256
PROJ_ROPE_TILES = 2
PROJ_SEQ_TILES = SEQ // PROJ_TM
PROJ_ROW_TILES = 2 * PROJ_SEQ_TILES
PROJ_VMEM_LIMIT = (2 * PROJ_TM * D_MODEL * 4 + 2 * D_MODEL * PROJ_TN * 4
                   + (1 + PROJ_ROW_TILES) * PROJ_TM * D_MODEL * 2
                   + 2 * PROJ_TM * PROJ_TN * 2 + 4 * 1024 * 1024)


def _rot_half(x):
    lane = lax.broadcasted_iota(jnp.int32, x.shape, 1)
    first = (lane % DIFF_QK_DIM) < (DIFF_QK_DIM // 2)
    return jnp.where(first, pltpu.roll(x, LANES - 32, 1), pltpu.roll(x, 32, 1))


def _proj_kernel(x_ref, shift_ref, scale_ref, nw_ref, w_ref, wg_ref, cos_ref, sin_ref,
                 p_ref, g_ref, hn_ref, hn_all_ref):
    j = pl.program_id(1)
    r = pl.program_id(2)
    n_sub = PROJ_TM // PROJ_SUB
    n_cb = PROJ_TN // LANES
    pos0 = pl.multiple_of((r % PROJ_SEQ_TILES) * PROJ_TM, PROJ_TM)

    def norm_rows(rows):
        mult = nw_ref[...] * (1.0 + scale_ref[0])
        xs = x_ref[rows, :]
        ms = jnp.mean(xs * xs, axis=-1, keepdims=True)
        hn_ref[rows, :] = (xs * lax.rsqrt(ms + NORM_EPS) * mult + shift_ref[0]).astype(BF16)

    def store_rope(sub, acc, sc):
        rows = pl.ds(sub * PROJ_SUB, PROJ_SUB)
        pos = pl.ds(pos0 + sub * PROJ_SUB, PROJ_SUB)
        cos = cos_ref[pos, :] * sc
        sin = sin_ref[pos, :] * sc
        for cb in range(n_cb):
            t = acc[:, cb * LANES:(cb + 1) * LANES]
            p_ref[cb, rows, :] = (t * cos + _rot_half(t) * sin).astype(BF16)

    @pl.when(j == 0)
    def _():
        for sub in range(n_sub):
            rows = pl.ds(sub * PROJ_SUB, PROJ_SUB)
            norm_rows(rows)
            acc = _nt_dot(hn_ref[rows, :], w_ref[...])
            store_rope(sub, acc, DIFF_QK_DIM ** -0.5 * LOG2_E)
        wg = jnp.concatenate(
            [wg_ref[...].astype(BF16), jnp.zeros((LANES - GLA_GATE_RANK, D_MODEL), BF16)], axis=0)
        g_ref[...] = _nt_dot(hn_ref[...], wg).astype(BF16)
        hn_all_ref[r] = hn_ref[...]

    @pl.when(j == 1)
    def _():
        for sub in range(n_sub):
            rows = pl.ds(sub * PROJ_SUB, PROJ_SUB)
            acc = _nt_dot(hn_all_ref[r, rows, :], w_ref[...])
            store_rope(sub, acc, 1.0)

    @pl.when(j >= PROJ_ROPE_TILES)
    def _():
        for sub in range(n_sub):
            rows = pl.ds(sub * PROJ_SUB, PROJ_SUB)
            acc = _nt_dot(hn_all_ref[r, rows, :], w_ref[...])
            for cb in range(n_cb):
                p_ref[cb, rows, :] = acc[:, cb * LANES:(cb + 1) * LANES].astype(BF16)


def _proj_call(xf, shift, scale, norm_w, w_t, w_gate, cos, sin):
    n_tiles = N_MAIN // PROJ_TN
    rt = PROJ_ROW_TILES
    n_cb = PROJ_TN // LANES
    x_tile = lambda b, j, r: (jnp.where(j == 0, b * rt + r, b * rt + rt - 1), 0)
    return pl.pallas_call(
        _proj_kernel,
        grid=(TOKENS // (rt * PROJ_TM), n_tiles, rt),
        in_specs=[
            pl.BlockSpec((PROJ_TM, D_MODEL), x_tile),
            pl.BlockSpec((1, 1, D_MODEL), lambda b, j, r: ((b * rt + r) // PROJ_SEQ_TILES, 0, 0)),
            pl.BlockSpec((1, 1, D_MODEL), lambda b, j, r: ((b * rt + r) // PROJ_SEQ_TILES, 0, 0)),
            pl.BlockSpec((1, D_MODEL), lambda b, j, r: (0, 0)),
            pl.BlockSpec((PROJ_TN, D_MODEL), lambda b, j, r: (j, 0)),
            pl.BlockSpec((GLA_GATE_RANK, D_MODEL), lambda b, j, r: (N_MAIN // GLA_GATE_RANK, 0)),
            pl.BlockSpec((SEQ, LANES), lambda b, j, r: (0, 0)),
            pl.BlockSpec((SEQ, LANES), lambda b, j, r: (0, 0)),
        ],
        out_specs=[
            pl.BlockSpec((n_cb, PROJ_TM, LANES), lambda b, j, r: (j, b * rt + r, 0)),
            pl.BlockSpec((PROJ_TM, LANES), x_tile),
        ],
        out_shape=[
            jax.ShapeDtypeStruct((N_COLBLK, TOKENS, LANES), BF16),
            jax.ShapeDtypeStruct((TOKENS, LANES), BF16),
        ],
        scratch_shapes=[
            pltpu.VMEM((PROJ_TM, D_MODEL), BF16),
            pltpu.VMEM((rt, PROJ_TM, D_MODEL), BF16),
        ],
        compiler_params=pltpu.CompilerParams(
            dimension_semantics=("arbitrary", "arbitrary", "arbitrary"),
            vmem_limit_bytes=PROJ_VMEM_LIMIT),
        name="norm_in_proj",
    )(xf, shift, scale, norm_w, w_t, w_gate, cos, sin)


ATT_T = 256
ATT_NBLK = SEQ // ATT_T
ATT_LEAD = 9
assert ATT_LEAD >= ATT_NBLK - 1
ATT_ONES_ROWS = 16
ATT_HEADS_PER_STEP = 2
ATT_SLOTS = ATT_LEAD + 4


def _attn_kernel(lq1_ref, lk1_ref, lq2_ref, lk2_ref, dnw_ref, q_ref, k_ref, v_ref, dg_ref,
                 c_ref, wa_ref, ba_ref, wo_ref, o_ref, gate_ref, wob_ref, vt_ref, s_ref):
    gate_ref[...] = _ada_block(c_ref, wa_ref, ba_ref)
    wob_ref[...] = wo_ref[...].astype(BF16)

    lam = (jnp.exp(jnp.sum(lq1_ref[...] * lk1_ref[...], keepdims=True))
           - jnp.exp(jnp.sum(lq2_ref[...] * lk2_ref[...], keepdims=True))
           + LAMBDA_INIT)
    out_gain = dnw_ref[...] * (1.0 - LAMBDA_INIT)

    for hd in range(ATT_HEADS_PER_STEP):
        for jb in range(ATT_NBLK):
            cols = pl.ds(jb * ATT_T, ATT_T)
            vb = v_ref[hd, jb * ATT_T:(jb + 1) * ATT_T, :].astype(F32)
            vt_ref[hd, 0:DIFF_HEAD_DIM, cols] = vb.T.astype(BF16)
        vt_ref[hd, DIFF_HEAD_DIM:, :] = jnp.ones((ATT_ONES_ROWS, SEQ), BF16)

    lane = lax.broadcasted_iota(jnp.int32, (ATT_T, LANES), 1)
    comp_a = lane < DIFF_QK_DIM
    krow = lax.broadcasted_iota(jnp.int32, (ATT_T, ATT_T), 0) // CHUNK
    qcol = lax.broadcasted_iota(jnp.int32, (ATT_T, ATT_T), 1) // CHUNK
    diag_mask = krow <= qcol
    diag_mask2 = jnp.concatenate([diag_mask, diag_mask], axis=1)

    units = [(hd, qi, j) for hd in range(ATT_HEADS_PER_STEP)
             for qi in range(ATT_NBLK) for j in range(qi + 1)]
    blocks = {(hd, qi): {"m": None}
              for hd in range(ATT_HEADS_PER_STEP) for qi in range(ATT_NBLK)}

    def block_queries(hd, qi):
        q = q_ref[hd, pl.ds(qi * ATT_T, ATT_T), :]
        zero = jnp.zeros_like(q)
        return jnp.concatenate([jnp.where(comp_a, q, zero), jnp.where(comp_a, zero, q)], axis=0)

    dyn0 = lax.shift_right_logical(pl.program_id(0), 20)

    def score_unit(t, hd, qi, j):
        st = blocks[hd, qi]
        if j == 0:
            st["q2"] = block_queries(hd, qi)
        s = _nt_dot(k_ref[hd, pl.ds(j * ATT_T, ATT_T), :], st["q2"])
        if j == qi:
            s = jnp.where(diag_mask2, s, NEG_BIG)
        s_ref[dyn0 + t % ATT_SLOTS] = s
        mj = jnp.max(s, axis=0, keepdims=True)
        st["m"] = mj if st["m"] is None else jnp.maximum(st["m"], mj)

    def value_unit(t, hd, qi, j):
        st = blocks[hd, qi]
        p = jnp.exp2(s_ref[dyn0 + t % ATT_SLOTS] - st["m"])
        pvj = jnp.dot(vt_ref[hd, :, j * ATT_T:(j + 1) * ATT_T], p.astype(BF16),
                      preferred_element_type=F32)
        st["pv"] = pvj if j == 0 else st["pv"] + pvj
        if j == qi:
            finish(hd, qi, st["pv"])

    def finish(hd, qi, pv_l):
        rows = pl.ds(qi * ATT_T, ATT_T)
        l = pv_l[DIFF_HEAD_DIM:DIFF_HEAD_DIM + 1, :]
        pv = pv_l[0:DIFF_HEAD_DIM, :] * (1.0 / l)
        out_t = pv[:, :ATT_T] - lam * pv[:, ATT_T:]
        out = out_t.T
        ms = jnp.mean(out * out, axis=-1, keepdims=True)
        y = out * lax.rsqrt(ms + NORM_EPS) * out_gain
        g = dg_ref[hd, rows, :].astype(F32)
        o_ref[hd, rows, :] = (y * _silu(g)).astype(BF16)

    for t in range(len(units) + ATT_LEAD):
        if t < len(units):
            score_unit(t, *units[t])
        if t >= ATT_LEAD:
            value_unit(t - ATT_LEAD, *units[t - ATT_LEAD])


def _attn_call(p, lq1, lk1, lq2, lk2, dnw, c_pad, w_ada, b_ada, w_out):
    hp = ATT_HEADS_PER_STEP
    steps_per_batch = DIFF_HEADS // hp
    side_w = D_MODEL // (BATCH * steps_per_batch)
    assert side_w % LANES == 0
    wo_rows = pl.BlockSpec((side_w, D_MODEL), lambda b, h: (b * steps_per_batch + h, 0))
    gate_cb0 = 2 * D_MODEL // side_w
    vec = lambda n: pl.BlockSpec((1, n), lambda b, h: (0, 0))
    slab = lambda cb0: pl.BlockSpec((hp, SEQ, LANES), lambda b, h: (cb0 // hp + h, b, 0))
    gate_blk = lambda rows, cb0: pl.BlockSpec(
        (rows, side_w), lambda b, h: (0, cb0 + b * steps_per_batch + h))
    return pl.pallas_call(
        _attn_kernel,
        grid=(BATCH, steps_per_batch),
        in_specs=[vec(DIFF_QK_DIM)] * 4 + [vec(DIFF_HEAD_DIM),
                  slab(CB_DQ), slab(CB_DK), slab(CB_DV), slab(CB_DG),
                  pl.BlockSpec((8, D_MODEL), lambda b, h: (0, 0)),
                  gate_blk(D_MODEL, gate_cb0), gate_blk(1, gate_cb0), wo_rows],
        out_specs=[pl.BlockSpec((hp, SEQ, LANES), lambda b, h: (h, b, 0)), gate_blk(8, 0),
                   wo_rows],
        out_shape=[jax.ShapeDtypeStruct((DIFF_HEADS, TOKENS, LANES), BF16),
                   jax.ShapeDtypeStruct((8, D_MODEL), F32),
                   jax.ShapeDtypeStruct((D_MODEL, D_MODEL), BF16)],
        scratch_shapes=[
            pltpu.VMEM((hp, DIFF_HEAD_DIM + ATT_ONES_ROWS, SEQ), BF16),
            pltpu.VMEM((ATT_SLOTS, ATT_T, 2 * ATT_T), F32),
        ],
        compiler_params=pltpu.CompilerParams(
            dimension_semantics=("arbitrary", "arbitrary"), vmem_limit_bytes=VMEM_LIMIT),
        name="diff_attn",
    )(lq1, lk1, lq2, lk2, dnw, p, p, p, p, c_pad, w_ada, b_ada, w_out)


GLA_BLK = 256
GLA_NCHUNK = SEQ // CHUNK
GLA_HEADS_PER_STEP = 2


def _split_hi_lo(x):
    hi = x.astype(BF16)
    lo = (x - hi.astype(F32)).astype(BF16)
    return hi, lo


def _gla_kernel(g_ref, w2_ref, b2_ref, gw_ref, q_ref, k_ref, v_ref, gg_ref, o_ref,
                kdec_ref, tot_ref, kv_ref, st_ref):
    for hd in range(GLA_HEADS_PER_STEP):
        _gla_head(g_ref, w2_ref.at[hd], b2_ref.at[hd], gw_ref, q_ref.at[hd], k_ref.at[hd],
                  v_ref.at[hd], gg_ref.at[hd], o_ref.at[hd], kdec_ref, tot_ref, kv_ref, st_ref)


def _gla_head(g_ref, w2_ref, b2_ref, gw_ref, q_ref, k_ref, v_ref, gg_ref, o_ref,
              kdec_ref, tot_ref, kv_ref, st_ref):
    z = jnp.dot(g_ref[...], w2_ref[...], preferred_element_type=F32) + b2_ref[...]
    log_a = (jnp.minimum(z, 0.0) - jnp.log(1.0 + jnp.exp(-jnp.abs(z)))) * (1.0 / GLA_GATE_TAU)

    r = lax.broadcasted_iota(jnp.int32, (GLA_BLK, GLA_BLK), 0)
    c = lax.broadcasted_iota(jnp.int32, (GLA_BLK, GLA_BLK), 1)
    after = jnp.where(((r // CHUNK) == (c // CHUNK)) & (c > r), 1.0, 0.0).astype(BF16)
    for blk in range(SEQ // GLA_BLK):
        rows = pl.ds(blk * GLA_BLK, GLA_BLK)
        la = log_a[blk * GLA_BLK:(blk + 1) * GLA_BLK, :]
        hi, lo = _split_hi_lo(la)
        both = jnp.dot(after, jnp.concatenate([hi, lo], axis=1), preferred_element_type=F32)
        suffix = both[:, :GLA_K_DIM] + both[:, GLA_K_DIM:]
        kdec_ref[rows, :] = (k_ref[rows, :].astype(F32) * jnp.exp(suffix)).astype(BF16)
        tot_ref[rows, :] = suffix + la

    dec_rows = jnp.exp(tot_ref[pl.ds(0, GLA_NCHUNK, stride=CHUNK), :])
    dec_cols = jnp.concatenate(
        [dec_rows, jnp.zeros((GLA_K_DIM - GLA_NCHUNK, GLA_K_DIM), F32)], axis=0).T

    dyn0 = lax.shift_right_logical(pl.program_id(0), 20)
    for ci in range(GLA_NCHUNK):
        rows = pl.ds(ci * CHUNK, CHUNK)
        v_c = jnp.concatenate([v_ref[0, rows, :], v_ref[1, rows, :]], axis=1)
        kv_ref[dyn0 + ci] = _tn_dot(kdec_ref[rows, :], v_c)

    state = jnp.zeros((GLA_K_DIM, GLA_V_DIM), F32)
    for ci in range(GLA_NCHUNK):
        state = state * dec_cols[:, ci:ci + 1] + kv_ref[dyn0 + ci]
        st_ref[ci] = state.astype(BF16)

    gw = gw_ref[...]
    qs = GLA_K_DIM ** -0.5
    for ci in range(GLA_NCHUNK):
        rows = pl.ds(ci * CHUNK, CHUNK)
        o = jnp.dot(q_ref[rows, :], st_ref[ci], preferred_element_type=F32)
        ms = jnp.mean(o * o, axis=-1, keepdims=True) * (qs * qs)
        y = o * (qs * lax.rsqrt(ms + NORM_EPS)) * gw
        gate = jnp.concatenate([gg_ref[0, rows, :], gg_ref[1, rows, :]], axis=1).astype(F32)
        y = y * _silu(gate)
        o_ref[0, rows, :] = y[:, :LANES].astype(BF16)
        o_ref[1, rows, :] = y[:, LANES:].astype(BF16)


def _gla_call(p, g, w2_heads, b2_heads, gw):
    hp = GLA_HEADS_PER_STEP
    qk = lambda cb0: pl.BlockSpec((hp, SEQ, LANES), lambda b, h: (cb0 // hp + h, b, 0))
    p4 = p.reshape(N_COLBLK // 2, 2, TOKENS, LANES)
    vg = lambda cb0: pl.BlockSpec((hp, 2, SEQ, LANES), lambda b, h: (cb0 // 2 // hp + h, 0, b, 0))
    out = pl.pallas_call(
        _gla_kernel,
        grid=(BATCH, GLA_HEADS // hp),
        in_specs=[
            pl.BlockSpec((SEQ, LANES), lambda b, h: (b, 0)),
            pl.BlockSpec((hp, LANES, GLA_K_DIM), lambda b, h: (h, 0, 0)),
            pl.BlockSpec((hp, 1, GLA_K_DIM), lambda b, h: (h, 0, 0)),
            pl.BlockSpec((1, GLA_V_DIM), lambda b, h: (0, 0)),
            qk(CB_GQ), qk(CB_GK), vg(CB_GV), vg(CB_GG),
        ],
        out_specs=pl.BlockSpec((hp, 2, SEQ, LANES), lambda b, h: (h, 0, b, 0)),
        out_shape=jax.ShapeDtypeStruct((GLA_HEADS, 2, TOKENS, LANES), BF16),
        scratch_shapes=[
            pltpu.VMEM((SEQ, GLA_K_DIM), BF16),
            pltpu.VMEM((SEQ, GLA_K_DIM), F32),
            pltpu.VMEM((GLA_NCHUNK, GLA_K_DIM, GLA_V_DIM), F32),
            pltpu.VMEM((GLA_NCHUNK, GLA_K_DIM, GLA_V_DIM), BF16),
        ],
        compiler_params=pltpu.CompilerParams(
            dimension_semantics=("arbitrary", "arbitrary"), vmem_limit_bytes=VMEM_LIMIT),
        name="gla_scan",
    )(g, w2_heads, b2_heads, gw, p, p, p4, p4)
    return out.reshape(2 * GLA_HEADS, TOKENS, LANES)


OUT_TM = 1024
OUT_SPLITS = (256, 256, 256, 256)
assert sum(OUT_SPLITS) == OUT_TM
OUT_VMEM_LIMIT = (4 * OUT_TM * D_MODEL * 4 + 2 * OUT_TM * D_MODEL * 2 + D_MODEL * D_MODEL * 2
                  + 8 * 1024 * 1024)


def _out_kernel(a_ref, b_ref, wb_ref, x_ref, gate_ref, fw_ref, o_ref):
    gate = gate_ref[0]
    fw = fw_ref[...]
    row0 = 0
    for n_rows in OUT_SPLITS:
        rows = pl.ds(row0, n_rows)
        row0 += n_rows
        mix = jnp.concatenate(
            [a_ref[h, rows, :] for h in range(a_ref.shape[0])]
            + [b_ref[h, rows, :] for h in range(b_ref.shape[0])], axis=1)
        y = jnp.dot(mix, wb_ref[...], preferred_element_type=F32)
        h_res = x_ref[rows, :] + gate * y
        ms = jnp.mean(h_res * h_res, axis=-1, keepdims=True)
        o_ref[rows, :] = h_res * lax.rsqrt(ms + NORM_EPS) * fw


def _out_call(a, b, w_out, xf, gate, fw):
    per_seq = SEQ // OUT_TM
    return pl.pallas_call(
        _out_kernel,
        grid=(TOKENS // OUT_TM,),
        in_specs=[
            pl.BlockSpec((DIFF_HEADS, OUT_TM, LANES), lambda i: (0, i, 0)),
            pl.BlockSpec((2 * GLA_HEADS, OUT_TM, LANES), lambda i: (0, i, 0)),
            pl.BlockSpec((D_MODEL, D_MODEL), lambda i: (0, 0), pipeline_mode=pl.Buffered(1)),
            pl.BlockSpec((OUT_TM, D_MODEL), lambda i: (i, 0)),
            pl.BlockSpec((1, 1, D_MODEL), lambda i: (i // per_seq, 0, 0)),
            pl.BlockSpec((1, D_MODEL), lambda i: (0, 0)),
        ],
        out_specs=pl.BlockSpec((OUT_TM, D_MODEL), lambda i: (i, 0)),
        out_shape=jax.ShapeDtypeStruct((TOKENS, D_MODEL), F32),
        compiler_params=pltpu.CompilerParams(
            dimension_semantics=("arbitrary",), vmem_limit_bytes=OUT_VMEM_LIMIT),
        name="out_proj_norm",
    )(a, b, w_out, xf, gate, fw)


def _rope_tables():
    inv_freq = ROPE_THETA ** (-np.arange(0, DIFF_QK_DIM, 2, dtype=np.float64) / DIFF_QK_DIM)
    ang = np.arange(SEQ, dtype=np.float64)[:, None] * inv_freq[None, :]
    cos, sin = np.cos(ang), np.sin(ang)
    cos_t = np.tile(cos, (1, 4)).astype(np.float32)
    sin_t = np.concatenate([-sin, sin, -sin, sin], axis=1).astype(np.float32)
    return jnp.asarray(cos_t), jnp.asarray(sin_t)


def kernel(x, c, norm_w, w_ada, b_ada, w_in, lambda_q1, lambda_k1, lambda_q2, lambda_k2,
           diff_norm_w, gla_gate_w2, gla_gate_b, gla_norm_w, w_out, final_norm_w):
    assert x.shape == (BATCH, SEQ, D_MODEL) and w_in.shape[0] == 1
    xf = x.reshape(TOKENS, D_MODEL).astype(F32)

    c_pad = jnp.pad(c.astype(F32), ((0, 8 - BATCH), (0, 0)))
    b_ada_row = b_ada[0][None, :]
    mod = _ada_call(c_pad, w_ada[0], b_ada_row, 2 * D_MODEL)[:BATCH]
    shift = mod[:, :D_MODEL].reshape(BATCH, 1, D_MODEL)
    scale = mod[:, D_MODEL:].reshape(BATCH, 1, D_MODEL)

    w_t = w_in[0].T
    cos, sin = _rope_tables()
    p, g = _proj_call(xf, shift, scale, norm_w[0][None, :], w_t, w_t, cos, sin)

    row = lambda v: v[0][None, :].astype(F32)
    a, gate8, w_out_bf16 = _attn_call(
        p, row(lambda_q1), row(lambda_k1), row(lambda_q2), row(lambda_k2), row(diff_norm_w),
        c_pad, w_ada[0], b_ada_row, w_out[0])
    gate = gate8[:BATCH].reshape(BATCH, 1, D_MODEL)

    w2_heads = jnp.pad(gla_gate_w2[0], ((0, LANES - GLA_GATE_RANK), (0, 0))).astype(BF16)
    w2_heads = w2_heads.reshape(LANES, GLA_HEADS, GLA_K_DIM).transpose(1, 0, 2)
    b2_heads = gla_gate_b[0].astype(F32).reshape(GLA_HEADS, 1, GLA_K_DIM)
    b = _gla_call(p, g, w2_heads, b2_heads, row(gla_norm_w))

    out = _out_call(a, b, w_out_bf16, xf, gate, final_norm_w[None, :].astype(F32))
    return out.reshape(BATCH, SEQ, D_MODEL).astype(x.dtype)
```

```python
import math

import jax
import jax.numpy as jnp
import numpy as np
from jax import lax
from jax.experimental import pallas as pl
from jax.experimental.pallas import tpu as pltpu

D_MODEL = 2048
BATCH = 4
SEQ = 2048
TOKENS = BATCH * SEQ
CHUNK = 64
LANES = 128

DIFF_HEADS = 8
DIFF_HEAD_DIM = 128
DIFF_QK_DIM = 64
GLA_HEADS = 4
GLA_K_DIM = 128
GLA_V_DIM = 256
GLA_GATE_RANK = 16
GLA_GATE_TAU = 16.0
ROPE_THETA = 10000.0
NORM_EPS = 1e-6
LAMBDA_INIT = 0.8 - 0.6 * math.exp(-0.3 * 0)

N_MAIN = 7168
N_COLBLK = N_MAIN // LANES
CB_DQ, CB_DK, CB_DV, CB_DG = 0, 8, 16, 24
CB_GQ, CB_GK, CB_GV, CB_GG = 32, 36, 40, 48

VMEM_LIMIT = 48 * 1024 * 1024
NEG_BIG = -1e30
LOG2_E = math.log2(math.e)

BF16 = jnp.bfloat16
F32 = jnp.float32


def _nt_dot(a, b):
    return lax.dot_general(a, b, (((1,), (1,)), ((), ())), preferred_element_type=F32)


def _silu(x):
    h = 0.5 * x
    return h + h * jnp.tanh(h)


def _tn_dot(a, b):
    return lax.dot_general(a, b, (((0,), (0,)), ((), ())), preferred_element_type=F32)


ADA_TN = 1024


def _ada_block(c_ref, w_ref, b_ref):
    c = c_ref[...]
    c_act = (c * jax.nn.sigmoid(c)).astype(BF16)
    return jnp.dot(c_act, w_ref[...].astype(BF16), preferred_element_type=F32) + b_ref[...]


def _ada_kernel(c_ref, w_ref, b_ref, o_ref):
    o_ref[...] = _ada_block(c_ref, w_ref, b_ref)


def _ada_call(c_pad, w_ada, b_ada, n):
    return pl.pallas_call(
        _ada_kernel,
        grid=(n // ADA_TN,),
        in_specs=[
            pl.BlockSpec((8, D_MODEL), lambda j: (0, 0)),
            pl.BlockSpec((D_MODEL, ADA_TN), lambda j: (0, j)),
            pl.BlockSpec((1, ADA_TN), lambda j: (0, j)),
        ],
        out_specs=pl.BlockSpec((8, ADA_TN), lambda j: (0, j)),
        out_shape=jax.ShapeDtypeStruct((8, n), F32),
        compiler_params=pltpu.CompilerParams(
            dimension_semantics=("arbitrary",), vmem_limit_bytes=VMEM_LIMIT),
        name="ada_mod",
    )(c_pad, w_ada, b_ada)


PROJ_TM = 1024
PROJ_TN = 1024
PROJ_SUB = 256
PROJ_ROPE_TILES = 2
PROJ_SEQ_TILES = SEQ // PROJ_TM
PROJ_ROW_TILES = 2 * PROJ_SEQ_TILES
PROJ_VMEM_LIMIT = (2 * PROJ_TM * D_MODEL * 4 + 2 * D_MODEL * PROJ_TN * 4
                   + (1 + PROJ_ROW_TILES) * PROJ_TM * D_MODEL * 2
                   + 2 * PROJ_TM * PROJ_TN * 2 + 4 * 1024 * 1024)


def _rot_half(x):
    lane = lax.broadcasted_iota(jnp.int32, x.shape, 1)
    first = (lane % DIFF_QK_DIM) < (DIFF_QK_DIM // 2)
    return jnp.where(first, pltpu.roll(x, LANES - 32, 1), pltpu.roll(x, 32, 1))


def _proj_kernel(x_ref, shift_ref, scale_ref, nw_ref, w_ref, wg_ref, cos_ref, sin_ref,
                 p_ref, g_ref, hn_ref, hn_all_ref):
    j = pl.program_id(1)
    r = pl.program_id(2)
    n_sub = PROJ_TM // PROJ_SUB
    n_cb = PROJ_TN // LANES
    pos0 = pl.multiple_of((r % PROJ_SEQ_TILES) * PROJ_TM, PROJ_TM)

    def norm_rows(rows):
        mult = nw_ref[...] * (1.0 + scale_ref[0])
        xs = x_ref[rows, :]
        ms = jnp.mean(xs * xs, axis=-1, keepdims=True)
        hn_ref[rows, :] = (xs * lax.rsqrt(ms + NORM_EPS) * mult + shift_ref[0]).astype(BF16)

    def store_rope(sub, acc, sc):
        rows = pl.ds(sub * PROJ_SUB, PROJ_SUB)
        pos = pl.ds(pos0 + sub * PROJ_SUB, PROJ_SUB)
        cos = cos_ref[pos, :] * sc
        sin = sin_ref[pos, :] * sc
        for cb in range(n_cb):
            t = acc[:, cb * LANES:(cb + 1) * LANES]
            p_ref[cb, rows, :] = (t * cos + _rot_half(t) * sin).astype(BF16)

    @pl.when(j == 0)
    def _():
        for sub in range(n_sub):
            rows = pl.ds(sub * PROJ_SUB, PROJ_SUB)
            norm_rows(rows)
            acc = _nt_dot(hn_ref[rows, :], w_ref[...])
            store_rope(sub, acc, DIFF_QK_DIM ** -0.5 * LOG2_E)
        wg = jnp.concatenate(
            [wg_ref[...].astype(BF16), jnp.zeros((LANES - GLA_GATE_RANK, D_MODEL), BF16)], axis=0)
        g_ref[...] = _nt_dot(hn_ref[...], wg).astype(BF16)
        hn_all_ref[r] = hn_ref[...]

    @pl.when(j == 1)
    def _():
        for sub in range(n_sub):
            rows = pl.ds(sub * PROJ_SUB, PROJ_SUB)
            acc = _nt_dot(hn_all_ref[r, rows, :], w_ref[...])
            store_rope(sub, acc, 1.0)

    @pl.when(j >= PROJ_ROPE_TILES)
    def _():
        for sub in range(n_sub):
            rows = pl.ds(sub * PROJ_SUB, PROJ_SUB)
            acc = _nt_dot(hn_all_ref[r, rows, :], w_ref[...])
            for cb in range(n_cb):
                p_ref[cb, rows, :] = acc[:, cb * LANES:(cb + 1) * LANES].astype(BF16)


def _proj_call(xf, shift, scale, norm_w, w_t, w_gate, cos, sin):
    n_tiles = N_MAIN // PROJ_TN
    rt = PROJ_ROW_TILES
    n_cb = PROJ_TN // LANES
    x_tile = lambda b, j, r: (jnp.where(j == 0, b * rt + r, b * rt + rt - 1), 0)
    return pl.pallas_call(
        _proj_kernel,
        grid=(TOKENS // (rt * PROJ_TM), n_tiles, rt),
        in_specs=[
            pl.BlockSpec((PROJ_TM, D_MODEL), x_tile),
            pl.BlockSpec((1, 1, D_MODEL), lambda b, j, r: ((b * rt + r) // PROJ_SEQ_TILES, 0, 0)),
            pl.BlockSpec((1, 1, D_MODEL), lambda b, j, r: ((b * rt + r) // PROJ_SEQ_TILES, 0, 0)),
            pl.BlockSpec((1, D_MODEL), lambda b, j, r: (0, 0)),
            pl.BlockSpec((PROJ_TN, D_MODEL), lambda b, j, r: (j, 0)),
            pl.BlockSpec((GLA_GATE_RANK, D_MODEL), lambda b, j, r: (N_MAIN // GLA_GATE_RANK, 0)),
            pl.BlockSpec((SEQ, LANES), lambda b, j, r: (0, 0)),
            pl.BlockSpec((SEQ, LANES), lambda b, j, r: (0, 0)),
        ],
        out_specs=[
            pl.BlockSpec((n_cb, PROJ_TM, LANES), lambda b, j, r: (j, b * rt + r, 0)),
            pl.BlockSpec((PROJ_TM, LANES), x_tile),
        ],
        out_shape=[
            jax.ShapeDtypeStruct((N_COLBLK, TOKENS, LANES), BF16),
            jax.ShapeDtypeStruct((TOKENS, LANES), BF16),
        ],
        scratch_shapes=[
            pltpu.VMEM((PROJ_TM, D_MODEL), BF16),
            pltpu.VMEM((rt, PROJ_TM, D_MODEL), BF16),
        ],
        compiler_params=pltpu.CompilerParams(
            dimension_semantics=("arbitrary", "arbitrary", "arbitrary"),
            vmem_limit_bytes=PROJ_VMEM_LIMIT),
        name="norm_in_proj",
    )(xf, shift, scale, norm_w, w_t, w_gate, cos, sin)


ATT_T = 256
ATT_NBLK = SEQ // ATT_T
ATT_LEAD = 9
assert ATT_LEAD >= ATT_NBLK - 1
ATT_ONES_ROWS = 16
ATT_HEADS_PER_STEP = 4
ATT_SLOTS = ATT_LEAD + 4


def _attn_kernel(lq1_ref, lk1_ref, lq2_ref, lk2_ref, dnw_ref, q_ref, k_ref, v_ref, dg_ref,
                 c_ref, wa_ref, ba_ref, wo_ref, o_ref, gate_ref, wob_ref, vt_ref, s_ref):
    gate_ref[...] = _ada_block(c_ref, wa_ref, ba_ref)
    wob_ref[...] = wo_ref[...].astype(BF16)

    lam = (jnp.exp(jnp.sum(lq1_ref[...] * lk1_ref[...], keepdims=True))
           - jnp.exp(jnp.sum(lq2_ref[...] * lk2_ref[...], keepdims=True))
           + LAMBDA_INIT)
    out_gain = dnw_ref[...] * (1.0 - LAMBDA_INIT)

    for hd in range(ATT_HEADS_PER_STEP):
        for jb in range(ATT_NBLK):
            cols = pl.ds(jb * ATT_T, ATT_T)
            vb = v_ref[hd, jb * ATT_T:(jb + 1) * ATT_T, :].astype(F32)
            vt_ref[hd, 0:DIFF_HEAD_DIM, cols] = vb.T.astype(BF16)
        vt_ref[hd, DIFF_HEAD_DIM:, :] = jnp.ones((ATT_ONES_ROWS, SEQ), BF16)

    lane = lax.broadcasted_iota(jnp.int32, (ATT_T, LANES), 1)
    comp_a = lane < DIFF_QK_DIM
    krow = lax.broadcasted_iota(jnp.int32, (ATT_T, ATT_T), 0) // CHUNK
    qcol = lax.broadcasted_iota(jnp.int32, (ATT_T, ATT_T), 1) // CHUNK
    diag_mask = krow <= qcol
    diag_mask2 = jnp.concatenate([diag_mask, diag_mask], axis=1)

    units = [(hd, qi, j) for hd in range(ATT_HEADS_PER_STEP)
             for qi in range(ATT_NBLK) for j in range(qi + 1)]
    blocks = {(hd, qi): {"m": None}
              for hd in range(ATT_HEADS_PER_STEP) for qi in range(ATT_NBLK)}

    def block_queries(hd, qi):
        q = q_ref[hd, pl.ds(qi * ATT_T, ATT_T), :]
        zero = jnp.zeros_like(q)
        return jnp.concatenate([jnp.where(comp_a, q, zero), jnp.where(comp_a, zero, q)], axis=0)

    dyn0 = lax.shift_right_logical(pl.program_id(0), 20)

    def score_unit(t, hd, qi, j):
        st = blocks[hd, qi]
        if j == 0:
            st["q2"] = block_queries(hd, qi)
        s = _nt_dot(k_ref[hd, pl.ds(j * ATT_T, ATT_T), :], st["q2"])
        if j == qi:
            s = jnp.where(diag_mask2, s, NEG_BIG)
        s_ref[dyn0 + t % ATT_SLOTS] = s
        mj = jnp.max(s, axis=0, keepdims=True)
        st["m"] = mj if st["m"] is None else jnp.maximum(st["m"], mj)

    def value_unit(t, hd, qi, j):
        st = blocks[hd, qi]
        p = jnp.exp2(s_ref[dyn0 + t % ATT_SLOTS] - st["m"])
        pvj = jnp.dot(vt_ref[hd, :, j * ATT_T:(j + 1) * ATT_T], p.astype(BF16),
                      preferred_element_type=F32)
        st["pv"] = pvj if j == 0 else st["pv"] + pvj
        if j == qi:
            finish(hd, qi, st["pv"])

    def finish(hd, qi, pv_l):
        rows = pl.ds(qi * ATT_T, ATT_T)
        l = pv_l[DIFF_HEAD_DIM:DIFF_HEAD_DIM + 1, :]
        pv = pv_l[0:DIFF_HEAD_DIM, :] * (1.0 / l)
        out_t = pv[:, :ATT_T] - lam * pv[:, ATT_T:]
        out = out_t.T
        ms = jnp.mean(out * out, axis=-1, keepdims=True)
        y = out * lax.rsqrt(ms + NORM_EPS) * out_gain
        g = dg_ref[hd, rows, :].astype(F32)
        o_ref[hd, rows, :] = (y * _silu(g)).astype(BF16)

    for t in range(len(units) + ATT_LEAD):
        if t < len(units):
            score_unit(t, *units[t])
        if t >= ATT_LEAD:
            value_unit(t - ATT_LEAD, *units[t - ATT_LEAD])


def _attn_call(p, lq1, lk1, lq2, lk2, dnw, c_pad, w_ada, b_ada, w_out):
    hp = ATT_HEADS_PER_STEP
    steps_per_batch = DIFF_HEADS // hp
    side_w = D_MODEL // (BATCH * steps_per_batch)
    assert side_w % LANES == 0
    wo_rows = pl.BlockSpec((side_w, D_MODEL), lambda b, h: (b * steps_per_batch + h, 0))
    gate_cb0 = 2 * D_MODEL // side_w
    vec = lambda n: pl.BlockSpec((1, n), lambda b, h: (0, 0))
    slab = lambda cb0: pl.BlockSpec((hp, SEQ, LANES), lambda b, h: (cb0 // hp + h, b, 0))
    gate_blk = lambda rows, cb0: pl.BlockSpec(
        (rows, side_w), lambda b, h: (0, cb0 + b * steps_per_batch + h))
    return pl.pallas_call(
        _attn_kernel,
        grid=(BATCH, steps_per_batch),
        in_specs=[vec(DIFF_QK_DIM)] * 4 + [vec(DIFF_HEAD_DIM),
                  slab(CB_DQ), slab(CB_DK), slab(CB_DV), slab(CB_DG),
                  pl.BlockSpec((8, D_MODEL), lambda b, h: (0, 0)),
                  gate_blk(D_MODEL, gate_cb0), gate_blk(1, gate_cb0), wo_rows],
        out_specs=[pl.BlockSpec((hp, SEQ, LANES), lambda b, h: (h, b, 0)), gate_blk(8, 0),
                   wo_rows],
        out_shape=[jax.ShapeDtypeStruct((DIFF_HEADS, TOKENS, LANES), BF16),
                   jax.ShapeDtypeStruct((8, D_MODEL), F32),
                   jax.ShapeDtypeStruct((D_MODEL, D_MODEL), BF16)],
        scratch_shapes=[
            pltpu.VMEM((hp, DIFF_HEAD_DIM + ATT_ONES_ROWS, SEQ), BF16),
            pltpu.VMEM((ATT_SLOTS, ATT_T, 2 * ATT_T), F32),
        ],
        compiler_params=pltpu.CompilerParams(
            dimension_semantics=("arbitrary", "arbitrary"), vmem_limit_bytes=VMEM_LIMIT),
        name="diff_attn",
    )(lq1, lk1, lq2, lk2, dnw, p, p, p, p, c_pad, w_ada, b_ada, w_out)


GLA_BLK = 256
GLA_NCHUNK = SEQ // CHUNK
GLA_HEADS_PER_STEP = 2


def _split_hi_lo(x):
    hi = x.astype(BF16)
    lo = (x - hi.astype(F32)).astype(BF16)
    return hi, lo


def _gla_kernel(g_ref, w2_ref, b2_ref, gw_ref, q_ref, k_ref, v_ref, gg_ref, o_ref,
                kdec_ref, tot_ref, kv_ref, st_ref):
    for hd in range(GLA_HEADS_PER_STEP):
        _gla_head(g_ref, w2_ref.at[hd], b2_ref.at[hd], gw_ref, q_ref.at[hd], k_ref.at[hd],
                  v_ref.at[hd], gg_ref.at[hd], o_ref.at[hd], kdec_ref, tot_ref, kv_ref, st_ref)


def _gla_head(g_ref, w2_ref, b2_ref, gw_ref, q_ref, k_ref, v_ref, gg_ref, o_ref,
              kdec_ref, tot_ref, kv_ref, st_ref):
    z = jnp.dot(g_ref[...], w2_ref[...], preferred_element_type=F32) + b2_ref[...]
    log_a = (jnp.minimum(z, 0.0) - jnp.log(1.0 + jnp.exp(-jnp.abs(z)))) * (1.0 / GLA_GATE_TAU)

    r = lax.broadcasted_iota(jnp.int32, (GLA_BLK, GLA_BLK), 0)
    c = lax.broadcasted_iota(jnp.int32, (GLA_BLK, GLA_BLK), 1)
    after = jnp.where(((r // CHUNK) == (c // CHUNK)) & (c > r), 1.0, 0.0).astype(BF16)
    for blk in range(SEQ // GLA_BLK):
        rows = pl.ds(blk * GLA_BLK, GLA_BLK)
        la = log_a[blk * GLA_BLK:(blk + 1) * GLA_BLK, :]
        hi, lo = _split_hi_lo(la)
        both = jnp.dot(after, jnp.concatenate([hi, lo], axis=1), preferred_element_type=F32)
        suffix = both[:, :GLA_K_DIM] + both[:, GLA_K_DIM:]
        kdec_ref[rows, :] = (k_ref[rows, :].astype(F32) * jnp.exp(suffix)).astype(BF16)
        tot_ref[rows, :] = suffix + la

    dec_rows = jnp.exp(tot_ref[pl.ds(0, GLA_NCHUNK, stride=CHUNK), :])
    dec_cols = jnp.concatenate(
        [dec_rows, jnp.zeros((GLA_K_DIM - GLA_NCHUNK, GLA_K_DIM), F32)], axis=0).T

    dyn0 = lax.shift_right_logical(pl.program_id(0), 20)
    for ci in range(GLA_NCHUNK):
        rows = pl.ds(ci * CHUNK, CHUNK)
        v_c = jnp.concatenate([v_ref[0, rows, :], v_ref[1, rows, :]], axis=1)
        kv_ref[dyn0 + ci] = _tn_dot(kdec_ref[rows, :], v_c)

    state = jnp.zeros((GLA_K_DIM, GLA_V_DIM), F32)
    for ci in range(GLA_NCHUNK):
        state = state * dec_cols[:, ci:ci + 1] + kv_ref[dyn0 + ci]
        st_ref[ci] = state.astype(BF16)

    gw = gw_ref[...]
    qs = GLA_K_DIM ** -0.5
    for ci in range(GLA_NCHUNK):
        rows = pl.ds(ci * CHUNK, CHUNK)
        o = jnp.dot(q_ref[rows, :], st_ref[ci], preferred_element_type=F32)
        ms = jnp.mean(o * o, axis=-1, keepdims=True) * (qs * qs)
        y = o * (qs * lax.rsqrt(ms + NORM_EPS)) * gw
        gate = jnp.concatenate([gg_ref[0, rows, :], gg_ref[1, rows, :]], axis=1).astype(F32)
        y = y * _silu(gate)
        o_ref[0, rows, :] = y[:, :LANES].astype(BF16)
        o_ref[1, rows, :] = y[:, LANES:].astype(BF16)


def _gla_call(p, g, w2_heads, b2_heads, gw):
    hp = GLA_HEADS_PER_STEP
    qk = lambda cb0: pl.BlockSpec((hp, SEQ, LANES), lambda b, h: (cb0 // hp + h, b, 0))
    p4 = p.reshape(N_COLBLK // 2, 2, TOKENS, LANES)
    vg = lambda cb0: pl.BlockSpec((hp, 2, SEQ, LANES), lambda b, h: (cb0 // 2 // hp + h, 0, b, 0))
    out = pl.pallas_call(
        _gla_kernel,
        grid=(BATCH, GLA_HEADS // hp),
        in_specs=[
            pl.BlockSpec((SEQ, LANES), lambda b, h: (b, 0)),
            pl.BlockSpec((hp, LANES, GLA_K_DIM), lambda b, h: (h, 0, 0)),
            pl.BlockSpec((hp, 1, GLA_K_DIM), lambda b, h: (h, 0, 0)),
            pl.BlockSpec((1, GLA_V_DIM), lambda b, h: (0, 0)),
            qk(CB_GQ), qk(CB_GK), vg(CB_GV), vg(CB_GG),
        ],
        out_specs=pl.BlockSpec((hp, 2, SEQ, LANES), lambda b, h: (h, 0, b, 0)),
        out_shape=jax.ShapeDtypeStruct((GLA_HEADS, 2, TOKENS, LANES), BF16),
        scratch_shapes=[
            pltpu.VMEM((SEQ, GLA_K_DIM), BF16),
            pltpu.VMEM((SEQ, GLA_K_DIM), F32),
            pltpu.VMEM((GLA_NCHUNK, GLA_K_DIM, GLA_V_DIM), F32),
            pltpu.VMEM((GLA_NCHUNK, GLA_K_DIM, GLA_V_DIM), BF16),
        ],
        compiler_params=pltpu.CompilerParams(
            dimension_semantics=("arbitrary", "arbitrary"), vmem_limit_bytes=VMEM_LIMIT),
        name="gla_scan",
    )(g, w2_heads, b2_heads, gw, p, p, p4, p4)
    return out.reshape(2 * GLA_HEADS, TOKENS, LANES)


OUT_TM = 1024
OUT_SPLITS = (256, 256, 256, 256)
assert sum(OUT_SPLITS) == OUT_TM
OUT_VMEM_LIMIT = (4 * OUT_TM * D_MODEL * 4 + 2 * OUT_TM * D_MODEL * 2 + D_MODEL * D_MODEL * 2
                  + 8 * 1024 * 1024)


def _out_kernel(a_ref, b_ref, wb_ref, x_ref, gate_ref, fw_ref, o_ref):
    gate = gate_ref[0]
    fw = fw_ref[...]
    row0 = 0
    for n_rows in OUT_SPLITS:
        rows = pl.ds(row0, n_rows)
        row0 += n_rows
        mix = jnp.concatenate(
            [a_ref[h, rows, :] for h in range(a_ref.shape[0])]
            + [b_ref[h, rows, :] for h in range(b_ref.shape[0])], axis=1)
        y = jnp.dot(mix, wb_ref[...], preferred_element_type=F32)
        h_res = x_ref[rows, :] + gate * y
        ms = jnp.mean(h_res * h_res, axis=-1, keepdims=True)
        o_ref[rows, :] = h_res * lax.rsqrt(ms + NORM_EPS) * fw


def _out_call(a, b, w_out, xf, gate, fw):
    per_seq = SEQ // OUT_TM
    return pl.pallas_call(
        _out_kernel,
        grid=(TOKENS // OUT_TM,),
        in_specs=[
            pl.BlockSpec((DIFF_HEADS, OUT_TM, LANES), lambda i: (0, i, 0)),
            pl.BlockSpec((2 * GLA_HEADS, OUT_TM, LANES), lambda i: (0, i, 0)),
            pl.BlockSpec((D_MODEL, D_MODEL), lambda i: (0, 0), pipeline_mode=pl.Buffered(1)),
            pl.BlockSpec((OUT_TM, D_MODEL), lambda i: (i, 0)),
            pl.BlockSpec((1, 1, D_MODEL), lambda i: (i // per_seq, 0, 0)),
            pl.BlockSpec((1, D_MODEL), lambda i: (0, 0)),
        ],
        out_specs=pl.BlockSpec((OUT_TM, D_MODEL), lambda i: (i, 0)),
        out_shape=jax.ShapeDtypeStruct((TOKENS, D_MODEL), F32),
        compiler_params=pltpu.CompilerParams(
            dimension_semantics=("arbitrary",), vmem_limit_bytes=OUT_VMEM_LIMIT),
        name="out_proj_norm",
    )(a, b, w_out, xf, gate, fw)


def _rope_tables():
    inv_freq = ROPE_THETA ** (-np.arange(0, DIFF_QK_DIM, 2, dtype=np.float64) / DIFF_QK_DIM)
    ang = np.arange(SEQ, dtype=np.float64)[:, None] * inv_freq[None, :]
    cos, sin = np.cos(ang), np.sin(ang)
    cos_t = np.tile(cos, (1, 4)).astype(np.float32)
    sin_t = np.concatenate([-sin, sin, -sin, sin], axis=1).astype(np.float32)
    return jnp.asarray(cos_t), jnp.asarray(sin_t)


def kernel(x, c, norm_w, w_ada, b_ada, w_in, lambda_q1, lambda_k1, lambda_q2, lambda_k2,
           diff_norm_w, gla_gate_w2, gla_gate_b, gla_norm_w, w_out, final_norm_w):
    assert x.shape == (BATCH, SEQ, D_MODEL) and w_in.shape[0] == 1
    xf = x.reshape(TOKENS, D_MODEL).astype(F32)

    c_pad = jnp.pad(c.astype(F32), ((0, 8 - BATCH), (0, 0)))
    b_ada_row = b_ada[0][None, :]
    mod = _ada_call(c_pad, w_ada[0], b_ada_row, 2 * D_MODEL)[:BATCH]
    shift = mod[:, :D_MODEL].reshape(BATCH, 1, D_MODEL)
    scale = mod[:, D_MODEL:].reshape(BATCH, 1, D_MODEL)

    w_t = w_in[0].T
    cos, sin = _rope_tables()
    p, g = _proj_call(xf, shift, scale, norm_w[0][None, :], w_t, w_t, cos, sin)

    row = lambda v: v[0][None, :].astype(F32)
    a, gate8, w_out_bf16 = _attn_call(
        p, row(lambda_q1), row(lambda_k1), row(lambda_q2), row(lambda_k2), row(diff_norm_w),
        c_pad, w_ada[0], b_ada_row, w_out[0])
    gate = gate8[:BATCH].reshape(BATCH, 1, D_MODEL)

    w2_heads = jnp.pad(gla_gate_w2[0], ((0, LANES - GLA_GATE_RANK), (0, 0))).astype(BF16)
    w2_heads = w2_heads.reshape(LANES, GLA_HEADS, GLA_K_DIM).transpose(1, 0, 2)
    b2_heads = gla_gate_b[0].astype(F32).reshape(GLA_HEADS, 1, GLA_K_DIM)
    b = _gla_call(p, g, w2_heads, b2_heads, row(gla_norm_w))

    out = _out_call(a, b, w_out_bf16, xf, gate, final_norm_w[None, :].astype(F32))
    return out.reshape(BATCH, SEQ, D_MODEL).astype(x.dtype)
```

```python
import math

import jax
import jax.numpy as jnp
import numpy as np
from jax import lax
from jax.experimental import pallas as pl
from jax.experimental.pallas import tpu as pltpu

D_MODEL = 2048
BATCH = 4
SEQ = 2048
TOKENS = BATCH * SEQ
CHUNK = 64
LANES = 128

DIFF_HEADS = 8
DIFF_HEAD_DIM = 128
DIFF_QK_DIM = 64
GLA_HEADS = 4
GLA_K_DIM = 128
GLA_V_DIM = 256
GLA_GATE_RANK = 16
GLA_GATE_TAU = 16.0
ROPE_THETA = 10000.0
NORM_EPS = 1e-6
LAMBDA_INIT = 0.8 - 0.6 * math.exp(-0.3 * 0)

N_MAIN = 7168
N_COLBLK = N_MAIN // LANES
CB_DQ, CB_DK, CB_DV, CB_DG = 0, 8, 16, 24
CB_GQ, CB_GK, CB_GV, CB_GG = 32, 36, 40, 48

VMEM_LIMIT = 48 * 1024 * 1024
NEG_BIG = -1e30
LOG2_E = math.log2(math.e)

BF16 = jnp.bfloat16
F32 = jnp.float32


def _nt_dot(a, b):
    return lax.dot_general(a, b, (((1,), (1,)), ((), ())), preferred_element_type=F32)


def _silu(x):
    h = 0.5 * x
    return h + h * jnp.tanh(h)


def _tn_dot(a, b):
    return lax.dot_general(a, b, (((0,), (0,)), ((), ())), preferred_element_type=F32)


ADA_TN = 1024


def _ada_block(c_ref, w_ref, b_ref):
    c = c_ref[...]
    c_act = (c * jax.nn.sigmoid(c)).astype(BF16)
    return jnp.dot(c_act, w_ref[...].astype(BF16), preferred_element_type=F32) + b_ref[...]


def _ada_kernel(c_ref, w_ref, b_ref, o_ref):
    o_ref[...] = _ada_block(c_ref, w_ref, b_ref)


def _ada_call(c_pad, w_ada, b_ada, n):
    return pl.pallas_call(
        _ada_kernel,
        grid=(n // ADA_TN,),
        in_specs=[
            pl.BlockSpec((8, D_MODEL), lambda j: (0, 0)),
            pl.BlockSpec((D_MODEL, ADA_TN), lambda j: (0, j)),
            pl.BlockSpec((1, ADA_TN), lambda j: (0, j)),
        ],
        out_specs=pl.BlockSpec((8, ADA_TN), lambda j: (0, j)),
        out_shape=jax.ShapeDtypeStruct((8, n), F32),
        compiler_params=pltpu.CompilerParams(
            dimension_semantics=("arbitrary",), vmem_limit_bytes=VMEM_LIMIT),
        name="ada_mod",
    )(c_pad, w_ada, b_ada)


PROJ_TM = 1024
PROJ_TN = 1024
PROJ_SUB = 512
PROJ_ROPE_TILES = 2
PROJ_SEQ_TILES = SEQ // PROJ_TM
PROJ_ROW_TILES = 2 * PROJ_SEQ_TILES
PROJ_VMEM_LIMIT = (2 * PROJ_TM * D_MODEL * 4 + 2 * D_MODEL * PROJ_TN * 4
                   + (1 + PROJ_ROW_TILES) * PROJ_TM * D_MODEL * 2
                   + 2 * PROJ_TM * PROJ_TN * 2 + 4 * 1024 * 1024)


def _rot_half(x):
    lane = lax.broadcasted_iota(jnp.int32, x.shape, 1)
    first = (lane % DIFF_QK_DIM) < (DIFF_QK_DIM // 2)
    return jnp.where(first, pltpu.roll(x, LANES - 32, 1), pltpu.roll(x, 32, 1))


def _proj_kernel(x_ref, shift_ref, scale_ref, nw_ref, w_ref, wg_ref, cos_ref, sin_ref,
                 p_ref, g_ref, hn_ref, hn_all_ref):
    j = pl.program_id(1)
    r = pl.program_id(2)
    n_sub = PROJ_TM // PROJ_SUB
    n_cb = PROJ_TN // LANES
    pos0 = pl.multiple_of((r % PROJ_SEQ_TILES) * PROJ_TM, PROJ_TM)

    def norm_rows(rows):
        mult = nw_ref[...] * (1.0 + scale_ref[0])
        xs = x_ref[rows, :]
        ms = jnp.mean(xs * xs, axis=-1, keepdims=True)
        hn_ref[rows, :] = (xs * lax.rsqrt(ms + NORM_EPS) * mult + shift_ref[0]).astype(BF16)

    def store_rope(sub, acc, sc):
        rows = pl.ds(sub * PROJ_SUB, PROJ_SUB)
        pos = pl.ds(pos0 + sub * PROJ_SUB, PROJ_SUB)
        cos = cos_ref[pos, :] * sc
        sin = sin_ref[pos, :] * sc
        for cb in range(n_cb):
            t = acc[:, cb * LANES:(cb + 1) * LANES]
            p_ref[cb, rows, :] = (t * cos + _rot_half(t) * sin).astype(BF16)

    @pl.when(j == 0)
    def _():
        for sub in range(n_sub):
            rows = pl.ds(sub * PROJ_SUB, PROJ_SUB)
            norm_rows(rows)
            acc = _nt_dot(hn_ref[rows, :], w_ref[...])
            store_rope(sub, acc, DIFF_QK_DIM ** -0.5 * LOG2_E)
        wg = jnp.concatenate(
            [wg_ref[...].astype(BF16), jnp.zeros((LANES - GLA_GATE_RANK, D_MODEL), BF16)], axis=0)
        g_ref[...] = _nt_dot(hn_ref[...], wg).astype(BF16)
        hn_all_ref[r] = hn_ref[...]

    @pl.when(j == 1)
    def _():
        for sub in range(n_sub):
            rows = pl.ds(sub * PROJ_SUB, PROJ_SUB)
            acc = _nt_dot(hn_all_ref[r, rows, :], w_ref[...])
            store_rope(sub, acc, 1.0)

    @pl.when(j >= PROJ_ROPE_TILES)
    def _():
        for sub in range(n_sub):
            rows = pl.ds(sub * PROJ_SUB, PROJ_SUB)
            acc = _nt_dot(hn_all_ref[r, rows, :], w_ref[...])
            for cb in range(n_cb):
                p_ref[cb, rows, :] = acc[:, cb * LANES:(cb + 1) * LANES].astype(BF16)


def _proj_call(xf, shift, scale, norm_w, w_t, w_gate, cos, sin):
    n_tiles = N_MAIN // PROJ_TN
    rt = PROJ_ROW_TILES
    n_cb = PROJ_TN // LANES
    x_tile = lambda b, j, r: (jnp.where(j == 0, b * rt + r, b * rt + rt - 1), 0)
    return pl.pallas_call(
        _proj_kernel,
        grid=(TOKENS // (rt * PROJ_TM), n_tiles, rt),
        in_specs=[
            pl.BlockSpec((PROJ_TM, D_MODEL), x_tile),
            pl.BlockSpec((1, 1, D_MODEL), lambda b, j, r: ((b * rt + r) // PROJ_SEQ_TILES, 0, 0)),
            pl.BlockSpec((1, 1, D_MODEL), lambda b, j, r: ((b * rt + r) // PROJ_SEQ_TILES, 0, 0)),
            pl.BlockSpec((1, D_MODEL), lambda b, j, r: (0, 0)),
            pl.BlockSpec((PROJ_TN, D_MODEL), lambda b, j, r: (j, 0)),
            pl.BlockSpec((GLA_GATE_RANK, D_MODEL), lambda b, j, r: (N_MAIN // GLA_GATE_RANK, 0)),
            pl.BlockSpec((SEQ, LANES), lambda b, j, r: (0, 0)),
            pl.BlockSpec((SEQ, LANES), lambda b, j, r: (0, 0)),
        ],
        out_specs=[
            pl.BlockSpec((n_cb, PROJ_TM, LANES), lambda b, j, r: (j, b * rt + r, 0)),
            pl.BlockSpec((PROJ_TM, LANES), x_tile),
        ],
        out_shape=[
            jax.ShapeDtypeStruct((N_COLBLK, TOKENS, LANES), BF16),
            jax.ShapeDtypeStruct((TOKENS, LANES), BF16),
        ],
        scratch_shapes=[
            pltpu.VMEM((PROJ_TM, D_MODEL), BF16),
            pltpu.VMEM((rt, PROJ_TM, D_MODEL), BF16),
        ],
        compiler_params=pltpu.CompilerParams(
            dimension_semantics=("arbitrary", "arbitrary", "arbitrary"),
            vmem_limit_bytes=PROJ_VMEM_LIMIT),
        name="norm_in_proj",
    )(xf, shift, scale, norm_w, w_t, w_gate, cos, sin)


ATT_T = 256
ATT_NBLK = SEQ // ATT_T
ATT_LEAD = 9
assert ATT_LEAD >= ATT_NBLK - 1
ATT_ONES_ROWS = 16
ATT_HEADS_PER_STEP = 2
ATT_SLOTS = ATT_LEAD + 4


def _attn_kernel(lq1_ref, lk1_ref, lq2_ref, lk2_ref, dnw_ref, q_ref, k_ref, v_ref, dg_ref,
                 c_ref, wa_ref, ba_ref, wo_ref, o_ref, gate_ref, wob_ref, vt_ref, s_ref):
    gate_ref[...] = _ada_block(c_ref, wa_ref, ba_ref)
    wob_ref[...] = wo_ref[...].astype(BF16)

    lam = (jnp.exp(jnp.sum(lq1_ref[...] * lk1_ref[...], keepdims=True))
           - jnp.exp(jnp.sum(lq2_ref[...] * lk2_ref[...], keepdims=True))
           + LAMBDA_INIT)
    out_gain = dnw_ref[...] * (1.0 - LAMBDA_INIT)

    for hd in range(ATT_HEADS_PER_STEP):
        for jb in range(ATT_NBLK):
            cols = pl.ds(jb * ATT_T, ATT_T)
            vb = v_ref[hd, jb * ATT_T:(jb + 1) * ATT_T, :].astype(F32)
            vt_ref[hd, 0:DIFF_HEAD_DIM, cols] = vb.T.astype(BF16)
        vt_ref[hd, DIFF_HEAD_DIM:, :] = jnp.ones((ATT_ONES_ROWS, SEQ), BF16)

    lane = lax.broadcasted_iota(jnp.int32, (ATT_T, LANES), 1)
    comp_a = lane < DIFF_QK_DIM
    krow = lax.broadcasted_iota(jnp.int32, (ATT_T, ATT_T), 0) // CHUNK
    qcol = lax.broadcasted_iota(jnp.int32, (ATT_T, ATT_T), 1) // CHUNK
    diag_mask = krow <= qcol
    diag_mask2 = jnp.concatenate([diag_mask, diag_mask], axis=1)

    units = [(hd, qi, j) for hd in range(ATT_HEADS_PER_STEP)
             for qi in range(ATT_NBLK) for j in range(qi + 1)]
    blocks = {(hd, qi): {"m": None}
              for hd in range(ATT_HEADS_PER_STEP) for qi in range(ATT_NBLK)}

    def block_queries(hd, qi):
        q = q_ref[hd, pl.ds(qi * ATT_T, ATT_T), :]
        zero = jnp.zeros_like(q)
        return jnp.concatenate([jnp.where(comp_a, q, zero), jnp.where(comp_a, zero, q)], axis=0)

    dyn0 = lax.shift_right_logical(pl.program_id(0), 20)

    def score_unit(t, hd, qi, j):
        st = blocks[hd, qi]
        if j == 0:
            st["q2"] = block_queries(hd, qi)
        s = _nt_dot(k_ref[hd, pl.ds(j * ATT_T, ATT_T), :], st["q2"])
        if j == qi:
            s = jnp.where(diag_mask2, s, NEG_BIG)
        s_ref[dyn0 + t % ATT_SLOTS] = s
        mj = jnp.max(s, axis=0, keepdims=True)
        st["m"] = mj if st["m"] is None else jnp.maximum(st["m"], mj)

    def value_unit(t, hd, qi, j):
        st = blocks[hd, qi]
        p = jnp.exp2(s_ref[dyn0 + t % ATT_SLOTS] - st["m"])
        pvj = jnp.dot(vt_ref[hd, :, j * ATT_T:(j + 1) * ATT_T], p.astype(BF16),
                      preferred_element_type=F32)
        st["pv"] = pvj if j == 0 else st["pv"] + pvj
        if j == qi:
            finish(hd, qi, st["pv"])

    def finish(hd, qi, pv_l):
        rows = pl.ds(qi * ATT_T, ATT_T)
        l = pv_l[DIFF_HEAD_DIM:DIFF_HEAD_DIM + 1, :]
        pv = pv_l[0:DIFF_HEAD_DIM, :] * (1.0 / l)
        out_t = pv[:, :ATT_T] - lam * pv[:, ATT_T:]
        out = out_t.T
        ms = jnp.mean(out * out, axis=-1, keepdims=True)
        y = out * lax.rsqrt(ms + NORM_EPS) * out_gain
        g = dg_ref[hd, rows, :].astype(F32)
        o_ref[hd, rows, :] = (y * _silu(g)).astype(BF16)

    for t in range(len(units) + ATT_LEAD):
        if t < len(units):
            score_unit(t, *units[t])
        if t >= ATT_LEAD:
            value_unit(t - ATT_LEAD, *units[t - ATT_LEAD])


def _attn_call(p, lq1, lk1, lq2, lk2, dnw, c_pad, w_ada, b_ada, w_out):
    hp = ATT_HEADS_PER_STEP
    steps_per_batch = DIFF_HEADS // hp
    side_w = D_MODEL // (BATCH * steps_per_batch)
    assert side_w % LANES == 0
    wo_rows = pl.BlockSpec((side_w, D_MODEL), lambda b, h: (b * steps_per_batch + h, 0))
    gate_cb0 = 2 * D_MODEL // side_w
    vec = lambda n: pl.BlockSpec((1, n), lambda b, h: (0, 0))
    slab = lambda cb0: pl.BlockSpec((hp, SEQ, LANES), lambda b, h: (cb0 // hp + h, b, 0))
    gate_blk = lambda rows, cb0: pl.BlockSpec(
        (rows, side_w), lambda b, h: (0, cb0 + b * steps_per_batch + h))
    return pl.pallas_call(
        _attn_kernel,
        grid=(BATCH, steps_per_batch),
        in_specs=[vec(DIFF_QK_DIM)] * 4 + [vec(DIFF_HEAD_DIM),
                  slab(CB_DQ), slab(CB_DK), slab(CB_DV), slab(CB_DG),
                  pl.BlockSpec((8, D_MODEL), lambda b, h: (0, 0)),
                  gate_blk(D_MODEL, gate_cb0), gate_blk(1, gate_cb0), wo_rows],
        out_specs=[pl.BlockSpec((hp, SEQ, LANES), lambda b, h: (h, b, 0)), gate_blk(8, 0),
                   wo_rows],
        out_shape=[jax.ShapeDtypeStruct((DIFF_HEADS, TOKENS, LANES), BF16),
                   jax.ShapeDtypeStruct((8, D_MODEL), F32),
                   jax.ShapeDtypeStruct((D_MODEL, D_MODEL), BF16)],
        scratch_shapes=[
            pltpu.VMEM((hp, DIFF_HEAD_DIM + ATT_ONES_ROWS, SEQ), BF16),
            pltpu.VMEM((ATT_SLOTS, ATT_T, 2 * ATT_T), F32),
        ],
        compiler_params=pltpu.CompilerParams(
            dimension_semantics=("arbitrary", "arbitrary"), vmem_limit_bytes=VMEM_LIMIT),
        name="diff_attn",
    )(lq1, lk1, lq2, lk2, dnw, p, p, p, p, c_pad, w_ada, b_ada, w_out)


GLA_BLK = 256
GLA_NCHUNK = SEQ // CHUNK
GLA_HEADS_PER_STEP = 2


def _split_hi_lo(x):
    hi = x.astype(BF16)
    lo = (x - hi.astype(F32)).astype(BF16)
    return hi, lo


def _gla_kernel(g_ref, w2_ref, b2_ref, gw_ref, q_ref, k_ref, v_ref, gg_ref, o_ref,
                kdec_ref, tot_ref, kv_ref, st_ref):
    for hd in range(GLA_HEADS_PER_STEP):
        _gla_head(g_ref, w2_ref.at[hd], b2_ref.at[hd], gw_ref, q_ref.at[hd], k_ref.at[hd],
                  v_ref.at[hd], gg_ref.at[hd], o_ref.at[hd], kdec_ref, tot_ref, kv_ref, st_ref)


def _gla_head(g_ref, w2_ref, b2_ref, gw_ref, q_ref, k_ref, v_ref, gg_ref, o_ref,
              kdec_ref, tot_ref, kv_ref, st_ref):
    z = jnp.dot(g_ref[...], w2_ref[...], preferred_element_type=F32) + b2_ref[...]
    log_a = (jnp.minimum(z, 0.0) - jnp.log(1.0 + jnp.exp(-jnp.abs(z)))) * (1.0 / GLA_GATE_TAU)

    r = lax.broadcasted_iota(jnp.int32, (GLA_BLK, GLA_BLK), 0)
    c = lax.broadcasted_iota(jnp.int32, (GLA_BLK, GLA_BLK), 1)
    after = jnp.where(((r // CHUNK) == (c // CHUNK)) & (c > r), 1.0, 0.0).astype(BF16)
    for blk in range(SEQ // GLA_BLK):
        rows = pl.ds(blk * GLA_BLK, GLA_BLK)
        la = log_a[blk * GLA_BLK:(blk + 1) * GLA_BLK, :]
        hi, lo = _split_hi_lo(la)
        both = jnp.dot(after, jnp.concatenate([hi, lo], axis=1), preferred_element_type=F32)
        suffix = both[:, :GLA_K_DIM] + both[:, GLA_K_DIM:]
        kdec_ref[rows, :] = (k_ref[rows, :].astype(F32) * jnp.exp(suffix)).astype(BF16)
        tot_ref[rows, :] = suffix + la

    dec_rows = jnp.exp(tot_ref[pl.ds(0, GLA_NCHUNK, stride=CHUNK), :])
    dec_cols = jnp.concatenate(
        [dec_rows, jnp.zeros((GLA_K_DIM - GLA_NCHUNK, GLA_K_DIM), F32)], axis=0).T

    dyn0 = lax.shift_right_logical(pl.program_id(0), 20)
    for ci in range(GLA_NCHUNK):
        rows = pl.ds(ci * CHUNK, CHUNK)
        v_c = jnp.concatenate([v_ref[0, rows, :], v_ref[1, rows, :]], axis=1)
        kv_ref[dyn0 + ci] = _tn_dot(kdec_ref[rows, :], v_c)

    state = jnp.zeros((GLA_K_DIM, GLA_V_DIM), F32)
    for ci in range(GLA_NCHUNK):
        state = state * dec_cols[:, ci:ci + 1] + kv_ref[dyn0 + ci]
        st_ref[ci] = state.astype(BF16)

    gw = gw_ref[...]
    qs = GLA_K_DIM ** -0.5
    for ci in range(GLA_NCHUNK):
        rows = pl.ds(ci * CHUNK, CHUNK)
        o = jnp.dot(q_ref[rows, :], st_ref[ci], preferred_element_type=F32)
        ms = jnp.mean(o * o, axis=-1, keepdims=True) * (qs * qs)
        y = o * (qs * lax.rsqrt(ms + NORM_EPS)) * gw
        gate = jnp.concatenate([gg_ref[0, rows, :], gg_ref[1, rows, :]], axis=1).astype(F32)
        y = y * _silu(gate)
        o_ref[0, rows, :] = y[:, :LANES].astype(BF16)
        o_ref[1, rows, :] = y[:, LANES:].astype(BF16)


def _gla_call(p, g, w2_heads, b2_heads, gw):
    hp = GLA_HEADS_PER_STEP
    qk = lambda cb0: pl.BlockSpec((hp, SEQ, LANES), lambda b, h: (cb0 // hp + h, b, 0))
    p4 = p.reshape(N_COLBLK // 2, 2, TOKENS, LANES)
    vg = lambda cb0: pl.BlockSpec((hp, 2, SEQ, LANES), lambda b, h: (cb0 // 2 // hp + h, 0, b, 0))
    out = pl.pallas_call(
        _gla_kernel,
        grid=(BATCH, GLA_HEADS // hp),
        in_specs=[
            pl.BlockSpec((SEQ, LANES), lambda b, h: (b, 0)),
            pl.BlockSpec((hp, LANES, GLA_K_DIM), lambda b, h: (h, 0, 0)),
            pl.BlockSpec((hp, 1, GLA_K_DIM), lambda b, h: (h, 0, 0)),
            pl.BlockSpec((1, GLA_V_DIM), lambda b, h: (0, 0)),
            qk(CB_GQ), qk(CB_GK), vg(CB_GV), vg(CB_GG),
        ],
        out_specs=pl.BlockSpec((hp, 2, SEQ, LANES), lambda b, h: (h, 0, b, 0)),
        out_shape=jax.ShapeDtypeStruct((GLA_HEADS, 2, TOKENS, LANES), BF16),
        scratch_shapes=[
            pltpu.VMEM((SEQ, GLA_K_DIM), BF16),
            pltpu.VMEM((SEQ, GLA_K_DIM), F32),
            pltpu.VMEM((GLA_NCHUNK, GLA_K_DIM, GLA_V_DIM), F32),
            pltpu.VMEM((GLA_NCHUNK, GLA_K_DIM, GLA_V_DIM), BF16),
        ],
        compiler_params=pltpu.CompilerParams(
            dimension_semantics=("arbitrary", "arbitrary"), vmem_limit_bytes=VMEM_LIMIT),
        name="gla_scan",
    )(g, w2_heads, b2_heads, gw, p, p, p4, p4)
    return out.reshape(2 * GLA_HEADS, TOKENS, LANES)


OUT_TM = 1024
OUT_SPLITS = (256, 256, 256, 256)
assert sum(OUT_SPLITS) == OUT_TM
OUT_VMEM_LIMIT = (4 * OUT_TM * D_MODEL * 4 + 2 * OUT_TM * D_MODEL * 2 + D_MODEL * D_MODEL * 2
                  + 8 * 1024 * 1024)


def _out_kernel(a_ref, b_ref, wb_ref, x_ref, gate_ref, fw_ref, o_ref):
    gate = gate_ref[0]
    fw = fw_ref[...]
    row0 = 0
    for n_rows in OUT_SPLITS:
        rows = pl.ds(row0, n_rows)
        row0 += n_rows
        mix = jnp.concatenate(
            [a_ref[h, rows, :] for h in range(a_ref.shape[0])]
            + [b_ref[h, rows, :] for h in range(b_ref.shape[0])], axis=1)
        y = jnp.dot(mix, wb_ref[...], preferred_element_type=F32)
        h_res = x_ref[rows, :] + gate * y
        ms = jnp.mean(h_res * h_res, axis=-1, keepdims=True)
        o_ref[rows, :] = h_res * lax.rsqrt(ms + NORM_EPS) * fw


def _out_call(a, b, w_out, xf, gate, fw):
    per_seq = SEQ // OUT_TM
    return pl.pallas_call(
        _out_kernel,
        grid=(TOKENS // OUT_TM,),
        in_specs=[
            pl.BlockSpec((DIFF_HEADS, OUT_TM, LANES), lambda i: (0, i, 0)),
            pl.BlockSpec((2 * GLA_HEADS, OUT_TM, LANES), lambda i: (0, i, 0)),
            pl.BlockSpec((D_MODEL, D_MODEL), lambda i: (0, 0), pipeline_mode=pl.Buffered(1)),
            pl.BlockSpec((OUT_TM, D_MODEL), lambda i: (i, 0)),
            pl.BlockSpec((1, 1, D_MODEL), lambda i: (i // per_seq, 0, 0)),
            pl.BlockSpec((1, D_MODEL), lambda i: (0, 0)),
        ],
        out_specs=pl.BlockSpec((OUT_TM, D_MODEL), lambda i: (i, 0)),
        out_shape=jax.ShapeDtypeStruct((TOKENS, D_MODEL), F32),
        compiler_params=pltpu.CompilerParams(
            dimension_semantics=("arbitrary",), vmem_limit_bytes=OUT_VMEM_LIMIT),
        name="out_proj_norm",
    )(a, b, w_out, xf, gate, fw)


def _rope_tables():
    inv_freq = ROPE_THETA ** (-np.arange(0, DIFF_QK_DIM, 2, dtype=np.float64) / DIFF_QK_DIM)
    ang = np.arange(SEQ, dtype=np.float64)[:, None] * inv_freq[None, :]
    cos, sin = np.cos(ang), np.sin(ang)
    cos_t = np.tile(cos, (1, 4)).astype(np.float32)
    sin_t = np.concatenate([-sin, sin, -sin, sin], axis=1).astype(np.float32)
    return jnp.asarray(cos_t), jnp.asarray(sin_t)


def kernel(x, c, norm_w, w_ada, b_ada, w_in, lambda_q1, lambda_k1, lambda_q2, lambda_k2,
           diff_norm_w, gla_gate_w2, gla_gate_b, gla_norm_w, w_out, final_norm_w):
    assert x.shape == (BATCH, SEQ, D_MODEL) and w_in.shape[0] == 1
    xf = x.reshape(TOKENS, D_MODEL).astype(F32)

    c_pad = jnp.pad(c.astype(F32), ((0, 8 - BATCH), (0, 0)))
    b_ada_row = b_ada[0][None, :]
    mod = _ada_call(c_pad, w_ada[0], b_ada_row, 2 * D_MODEL)[:BATCH]
    shift = mod[:, :D_MODEL].reshape(BATCH, 1, D_MODEL)
    scale = mod[:, D_MODEL:].reshape(BATCH, 1, D_MODEL)

    w_t = w_in[0].T
    cos, sin = _rope_tables()
    p, g = _proj_call(xf, shift, scale, norm_w[0][None, :], w_t, w_t, cos, sin)

    row = lambda v: v[0][None, :].astype(F32)
    a, gate8, w_out_bf16 = _attn_call(
        p, row(lambda_q1), row(lambda_k1), row(lambda_q2), row(lambda_k2), row(diff_norm_w),
        c_pad, w_ada[0], b_ada_row, w_out[0])
    gate = gate8[:BATCH].reshape(BATCH, 1, D_MODEL)

    w2_heads = jnp.pad(gla_gate_w2[0], ((0, LANES - GLA_GATE_RANK), (0, 0))).astype(BF16)
    w2_heads = w2_heads.reshape(LANES, GLA_HEADS, GLA_K_DIM).transpose(1, 0, 2)
    b2_heads = gla_gate_b[0].astype(F32).reshape(GLA_HEADS, 1, GLA_K_DIM)
    b = _gla_call(p, g, w2_heads, b2_heads, row(gla_norm_w))

    out = _out_call(a, b, w_out_bf16, xf, gate, final_norm_w[None, :].astype(F32))
    return out.reshape(BATCH, SEQ, D_MODEL).astype(x.dtype)
```

```python
import math

import jax
import jax.numpy as jnp
import numpy as np
from jax import lax
from jax.experimental import pallas as pl
from jax.experimental.pallas import tpu as pltpu

D_MODEL = 2048
BATCH = 4
SEQ = 2048
TOKENS = BATCH * SEQ
CHUNK = 64
LANES = 128

DIFF_HEADS = 8
DIFF_HEAD_DIM = 128
DIFF_QK_DIM = 64
GLA_HEADS = 4
GLA_K_DIM = 128
GLA_V_DIM = 256
GLA_GATE_RANK = 16
GLA_GATE_TAU = 16.0
ROPE_THETA = 10000.0
NORM_EPS = 1e-6
LAMBDA_INIT = 0.8 - 0.6 * math.exp(-0.3 * 0)

N_MAIN = 7168
N_COLBLK = N_MAIN // LANES
CB_DQ, CB_DK, CB_DV, CB_DG = 0, 8, 16, 24
CB_GQ, CB_GK, CB_GV, CB_GG = 32, 36, 40, 48

VMEM_LIMIT = 48 * 1024 * 1024
NEG_BIG = -1e30
LOG2_E = math.log2(math.e)

BF16 = jnp.bfloat16
F32 = jnp.float32


def _nt_dot(a, b):
    return lax.dot_general(a, b, (((1,), (1,)), ((), ())), preferred_element_type=F32)


def _silu(x):
    h = 0.5 * x
    return h + h * jnp.tanh(h)


def _tn_dot(a, b):
    return lax.dot_general(a, b, (((0,), (0,)), ((), ())), preferred_element_type=F32)


ADA_TN = 1024


def _ada_block(c_ref, w_ref, b_ref):
    c = c_ref[...]
    c_act = (c * jax.nn.sigmoid(c)).astype(BF16)
    return jnp.dot(c_act, w_ref[...].astype(BF16), preferred_element_type=F32) + b_ref[...]


def _ada_kernel(c_ref, w_ref, b_ref, o_ref):
    o_ref[...] = _ada_block(c_ref, w_ref, b_ref)


def _ada_call(c_pad, w_ada, b_ada, n):
    return pl.pallas_call(
        _ada_kernel,
        grid=(n // ADA_TN,),
        in_specs=[
            pl.BlockSpec((8, D_MODEL), lambda j: (0, 0)),
            pl.BlockSpec((D_MODEL, ADA_TN), lambda j: (0, j)),
            pl.BlockSpec((1, ADA_TN), lambda j: (0, j)),
        ],
        out_specs=pl.BlockSpec((8, ADA_TN), lambda j: (0, j)),
        out_shape=jax.ShapeDtypeStruct((8, n), F32),
        compiler_params=pltpu.CompilerParams(
            dimension_semantics=("arbitrary",), vmem_limit_bytes=VMEM_LIMIT),
        name="ada_mod",
    )(c_pad, w_ada, b_ada)


PROJ_TM = 1024
PROJ_TN = 1024
PROJ_SUB = 512
PROJ_ROPE_TILES = 2
PROJ_SEQ_TILES = SEQ // PROJ_TM
PROJ_ROW_TILES = 2 * PROJ_SEQ_TILES
PROJ_VMEM_LIMIT = (2 * PROJ_TM * D_MODEL * 4 + 2 * D_MODEL * PROJ_TN * 4
                   + (1 + PROJ_ROW_TILES) * PROJ_TM * D_MODEL * 2
                   + 2 * PROJ_TM * PROJ_TN * 2 + 4 * 1024 * 1024)


def _rot_half(x):
    lane = lax.broadcasted_iota(jnp.int32, x.shape, 1)
    first = (lane % DIFF_QK_DIM) < (DIFF_QK_DIM // 2)
    return jnp.where(first, pltpu.roll(x, LANES - 32, 1), pltpu.roll(x, 32, 1))


def _proj_kernel(x_ref, shift_ref, scale_ref, nw_ref, w_ref, wg_ref, cos_ref, sin_ref,
                 p_ref, g_ref, hn_ref, hn_all_ref):
    j = pl.program_id(1)
    r = pl.program_id(2)
    n_sub = PROJ_TM // PROJ_SUB
    n_cb = PROJ_TN // LANES
    pos0 = pl.multiple_of((r % PROJ_SEQ_TILES) * PROJ_TM, PROJ_TM)

    def norm_rows(rows):
        mult = nw_ref[...] * (1.0 + scale_ref[0])
        xs = x_ref[rows, :]
        ms = jnp.mean(xs * xs, axis=-1, keepdims=True)
        hn_ref[rows, :] = (xs * lax.rsqrt(ms + NORM_EPS) * mult + shift_ref[0]).astype(BF16)

    def store_rope(sub, acc, sc):
        rows = pl.ds(sub * PROJ_SUB, PROJ_SUB)
        pos = pl.ds(pos0 + sub * PROJ_SUB, PROJ_SUB)
        cos = cos_ref[pos, :] * sc
        sin = sin_ref[pos, :] * sc
        for cb in range(n_cb):
            t = acc[:, cb * LANES:(cb + 1) * LANES]
            p_ref[cb, rows, :] = (t * cos + _rot_half(t) * sin).astype(BF16)

    @pl.when(j == 0)
    def _():
        for sub in range(n_sub):
            rows = pl.ds(sub * PROJ_SUB, PROJ_SUB)
            norm_rows(rows)
            acc = _nt_dot(hn_ref[rows, :], w_ref[...])
            store_rope(sub, acc, DIFF_QK_DIM ** -0.5 * LOG2_E)
        wg = jnp.concatenate(
            [wg_ref[...].astype(BF16), jnp.zeros((LANES - GLA_GATE_RANK, D_MODEL), BF16)], axis=0)
        g_ref[...] = _nt_dot(hn_ref[...], wg).astype(BF16)
        hn_all_ref[r] = hn_ref[...]

    @pl.when(j == 1)
    def _():
        for sub in range(n_sub):
            rows = pl.ds(sub * PROJ_SUB, PROJ_SUB)
            acc = _nt_dot(hn_all_ref[r, rows, :], w_ref[...])
            store_rope(sub, acc, 1.0)

    @pl.when(j >= PROJ_ROPE_TILES)
    def _():
        for sub in range(n_sub):
            rows = pl.ds(sub * PROJ_SUB, PROJ_SUB)
            acc = _nt_dot(hn_all_ref[r, rows, :], w_ref[...])
            for cb in range(n_cb):
                p_ref[cb, rows, :] = acc[:, cb * LANES:(cb + 1) * LANES].astype(BF16)


def _proj_call(xf, shift, scale, norm_w, w_t, w_gate, cos, sin):
    n_tiles = N_MAIN // PROJ_TN
    rt = PROJ_ROW_TILES
    n_cb = PROJ_TN // LANES
    x_tile = lambda b, j, r: (jnp.where(j == 0, b * rt + r, b * rt + rt - 1), 0)
    return pl.pallas_call(
        _proj_kernel,
        grid=(TOKENS // (rt * PROJ_TM), n_tiles, rt),
        in_specs=[
            pl.BlockSpec((PROJ_TM, D_MODEL), x_tile),
            pl.BlockSpec((1, 1, D_MODEL), lambda b, j, r: ((b * rt + r) // PROJ_SEQ_TILES, 0, 0)),
            pl.BlockSpec((1, 1, D_MODEL), lambda b, j, r: ((b * rt + r) // PROJ_SEQ_TILES, 0, 0)),
            pl.BlockSpec((1, D_MODEL), lambda b, j, r: (0, 0)),
            pl.BlockSpec((PROJ_TN, D_MODEL), lambda b, j, r: (j, 0)),
            pl.BlockSpec((GLA_GATE_RANK, D_MODEL), lambda b, j, r: (N_MAIN // GLA_GATE_RANK, 0)),
            pl.BlockSpec((SEQ, LANES), lambda b, j, r: (0, 0)),
            pl.BlockSpec((SEQ, LANES), lambda b, j, r: (0, 0)),
        ],
        out_specs=[
            pl.BlockSpec((n_cb, PROJ_TM, LANES), lambda b, j, r: (j, b * rt + r, 0)),
            pl.BlockSpec((PROJ_TM, LANES), x_tile),
        ],
        out_shape=[
            jax.ShapeDtypeStruct((N_COLBLK, TOKENS, LANES), BF16),
            jax.ShapeDtypeStruct((TOKENS, LANES), BF16),
        ],
        scratch_shapes=[
            pltpu.VMEM((PROJ_TM, D_MODEL), BF16),
            pltpu.VMEM((rt, PROJ_TM, D_MODEL), BF16),
        ],
        compiler_params=pltpu.CompilerParams(
            dimension_semantics=("arbitrary", "arbitrary", "arbitrary"),
            vmem_limit_bytes=PROJ_VMEM_LIMIT),
        name="norm_in_proj",
    )(xf, shift, scale, norm_w, w_t, w_gate, cos, sin)


ATT_T = 256
ATT_NBLK = SEQ // ATT_T
ATT_LEAD = 9
assert ATT_LEAD >= ATT_NBLK - 1
ATT_ONES_ROWS = 16
ATT_HEADS_PER_STEP = 2
ATT_SLOTS = ATT_LEAD + 4


def _attn_kernel(lq1_ref, lk1_ref, lq2_ref, lk2_ref, dnw_ref, q_ref, k_ref, v_ref, dg_ref,
                 c_ref, wa_ref, ba_ref, wo_ref, o_ref, gate_ref, wob_ref, vt_ref, s_ref):
    gate_ref[...] = _ada_block(c_ref, wa_ref, ba_ref)
    wob_ref[...] = wo_ref[...].astype(BF16)

    lam = (jnp.exp(jnp.sum(lq1_ref[...] * lk1_ref[...], keepdims=True))
           - jnp.exp(jnp.sum(lq2_ref[...] * lk2_ref[...], keepdims=True))
           + LAMBDA_INIT)
    out_gain = dnw_ref[...] * (1.0 - LAMBDA_INIT)

    for hd in range(ATT_HEADS_PER_STEP):
        for jb in range(ATT_NBLK):
            cols = pl.ds(jb * ATT_T, ATT_T)
            vb = v_ref[hd, jb * ATT_T:(jb + 1) * ATT_T, :].astype(F32)
            vt_ref[hd, 0:DIFF_HEAD_DIM, cols] = vb.T.astype(BF16)
        vt_ref[hd, DIFF_HEAD_DIM:, :] = jnp.ones((ATT_ONES_ROWS, SEQ), BF16)

    lane = lax.broadcasted_iota(jnp.int32, (ATT_T, LANES), 1)
    comp_a = lane < DIFF_QK_DIM
    krow = lax.broadcasted_iota(jnp.int32, (ATT_T, ATT_T), 0) // CHUNK
    qcol = lax.broadcasted_iota(jnp.int32, (ATT_T, ATT_T), 1) // CHUNK
    diag_mask = krow <= qcol
    diag_mask2 = jnp.concatenate([diag_mask, diag_mask], axis=1)

    units = [(hd, qi, j) for hd in range(ATT_HEADS_PER_STEP)
             for qi in range(ATT_NBLK) for j in range(qi + 1)]
    blocks = {(hd, qi): {"m": None}
              for hd in range(ATT_HEADS_PER_STEP) for qi in range(ATT_NBLK)}

    def block_queries(hd, qi):
        q = q_ref[hd, pl.ds(qi * ATT_T, ATT_T), :]
        zero = jnp.zeros_like(q)
        return jnp.concatenate([jnp.where(comp_a, q, zero), jnp.where(comp_a, zero, q)], axis=0)

    dyn0 = lax.shift_right_logical(pl.program_id(0), 20)

    def score_unit(t, hd, qi, j):
        st = blocks[hd, qi]
        if j == 0:
            st["q2"] = block_queries(hd, qi)
        s = _nt_dot(k_ref[hd, pl.ds(j * ATT_T, ATT_T), :], st["q2"])
        if j == qi:
            s = jnp.where(diag_mask2, s, NEG_BIG)
        s_ref[dyn0 + t % ATT_SLOTS] = s
        mj = jnp.max(s, axis=0, keepdims=True)
        st["m"] = mj if st["m"] is None else jnp.maximum(st["m"], mj)

    def value_unit(t, hd, qi, j):
        st = blocks[hd, qi]
        p = jnp.exp2(s_ref[dyn0 + t % ATT_SLOTS] - st["m"])
        pvj = jnp.dot(vt_ref[hd, :, j * ATT_T:(j + 1) * ATT_T], p.astype(BF16),
                      preferred_element_type=F32)
        st["pv"] = pvj if j == 0 else st["pv"] + pvj
        if j == qi:
            finish(hd, qi, st["pv"])

    def finish(hd, qi, pv_l):
        rows = pl.ds(qi * ATT_T, ATT_T)
        l = pv_l[DIFF_HEAD_DIM:DIFF_HEAD_DIM + 1, :]
        pv = pv_l[0:DIFF_HEAD_DIM, :] * (1.0 / l)
        out_t = pv[:, :ATT_T] - lam * pv[:, ATT_T:]
        out = out_t.T
        ms = jnp.mean(out * out, axis=-1, keepdims=True)
        y = out * lax.rsqrt(ms + NORM_EPS) * out_gain
        g = dg_ref[hd, rows, :].astype(F32)
        o_ref[hd, rows, :] = (y * _silu(g)).astype(BF16)

    for t in range(len(units) + ATT_LEAD):
        if t < len(units):
            score_unit(t, *units[t])
        if t >= ATT_LEAD:
            value_unit(t - ATT_LEAD, *units[t - ATT_LEAD])


def _attn_call(p, lq1, lk1, lq2, lk2, dnw, c_pad, w_ada, b_ada, w_out):
    hp = ATT_HEADS_PER_STEP
    steps_per_batch = DIFF_HEADS // hp
    side_w = D_MODEL // (BATCH * steps_per_batch)
    assert side_w % LANES == 0
    wo_rows = pl.BlockSpec((side_w, D_MODEL), lambda b, h: (b * steps_per_batch + h, 0))
    gate_cb0 = 2 * D_MODEL // side_w
    vec = lambda n: pl.BlockSpec((1, n), lambda b, h: (0, 0))
    slab = lambda cb0: pl.BlockSpec((hp, SEQ, LANES), lambda b, h: (cb0 // hp + h, b, 0))
    gate_blk = lambda rows, cb0: pl.BlockSpec(
        (rows, side_w), lambda b, h: (0, cb0 + b * steps_per_batch + h))
    return pl.pallas_call(
        _attn_kernel,
        grid=(BATCH, steps_per_batch),
        in_specs=[vec(DIFF_QK_DIM)] * 4 + [vec(DIFF_HEAD_DIM),
                  slab(CB_DQ), slab(CB_DK), slab(CB_DV), slab(CB_DG),
                  pl.BlockSpec((8, D_MODEL), lambda b, h: (0, 0)),
                  gate_blk(D_MODEL, gate_cb0), gate_blk(1, gate_cb0), wo_rows],
        out_specs=[pl.BlockSpec((hp, SEQ, LANES), lambda b, h: (h, b, 0)), gate_blk(8, 0),
                   wo_rows],
        out_shape=[jax.ShapeDtypeStruct((DIFF_HEADS, TOKENS, LANES), BF16),
                   jax.ShapeDtypeStruct((8, D_MODEL), F32),
                   jax.ShapeDtypeStruct((D_MODEL, D_MODEL), BF16)],
        scratch_shapes=[
            pltpu.VMEM((hp, DIFF_HEAD_DIM + ATT_ONES_ROWS, SEQ), BF16),
            pltpu.VMEM((ATT_SLOTS, ATT_T, 2 * ATT_T), F32),
        ],
        compiler_params=pltpu.CompilerParams(
            dimension_semantics=("arbitrary", "arbitrary"), vmem_limit_bytes=VMEM_LIMIT),
        name="diff_attn",
    )(lq1, lk1, lq2, lk2, dnw, p, p, p, p, c_pad, w_ada, b_ada, w_out)


GLA_BLK = 256
GLA_NCHUNK = SEQ // CHUNK
GLA_HEADS_PER_STEP = 2


def _split_hi_lo(x):
    hi = x.astype(BF16)
    lo = (x - hi.astype(F32)).astype(BF16)
    return hi, lo


def _gla_kernel(g_ref, w2_ref, b2_ref, gw_ref, q_ref, k_ref, v_ref, gg_ref, o_ref,
                kdec_ref, tot_ref, kv_ref, st_ref):
    for hd in range(GLA_HEADS_PER_STEP):
        _gla_head(g_ref, w2_ref.at[hd], b2_ref.at[hd], gw_ref, q_ref.at[hd], k_ref.at[hd],
                  v_ref.at[hd], gg_ref.at[hd], o_ref.at[hd], kdec_ref, tot_ref, kv_ref, st_ref)


def _gla_head(g_ref, w2_ref, b2_ref, gw_ref, q_ref, k_ref, v_ref, gg_ref, o_ref,
              kdec_ref, tot_ref, kv_ref, st_ref):
    z = jnp.dot(g_ref[...], w2_ref[...], preferred_element_type=F32) + b2_ref[...]
    log_a = (jnp.minimum(z, 0.0) - jnp.log(1.0 + jnp.exp(-jnp.abs(z)))) * (1.0 / GLA_GATE_TAU)

    r = lax.broadcasted_iota(jnp.int32, (GLA_BLK, GLA_BLK), 0)
    c = lax.broadcasted_iota(jnp.int32, (GLA_BLK, GLA_BLK), 1)
    after = jnp.where(((r // CHUNK) == (c // CHUNK)) & (c > r), 1.0, 0.0).astype(BF16)
    for blk in range(SEQ // GLA_BLK):
        rows = pl.ds(blk * GLA_BLK, GLA_BLK)
        la = log_a[blk * GLA_BLK:(blk + 1) * GLA_BLK, :]
        hi, lo = _split_hi_lo(la)
        both = jnp.dot(after, jnp.concatenate([hi, lo], axis=1), preferred_element_type=F32)
        suffix = both[:, :GLA_K_DIM] + both[:, GLA_K_DIM:]
        kdec_ref[rows, :] = (k_ref[rows, :].astype(F32) * jnp.exp(suffix)).astype(BF16)
        tot_ref[rows, :] = suffix + la

    dec_rows = jnp.exp(tot_ref[pl.ds(0, GLA_NCHUNK, stride=CHUNK), :])
    dec_cols = jnp.concatenate(
        [dec_rows, jnp.zeros((GLA_K_DIM - GLA_NCHUNK, GLA_K_DIM), F32)], axis=0).T

    dyn0 = lax.shift_right_logical(pl.program_id(0), 20)
    for ci in range(GLA_NCHUNK):
        rows = pl.ds(ci * CHUNK, CHUNK)
        v_c = jnp.concatenate([v_ref[0, rows, :], v_ref[1, rows, :]], axis=1)
        kv_ref[dyn0 + ci] = _tn_dot(kdec_ref[rows, :], v_c)

    state = jnp.zeros((GLA_K_DIM, GLA_V_DIM), F32)
    for ci in range(GLA_NCHUNK):
        state = state * dec_cols[:, ci:ci + 1] + kv_ref[dyn0 + ci]
        st_ref[ci] = state.astype(BF16)

    gw = gw_ref[...]
    qs = GLA_K_DIM ** -0.5
    for ci in range(GLA_NCHUNK):
        rows = pl.ds(ci * CHUNK, CHUNK)
        o = jnp.dot(q_ref[rows, :], st_ref[ci], preferred_element_type=F32)
        ms = jnp.mean(o * o, axis=-1, keepdims=True) * (qs * qs)
        y = o * (qs * lax.rsqrt(ms + NORM_EPS)) * gw
        gate = jnp.concatenate([gg_ref[0, rows, :], gg_ref[1, rows, :]], axis=1).astype(F32)
        y = y * _silu(gate)
        o_ref[0, rows, :] = y[:, :LANES].astype(BF16)
        o_ref[1, rows, :] = y[:, LANES:].astype(BF16)


def _gla_call(p, g, w2_heads, b2_heads, gw):
    hp = GLA_HEADS_PER_STEP
    qk = lambda cb0: pl.BlockSpec((hp, SEQ, LANES), lambda b, h: (cb0 // hp + h, b, 0))
    p4 = p.reshape(N_COLBLK // 2, 2, TOKENS, LANES)
    vg = lambda cb0: pl.BlockSpec((hp, 2, SEQ, LANES), lambda b, h: (cb0 // 2 // hp + h, 0, b, 0))
    out = pl.pallas_call(
        _gla_kernel,
        grid=(BATCH, GLA_HEADS // hp),
        in_specs=[
            pl.BlockSpec((SEQ, LANES), lambda b, h: (b, 0)),
            pl.BlockSpec((hp, LANES, GLA_K_DIM), lambda b, h: (h, 0, 0)),
            pl.BlockSpec((hp, 1, GLA_K_DIM), lambda b, h: (h, 0, 0)),
            pl.BlockSpec((1, GLA_V_DIM), lambda b, h: (0, 0)),
            qk(CB_GQ), qk(CB_GK), vg(CB_GV), vg(CB_GG),
        ],
        out_specs=pl.BlockSpec((hp, 2, SEQ, LANES), lambda b, h: (h, 0, b, 0)),
        out_shape=jax.ShapeDtypeStruct((GLA_HEADS, 2, TOKENS, LANES), BF16),
        scratch_shapes=[
            pltpu.VMEM((SEQ, GLA_K_DIM), BF16),
            pltpu.VMEM((SEQ, GLA_K_DIM), F32),
            pltpu.VMEM((GLA_NCHUNK, GLA_K_DIM, GLA_V_DIM), F32),
            pltpu.VMEM((GLA_NCHUNK, GLA_K_DIM, GLA_V_DIM), BF16),
        ],
        compiler_params=pltpu.CompilerParams(
            dimension_semantics=("arbitrary", "arbitrary"), vmem_limit_bytes=VMEM_LIMIT),
        name="gla_scan",
    )(g, w2_heads, b2_heads, gw, p, p, p4, p4)
    return out.reshape(2 * GLA_HEADS, TOKENS, LANES)


OUT_TM = 1024
OUT_SPLITS = (512, 256, 256)
assert sum(OUT_SPLITS) == OUT_TM
OUT_VMEM_LIMIT = (4 * OUT_TM * D_MODEL * 4 + 2 * OUT_TM * D_MODEL * 2 + D_MODEL * D_MODEL * 2
                  + 8 * 1024 * 1024)


def _out_kernel(a_ref, b_ref, wb_ref, x_ref, gate_ref, fw_ref, o_ref):
    gate = gate_ref[0]
    fw = fw_ref[...]
    row0 = 0
    for n_rows in OUT_SPLITS:
        rows = pl.ds(row0, n_rows)
        row0 += n_rows
        mix = jnp.concatenate(
            [a_ref[h, rows, :] for h in range(a_ref.shape[0])]
            + [b_ref[h, rows, :] for h in range(b_ref.shape[0])], axis=1)
        y = jnp.dot(mix, wb_ref[...], preferred_element_type=F32)
        h_res = x_ref[rows, :] + gate * y
        ms = jnp.mean(h_res * h_res, axis=-1, keepdims=True)
        o_ref[rows, :] = h_res * lax.rsqrt(ms + NORM_EPS) * fw


def _out_call(a, b, w_out, xf, gate, fw):
    per_seq = SEQ // OUT_TM
    return pl.pallas_call(
        _out_kernel,
        grid=(TOKENS // OUT_TM,),
        in_specs=[
            pl.BlockSpec((DIFF_HEADS, OUT_TM, LANES), lambda i: (0, i, 0)),
            pl.BlockSpec((2 * GLA_HEADS, OUT_TM, LANES), lambda i: (0, i, 0)),
            pl.BlockSpec((D_MODEL, D_MODEL), lambda i: (0, 0), pipeline_mode=pl.Buffered(1)),
            pl.BlockSpec((OUT_TM, D_MODEL), lambda i: (i, 0)),
            pl.BlockSpec((1, 1, D_MODEL), lambda i: (i // per_seq, 0, 0)),
            pl.BlockSpec((1, D_MODEL), lambda i: (0, 0)),
        ],
        out_specs=pl.BlockSpec((OUT_TM, D_MODEL), lambda i: (i, 0)),
        out_shape=jax.ShapeDtypeStruct((TOKENS, D_MODEL), F32),
        compiler_params=pltpu.CompilerParams(
            dimension_semantics=("arbitrary",), vmem_limit_bytes=OUT_VMEM_LIMIT),
        name="out_proj_norm",
    )(a, b, w_out, xf, gate, fw)


def _rope_tables():
    inv_freq = ROPE_THETA ** (-np.arange(0, DIFF_QK_DIM, 2, dtype=np.float64) / DIFF_QK_DIM)
    ang = np.arange(SEQ, dtype=np.float64)[:, None] * inv_freq[None, :]
    cos, sin = np.cos(ang), np.sin(ang)
    cos_t = np.tile(cos, (1, 4)).astype(np.float32)
    sin_t = np.concatenate([-sin, sin, -sin, sin], axis=1).astype(np.float32)
    return jnp.asarray(cos_t), jnp.asarray(sin_t)


def kernel(x, c, norm_w, w_ada, b_ada, w_in, lambda_q1, lambda_k1, lambda_q2, lambda_k2,
           diff_norm_w, gla_gate_w2, gla_gate_b, gla_norm_w, w_out, final_norm_w):
    assert x.shape == (BATCH, SEQ, D_MODEL) and w_in.shape[0] == 1
    xf = x.reshape(TOKENS, D_MODEL).astype(F32)

    c_pad = jnp.pad(c.astype(F32), ((0, 8 - BATCH), (0, 0)))
    b_ada_row = b_ada[0][None, :]
    mod = _ada_call(c_pad, w_ada[0], b_ada_row, 2 * D_MODEL)[:BATCH]
    shift = mod[:, :D_MODEL].reshape(BATCH, 1, D_MODEL)
    scale = mod[:, D_MODEL:].reshape(BATCH, 1, D_MODEL)

    w_t = w_in[0].T
    cos, sin = _rope_tables()
    p, g = _proj_call(xf, shift, scale, norm_w[0][None, :], w_t, w_t, cos, sin)

    row = lambda v: v[0][None, :].astype(F32)
    a, gate8, w_out_bf16 = _attn_call(
        p, row(lambda_q1), row(lambda_k1), row(lambda_q2), row(lambda_k2), row(diff_norm_w),
        c_pad, w_ada[0], b_ada_row, w_out[0])
    gate = gate8[:BATCH].reshape(BATCH, 1, D_MODEL)

    w2_heads = jnp.pad(gla_gate_w2[0], ((0, LANES - GLA_GATE_RANK), (0, 0))).astype(BF16)
    w2_heads = w2_heads.reshape(LANES, GLA_HEADS, GLA_K_DIM).transpose(1, 0, 2)
    b2_heads = gla_gate_b[0].astype(F32).reshape(GLA_HEADS, 1, GLA_K_DIM)
    b = _gla_call(p, g, w2_heads, b2_heads, row(gla_norm_w))

    out = _out_call(a, b, w_out_bf16, xf, gate, final_norm_w[None, :].astype(F32))
    return out.reshape(BATCH, SEQ, D_MODEL).astype(x.dtype)
```

```python
import math

import jax
import jax.numpy as jnp
import numpy as np
from jax import lax
from jax.experimental import pallas as pl
from jax.experimental.pallas import tpu as pltpu

D_MODEL = 2048
BATCH = 4
SEQ = 2048
TOKENS = BATCH * SEQ
CHUNK = 64
LANES = 128

DIFF_HEADS = 8
DIFF_HEAD_DIM = 128
DIFF_QK_DIM = 64
GLA_HEADS = 4
GLA_K_DIM = 128
GLA_V_DIM = 256
GLA_GATE_RANK = 16
GLA_GATE_TAU = 16.0
ROPE_THETA = 10000.0
NORM_EPS = 1e-6
LAMBDA_INIT = 0.8 - 0.6 * math.exp(-0.3 * 0)

N_MAIN = 7168
N_COLBLK = N_MAIN // LANES
CB_DQ, CB_DK, CB_DV, CB_DG = 0, 8, 16, 24
CB_GQ, CB_GK, CB_GV, CB_GG = 32, 36, 40, 48

VMEM_LIMIT = 48 * 1024 * 1024
NEG_BIG = -1e30
LOG2_E = math.log2(math.e)

BF16 = jnp.bfloat16
F32 = jnp.float32


def _nt_dot(a, b):
    return lax.dot_general(a, b, (((1,), (1,)), ((), ())), preferred_element_type=F32)


def _silu(x):
    h = 0.5 * x
    return h + h * jnp.tanh(h)


def _tn_dot(a, b):
    return lax.dot_general(a, b, (((0,), (0,)), ((), ())), preferred_element_type=F32)


ADA_TN = 1024


def _ada_block(c_ref, w_ref, b_ref):
    c = c_ref[...]
    c_act = (c * jax.nn.sigmoid(c)).astype(BF16)
    return jnp.dot(c_act, w_ref[...].astype(BF16), preferred_element_type=F32) + b_ref[...]


def _ada_kernel(c_ref, w_ref, b_ref, o_ref):
    o_ref[...] = _ada_block(c_ref, w_ref, b_ref)


def _ada_call(c_pad, w_ada, b_ada, n):
    return pl.pallas_call(
        _ada_kernel,
        grid=(n // ADA_TN,),
        in_specs=[
            pl.BlockSpec((8, D_MODEL), lambda j: (0, 0)),
            pl.BlockSpec((D_MODEL, ADA_TN), lambda j: (0, j)),
            pl.BlockSpec((1, ADA_TN), lambda j: (0, j)),
        ],
        out_specs=pl.BlockSpec((8, ADA_TN), lambda j: (0, j)),
        out_shape=jax.ShapeDtypeStruct((8, n), F32),
        compiler_params=pltpu.CompilerParams(
            dimension_semantics=("arbitrary",), vmem_limit_bytes=VMEM_LIMIT),
        name="ada_mod",
    )(c_pad, w_ada, b_ada)


PROJ_TM = 1024
PROJ_TN = 1024
PROJ_SUB = 512
PROJ_ROPE_TILES = 2
PROJ_SEQ_TILES = SEQ // PROJ_TM
PROJ_ROW_TILES = 2 * PROJ_SEQ_TILES
PROJ_VMEM_LIMIT = (2 * PROJ_TM * D_MODEL * 4 + 2 * D_MODEL * PROJ_TN * 4
                   + (1 + PROJ_ROW_TILES) * PROJ_TM * D_MODEL * 2
                   + 2 * PROJ_TM * PROJ_TN * 2 + 4 * 1024 * 1024)


def _rot_half(x):
    lane = lax.broadcasted_iota(jnp.int32, x.shape, 1)
    first = (lane % DIFF_QK_DIM) < (DIFF_QK_DIM // 2)
    return jnp.where(first, pltpu.roll(x, LANES - 32, 1), pltpu.roll(x, 32, 1))


def _proj_kernel(x_ref, shift_ref, scale_ref, nw_ref, w_ref, wg_ref, cos_ref, sin_ref,
                 p_ref, g_ref, hn_ref, hn_all_ref):
    j = pl.program_id(1)
    r = pl.program_id(2)
    n_sub = PROJ_TM // PROJ_SUB
    n_cb = PROJ_TN // LANES
    pos0 = pl.multiple_of((r % PROJ_SEQ_TILES) * PROJ_TM, PROJ_TM)

    def norm_rows(rows):
        mult = nw_ref[...] * (1.0 + scale_ref[0])
        xs = x_ref[rows, :]
        ms = jnp.mean(xs * xs, axis=-1, keepdims=True)
        hn_ref[rows, :] = (xs * lax.rsqrt(ms + NORM_EPS) * mult + shift_ref[0]).astype(BF16)

    def store_rope(sub, acc, sc):
        rows = pl.ds(sub * PROJ_SUB, PROJ_SUB)
        pos = pl.ds(pos0 + sub * PROJ_SUB, PROJ_SUB)
        cos = cos_ref[pos, :] * sc
        sin = sin_ref[pos, :] * sc
        for cb in range(n_cb):
            t = acc[:, cb * LANES:(cb + 1) * LANES]
            p_ref[cb, rows, :] = (t * cos + _rot_half(t) * sin).astype(BF16)

    @pl.when(j == 0)
    def _():
        for sub in range(n_sub):
            rows = pl.ds(sub * PROJ_SUB, PROJ_SUB)
            norm_rows(rows)
            acc = _nt_dot(hn_ref[rows, :], w_ref[...])
            store_rope(sub, acc, DIFF_QK_DIM ** -0.5 * LOG2_E)
        wg = jnp.concatenate(
            [wg_ref[...].astype(BF16), jnp.zeros((LANES - GLA_GATE_RANK, D_MODEL), BF16)], axis=0)
        g_ref[...] = _nt_dot(hn_ref[...], wg).astype(BF16)
        hn_all_ref[r] = hn_ref[...]

    @pl.when(j == 1)
    def _():
        for sub in range(n_sub):
            rows = pl.ds(sub * PROJ_SUB, PROJ_SUB)
            acc = _nt_dot(hn_all_ref[r, rows, :], w_ref[...])
            store_rope(sub, acc, 1.0)

    @pl.when(j >= PROJ_ROPE_TILES)
    def _():
        for sub in range(n_sub):
            rows = pl.ds(sub * PROJ_SUB, PROJ_SUB)
            acc = _nt_dot(hn_all_ref[r, rows, :], w_ref[...])
            for cb in range(n_cb):
                p_ref[cb, rows, :] = acc[:, cb * LANES:(cb + 1) * LANES].astype(BF16)


def _proj_call(xf, shift, scale, norm_w, w_t, w_gate, cos, sin):
    n_tiles = N_MAIN // PROJ_TN
    rt = PROJ_ROW_TILES
    n_cb = PROJ_TN // LANES
    x_tile = lambda b, j, r: (jnp.where(j == 0, b * rt + r, b * rt + rt - 1), 0)
    return pl.pallas_call(
        _proj_kernel,
        grid=(TOKENS // (rt * PROJ_TM), n_tiles, rt),
        in_specs=[
            pl.BlockSpec((PROJ_TM, D_MODEL), x_tile),
            pl.BlockSpec((1, 1, D_MODEL), lambda b, j, r: ((b * rt + r) // PROJ_SEQ_TILES, 0, 0)),
            pl.BlockSpec((1, 1, D_MODEL), lambda b, j, r: ((b * rt + r) // PROJ_SEQ_TILES, 0, 0)),
            pl.BlockSpec((1, D_MODEL), lambda b, j, r: (0, 0)),
            pl.BlockSpec((PROJ_TN, D_MODEL), lambda b, j, r: (j, 0)),
            pl.BlockSpec((GLA_GATE_RANK, D_MODEL), lambda b, j, r: (N_MAIN // GLA_GATE_RANK, 0)),
            pl.BlockSpec((SEQ, LANES), lambda b, j, r: (0, 0)),
            pl.BlockSpec((SEQ, LANES), lambda b, j, r: (0, 0)),
        ],
        out_specs=[
            pl.BlockSpec((n_cb, PROJ_TM, LANES), lambda b, j, r: (j, b * rt + r, 0)),
            pl.BlockSpec((PROJ_TM, LANES), x_tile),
        ],
        out_shape=[
            jax.ShapeDtypeStruct((N_COLBLK, TOKENS, LANES), BF16),
            jax.ShapeDtypeStruct((TOKENS, LANES), BF16),
        ],
        scratch_shapes=[
            pltpu.VMEM((PROJ_TM, D_MODEL), BF16),
            pltpu.VMEM((rt, PROJ_TM, D_MODEL), BF16),
        ],
        compiler_params=pltpu.CompilerParams(
            dimension_semantics=("arbitrary", "arbitrary", "arbitrary"),
            vmem_limit_bytes=PROJ_VMEM_LIMIT),
        name="norm_in_proj",
    )(xf, shift, scale, norm_w, w_t, w_gate, cos, sin)


ATT_T = 256
ATT_NBLK = SEQ // ATT_T
ATT_LEAD = 12
assert ATT_LEAD >= ATT_NBLK - 1
ATT_ONES_ROWS = 16
ATT_HEADS_PER_STEP = 2
ATT_SLOTS = ATT_LEAD + 4


def _attn_kernel(lq1_ref, lk1_ref, lq2_ref, lk2_ref, dnw_ref, q_ref, k_ref, v_ref, dg_ref,
                 c_ref, wa_ref, ba_ref, wo_ref, o_ref, gate_ref, wob_ref, vt_ref, s_ref):
    gate_ref[...] = _ada_block(c_ref, wa_ref, ba_ref)
    wob_ref[...] = wo_ref[...].astype(BF16)

    lam = (jnp.exp(jnp.sum(lq1_ref[...] * lk1_ref[...], keepdims=True))
           - jnp.exp(jnp.sum(lq2_ref[...] * lk2_ref[...], keepdims=True))
           + LAMBDA_INIT)
    out_gain = dnw_ref[...] * (1.0 - LAMBDA_INIT)

    for hd in range(ATT_HEADS_PER_STEP):
        for jb in range(ATT_NBLK):
            cols = pl.ds(jb * ATT_T, ATT_T)
            vb = v_ref[hd, jb * ATT_T:(jb + 1) * ATT_T, :].astype(F32)
            vt_ref[hd, 0:DIFF_HEAD_DIM, cols] = vb.T.astype(BF16)
        vt_ref[hd, DIFF_HEAD_DIM:, :] = jnp.ones((ATT_ONES_ROWS, SEQ), BF16)

    lane = lax.broadcasted_iota(jnp.int32, (ATT_T, LANES), 1)
    comp_a = lane < DIFF_QK_DIM
    krow = lax.broadcasted_iota(jnp.int32, (ATT_T, ATT_T), 0) // CHUNK
    qcol = lax.broadcasted_iota(jnp.int32, (ATT_T, ATT_T), 1) // CHUNK
    diag_mask = krow <= qcol
    diag_mask2 = jnp.concatenate([diag_mask, diag_mask], axis=1)

    units = [(hd, qi, j) for hd in range(ATT_HEADS_PER_STEP)
             for qi in range(ATT_NBLK) for j in range(qi + 1)]
    blocks = {(hd, qi): {"m": None}
              for hd in range(ATT_HEADS_PER_STEP) for qi in range(ATT_NBLK)}

    def block_queries(hd, qi):
        q = q_ref[hd, pl.ds(qi * ATT_T, ATT_T), :]
        zero = jnp.zeros_like(q)
        return jnp.concatenate([jnp.where(comp_a, q, zero), jnp.where(comp_a, zero, q)], axis=0)

    dyn0 = lax.shift_right_logical(pl.program_id(0), 20)

    def score_unit(t, hd, qi, j):
        st = blocks[hd, qi]
        if j == 0:
            st["q2"] = block_queries(hd, qi)
        s = _nt_dot(k_ref[hd, pl.ds(j * ATT_T, ATT_T), :], st["q2"])
        if j == qi:
            s = jnp.where(diag_mask2, s, NEG_BIG)
        s_ref[dyn0 + t % ATT_SLOTS] = s
        mj = jnp.max(s, axis=0, keepdims=True)
        st["m"] = mj if st["m"] is None else jnp.maximum(st["m"], mj)

    def value_unit(t, hd, qi, j):
        st = blocks[hd, qi]
        p = jnp.exp2(s_ref[dyn0 + t % ATT_SLOTS] - st["m"])
        pvj = jnp.dot(vt_ref[hd, :, j * ATT_T:(j + 1) * ATT_T], p.astype(BF16),
                      preferred_element_type=F32)
        st["pv"] = pvj if j == 0 else st["pv"] + pvj
        if j == qi:
            finish(hd, qi, st["pv"])

    def finish(hd, qi, pv_l):
        rows = pl.ds(qi * ATT_T, ATT_T)
        l = pv_l[DIFF_HEAD_DIM:DIFF_HEAD_DIM + 1, :]
        pv = pv_l[0:DIFF_HEAD_DIM, :] * (1.0 / l)
        out_t = pv[:, :ATT_T] - lam * pv[:, ATT_T:]
        out = out_t.T
        ms = jnp.mean(out * out, axis=-1, keepdims=True)
        y = out * lax.rsqrt(ms + NORM_EPS) * out_gain
        g = dg_ref[hd, rows, :].astype(F32)
        o_ref[hd, rows, :] = (y * _silu(g)).astype(BF16)

    for t in range(len(units) + ATT_LEAD):
        if t < len(units):
            score_unit(t, *units[t])
        if t >= ATT_LEAD:
            value_unit(t - ATT_LEAD, *units[t - ATT_LEAD])


def _attn_call(p, lq1, lk1, lq2, lk2, dnw, c_pad, w_ada, b_ada, w_out):
    hp = ATT_HEADS_PER_STEP
    steps_per_batch = DIFF_HEADS // hp
    side_w = D_MODEL // (BATCH * steps_per_batch)
    assert side_w % LANES == 0
    wo_rows = pl.BlockSpec((side_w, D_MODEL), lambda b, h: (b * steps_per_batch + h, 0))
    gate_cb0 = 2 * D_MODEL // side_w
    vec = lambda n: pl.BlockSpec((1, n), lambda b, h: (0, 0))
    slab = lambda cb0: pl.BlockSpec((hp, SEQ, LANES), lambda b, h: (cb0 // hp + h, b, 0))
    gate_blk = lambda rows, cb0: pl.BlockSpec(
        (rows, side_w), lambda b, h: (0, cb0 + b * steps_per_batch + h))
    return pl.pallas_call(
        _attn_kernel,
        grid=(BATCH, steps_per_batch),
        in_specs=[vec(DIFF_QK_DIM)] * 4 + [vec(DIFF_HEAD_DIM),
                  slab(CB_DQ), slab(CB_DK), slab(CB_DV), slab(CB_DG),
                  pl.BlockSpec((8, D_MODEL), lambda b, h: (0, 0)),
                  gate_blk(D_MODEL, gate_cb0), gate_blk(1, gate_cb0), wo_rows],
        out_specs=[pl.BlockSpec((hp, SEQ, LANES), lambda b, h: (h, b, 0)), gate_blk(8, 0),
                   wo_rows],
        out_shape=[jax.ShapeDtypeStruct((DIFF_HEADS, TOKENS, LANES), BF16),
                   jax.ShapeDtypeStruct((8, D_MODEL), F32),
                   jax.ShapeDtypeStruct((D_MODEL, D_MODEL), BF16)],
        scratch_shapes=[
            pltpu.VMEM((hp, DIFF_HEAD_DIM + ATT_ONES_ROWS, SEQ), BF16),
            pltpu.VMEM((ATT_SLOTS, ATT_T, 2 * ATT_T), F32),
        ],
        compiler_params=pltpu.CompilerParams(
            dimension_semantics=("arbitrary", "arbitrary"), vmem_limit_bytes=VMEM_LIMIT),
        name="diff_attn",
    )(lq1, lk1, lq2, lk2, dnw, p, p, p, p, c_pad, w_ada, b_ada, w_out)


GLA_BLK = 256
GLA_NCHUNK = SEQ // CHUNK
GLA_HEADS_PER_STEP = 2


def _split_hi_lo(x):
    hi = x.astype(BF16)
    lo = (x - hi.astype(F32)).astype(BF16)
    return hi, lo


def _gla_kernel(g_ref, w2_ref, b2_ref, gw_ref, q_ref, k_ref, v_ref, gg_ref, o_ref,
                kdec_ref, tot_ref, kv_ref, st_ref):
    for hd in range(GLA_HEADS_PER_STEP):
        _gla_head(g_ref, w2_ref.at[hd], b2_ref.at[hd], gw_ref, q_ref.at[hd], k_ref.at[hd],
                  v_ref.at[hd], gg_ref.at[hd], o_ref.at[hd], kdec_ref, tot_ref, kv_ref, st_ref)


def _gla_head(g_ref, w2_ref, b2_ref, gw_ref, q_ref, k_ref, v_ref, gg_ref, o_ref,
              kdec_ref, tot_ref, kv_ref, st_ref):
    z = jnp.dot(g_ref[...], w2_ref[...], preferred_element_type=F32) + b2_ref[...]
    log_a = (jnp.minimum(z, 0.0) - jnp.log(1.0 + jnp.exp(-jnp.abs(z)))) * (1.0 / GLA_GATE_TAU)

    r = lax.broadcasted_iota(jnp.int32, (GLA_BLK, GLA_BLK), 0)
    c = lax.broadcasted_iota(jnp.int32, (GLA_BLK, GLA_BLK), 1)
    after = jnp.where(((r // CHUNK) == (c // CHUNK)) & (c > r), 1.0, 0.0).astype(BF16)
    for blk in range(SEQ // GLA_BLK):
        rows = pl.ds(blk * GLA_BLK, GLA_BLK)
        la = log_a[blk * GLA_BLK:(blk + 1) * GLA_BLK, :]
        hi, lo = _split_hi_lo(la)
        both = jnp.dot(after, jnp.concatenate([hi, lo], axis=1), preferred_element_type=F32)
        suffix = both[:, :GLA_K_DIM] + both[:, GLA_K_DIM:]
        kdec_ref[rows, :] = (k_ref[rows, :].astype(F32) * jnp.exp(suffix)).astype(BF16)
        tot_ref[rows, :] = suffix + la

    dec_rows = jnp.exp(tot_ref[pl.ds(0, GLA_NCHUNK, stride=CHUNK), :])
    dec_cols = jnp.concatenate(
        [dec_rows, jnp.zeros((GLA_K_DIM - GLA_NCHUNK, GLA_K_DIM), F32)], axis=0).T

    dyn0 = lax.shift_right_logical(pl.program_id(0), 20)
    for ci in range(GLA_NCHUNK):
        rows = pl.ds(ci * CHUNK, CHUNK)
        v_c = jnp.concatenate([v_ref[0, rows, :], v_ref[1, rows, :]], axis=1)
        kv_ref[dyn0 + ci] = _tn_dot(kdec_ref[rows, :], v_c)

    state = jnp.zeros((GLA_K_DIM, GLA_V_DIM), F32)
    for ci in range(GLA_NCHUNK):
        state = state * dec_cols[:, ci:ci + 1] + kv_ref[dyn0 + ci]
        st_ref[ci] = state.astype(BF16)

    gw = gw_ref[...]
    qs = GLA_K_DIM ** -0.5
    for ci in range(GLA_NCHUNK):
        rows = pl.ds(ci * CHUNK, CHUNK)
        o = jnp.dot(q_ref[rows, :], st_ref[ci], preferred_element_type=F32)
        ms = jnp.mean(o * o, axis=-1, keepdims=True) * (qs * qs)
        y = o * (qs * lax.rsqrt(ms + NORM_EPS)) * gw
        gate = jnp.concatenate([gg_ref[0, rows, :], gg_ref[1, rows, :]], axis=1).astype(F32)
        y = y * _silu(gate)
        o_ref[0, rows, :] = y[:, :LANES].astype(BF16)
        o_ref[1, rows, :] = y[:, LANES:].astype(BF16)


def _gla_call(p, g, w2_heads, b2_heads, gw):
    hp = GLA_HEADS_PER_STEP
    qk = lambda cb0: pl.BlockSpec((hp, SEQ, LANES), lambda b, h: (cb0 // hp + h, b, 0))
    p4 = p.reshape(N_COLBLK // 2, 2, TOKENS, LANES)
    vg = lambda cb0: pl.BlockSpec((hp, 2, SEQ, LANES), lambda b, h: (cb0 // 2 // hp + h, 0, b, 0))
    out = pl.pallas_call(
        _gla_kernel,
        grid=(BATCH, GLA_HEADS // hp),
        in_specs=[
            pl.BlockSpec((SEQ, LANES), lambda b, h: (b, 0)),
            pl.BlockSpec((hp, LANES, GLA_K_DIM), lambda b, h: (h, 0, 0)),
            pl.BlockSpec((hp, 1, GLA_K_DIM), lambda b, h: (h, 0, 0)),
            pl.BlockSpec((1, GLA_V_DIM), lambda b, h: (0, 0)),
            qk(CB_GQ), qk(CB_GK), vg(CB_GV), vg(CB_GG),
        ],
        out_specs=pl.BlockSpec((hp, 2, SEQ, LANES), lambda b, h: (h, 0, b, 0)),
        out_shape=jax.ShapeDtypeStruct((GLA_HEADS, 2, TOKENS, LANES), BF16),
        scratch_shapes=[
            pltpu.VMEM((SEQ, GLA_K_DIM), BF16),
            pltpu.VMEM((SEQ, GLA_K_DIM), F32),
            pltpu.VMEM((GLA_NCHUNK, GLA_K_DIM, GLA_V_DIM), F32),
            pltpu.VMEM((GLA_NCHUNK, GLA_K_DIM, GLA_V_DIM), BF16),
        ],
        compiler_params=pltpu.CompilerParams(
            dimension_semantics=("arbitrary", "arbitrary"), vmem_limit_bytes=VMEM_LIMIT),
        name="gla_scan",
    )(g, w2_heads, b2_heads, gw, p, p, p4, p4)
    return out.reshape(2 * GLA_HEADS, TOKENS, LANES)


OUT_TM = 1024
OUT_SPLITS = (256, 256, 256, 256)
assert sum(OUT_SPLITS) == OUT_TM
OUT_VMEM_LIMIT = (4 * OUT_TM * D_MODEL * 4 + 2 * OUT_TM * D_MODEL * 2 + D_MODEL * D_MODEL * 2
                  + 8 * 1024 * 1024)


def _out_kernel(a_ref, b_ref, wb_ref, x_ref, gate_ref, fw_ref, o_ref):
    gate = gate_ref[0]
    fw = fw_ref[...]
    row0 = 0
    for n_rows in OUT_SPLITS:
        rows = pl.ds(row0, n_rows)
        row0 += n_rows
        mix = jnp.concatenate(
            [a_ref[h, rows, :] for h in range(a_ref.shape[0])]
            + [b_ref[h, rows, :] for h in range(b_ref.shape[0])], axis=1)
        y = jnp.dot(mix, wb_ref[...], preferred_element_type=F32)
        h_res = x_ref[rows, :] + gate * y
        ms = jnp.mean(h_res * h_res, axis=-1, keepdims=True)
        o_ref[rows, :] = h_res * lax.rsqrt(ms + NORM_EPS) * fw


def _out_call(a, b, w_out, xf, gate, fw):
    per_seq = SEQ // OUT_TM
    return pl.pallas_call(
        _out_kernel,
        grid=(TOKENS // OUT_TM,),
        in_specs=[
            pl.BlockSpec((DIFF_HEADS, OUT_TM, LANES), lambda i: (0, i, 0)),
            pl.BlockSpec((2 * GLA_HEADS, OUT_TM, LANES), lambda i: (0, i, 0)),
            pl.BlockSpec((D_MODEL, D_MODEL), lambda i: (0, 0), pipeline_mode=pl.Buffered(1)),
            pl.BlockSpec((OUT_TM, D_MODEL), lambda i: (i, 0)),
            pl.BlockSpec((1, 1, D_MODEL), lambda i: (i // per_seq, 0, 0)),
            pl.BlockSpec((1, D_MODEL), lambda i: (0, 0)),
        ],
        out_specs=pl.BlockSpec((OUT_TM, D_MODEL), lambda i: (i, 0)),
        out_shape=jax.ShapeDtypeStruct((TOKENS, D_MODEL), F32),
        compiler_params=pltpu.CompilerParams(
            dimension_semantics=("arbitrary",), vmem_limit_bytes=OUT_VMEM_LIMIT),
        name="out_proj_norm",
    )(a, b, w_out, xf, gate, fw)


def _rope_tables():
    inv_freq = ROPE_THETA ** (-np.arange(0, DIFF_QK_DIM, 2, dtype=np.float64) / DIFF_QK_DIM)
    ang = np.arange(SEQ, dtype=np.float64)[:, None] * inv_freq[None, :]
    cos, sin = np.cos(ang), np.sin(ang)
    cos_t = np.tile(cos, (1, 4)).astype(np.float32)
    sin_t = np.concatenate([-sin, sin, -sin, sin], axis=1).astype(np.float32)
    return jnp.asarray(cos_t), jnp.asarray(sin_t)


def kernel(x, c, norm_w, w_ada, b_ada, w_in, lambda_q1, lambda_k1, lambda_q2, lambda_k2,
           diff_norm_w, gla_gate_w2, gla_gate_b, gla_norm_w, w_out, final_norm_w):
    assert x.shape == (BATCH, SEQ, D_MODEL) and w_in.shape[0] == 1
    xf = x.reshape(TOKENS, D_MODEL).astype(F32)

    c_pad = jnp.pad(c.astype(F32), ((0, 8 - BATCH), (0, 0)))
    b_ada_row = b_ada[0][None, :]
    mod = _ada_call(c_pad, w_ada[0], b_ada_row, 2 * D_MODEL)[:BATCH]
    shift = mod[:, :D_MODEL].reshape(BATCH, 1, D_MODEL)
    scale = mod[:, D_MODEL:].reshape(BATCH, 1, D_MODEL)

    w_t = w_in[0].T
    cos, sin = _rope_tables()
    p, g = _proj_call(xf, shift, scale, norm_w[0][None, :], w_t, w_t, cos, sin)

    row = lambda v: v[0][None, :].astype(F32)
    a, gate8, w_out_bf16 = _attn_call(
        p, row(lambda_q1), row(lambda_k1), row(lambda_q2), row(lambda_k2), row(diff_norm_w),
        c_pad, w_ada[0], b_ada_row, w_out[0])
    gate = gate8[:BATCH].reshape(BATCH, 1, D_MODEL)

    w2_heads = jnp.pad(gla_gate_w2[0], ((0, LANES - GLA_GATE_RANK), (0, 0))).astype(BF16)
    w2_heads = w2_heads.reshape(LANES, GLA_HEADS, GLA_K_DIM).transpose(1, 0, 2)
    b2_heads = gla_gate_b[0].astype(F32).reshape(GLA_HEADS, 1, GLA_K_DIM)
    b = _gla_call(p, g, w2_heads, b2_heads, row(gla_norm_w))

    out = _out_call(a, b, w_out_bf16, xf, gate, final_norm_w[None, :].astype(F32))
    return out.reshape(BATCH, SEQ, D_MODEL).astype(x.dtype)
```

```python
import math

import jax
import jax.numpy as jnp
import numpy as np
from jax import lax
from jax.experimental import pallas as pl
from jax.experimental.pallas import tpu as pltpu

D_MODEL = 2048
BATCH = 4
SEQ = 2048
TOKENS = BATCH * SEQ
CHUNK = 64
LANES = 128

DIFF_HEADS = 8
DIFF_HEAD_DIM = 128
DIFF_QK_DIM = 64
GLA_HEADS = 4
GLA_K_DIM = 128
GLA_V_DIM = 256
GLA_GATE_RANK = 16
GLA_GATE_TAU = 16.0
ROPE_THETA = 10000.0
NORM_EPS = 1e-6
LAMBDA_INIT = 0.8 - 0.6 * math.exp(-0.3 * 0)

N_MAIN = 7168
N_COLBLK = N_MAIN // LANES
CB_DQ, CB_DK, CB_DV, CB_DG = 0, 8, 16, 24
CB_GQ, CB_GK, CB_GV, CB_GG = 32, 36, 40, 48

VMEM_LIMIT = 48 * 1024 * 1024
NEG_BIG = -1e30
LOG2_E = math.log2(math.e)

BF16 = jnp.bfloat16
F32 = jnp.float32


def _nt_dot(a, b):
    return lax.dot_general(a, b, (((1,), (1,)), ((), ())), preferred_element_type=F32)


def _silu(x):
    h = 0.5 * x
    return h + h * jnp.tanh(h)


def _tn_dot(a, b):
    return lax.dot_general(a, b, (((0,), (0,)), ((), ())), preferred_element_type=F32)


ADA_TN = 1024


def _ada_block(c_ref, w_ref, b_ref):
    c = c_ref[...]
    c_act = (c * jax.nn.sigmoid(c)).astype(BF16)
    return jnp.dot(c_act, w_ref[...].astype(BF16), preferred_element_type=F32) + b_ref[...]


def _ada_kernel(c_ref, w_ref, b_ref, o_ref):
    o_ref[...] = _ada_block(c_ref, w_ref, b_ref)


def _ada_call(c_pad, w_ada, b_ada, n):
    return pl.pallas_call(
        _ada_kernel,
        grid=(n // ADA_TN,),
        in_specs=[
            pl.BlockSpec((8, D_MODEL), lambda j: (0, 0)),
            pl.BlockSpec((D_MODEL, ADA_TN), lambda j: (0, j)),
            pl.BlockSpec((1, ADA_TN), lambda j: (0, j)),
        ],
        out_specs=pl.BlockSpec((8, ADA_TN), lambda j: (0, j)),
        out_shape=jax.ShapeDtypeStruct((8, n), F32),
        compiler_params=pltpu.CompilerParams(
            dimension_semantics=("arbitrary",), vmem_limit_bytes=VMEM_LIMIT),
        name="ada_mod",
    )(c_pad, w_ada, b_ada)


PROJ_TM = 1024
PROJ_TN = 1024
PROJ_SUB = 512
PROJ_ROPE_TILES = 2
PROJ_SEQ_TILES = SEQ // PROJ_TM
PROJ_ROW_TILES = 2 * PROJ_SEQ_TILES
PROJ_VMEM_LIMIT = (2 * PROJ_TM * D_MODEL * 4 + 2 * D_MODEL * PROJ_TN * 4
                   + (1 + PROJ_ROW_TILES) * PROJ_TM * D_MODEL * 2
                   + 2 * PROJ_TM * PROJ_TN * 2 + 4 * 1024 * 1024)


def _rot_half(x):
    lane = lax.broadcasted_iota(jnp.int32, x.shape, 1)
    first = (lane % DIFF_QK_DIM) < (DIFF_QK_DIM // 2)
    return jnp.where(first, pltpu.roll(x, LANES - 32, 1), pltpu.roll(x, 32, 1))


def _proj_kernel(x_ref, shift_ref, scale_ref, nw_ref, w_ref, wg_ref, cos_ref, sin_ref,
                 p_ref, g_ref, hn_ref, hn_all_ref):
    j = pl.program_id(1)
    r = pl.program_id(2)
    n_sub = PROJ_TM // PROJ_SUB
    n_cb = PROJ_TN // LANES
    pos0 = pl.multiple_of((r % PROJ_SEQ_TILES) * PROJ_TM, PROJ_TM)

    def norm_rows(rows):
        mult = nw_ref[...] * (1.0 + scale_ref[0])
        xs = x_ref[rows, :]
        ms = jnp.mean(xs * xs, axis=-1, keepdims=True)
        hn_ref[rows, :] = (xs * lax.rsqrt(ms + NORM_EPS) * mult + shift_ref[0]).astype(BF16)

    def store_rope(sub, acc, sc):
        rows = pl.ds(sub * PROJ_SUB, PROJ_SUB)
        pos = pl.ds(pos0 + sub * PROJ_SUB, PROJ_SUB)
        cos = cos_ref[pos, :] * sc
        sin = sin_ref[pos, :] * sc
        for cb in range(n_cb):
            t = acc[:, cb * LANES:(cb + 1) * LANES]
            p_ref[cb, rows, :] = (t * cos + _rot_half(t) * sin).astype(BF16)

    @pl.when(j == 0)
    def _():
        for sub in range(n_sub):
            rows = pl.ds(sub * PROJ_SUB, PROJ_SUB)
            norm_rows(rows)
            acc = _nt_dot(hn_ref[rows, :], w_ref[...])
            store_rope(sub, acc, DIFF_QK_DIM ** -0.5 * LOG2_E)
        wg = jnp.concatenate(
            [wg_ref[...].astype(BF16), jnp.zeros((LANES - GLA_GATE_RANK, D_MODEL), BF16)], axis=0)
        g_ref[...] = _nt_dot(hn_ref[...], wg).astype(BF16)
        hn_all_ref[r] = hn_ref[...]

    @pl.when(j == 1)
    def _():
        for sub in range(n_sub):
            rows = pl.ds(sub * PROJ_SUB, PROJ_SUB)
            acc = _nt_dot(hn_all_ref[r, rows, :], w_ref[...])
            store_rope(sub, acc, 1.0)

    @pl.when(j >= PROJ_ROPE_TILES)
    def _():
        for sub in range(n_sub):
            rows = pl.ds(sub * PROJ_SUB, PROJ_SUB)
            acc = _nt_dot(hn_all_ref[r, rows, :], w_ref[...])
            for cb in range(n_cb):
                p_ref[cb, rows, :] = acc[:, cb * LANES:(cb + 1) * LANES].astype(BF16)


def _proj_call(xf, shift, scale, norm_w, w_t, w_gate, cos, sin):
    n_tiles = N_MAIN // PROJ_TN
    rt = PROJ_ROW_TILES
    n_cb = PROJ_TN // LANES
    x_tile = lambda b, j, r: (jnp.where(j == 0, b * rt + r, b * rt + rt - 1), 0)
    return pl.pallas_call(
        _proj_kernel,
        grid=(TOKENS // (rt * PROJ_TM), n_tiles, rt),
        in_specs=[
            pl.BlockSpec((PROJ_TM, D_MODEL), x_tile),
            pl.BlockSpec((1, 1, D_MODEL), lambda b, j, r: ((b * rt + r) // PROJ_SEQ_TILES, 0, 0)),
            pl.BlockSpec((1, 1, D_MODEL), lambda b, j, r: ((b * rt + r) // PROJ_SEQ_TILES, 0, 0)),
            pl.BlockSpec((1, D_MODEL), lambda b, j, r: (0, 0)),
            pl.BlockSpec((PROJ_TN, D_MODEL), lambda b, j, r: (j, 0)),
            pl.BlockSpec((GLA_GATE_RANK, D_MODEL), lambda b, j, r: (N_MAIN // GLA_GATE_RANK, 0)),
            pl.BlockSpec((SEQ, LANES), lambda b, j, r: (0, 0)),
            pl.BlockSpec((SEQ, LANES), lambda b, j, r: (0, 0)),
        ],
        out_specs=[
            pl.BlockSpec((n_cb, PROJ_TM, LANES), lambda b, j, r: (j, b * rt + r, 0)),
            pl.BlockSpec((PROJ_TM, LANES), x_tile),
        ],
        out_shape=[
            jax.ShapeDtypeStruct((N_COLBLK, TOKENS, LANES), BF16),
            jax.ShapeDtypeStruct((TOKENS, LANES), BF16),
        ],
        scratch_shapes=[
            pltpu.VMEM((PROJ_TM, D_MODEL), BF16),
            pltpu.VMEM((rt, PROJ_TM, D_MODEL), BF16),
        ],
        compiler_params=pltpu.CompilerParams(
            dimension_semantics=("arbitrary", "arbitrary", "arbitrary"),
            vmem_limit_bytes=PROJ_VMEM_LIMIT),
        name="norm_in_proj",
    )(xf, shift, scale, norm_w, w_t, w_gate, cos, sin)


ATT_T = 256
ATT_NBLK = SEQ // ATT_T
ATT_LEAD = 9
assert ATT_LEAD >= ATT_NBLK - 1
ATT_ONES_ROWS = 16
ATT_HEADS_PER_STEP = 2
ATT_SLOTS = ATT_LEAD + 4


def _attn_kernel(lq1_ref, lk1_ref, lq2_ref, lk2_ref, dnw_ref, q_ref, k_ref, v_ref, dg_ref,
                 c_ref, wa_ref, ba_ref, wo_ref, o_ref, gate_ref, wob_ref, vt_ref, s_ref):
    gate_ref[...] = _ada_block(c_ref, wa_ref, ba_ref)
    wob_ref[...] = wo_ref[...].astype(BF16)

    lam = (jnp.exp(jnp.sum(lq1_ref[...] * lk1_ref[...], keepdims=True))
           - jnp.exp(jnp.sum(lq2_ref[...] * lk2_ref[...], keepdims=True))
           + LAMBDA_INIT)
    out_gain = dnw_ref[...] * (1.0 - LAMBDA_INIT)

    for hd in range(ATT_HEADS_PER_STEP):
        for jb in range(ATT_NBLK):
            cols = pl.ds(jb * ATT_T, ATT_T)
            vb = v_ref[hd, jb * ATT_T:(jb + 1) * ATT_T, :].astype(F32)
            vt_ref[hd, 0:DIFF_HEAD_DIM, cols] = vb.T.astype(BF16)
        vt_ref[hd, DIFF_HEAD_DIM:, :] = jnp.ones((ATT_ONES_ROWS, SEQ), BF16)

    lane = lax.broadcasted_iota(jnp.int32, (ATT_T, LANES), 1)
    comp_a = lane < DIFF_QK_DIM
    krow = lax.broadcasted_iota(jnp.int32, (ATT_T, ATT_T), 0) // CHUNK
    qcol = lax.broadcasted_iota(jnp.int32, (ATT_T, ATT_T), 1) // CHUNK
    diag_mask = krow <= qcol
    diag_mask2 = jnp.concatenate([diag_mask, diag_mask], axis=1)

    units = [(hd, qi, j) for hd in range(ATT_HEADS_PER_STEP)
             for qi in range(ATT_NBLK) for j in range(qi + 1)]
    blocks = {(hd, qi): {"m": None}
              for hd in range(ATT_HEADS_PER_STEP) for qi in range(ATT_NBLK)}

    def block_queries(hd, qi):
        q = q_ref[hd, pl.ds(qi * ATT_T, ATT_T), :]
        zero = jnp.zeros_like(q)
        return jnp.concatenate([jnp.where(comp_a, q, zero), jnp.where(comp_a, zero, q)], axis=0)

    dyn0 = lax.shift_right_logical(pl.program_id(0), 20)

    def score_unit(t, hd, qi, j):
        st = blocks[hd, qi]
        if j == 0:
            st["q2"] = block_queries(hd, qi)
        s = _nt_dot(k_ref[hd, pl.ds(j * ATT_T, ATT_T), :], st["q2"])
        if j == qi:
            s = jnp.where(diag_mask2, s, NEG_BIG)
        s_ref[dyn0 + t % ATT_SLOTS] = s
        mj = jnp.max(s, axis=0, keepdims=True)
        st["m"] = mj if st["m"] is None else jnp.maximum(st["m"], mj)

    def value_unit(t, hd, qi, j):
        st = blocks[hd, qi]
        p = jnp.exp2(s_ref[dyn0 + t % ATT_SLOTS] - st["m"])
        pvj = jnp.dot(vt_ref[hd, :, j * ATT_T:(j + 1) * ATT_T], p.astype(BF16),
                      preferred_element_type=F32)
        st["pv"] = pvj if j == 0 else st["pv"] + pvj
        if j == qi:
            finish(hd, qi, st["pv"])

    def finish(hd, qi, pv_l):
        rows = pl.ds(qi * ATT_T, ATT_T)
        l = pv_l[DIFF_HEAD_DIM:DIFF_HEAD_DIM + 1, :]
        pv = pv_l[0:DIFF_HEAD_DIM, :] * (1.0 / l)
        out_t = pv[:, :ATT_T] - lam * pv[:, ATT_T:]
        out = out_t.T
        ms = jnp.mean(out * out, axis=-1, keepdims=True)
        y = out * lax.rsqrt(ms + NORM_EPS) * out_gain
        g = dg_ref[hd, rows, :].astype(F32)
        o_ref[hd, rows, :] = (y * _silu(g)).astype(BF16)

    for t in range(len(units) + ATT_LEAD):
        if t < len(units):
            score_unit(t, *units[t])
        if t >= ATT_LEAD:
            value_unit(t - ATT_LEAD, *units[t - ATT_LEAD])


def _attn_call(p, lq1, lk1, lq2, lk2, dnw, c_pad, w_ada, b_ada, w_out):
    hp = ATT_HEADS_PER_STEP
    steps_per_batch = DIFF_HEADS // hp
    side_w = D_MODEL // (BATCH * steps_per_batch)
    assert side_w % LANES == 0
    wo_rows = pl.BlockSpec((side_w, D_MODEL), lambda b, h: (b * steps_per_batch + h, 0))
    gate_cb0 = 2 * D_MODEL // side_w
    vec = lambda n: pl.BlockSpec((1, n), lambda b, h: (0, 0))
    slab = lambda cb0: pl.BlockSpec((hp, SEQ, LANES), lambda b, h: (cb0 // hp + h, b, 0))
    gate_blk = lambda rows, cb0: pl.BlockSpec(
        (rows, side_w), lambda b, h: (0, cb0 + b * steps_per_batch + h))
    return pl.pallas_call(
        _attn_kernel,
        grid=(BATCH, steps_per_batch),
        in_specs=[vec(DIFF_QK_DIM)] * 4 + [vec(DIFF_HEAD_DIM),
                  slab(CB_DQ), slab(CB_DK), slab(CB_DV), slab(CB_DG),
                  pl.BlockSpec((8, D_MODEL), lambda b, h: (0, 0)),
                  gate_blk(D_MODEL, gate_cb0), gate_blk(1, gate_cb0), wo_rows],
        out_specs=[pl.BlockSpec((hp, SEQ, LANES), lambda b, h: (h, b, 0)), gate_blk(8, 0),
                   wo_rows],
        out_shape=[jax.ShapeDtypeStruct((DIFF_HEADS, TOKENS, LANES), BF16),
                   jax.ShapeDtypeStruct((8, D_MODEL), F32),
                   jax.ShapeDtypeStruct((D_MODEL, D_MODEL), BF16)],
        scratch_shapes=[
            pltpu.VMEM((hp, DIFF_HEAD_DIM + ATT_ONES_ROWS, SEQ), BF16),
            pltpu.VMEM((ATT_SLOTS, ATT_T, 2 * ATT_T), F32),
        ],
        compiler_params=pltpu.CompilerParams(
            dimension_semantics=("arbitrary", "arbitrary"), vmem_limit_bytes=VMEM_LIMIT),
        name="diff_attn",
    )(lq1, lk1, lq2, lk2, dnw, p, p, p, p, c_pad, w_ada, b_ada, w_out)


GLA_BLK = 256
GLA_NCHUNK = SEQ // CHUNK
GLA_HEADS_PER_STEP = 2


def _split_hi_lo(x):
    hi = x.astype(BF16)
    lo = (x - hi.astype(F32)).astype(BF16)
    return hi, lo


def _gla_kernel(g_ref, w2_ref, b2_ref, gw_ref, q_ref, k_ref, v_ref, gg_ref, o_ref,
                kdec_ref, tot_ref, kv_ref, st_ref):
    for hd in range(GLA_HEADS_PER_STEP):
        _gla_head(g_ref, w2_ref.at[hd], b2_ref.at[hd], gw_ref, q_ref.at[hd], k_ref.at[hd],
                  v_ref.at[hd], gg_ref.at[hd], o_ref.at[hd], kdec_ref, tot_ref, kv_ref, st_ref)


def _gla_head(g_ref, w2_ref, b2_ref, gw_ref, q_ref, k_ref, v_ref, gg_ref, o_ref,
              kdec_ref, tot_ref, kv_ref, st_ref):
    z = jnp.dot(g_ref[...], w2_ref[...], preferred_element_type=F32) + b2_ref[...]
    log_a = (jnp.minimum(z, 0.0) - jnp.log(1.0 + jnp.exp(-jnp.abs(z)))) * (1.0 / GLA_GATE_TAU)

    r = lax.broadcasted_iota(jnp.int32, (GLA_BLK, GLA_BLK), 0)
    c = lax.broadcasted_iota(jnp.int32, (GLA_BLK, GLA_BLK), 1)
    after = jnp.where(((r // CHUNK) == (c // CHUNK)) & (c > r), 1.0, 0.0).astype(BF16)
    for blk in range(SEQ // GLA_BLK):
        rows = pl.ds(blk * GLA_BLK, GLA_BLK)
        la = log_a[blk * GLA_BLK:(blk + 1) * GLA_BLK, :]
        hi, lo = _split_hi_lo(la)
        both = jnp.dot(after, jnp.concatenate([hi, lo], axis=1), preferred_element_type=F32)
        suffix = both[:, :GLA_K_DIM] + both[:, GLA_K_DIM:]
        kdec_ref[rows, :] = (k_ref[rows, :].astype(F32) * jnp.exp(suffix)).astype(BF16)
        tot_ref[rows, :] = suffix + la

    dec_rows = jnp.exp(tot_ref[pl.ds(0, GLA_NCHUNK, stride=CHUNK), :])
    dec_cols = jnp.concatenate(
        [dec_rows, jnp.zeros((GLA_K_DIM - GLA_NCHUNK, GLA_K_DIM), F32)], axis=0).T

    dyn0 = lax.shift_right_logical(pl.program_id(0), 20)
    for ci in range(GLA_NCHUNK):
        rows = pl.ds(ci * CHUNK, CHUNK)
        v_c = jnp.concatenate([v_ref[0, rows, :], v_ref[1, rows, :]], axis=1)
        kv_ref[dyn0 + ci] = _tn_dot(kdec_ref[rows, :], v_c)

    state = jnp.zeros((GLA_K_DIM, GLA_V_DIM), F32)
    for ci in range(GLA_NCHUNK):
        state = state * dec_cols[:, ci:ci + 1] + kv_ref[dyn0 + ci]
        st_ref[ci] = state.astype(BF16)

    gw = gw_ref[...]
    qs = GLA_K_DIM ** -0.5
    for ci in range(GLA_NCHUNK):
        rows = pl.ds(ci * CHUNK, CHUNK)
        o = jnp.dot(q_ref[rows, :], st_ref[ci], preferred_element_type=F32)
        ms = jnp.mean(o * o, axis=-1, keepdims=True) * (qs * qs)
        y = o * (qs * lax.rsqrt(ms + NORM_EPS)) * gw
        gate = jnp.concatenate([gg_ref[0, rows, :], gg_ref[1, rows, :]], axis=1).astype(F32)
        y = y * _silu(gate)
        o_ref[0, rows, :] = y[:, :LANES].astype(BF16)
        o_ref[1, rows, :] = y[:, LANES:].astype(BF16)


def _gla_call(p, g, w2_heads, b2_heads, gw):
    hp = GLA_HEADS_PER_STEP
    qk = lambda cb0: pl.BlockSpec((hp, SEQ, LANES), lambda b, h: (cb0 // hp + h, b, 0))
    p4 = p.reshape(N_COLBLK // 2, 2, TOKENS, LANES)
    vg = lambda cb0: pl.BlockSpec((hp, 2, SEQ, LANES), lambda b, h: (cb0 // 2 // hp + h, 0, b, 0))
    out = pl.pallas_call(
        _gla_kernel,
        grid=(BATCH, GLA_HEADS // hp),
        in_specs=[
            pl.BlockSpec((SEQ, LANES), lambda b, h: (b, 0)),
            pl.BlockSpec((hp, LANES, GLA_K_DIM), lambda b, h: (h, 0, 0)),
            pl.BlockSpec((hp, 1, GLA_K_DIM), lambda b, h: (h, 0, 0)),
            pl.BlockSpec((1, GLA_V_DIM), lambda b, h: (0, 0)),
            qk(CB_GQ), qk(CB_GK), vg(CB_GV), vg(CB_GG),
        ],
        out_specs=pl.BlockSpec((hp, 2, SEQ, LANES), lambda b, h: (h, 0, b, 0)),
        out_shape=jax.ShapeDtypeStruct((GLA_HEADS, 2, TOKENS, LANES), BF16),
        scratch_shapes=[
            pltpu.VMEM((SEQ, GLA_K_DIM), BF16),
            pltpu.VMEM((SEQ, GLA_K_DIM), F32),
            pltpu.VMEM((GLA_NCHUNK, GLA_K_DIM, GLA_V_DIM), F32),
            pltpu.VMEM((GLA_NCHUNK, GLA_K_DIM, GLA_V_DIM), BF16),
        ],
        compiler_params=pltpu.CompilerParams(
            dimension_semantics=("arbitrary", "arbitrary"), vmem_limit_bytes=VMEM_LIMIT),
        name="gla_scan",
    )(g, w2_heads, b2_heads, gw, p, p, p4, p4)
    return out.reshape(2 * GLA_HEADS, TOKENS, LANES)


OUT_TM = 512
OUT_SPLITS = (256, 256)
assert sum(OUT_SPLITS) == OUT_TM
OUT_VMEM_LIMIT = (5 * OUT_TM * D_MODEL * 4 + 2 * OUT_TM * D_MODEL * 2 + D_MODEL * D_MODEL * 2
                  + 12 * 1024 * 1024)


def _out_kernel(a_ref, b_ref, wb_ref, x_ref, gate_ref, fw_ref, o_ref):
    gate = gate_ref[0]
    fw = fw_ref[...]
    row0 = 0
    for n_rows in OUT_SPLITS:
        rows = pl.ds(row0, n_rows)
        row0 += n_rows
        mix = jnp.concatenate(
            [a_ref[h, rows, :] for h in range(a_ref.shape[0])]
            + [b_ref[h, rows, :] for h in range(b_ref.shape[0])], axis=1)
        y = jnp.dot(mix, wb_ref[...], preferred_element_type=F32)
        h_res = x_ref[rows, :] + gate * y
        ms = jnp.mean(h_res * h_res, axis=-1, keepdims=True)
        o_ref[rows, :] = h_res * lax.rsqrt(ms + NORM_EPS) * fw


def _out_call(a, b, w_out, xf, gate, fw):
    per_seq = SEQ // OUT_TM

    def outer(a_hbm, b_hbm, w_ref, x_hbm, gate_hbm, fw_ref, o_hbm):
        def body(a_ref, b_ref, x_ref, gate_ref, o_ref):
            _out_kernel(a_ref, b_ref, w_ref, x_ref, gate_ref, fw_ref, o_ref)

        pltpu.emit_pipeline(
            body,
            grid=(TOKENS // OUT_TM,),
            in_specs=[
                pl.BlockSpec((DIFF_HEADS, OUT_TM, LANES), lambda i: (0, i, 0)),
                pl.BlockSpec((2 * GLA_HEADS, OUT_TM, LANES), lambda i: (0, i, 0)),
                pl.BlockSpec((OUT_TM, D_MODEL), lambda i: (i, 0), pipeline_mode=pl.Buffered(3)),
                pl.BlockSpec((1, 1, D_MODEL), lambda i: (i // per_seq, 0, 0)),
            ],
            out_specs=[pl.BlockSpec((OUT_TM, D_MODEL), lambda i: (i, 0))],
        )(a_hbm, b_hbm, x_hbm, gate_hbm, o_hbm)

    hbm = pl.BlockSpec(memory_space=pl.ANY)
    vmem = pl.BlockSpec(memory_space=pltpu.VMEM)
    return pl.pallas_call(
        outer,
        in_specs=[hbm, hbm, vmem, hbm, hbm, vmem],
        out_specs=hbm,
        out_shape=jax.ShapeDtypeStruct((TOKENS, D_MODEL), F32),
        compiler_params=pltpu.CompilerParams(vmem_limit_bytes=OUT_VMEM_LIMIT),
        name="out_proj_norm",
    )(a, b, w_out, xf, gate, fw)


def _rope_tables():
    inv_freq = ROPE_THETA ** (-np.arange(0, DIFF_QK_DIM, 2, dtype=np.float64) / DIFF_QK_DIM)
    ang = np.arange(SEQ, dtype=np.float64)[:, None] * inv_freq[None, :]
    cos, sin = np.cos(ang), np.sin(ang)
    cos_t = np.tile(cos, (1, 4)).astype(np.float32)
    sin_t = np.concatenate([-sin, sin, -sin, sin], axis=1).astype(np.float32)
    return jnp.asarray(cos_t), jnp.asarray(sin_t)


def kernel(x, c, norm_w, w_ada, b_ada, w_in, lambda_q1, lambda_k1, lambda_q2, lambda_k2,
           diff_norm_w, gla_gate_w2, gla_gate_b, gla_norm_w, w_out, final_norm_w):
    assert x.shape == (BATCH, SEQ, D_MODEL) and w_in.shape[0] == 1
    xf = x.reshape(TOKENS, D_MODEL).astype(F32)

    c_pad = jnp.pad(c.astype(F32), ((0, 8 - BATCH), (0, 0)))
    b_ada_row = b_ada[0][None, :]
    mod = _ada_call(c_pad, w_ada[0], b_ada_row, 2 * D_MODEL)[:BATCH]
    shift = mod[:, :D_MODEL].reshape(BATCH, 1, D_MODEL)
    scale = mod[:, D_MODEL:].reshape(BATCH, 1, D_MODEL)

    w_t = w_in[0].T
    cos, sin = _rope_tables()
    p, g = _proj_call(xf, shift, scale, norm_w[0][None, :], w_t, w_t, cos, sin)

    row = lambda v: v[0][None, :].astype(F32)
    a, gate8, w_out_bf16 = _attn_call(
        p, row(lambda_q1), row(lambda_k1), row(lambda_q2), row(lambda_k2), row(diff_norm_w),
        c_pad, w_ada[0], b_ada_row, w_out[0])
    gate = gate8[:BATCH].reshape(BATCH, 1, D_MODEL)

    w2_heads = jnp.pad(gla_gate_w2[0], ((0, LANES - GLA_GATE_RANK), (0, 0))).astype(BF16)
    w2_heads = w2_heads.reshape(LANES, GLA_HEADS, GLA_K_DIM).transpose(1, 0, 2)
    b2_heads = gla_gate_b[0].astype(F32).reshape(GLA_HEADS, 1, GLA_K_DIM)
    b = _gla_call(p, g, w2_heads, b2_heads, row(gla_norm_w))

    out = _out_call(a, b, w_out_bf16, xf, gate, final_norm_w[None, :].astype(F32))
    return out.reshape(BATCH, SEQ, D_MODEL).astype(x.dtype)
```

```python
import math

import jax
import jax.numpy as jnp
import numpy as np
from jax import lax
from jax.experimental import pallas as pl
from jax.experimental.pallas import tpu as pltpu

D_MODEL = 2048
BATCH = 4
SEQ = 2048
TOKENS = BATCH * SEQ
CHUNK = 64
LANES = 128

DIFF_HEADS = 8
DIFF_HEAD_DIM = 128
DIFF_QK_DIM = 64
GLA_HEADS = 4
GLA_K_DIM = 128
GLA_V_DIM = 256
GLA_GATE_RANK = 16
GLA_GATE_TAU = 16.0
ROPE_THETA = 10000.0
NORM_EPS = 1e-6
LAMBDA_INIT = 0.8 - 0.6 * math.exp(-0.3 * 0)

N_MAIN = 7168
N_COLBLK = N_MAIN // LANES
CB_DQ, CB_DK, CB_DV, CB_DG = 0, 8, 16, 24
CB_GQ, CB_GK, CB_GV, CB_GG = 32, 36, 40, 48

VMEM_LIMIT = 48 * 1024 * 1024
NEG_BIG = -1e30
LOG2_E = math.log2(math.e)

BF16 = jnp.bfloat16
F32 = jnp.float32


def _nt_dot(a, b):
    return lax.dot_general(a, b, (((1,), (1,)), ((), ())), preferred_element_type=F32)


def _silu(x):
    h = 0.5 * x
    return h + h * jnp.tanh(h)


def _tn_dot(a, b):
    return lax.dot_general(a, b, (((0,), (0,)), ((), ())), preferred_element_type=F32)


ADA_TN = 1024


def _ada_block(c_ref, w_ref, b_ref):
    c = c_ref[...]
    c_act = (c * jax.nn.sigmoid(c)).astype(BF16)
    return jnp.dot(c_act, w_ref[...].astype(BF16), preferred_element_type=F32) + b_ref[...]


def _ada_kernel(c_ref, w_ref, b_ref, o_ref):
    o_ref[...] = _ada_block(c_ref, w_ref, b_ref)


def _ada_call(c_pad, w_ada, b_ada, n):
    return pl.pallas_call(
        _ada_kernel,
        grid=(n // ADA_TN,),
        in_specs=[
            pl.BlockSpec((8, D_MODEL), lambda j: (0, 0)),
            pl.BlockSpec((D_MODEL, ADA_TN), lambda j: (0, j)),
            pl.BlockSpec((1, ADA_TN), lambda j: (0, j)),
        ],
        out_specs=pl.BlockSpec((8, ADA_TN), lambda j: (0, j)),
        out_shape=jax.ShapeDtypeStruct((8, n), F32),
        compiler_params=pltpu.CompilerParams(
            dimension_semantics=("arbitrary",), vmem_limit_bytes=VMEM_LIMIT),
        name="ada_mod",
    )(c_pad, w_ada, b_ada)


PROJ_TM = 1024
PROJ_TN = 1024
PROJ_SUB = 512
PROJ_ROPE_TILES = 2
PROJ_SEQ_TILES = SEQ // PROJ_TM
PROJ_ROW_TILES = 2 * PROJ_SEQ_TILES
PROJ_VMEM_LIMIT = (2 * PROJ_TM * D_MODEL * 4 + 2 * D_MODEL * PROJ_TN * 4
                   + (1 + PROJ_ROW_TILES) * PROJ_TM * D_MODEL * 2
                   + 2 * PROJ_TM * PROJ_TN * 2 + 4 * 1024 * 1024)


def _rot_half(x):
    lane = lax.broadcasted_iota(jnp.int32, x.shape, 1)
    first = (lane % DIFF_QK_DIM) < (DIFF_QK_DIM // 2)
    return jnp.where(first, pltpu.roll(x, LANES - 32, 1), pltpu.roll(x, 32, 1))


def _proj_kernel(x_ref, shift_ref, scale_ref, nw_ref, w_ref, wg_ref, cos_ref, sin_ref,
                 p_ref, g_ref, hn_ref, hn_all_ref):
    j = pl.program_id(1)
    r = pl.program_id(2)
    n_sub = PROJ_TM // PROJ_SUB
    n_cb = PROJ_TN // LANES
    pos0 = pl.multiple_of((r % PROJ_SEQ_TILES) * PROJ_TM, PROJ_TM)

    def norm_rows(rows):
        mult = nw_ref[...] * (1.0 + scale_ref[0])
        xs = x_ref[rows, :]
        ms = jnp.mean(xs * xs, axis=-1, keepdims=True)
        hn_ref[rows, :] = (xs * lax.rsqrt(ms + NORM_EPS) * mult + shift_ref[0]).astype(BF16)

    def store_rope(sub, acc, sc):
        rows = pl.ds(sub * PROJ_SUB, PROJ_SUB)
        pos = pl.ds(pos0 + sub * PROJ_SUB, PROJ_SUB)
        cos = cos_ref[pos, :] * sc
        sin = sin_ref[pos, :] * sc
        for cb in range(n_cb):
            t = acc[:, cb * LANES:(cb + 1) * LANES]
            p_ref[cb, rows, :] = (t * cos + _rot_half(t) * sin).astype(BF16)

    @pl.when(j == 0)
    def _():
        for sub in range(n_sub):
            rows = pl.ds(sub * PROJ_SUB, PROJ_SUB)
            norm_rows(rows)
            acc = _nt_dot(hn_ref[rows, :], w_ref[...])
            store_rope(sub, acc, DIFF_QK_DIM ** -0.5 * LOG2_E)
        wg = jnp.concatenate(
            [wg_ref[...].astype(BF16), jnp.zeros((LANES - GLA_GATE_RANK, D_MODEL), BF16)], axis=0)
        g_ref[...] = _nt_dot(hn_ref[...], wg).astype(BF16)
        hn_all_ref[r] = hn_ref[...]

    @pl.when(j == 1)
    def _():
        for sub in range(n_sub):
            rows = pl.ds(sub * PROJ_SUB, PROJ_SUB)
            acc = _nt_dot(hn_all_ref[r, rows, :], w_ref[...])
            store_rope(sub, acc, 1.0)

    @pl.when(j >= PROJ_ROPE_TILES)
    def _():
        for sub in range(n_sub):
            rows = pl.ds(sub * PROJ_SUB, PROJ_SUB)
            acc = _nt_dot(hn_all_ref[r, rows, :], w_ref[...])
            for cb in range(n_cb):
                p_ref[cb, rows, :] = acc[:, cb * LANES:(cb + 1) * LANES].astype(BF16)


def _proj_call(xf, shift, scale, norm_w, w_t, w_gate, cos, sin):
    n_tiles = N_MAIN // PROJ_TN
    rt = PROJ_ROW_TILES
    n_cb = PROJ_TN // LANES
    x_tile = lambda b, j, r: (jnp.where(j == 0, b * rt + r, b * rt + rt - 1), 0)
    return pl.pallas_call(
        _proj_kernel,
        grid=(TOKENS // (rt * PROJ_TM), n_tiles, rt),
        in_specs=[
            pl.BlockSpec((PROJ_TM, D_MODEL), x_tile),
            pl.BlockSpec((1, 1, D_MODEL), lambda b, j, r: ((b * rt + r) // PROJ_SEQ_TILES, 0, 0)),
            pl.BlockSpec((1, 1, D_MODEL), lambda b, j, r: ((b * rt + r) // PROJ_SEQ_TILES, 0, 0)),
            pl.BlockSpec((1, D_MODEL), lambda b, j, r: (0, 0)),
            pl.BlockSpec((PROJ_TN, D_MODEL), lambda b, j, r: (j, 0)),
            pl.BlockSpec((GLA_GATE_RANK, D_MODEL), lambda b, j, r: (N_MAIN // GLA_GATE_RANK, 0)),
            pl.BlockSpec((SEQ, LANES), lambda b, j, r: (0, 0)),
            pl.BlockSpec((SEQ, LANES), lambda b, j, r: (0, 0)),
        ],
        out_specs=[
            pl.BlockSpec((n_cb, PROJ_TM, LANES), lambda b, j, r: (j, b * rt + r, 0)),
            pl.BlockSpec((PROJ_TM, LANES), x_tile),
        ],
        out_shape=[
            jax.ShapeDtypeStruct((N_COLBLK, TOKENS, LANES), BF16),
            jax.ShapeDtypeStruct((TOKENS, LANES), BF16),
        ],
        scratch_shapes=[
            pltpu.VMEM((PROJ_TM, D_MODEL), BF16),
            pltpu.VMEM((rt, PROJ_TM, D_MODEL), BF16),
        ],
        compiler_params=pltpu.CompilerParams(
            dimension_semantics=("arbitrary", "arbitrary", "arbitrary"),
            vmem_limit_bytes=PROJ_VMEM_LIMIT),
        name="norm_in_proj",
    )(xf, shift, scale, norm_w, w_t, w_gate, cos, sin)


ATT_T = 256
ATT_NBLK = SEQ // ATT_T
ATT_LEAD = 9
assert ATT_LEAD >= ATT_NBLK - 1
ATT_ONES_ROWS = 16
ATT_HEADS_PER_STEP = 2
ATT_SLOTS = ATT_LEAD + 4


def _attn_kernel(lq1_ref, lk1_ref, lq2_ref, lk2_ref, dnw_ref, q_ref, k_ref, v_ref, dg_ref,
                 c_ref, wa_ref, ba_ref, wo_ref, o_ref, gate_ref, wob_ref, vt_ref, s_ref):
    gate_ref[...] = _ada_block(c_ref, wa_ref, ba_ref)
    wob_ref[...] = wo_ref[...].astype(BF16)

    lam = (jnp.exp(jnp.sum(lq1_ref[...] * lk1_ref[...], keepdims=True))
           - jnp.exp(jnp.sum(lq2_ref[...] * lk2_ref[...], keepdims=True))
           + LAMBDA_INIT)
    out_gain = dnw_ref[...] * (1.0 - LAMBDA_INIT)

    for hd in range(ATT_HEADS_PER_STEP):
        for jb in range(ATT_NBLK):
            cols = pl.ds(jb * ATT_T, ATT_T)
            vb = v_ref[hd, jb * ATT_T:(jb + 1) * ATT_T, :].astype(F32)
            vt_ref[hd, 0:DIFF_HEAD_DIM, cols] = vb.T.astype(BF16)
        vt_ref[hd, DIFF_HEAD_DIM:, :] = jnp.ones((ATT_ONES_ROWS, SEQ), BF16)

    lane = lax.broadcasted_iota(jnp.int32, (ATT_T, LANES), 1)
    comp_a = lane < DIFF_QK_DIM
    krow = lax.broadcasted_iota(jnp.int32, (ATT_T, ATT_T), 0) // CHUNK
    qcol = lax.broadcasted_iota(jnp.int32, (ATT_T, ATT_T), 1) // CHUNK
    diag_mask = krow <= qcol
    diag_mask2 = jnp.concatenate([diag_mask, diag_mask], axis=1)

    units = [(hd, qi, j) for hd in range(ATT_HEADS_PER_STEP)
             for qi in range(ATT_NBLK) for j in range(qi + 1)]
    blocks = {(hd, qi): {"m": None}
              for hd in range(ATT_HEADS_PER_STEP) for qi in range(ATT_NBLK)}

    def block_queries(hd, qi):
        q = q_ref[hd, pl.ds(qi * ATT_T, ATT_T), :]
        zero = jnp.zeros_like(q)
        return jnp.concatenate([jnp.where(comp_a, q, zero), jnp.where(comp_a, zero, q)], axis=0)

    dyn0 = lax.shift_right_logical(pl.program_id(0), 20)

    def score_unit(t, hd, qi, j):
        st = blocks[hd, qi]
        if j == 0:
            st["q2"] = block_queries(hd, qi)
        s = _nt_dot(k_ref[hd, pl.ds(j * ATT_T, ATT_T), :], st["q2"])
        if j == qi:
            s = jnp.where(diag_mask2, s, NEG_BIG)
        s_ref[dyn0 + t % ATT_SLOTS] = s
        mj = jnp.max(s, axis=0, keepdims=True)
        st["m"] = mj if st["m"] is None else jnp.maximum(st["m"], mj)

    def value_unit(t, hd, qi, j):
        st = blocks[hd, qi]
        p = jnp.exp2(s_ref[dyn0 + t % ATT_SLOTS] - st["m"])
        pvj = jnp.dot(vt_ref[hd, :, j * ATT_T:(j + 1) * ATT_T], p.astype(BF16),
                      preferred_element_type=F32)
        st["pv"] = pvj if j == 0 else st["pv"] + pvj
        if j == qi:
            finish(hd, qi, st["pv"])

    def finish(hd, qi, pv_l):
        rows = pl.ds(qi * ATT_T, ATT_T)
        l = pv_l[DIFF_HEAD_DIM:DIFF_HEAD_DIM + 1, :]
        pv = pv_l[0:DIFF_HEAD_DIM, :] * (1.0 / l)
        out_t = pv[:, :ATT_T] - lam * pv[:, ATT_T:]
        out = out_t.T
        ms = jnp.mean(out * out, axis=-1, keepdims=True)
        y = out * lax.rsqrt(ms + NORM_EPS) * out_gain
        g = dg_ref[hd, rows, :].astype(F32)
        o_ref[hd, rows, :] = (y * _silu(g)).astype(BF16)

    for t in range(len(units) + ATT_LEAD):
        if t < len(units):
            score_unit(t, *units[t])
        if t >= ATT_LEAD:
            value_unit(t - ATT_LEAD, *units[t - ATT_LEAD])


def _attn_call(p, lq1, lk1, lq2, lk2, dnw, c_pad, w_ada, b_ada, w_out):
    hp = ATT_HEADS_PER_STEP
    steps_per_batch = DIFF_HEADS // hp
    side_w = D_MODEL // (BATCH * steps_per_batch)
    assert side_w % LANES == 0
    wo_rows = pl.BlockSpec((side_w, D_MODEL), lambda b, h: (b * steps_per_batch + h, 0))
    gate_cb0 = 2 * D_MODEL // side_w
    vec = lambda n: pl.BlockSpec((1, n), lambda b, h: (0, 0))
    slab = lambda cb0: pl.BlockSpec((hp, SEQ, LANES), lambda b, h: (cb0 // hp + h, b, 0))
    gate_blk = lambda rows, cb0: pl.BlockSpec(
        (rows, side_w), lambda b, h: (0, cb0 + b * steps_per_batch + h))
    return pl.pallas_call(
        _attn_kernel,
        grid=(BATCH, steps_per_batch),
        in_specs=[vec(DIFF_QK_DIM)] * 4 + [vec(DIFF_HEAD_DIM),
                  slab(CB_DQ), slab(CB_DK), slab(CB_DV), slab(CB_DG),
                  pl.BlockSpec((8, D_MODEL), lambda b, h: (0, 0)),
                  gate_blk(D_MODEL, gate_cb0), gate_blk(1, gate_cb0), wo_rows],
        out_specs=[pl.BlockSpec((hp, SEQ, LANES), lambda b, h: (h, b, 0)), gate_blk(8, 0),
                   wo_rows],
        out_shape=[jax.ShapeDtypeStruct((DIFF_HEADS, TOKENS, LANES), BF16),
                   jax.ShapeDtypeStruct((8, D_MODEL), F32),
                   jax.ShapeDtypeStruct((D_MODEL, D_MODEL), BF16)],
        scratch_shapes=[
            pltpu.VMEM((hp, DIFF_HEAD_DIM + ATT_ONES_ROWS, SEQ), BF16),
            pltpu.VMEM((ATT_SLOTS, ATT_T, 2 * ATT_T), F32),
        ],
        compiler_params=pltpu.CompilerParams(
            dimension_semantics=("arbitrary", "arbitrary"), vmem_limit_bytes=VMEM_LIMIT),
        name="diff_attn",
    )(lq1, lk1, lq2, lk2, dnw, p, p, p, p, c_pad, w_ada, b_ada, w_out)


GLA_BLK = 256
GLA_NCHUNK = SEQ // CHUNK
GLA_HEADS_PER_STEP = 2


def _split_hi_lo(x):
    hi = x.astype(BF16)
    lo = (x - hi.astype(F32)).astype(BF16)
    return hi, lo


def _gla_kernel(g_ref, w2_ref, b2_ref, gw_ref, q_ref, k_ref, v_ref, gg_ref, o_ref,
                kdec_ref, tot_ref, kv_ref, st_ref):
    for hd in range(GLA_HEADS_PER_STEP):
        _gla_head(g_ref, w2_ref.at[hd], b2_ref.at[hd], gw_ref, q_ref.at[hd], k_ref.at[hd],
                  v_ref.at[hd], gg_ref.at[hd], o_ref.at[hd], kdec_ref, tot_ref, kv_ref, st_ref)


def _gla_head(g_ref, w2_ref, b2_ref, gw_ref, q_ref, k_ref, v_ref, gg_ref, o_ref,
              kdec_ref, tot_ref, kv_ref, st_ref):
    z = jnp.dot(g_ref[...], w2_ref[...], preferred_element_type=F32) + b2_ref[...]
    log_a = (jnp.minimum(z, 0.0) - jnp.log(1.0 + jnp.exp(-jnp.abs(z)))) * (1.0 / GLA_GATE_TAU)

    r = lax.broadcasted_iota(jnp.int32, (GLA_BLK, GLA_BLK), 0)
    c = lax.broadcasted_iota(jnp.int32, (GLA_BLK, GLA_BLK), 1)
    after = jnp.where(((r // CHUNK) == (c // CHUNK)) & (c > r), 1.0, 0.0).astype(BF16)
    for blk in range(SEQ // GLA_BLK):
        rows = pl.ds(blk * GLA_BLK, GLA_BLK)
        la = log_a[blk * GLA_BLK:(blk + 1) * GLA_BLK, :]
        hi, lo = _split_hi_lo(la)
        both = jnp.dot(after, jnp.concatenate([hi, lo], axis=1), preferred_element_type=F32)
        suffix = both[:, :GLA_K_DIM] + both[:, GLA_K_DIM:]
        kdec_ref[rows, :] = (k_ref[rows, :].astype(F32) * jnp.exp(suffix)).astype(BF16)
        tot_ref[rows, :] = suffix + la

    dec_rows = jnp.exp(tot_ref[pl.ds(0, GLA_NCHUNK, stride=CHUNK), :])
    dec_cols = jnp.concatenate(
        [dec_rows, jnp.zeros((GLA_K_DIM - GLA_NCHUNK, GLA_K_DIM), F32)], axis=0).T

    dyn0 = lax.shift_right_logical(pl.program_id(0), 20)
    for ci in range(GLA_NCHUNK):
        rows = pl.ds(ci * CHUNK, CHUNK)
        v_c = jnp.concatenate([v_ref[0, rows, :], v_ref[1, rows, :]], axis=1)
        kv_ref[dyn0 + ci] = _tn_dot(kdec_ref[rows, :], v_c)

    state = jnp.zeros((GLA_K_DIM, GLA_V_DIM), F32)
    for ci in range(GLA_NCHUNK):
        state = state * dec_cols[:, ci:ci + 1] + kv_ref[dyn0 + ci]
        st_ref[ci] = state.astype(BF16)

    gw = gw_ref[...]
    qs = GLA_K_DIM ** -0.5
    for ci in range(GLA_NCHUNK):
        rows = pl.ds(ci * CHUNK, CHUNK)
        o = jnp.dot(q_ref[rows, :], st_ref[ci], preferred_element_type=F32)
        ms = jnp.mean(o * o, axis=-1, keepdims=True) * (qs * qs)
        y = o * (qs * lax.rsqrt(ms + NORM_EPS)) * gw
        gate = jnp.concatenate([gg_ref[0, rows, :], gg_ref[1, rows, :]], axis=1).astype(F32)
        y = y * _silu(gate)
        o_ref[0, rows, :] = y[:, :LANES].astype(BF16)
        o_ref[1, rows, :] = y[:, LANES:].astype(BF16)


def _gla_call(p, g, w2_heads, b2_heads, gw):
    hp = GLA_HEADS_PER_STEP
    qk = lambda cb0: pl.BlockSpec((hp, SEQ, LANES), lambda b, h: (cb0 // hp + h, b, 0))
    p4 = p.reshape(N_COLBLK // 2, 2, TOKENS, LANES)
    vg = lambda cb0: pl.BlockSpec((hp, 2, SEQ, LANES), lambda b, h: (cb0 // 2 // hp + h, 0, b, 0))
    out = pl.pallas_call(
        _gla_kernel,
        grid=(BATCH, GLA_HEADS // hp),
        in_specs=[
            pl.BlockSpec((SEQ, LANES), lambda b, h: (b, 0)),
            pl.BlockSpec((hp, LANES, GLA_K_DIM), lambda b, h: (h, 0, 0)),
            pl.BlockSpec((hp, 1, GLA_K_DIM), lambda b, h: (h, 0, 0)),
            pl.BlockSpec((1, GLA_V_DIM), lambda b, h: (0, 0)),
            qk(CB_GQ), qk(CB_GK), vg(CB_GV), vg(CB_GG),
        ],
        out_specs=pl.BlockSpec((hp, 2, SEQ, LANES), lambda b, h: (h, 0, b, 0)),
        out_shape=jax.ShapeDtypeStruct((GLA_HEADS, 2, TOKENS, LANES), BF16),
        scratch_shapes=[
            pltpu.VMEM((SEQ, GLA_K_DIM), BF16),
            pltpu.VMEM((SEQ, GLA_K_DIM), F32),
            pltpu.VMEM((GLA_NCHUNK, GLA_K_DIM, GLA_V_DIM), F32),
            pltpu.VMEM((GLA_NCHUNK, GLA_K_DIM, GLA_V_DIM), BF16),
        ],
        compiler_params=pltpu.CompilerParams(
            dimension_semantics=("arbitrary", "arbitrary"), vmem_limit_bytes=VMEM_LIMIT),
        name="gla_scan",
    )(g, w2_heads, b2_heads, gw, p, p, p4, p4)
    return out.reshape(2 * GLA_HEADS, TOKENS, LANES)


OUT_TM = 512
OUT_SPLITS = (256, 256)
assert sum(OUT_SPLITS) == OUT_TM
OUT_VMEM_LIMIT = (5 * OUT_TM * D_MODEL * 4 + 2 * OUT_TM * D_MODEL * 2 + D_MODEL * D_MODEL * 2
                  + 12 * 1024 * 1024)


def _out_kernel(a_ref, b_ref, wb_ref, x_ref, gate_ref, fw_ref, o_ref):
    gate = gate_ref[0]
    fw = fw_ref[...]
    row0 = 0
    for n_rows in OUT_SPLITS:
        rows = pl.ds(row0, n_rows)
        row0 += n_rows
        mix = jnp.concatenate(
            [a_ref[h, rows, :] for h in range(a_ref.shape[0])]
            + [b_ref[h, rows, :] for h in range(b_ref.shape[0])], axis=1)
        y = jnp.dot(mix, wb_ref[...], preferred_element_type=F32)
        h_res = x_ref[rows, :] + gate * y
        ms = jnp.mean(h_res * h_res, axis=-1, keepdims=True)
        o_ref[rows, :] = h_res * lax.rsqrt(ms + NORM_EPS) * fw


def _out_call(a, b, w_out, xf, gate, fw):
    per_seq = SEQ // OUT_TM

    def outer(a_hbm, b_hbm, w_ref, x_hbm, gate_hbm, fw_ref, o_hbm):
        def body(a_ref, b_ref, x_ref, gate_ref, o_ref):
            _out_kernel(a_ref, b_ref, w_ref, x_ref, gate_ref, fw_ref, o_ref)

        pltpu.emit_pipeline(
            body,
            grid=(TOKENS // OUT_TM,),
            in_specs=[
                pl.BlockSpec((DIFF_HEADS, OUT_TM, LANES), lambda i: (0, i, 0),
                             pipeline_mode=pl.Buffered(3)),
                pl.BlockSpec((2 * GLA_HEADS, OUT_TM, LANES), lambda i: (0, i, 0),
                             pipeline_mode=pl.Buffered(3)),
                pl.BlockSpec((OUT_TM, D_MODEL), lambda i: (i, 0), pipeline_mode=pl.Buffered(3)),
                pl.BlockSpec((1, 1, D_MODEL), lambda i: (i // per_seq, 0, 0)),
            ],
            out_specs=[pl.BlockSpec((OUT_TM, D_MODEL), lambda i: (i, 0))],
        )(a_hbm, b_hbm, x_hbm, gate_hbm, o_hbm)

    hbm = pl.BlockSpec(memory_space=pl.ANY)
    vmem = pl.BlockSpec(memory_space=pltpu.VMEM)
    return pl.pallas_call(
        outer,
        in_specs=[hbm, hbm, vmem, hbm, hbm, vmem],
        out_specs=hbm,
        out_shape=jax.ShapeDtypeStruct((TOKENS, D_MODEL), F32),
        compiler_params=pltpu.CompilerParams(vmem_limit_bytes=OUT_VMEM_LIMIT),
        name="out_proj_norm",
    )(a, b, w_out, xf, gate, fw)


def _rope_tables():
    inv_freq = ROPE_THETA ** (-np.arange(0, DIFF_QK_DIM, 2, dtype=np.float64) / DIFF_QK_DIM)
    ang = np.arange(SEQ, dtype=np.float64)[:, None] * inv_freq[None, :]
    cos, sin = np.cos(ang), np.sin(ang)
    cos_t = np.tile(cos, (1, 4)).astype(np.float32)
    sin_t = np.concatenate([-sin, sin, -sin, sin], axis=1).astype(np.float32)
    return jnp.asarray(cos_t), jnp.asarray(sin_t)


def kernel(x, c, norm_w, w_ada, b_ada, w_in, lambda_q1, lambda_k1, lambda_q2, lambda_k2,
           diff_norm_w, gla_gate_w2, gla_gate_b, gla_norm_w, w_out, final_norm_w):
    assert x.shape == (BATCH, SEQ, D_MODEL) and w_in.shape[0] == 1
    xf = x.reshape(TOKENS, D_MODEL).astype(F32)

    c_pad = jnp.pad(c.astype(F32), ((0, 8 - BATCH), (0, 0)))
    b_ada_row = b_ada[0][None, :]
    mod = _ada_call(c_pad, w_ada[0], b_ada_row, 2 * D_MODEL)[:BATCH]
    shift = mod[:, :D_MODEL].reshape(BATCH, 1, D_MODEL)
    scale = mod[:, D_MODEL:].reshape(BATCH, 1, D_MODEL)

    w_t = w_in[0].T
    cos, sin = _rope_tables()
    p, g = _proj_call(xf, shift, scale, norm_w[0][None, :], w_t, w_t, cos, sin)

    row = lambda v: v[0][None, :].astype(F32)
    a, gate8, w_out_bf16 = _attn_call(
        p, row(lambda_q1), row(lambda_k1), row(lambda_q2), row(lambda_k2), row(diff_norm_w),
        c_pad, w_ada[0], b_ada_row, w_out[0])
    gate = gate8[:BATCH].reshape(BATCH, 1, D_MODEL)

    w2_heads = jnp.pad(gla_gate_w2[0], ((0, LANES - GLA_GATE_RANK), (0, 0))).astype(BF16)
    w2_heads = w2_heads.reshape(LANES, GLA_HEADS, GLA_K_DIM).transpose(1, 0, 2)
    b2_heads = gla_gate_b[0].astype(F32).reshape(GLA_HEADS, 1, GLA_K_DIM)
    b = _gla_call(p, g, w2_heads, b2_heads, row(gla_norm_w))

    out = _out_call(a, b, w_out_bf16, xf, gate, final_norm_w[None, :].astype(F32))
    return out.reshape(BATCH, SEQ, D_MODEL).astype(x.dtype)
```

```python
import math

import jax
import jax.numpy as jnp
import numpy as np
from jax import lax
from jax.experimental import pallas as pl
from jax.experimental.pallas import tpu as pltpu

D_MODEL = 2048
BATCH = 4
SEQ = 2048
TOKENS = BATCH * SEQ
CHUNK = 64
LANES = 128

DIFF_HEADS = 8
DIFF_HEAD_DIM = 128
DIFF_QK_DIM = 64
GLA_HEADS = 4
GLA_K_DIM = 128
GLA_V_DIM = 256
GLA_GATE_RANK = 16
GLA_GATE_TAU = 16.0
ROPE_THETA = 10000.0
NORM_EPS = 1e-6
LAMBDA_INIT = 0.8 - 0.6 * math.exp(-0.3 * 0)

N_MAIN = 7168
N_COLBLK = N_MAIN // LANES
CB_DQ, CB_DK, CB_DV, CB_DG = 0, 8, 16, 24
CB_GQ, CB_GK, CB_GV, CB_GG = 32, 36, 40, 48

VMEM_LIMIT = 48 * 1024 * 1024
NEG_BIG = -1e30
LOG2_E = math.log2(math.e)

BF16 = jnp.bfloat16
F32 = jnp.float32


def _nt_dot(a, b):
    return lax.dot_general(a, b, (((1,), (1,)), ((), ())), preferred_element_type=F32)


def _silu(x):
    h = 0.5 * x
    return h + h * jnp.tanh(h)


def _tn_dot(a, b):
    return lax.dot_general(a, b, (((0,), (0,)), ((), ())), preferred_element_type=F32)


ADA_TN = 512


def _ada_block(c_ref, w_ref, b_ref):
    c = c_ref[...]
    c_act = (c * jax.nn.sigmoid(c)).astype(BF16)
    return jnp.dot(c_act, w_ref[...].astype(BF16), preferred_element_type=F32) + b_ref[...]


def _ada_kernel(c_ref, w_ref, b_ref, o_ref):
    o_ref[...] = _ada_block(c_ref, w_ref, b_ref)


def _ada_call(c_pad, w_ada, b_ada, n):
    def outer(c_ref, w_hbm, b_hbm, o_hbm):
        def body(w_ref, b_ref, o_ref):
            _ada_kernel(c_ref, w_ref, b_ref, o_ref)

        pltpu.emit_pipeline(
            body,
            grid=(n // ADA_TN,),
            in_specs=[
                pl.BlockSpec((D_MODEL, ADA_TN), lambda j: (0, j), pipeline_mode=pl.Buffered(3)),
                pl.BlockSpec((1, ADA_TN), lambda j: (0, j)),
            ],
            out_specs=[pl.BlockSpec((8, ADA_TN), lambda j: (0, j))],
        )(w_hbm, b_hbm, o_hbm)

    hbm = pl.BlockSpec(memory_space=pl.ANY)
    return pl.pallas_call(
        outer,
        in_specs=[pl.BlockSpec(memory_space=pltpu.VMEM), hbm, hbm],
        out_specs=hbm,
        out_shape=jax.ShapeDtypeStruct((8, n), F32),
        compiler_params=pltpu.CompilerParams(vmem_limit_bytes=VMEM_LIMIT),
        name="ada_mod",
    )(c_pad, w_ada, b_ada)


PROJ_TM = 1024
PROJ_TN = 1024
PROJ_SUB = 512
PROJ_ROPE_TILES = 2
PROJ_SEQ_TILES = SEQ // PROJ_TM
PROJ_ROW_TILES = 2 * PROJ_SEQ_TILES
PROJ_VMEM_LIMIT = (2 * PROJ_TM * D_MODEL * 4 + 2 * D_MODEL * PROJ_TN * 4
                   + (1 + PROJ_ROW_TILES) * PROJ_TM * D_MODEL * 2
                   + 2 * PROJ_TM * PROJ_TN * 2 + 4 * 1024 * 1024)


def _rot_half(x):
    lane = lax.broadcasted_iota(jnp.int32, x.shape, 1)
    first = (lane % DIFF_QK_DIM) < (DIFF_QK_DIM // 2)
    return jnp.where(first, pltpu.roll(x, LANES - 32, 1), pltpu.roll(x, 32, 1))


def _proj_kernel(x_ref, shift_ref, scale_ref, nw_ref, w_ref, wg_ref, cos_ref, sin_ref,
                 p_ref, g_ref, hn_ref, hn_all_ref):
    j = pl.program_id(1)
    r = pl.program_id(2)
    n_sub = PROJ_TM // PROJ_SUB
    n_cb = PROJ_TN // LANES
    pos0 = pl.multiple_of((r % PROJ_SEQ_TILES) * PROJ_TM, PROJ_TM)

    def norm_rows(rows):
        mult = nw_ref[...] * (1.0 + scale_ref[0])
        xs = x_ref[rows, :]
        ms = jnp.mean(xs * xs, axis=-1, keepdims=True)
        hn_ref[rows, :] = (xs * lax.rsqrt(ms + NORM_EPS) * mult + shift_ref[0]).astype(BF16)

    def store_rope(sub, acc, sc):
        rows = pl.ds(sub * PROJ_SUB, PROJ_SUB)
        pos = pl.ds(pos0 + sub * PROJ_SUB, PROJ_SUB)
        cos = cos_ref[pos, :] * sc
        sin = sin_ref[pos, :] * sc
        for cb in range(n_cb):
            t = acc[:, cb * LANES:(cb + 1) * LANES]
            p_ref[cb, rows, :] = (t * cos + _rot_half(t) * sin).astype(BF16)

    @pl.when(j == 0)
    def _():
        for sub in range(n_sub):
            rows = pl.ds(sub * PROJ_SUB, PROJ_SUB)
            norm_rows(rows)
            acc = _nt_dot(hn_ref[rows, :], w_ref[...])
            store_rope(sub, acc, DIFF_QK_DIM ** -0.5 * LOG2_E)
        wg = jnp.concatenate(
            [wg_ref[...].astype(BF16), jnp.zeros((LANES - GLA_GATE_RANK, D_MODEL), BF16)], axis=0)
        g_ref[...] = _nt_dot(hn_ref[...], wg).astype(BF16)
        hn_all_ref[r] = hn_ref[...]

    @pl.when(j == 1)
    def _():
        for sub in range(n_sub):
            rows = pl.ds(sub * PROJ_SUB, PROJ_SUB)
            acc = _nt_dot(hn_all_ref[r, rows, :], w_ref[...])
            store_rope(sub, acc, 1.0)

    @pl.when(j >= PROJ_ROPE_TILES)
    def _():
        for sub in range(n_sub):
            rows = pl.ds(sub * PROJ_SUB, PROJ_SUB)
            acc = _nt_dot(hn_all_ref[r, rows, :], w_ref[...])
            for cb in range(n_cb):
                p_ref[cb, rows, :] = acc[:, cb * LANES:(cb + 1) * LANES].astype(BF16)


def _proj_call(xf, shift, scale, norm_w, w_t, w_gate, cos, sin):
    n_tiles = N_MAIN // PROJ_TN
    rt = PROJ_ROW_TILES
    n_cb = PROJ_TN // LANES
    x_tile = lambda b, j, r: (jnp.where(j == 0, b * rt + r, b * rt + rt - 1), 0)
    return pl.pallas_call(
        _proj_kernel,
        grid=(TOKENS // (rt * PROJ_TM), n_tiles, rt),
        in_specs=[
            pl.BlockSpec((PROJ_TM, D_MODEL), x_tile),
            pl.BlockSpec((1, 1, D_MODEL), lambda b, j, r: ((b * rt + r) // PROJ_SEQ_TILES, 0, 0)),
            pl.BlockSpec((1, 1, D_MODEL), lambda b, j, r: ((b * rt + r) // PROJ_SEQ_TILES, 0, 0)),
            pl.BlockSpec((1, D_MODEL), lambda b, j, r: (0, 0)),
            pl.BlockSpec((PROJ_TN, D_MODEL), lambda b, j, r: (j, 0)),
            pl.BlockSpec((GLA_GATE_RANK, D_MODEL), lambda b, j, r: (N_MAIN // GLA_GATE_RANK, 0)),
            pl.BlockSpec((SEQ, LANES), lambda b, j, r: (0, 0)),
            pl.BlockSpec((SEQ, LANES), lambda b, j, r: (0, 0)),
        ],
        out_specs=[
            pl.BlockSpec((n_cb, PROJ_TM, LANES), lambda b, j, r: (j, b * rt + r, 0)),
            pl.BlockSpec((PROJ_TM, LANES), x_tile),
        ],
        out_shape=[
            jax.ShapeDtypeStruct((N_COLBLK, TOKENS, LANES), BF16),
            jax.ShapeDtypeStruct((TOKENS, LANES), BF16),
        ],
        scratch_shapes=[
            pltpu.VMEM((PROJ_TM, D_MODEL), BF16),
            pltpu.VMEM((rt, PROJ_TM, D_MODEL), BF16),
        ],
        compiler_params=pltpu.CompilerParams(
            dimension_semantics=("arbitrary", "arbitrary", "arbitrary"),
            vmem_limit_bytes=PROJ_VMEM_LIMIT),
        name="norm_in_proj",
    )(xf, shift, scale, norm_w, w_t, w_gate, cos, sin)


ATT_T = 256
ATT_NBLK = SEQ // ATT_T
ATT_LEAD = 9
assert ATT_LEAD >= ATT_NBLK - 1
ATT_ONES_ROWS = 16
ATT_HEADS_PER_STEP = 2
ATT_SLOTS = ATT_LEAD + 4


def _attn_kernel(lq1_ref, lk1_ref, lq2_ref, lk2_ref, dnw_ref, q_ref, k_ref, v_ref, dg_ref,
                 c_ref, wa_ref, ba_ref, wo_ref, o_ref, gate_ref, wob_ref, vt_ref, s_ref):
    gate_ref[...] = _ada_block(c_ref, wa_ref, ba_ref)
    wob_ref[...] = wo_ref[...].astype(BF16)

    lam = (jnp.exp(jnp.sum(lq1_ref[...] * lk1_ref[...], keepdims=True))
           - jnp.exp(jnp.sum(lq2_ref[...] * lk2_ref[...], keepdims=True))
           + LAMBDA_INIT)
    out_gain = dnw_ref[...] * (1.0 - LAMBDA_INIT)

    for hd in range(ATT_HEADS_PER_STEP):
        for jb in range(ATT_NBLK):
            cols = pl.ds(jb * ATT_T, ATT_T)
            vb = v_ref[hd, jb * ATT_T:(jb + 1) * ATT_T, :].astype(F32)
            vt_ref[hd, 0:DIFF_HEAD_DIM, cols] = vb.T.astype(BF16)
        vt_ref[hd, DIFF_HEAD_DIM:, :] = jnp.ones((ATT_ONES_ROWS, SEQ), BF16)

    lane = lax.broadcasted_iota(jnp.int32, (ATT_T, LANES), 1)
    comp_a = lane < DIFF_QK_DIM
    krow = lax.broadcasted_iota(jnp.int32, (ATT_T, ATT_T), 0) // CHUNK
    qcol = lax.broadcasted_iota(jnp.int32, (ATT_T, ATT_T), 1) // CHUNK
    diag_mask = krow <= qcol
    diag_mask2 = jnp.concatenate([diag_mask, diag_mask], axis=1)

    units = [(hd, qi, j) for hd in range(ATT_HEADS_PER_STEP)
             for qi in range(ATT_NBLK) for j in range(qi + 1)]
    blocks = {(hd, qi): {"m": None}
              for hd in range(ATT_HEADS_PER_STEP) for qi in range(ATT_NBLK)}

    def block_queries(hd, qi):
        q = q_ref[hd, pl.ds(qi * ATT_T, ATT_T), :]
        zero = jnp.zeros_like(q)
        return jnp.concatenate([jnp.where(comp_a, q, zero), jnp.where(comp_a, zero, q)], axis=0)

    dyn0 = lax.shift_right_logical(pl.program_id(0), 20)

    def score_unit(t, hd, qi, j):
        st = blocks[hd, qi]
        if j == 0:
            st["q2"] = block_queries(hd, qi)
        s = _nt_dot(k_ref[hd, pl.ds(j * ATT_T, ATT_T), :], st["q2"])
        if j == qi:
            s = jnp.where(diag_mask2, s, NEG_BIG)
        s_ref[dyn0 + t % ATT_SLOTS] = s
        mj = jnp.max(s, axis=0, keepdims=True)
        st["m"] = mj if st["m"] is None else jnp.maximum(st["m"], mj)

    def value_unit(t, hd, qi, j):
        st = blocks[hd, qi]
        p = jnp.exp2(s_ref[dyn0 + t % ATT_SLOTS] - st["m"])
        pvj = jnp.dot(vt_ref[hd, :, j * ATT_T:(j + 1) * ATT_T], p.astype(BF16),
                      preferred_element_type=F32)
        st["pv"] = pvj if j == 0 else st["pv"] + pvj
        if j == qi:
            finish(hd, qi, st["pv"])

    def finish(hd, qi, pv_l):
        rows = pl.ds(qi * ATT_T, ATT_T)
        l = pv_l[DIFF_HEAD_DIM:DIFF_HEAD_DIM + 1, :]
        pv = pv_l[0:DIFF_HEAD_DIM, :] * (1.0 / l)
        out_t = pv[:, :ATT_T] - lam * pv[:, ATT_T:]
        out = out_t.T
        ms = jnp.mean(out * out, axis=-1, keepdims=True)
        y = out * lax.rsqrt(ms + NORM_EPS) * out_gain
        g = dg_ref[hd, rows, :].astype(F32)
        o_ref[hd, rows, :] = (y * _silu(g)).astype(BF16)

    for t in range(len(units) + ATT_LEAD):
        if t < len(units):
            score_unit(t, *units[t])
        if t >= ATT_LEAD:
            value_unit(t - ATT_LEAD, *units[t - ATT_LEAD])


def _attn_call(p, lq1, lk1, lq2, lk2, dnw, c_pad, w_ada, b_ada, w_out):
    hp = ATT_HEADS_PER_STEP
    steps_per_batch = DIFF_HEADS // hp
    side_w = D_MODEL // (BATCH * steps_per_batch)
    assert side_w % LANES == 0
    wo_rows = pl.BlockSpec((side_w, D_MODEL), lambda b, h: (b * steps_per_batch + h, 0))
    gate_cb0 = 2 * D_MODEL // side_w
    vec = lambda n: pl.BlockSpec((1, n), lambda b, h: (0, 0))
    slab = lambda cb0: pl.BlockSpec((hp, SEQ, LANES), lambda b, h: (cb0 // hp + h, b, 0))
    gate_blk = lambda rows, cb0: pl.BlockSpec(
        (rows, side_w), lambda b, h: (0, cb0 + b * steps_per_batch + h))
    return pl.pallas_call(
        _attn_kernel,
        grid=(BATCH, steps_per_batch),
        in_specs=[vec(DIFF_QK_DIM)] * 4 + [vec(DIFF_HEAD_DIM),
                  slab(CB_DQ), slab(CB_DK), slab(CB_DV), slab(CB_DG),
                  pl.BlockSpec((8, D_MODEL), lambda b, h: (0, 0)),
                  gate_blk(D_MODEL, gate_cb0), gate_blk(1, gate_cb0), wo_rows],
        out_specs=[pl.BlockSpec((hp, SEQ, LANES), lambda b, h: (h, b, 0)), gate_blk(8, 0),
                   wo_rows],
        out_shape=[jax.ShapeDtypeStruct((DIFF_HEADS, TOKENS, LANES), BF16),
                   jax.ShapeDtypeStruct((8, D_MODEL), F32),
                   jax.ShapeDtypeStruct((D_MODEL, D_MODEL), BF16)],
        scratch_shapes=[
            pltpu.VMEM((hp, DIFF_HEAD_DIM + ATT_ONES_ROWS, SEQ), BF16),
            pltpu.VMEM((ATT_SLOTS, ATT_T, 2 * ATT_T), F32),
        ],
        compiler_params=pltpu.CompilerParams(
            dimension_semantics=("arbitrary", "arbitrary"), vmem_limit_bytes=VMEM_LIMIT),
        name="diff_attn",
    )(lq1, lk1, lq2, lk2, dnw, p, p, p, p, c_pad, w_ada, b_ada, w_out)


GLA_BLK = 256
GLA_NCHUNK = SEQ // CHUNK
GLA_HEADS_PER_STEP = 2


def _split_hi_lo(x):
    hi = x.astype(BF16)
    lo = (x - hi.astype(F32)).astype(BF16)
    return hi, lo


def _gla_kernel(g_ref, w2_ref, b2_ref, gw_ref, q_ref, k_ref, v_ref, gg_ref, o_ref,
                kdec_ref, tot_ref, kv_ref, st_ref):
    for hd in range(GLA_HEADS_PER_STEP):
        _gla_head(g_ref, w2_ref.at[hd], b2_ref.at[hd], gw_ref, q_ref.at[hd], k_ref.at[hd],
                  v_ref.at[hd], gg_ref.at[hd], o_ref.at[hd], kdec_ref, tot_ref, kv_ref, st_ref)


def _gla_head(g_ref, w2_ref, b2_ref, gw_ref, q_ref, k_ref, v_ref, gg_ref, o_ref,
              kdec_ref, tot_ref, kv_ref, st_ref):
    z = jnp.dot(g_ref[...], w2_ref[...], preferred_element_type=F32) + b2_ref[...]
    log_a = (jnp.minimum(z, 0.0) - jnp.log(1.0 + jnp.exp(-jnp.abs(z)))) * (1.0 / GLA_GATE_TAU)

    r = lax.broadcasted_iota(jnp.int32, (GLA_BLK, GLA_BLK), 0)
    c = lax.broadcasted_iota(jnp.int32, (GLA_BLK, GLA_BLK), 1)
    after = jnp.where(((r // CHUNK) == (c // CHUNK)) & (c > r), 1.0, 0.0).astype(BF16)
    for blk in range(SEQ // GLA_BLK):
        rows = pl.ds(blk * GLA_BLK, GLA_BLK)
        la = log_a[blk * GLA_BLK:(blk + 1) * GLA_BLK, :]
        hi, lo = _split_hi_lo(la)
        both = jnp.dot(after, jnp.concatenate([hi, lo], axis=1), preferred_element_type=F32)
        suffix = both[:, :GLA_K_DIM] + both[:, GLA_K_DIM:]
        kdec_ref[rows, :] = (k_ref[rows, :].astype(F32) * jnp.exp(suffix)).astype(BF16)
        tot_ref[rows, :] = suffix + la

    dec_rows = jnp.exp(tot_ref[pl.ds(0, GLA_NCHUNK, stride=CHUNK), :])
    dec_cols = jnp.concatenate(
        [dec_rows, jnp.zeros((GLA_K_DIM - GLA_NCHUNK, GLA_K_DIM), F32)], axis=0).T

    dyn0 = lax.shift_right_logical(pl.program_id(0), 20)
    for ci in range(GLA_NCHUNK):
        rows = pl.ds(ci * CHUNK, CHUNK)
        v_c = jnp.concatenate([v_ref[0, rows, :], v_ref[1, rows, :]], axis=1)
        kv_ref[dyn0 + ci] = _tn_dot(kdec_ref[rows, :], v_c)

    state = jnp.zeros((GLA_K_DIM, GLA_V_DIM), F32)
    for ci in range(GLA_NCHUNK):
        state = state * dec_cols[:, ci:ci + 1] + kv_ref[dyn0 + ci]
        st_ref[ci] = state.astype(BF16)

    gw = gw_ref[...]
    qs = GLA_K_DIM ** -0.5
    for ci in range(GLA_NCHUNK):
        rows = pl.ds(ci * CHUNK, CHUNK)
        o = jnp.dot(q_ref[rows, :], st_ref[ci], preferred_element_type=F32)
        ms = jnp.mean(o * o, axis=-1, keepdims=True) * (qs * qs)
        y = o * (qs * lax.rsqrt(ms + NORM_EPS)) * gw
        gate = jnp.concatenate([gg_ref[0, rows, :], gg_ref[1, rows, :]], axis=1).astype(F32)
        y = y * _silu(gate)
        o_ref[0, rows, :] = y[:, :LANES].astype(BF16)
        o_ref[1, rows, :] = y[:, LANES:].astype(BF16)


def _gla_call(p, g, w2_heads, b2_heads, gw):
    hp = GLA_HEADS_PER_STEP
    qk = lambda cb0: pl.BlockSpec((hp, SEQ, LANES), lambda b, h: (cb0 // hp + h, b, 0))
    p4 = p.reshape(N_COLBLK // 2, 2, TOKENS, LANES)
    vg = lambda cb0: pl.BlockSpec((hp, 2, SEQ, LANES), lambda b, h: (cb0 // 2 // hp + h, 0, b, 0))
    out = pl.pallas_call(
        _gla_kernel,
        grid=(BATCH, GLA_HEADS // hp),
        in_specs=[
            pl.BlockSpec((SEQ, LANES), lambda b, h: (b, 0)),
            pl.BlockSpec((hp, LANES, GLA_K_DIM), lambda b, h: (h, 0, 0)),
            pl.BlockSpec((hp, 1, GLA_K_DIM), lambda b, h: (h, 0, 0)),
            pl.BlockSpec((1, GLA_V_DIM), lambda b, h: (0, 0)),
            qk(CB_GQ), qk(CB_GK), vg(CB_GV), vg(CB_GG),
        ],
        out_specs=pl.BlockSpec((hp, 2, SEQ, LANES), lambda b, h: (h, 0, b, 0)),
        out_shape=jax.ShapeDtypeStruct((GLA_HEADS, 2, TOKENS, LANES), BF16),
        scratch_shapes=[
            pltpu.VMEM((SEQ, GLA_K_DIM), BF16),
            pltpu.VMEM((SEQ, GLA_K_DIM), F32),
            pltpu.VMEM((GLA_NCHUNK, GLA_K_DIM, GLA_V_DIM), F32),
            pltpu.VMEM((GLA_NCHUNK, GLA_K_DIM, GLA_V_DIM), BF16),
        ],
        compiler_params=pltpu.CompilerParams(
            dimension_semantics=("arbitrary", "arbitrary"), vmem_limit_bytes=VMEM_LIMIT),
        name="gla_scan",
    )(g, w2_heads, b2_heads, gw, p, p, p4, p4)
    return out.reshape(2 * GLA_HEADS, TOKENS, LANES)


OUT_TM = 512
OUT_SPLITS = (256, 256)
assert sum(OUT_SPLITS) == OUT_TM
OUT_VMEM_LIMIT = (5 * OUT_TM * D_MODEL * 4 + 2 * OUT_TM * D_MODEL * 2 + D_MODEL * D_MODEL * 2
                  + 12 * 1024 * 1024)


def _out_kernel(a_ref, b_ref, wb_ref, x_ref, gate_ref, fw_ref, o_ref):
    gate = gate_ref[0]
    fw = fw_ref[...]
    row0 = 0
    for n_rows in OUT_SPLITS:
        rows = pl.ds(row0, n_rows)
        row0 += n_rows
        mix = jnp.concatenate(
            [a_ref[h, rows, :] for h in range(a_ref.shape[0])]
            + [b_ref[h, rows, :] for h in range(b_ref.shape[0])], axis=1)
        y = jnp.dot(mix, wb_ref[...], preferred_element_type=F32)
        h_res = x_ref[rows, :] + gate * y
        ms = jnp.mean(h_res * h_res, axis=-1, keepdims=True)
        o_ref[rows, :] = h_res * lax.rsqrt(ms + NORM_EPS) * fw


def _out_call(a, b, w_out, xf, gate, fw):
    per_seq = SEQ // OUT_TM

    def outer(a_hbm, b_hbm, w_ref, x_hbm, gate_hbm, fw_ref, o_hbm):
        def body(a_ref, b_ref, x_ref, gate_ref, o_ref):
            _out_kernel(a_ref, b_ref, w_ref, x_ref, gate_ref, fw_ref, o_ref)

        pltpu.emit_pipeline(
            body,
            grid=(TOKENS // OUT_TM,),
            in_specs=[
                pl.BlockSpec((DIFF_HEADS, OUT_TM, LANES), lambda i: (0, i, 0)),
                pl.BlockSpec((2 * GLA_HEADS, OUT_TM, LANES), lambda i: (0, i, 0)),
                pl.BlockSpec((OUT_TM, D_MODEL), lambda i: (i, 0), pipeline_mode=pl.Buffered(3)),
                pl.BlockSpec((1, 1, D_MODEL), lambda i: (i // per_seq, 0, 0)),
            ],
            out_specs=[pl.BlockSpec((OUT_TM, D_MODEL), lambda i: (i, 0))],
        )(a_hbm, b_hbm, x_hbm, gate_hbm, o_hbm)

    hbm = pl.BlockSpec(memory_space=pl.ANY)
    vmem = pl.BlockSpec(memory_space=pltpu.VMEM)
    return pl.pallas_call(
        outer,
        in_specs=[hbm, hbm, vmem, hbm, hbm, vmem],
        out_specs=hbm,
        out_shape=jax.ShapeDtypeStruct((TOKENS, D_MODEL), F32),
        compiler_params=pltpu.CompilerParams(vmem_limit_bytes=OUT_VMEM_LIMIT),
        name="out_proj_norm",
    )(a, b, w_out, xf, gate, fw)


def _rope_tables():
    inv_freq = ROPE_THETA ** (-np.arange(0, DIFF_QK_DIM, 2, dtype=np.float64) / DIFF_QK_DIM)
    ang = np.arange(SEQ, dtype=np.float64)[:, None] * inv_freq[None, :]
    cos, sin = np.cos(ang), np.sin(ang)
    cos_t = np.tile(cos, (1, 4)).astype(np.float32)
    sin_t = np.concatenate([-sin, sin, -sin, sin], axis=1).astype(np.float32)
    return jnp.asarray(cos_t), jnp.asarray(sin_t)


def kernel(x, c, norm_w, w_ada, b_ada, w_in, lambda_q1, lambda_k1, lambda_q2, lambda_k2,
           diff_norm_w, gla_gate_w2, gla_gate_b, gla_norm_w, w_out, final_norm_w):
    assert x.shape == (BATCH, SEQ, D_MODEL) and w_in.shape[0] == 1
    xf = x.reshape(TOKENS, D_MODEL).astype(F32)

    c_pad = jnp.pad(c.astype(F32), ((0, 8 - BATCH), (0, 0)))
    b_ada_row = b_ada[0][None, :]
    mod = _ada_call(c_pad, w_ada[0], b_ada_row, 2 * D_MODEL)[:BATCH]
    shift = mod[:, :D_MODEL].reshape(BATCH, 1, D_MODEL)
    scale = mod[:, D_MODEL:].reshape(BATCH, 1, D_MODEL)

    w_t = w_in[0].T
    cos, sin = _rope_tables()
    p, g = _proj_call(xf, shift, scale, norm_w[0][None, :], w_t, w_t, cos, sin)

    row = lambda v: v[0][None, :].astype(F32)
    a, gate8, w_out_bf16 = _attn_call(
        p, row(lambda_q1), row(lambda_k1), row(lambda_q2), row(lambda_k2), row(diff_norm_w),
        c_pad, w_ada[0], b_ada_row, w_out[0])
    gate = gate8[:BATCH].reshape(BATCH, 1, D_MODEL)

    w2_heads = jnp.pad(gla_gate_w2[0], ((0, LANES - GLA_GATE_RANK), (0, 0))).astype(BF16)
    w2_heads = w2_heads.reshape(LANES, GLA_HEADS, GLA_K_DIM).transpose(1, 0, 2)
    b2_heads = gla_gate_b[0].astype(F32).reshape(GLA_HEADS, 1, GLA_K_DIM)
    b = _gla_call(p, g, w2_heads, b2_heads, row(gla_norm_w))

    out = _out_call(a, b, w_out_bf16, xf, gate, final_norm_w[None, :].astype(F32))
    return out.reshape(BATCH, SEQ, D_MODEL).astype(x.dtype)
```
